```python
import math
import jax, jax.numpy as jnp
from jax import lax
import numpy as np

D_MODEL = 1024
BATCH = 16
SEQ = 4096
DEPTH = 2

CHUNK = 64
HEAD_DIM = 64
D_MIX = D_MODEL
N_GH = 4
GROUP_W = N_GH * HEAD_DIM
IDX_HEADS = 8
IDX_DIM = 64
TOPK_MAX = 256
RET_THETA = 10000.0
GLA_DK = HEAD_DIM // 2
GLA_RANK = 16
GLA_TAU = 16.0
DIFF_DK = HEAD_DIM // 2
ROPE_THETA = 500000.0
ROPE_FRAC = 4
D_FF = 4 * D_MODEL
Q_BLOCK = 128
EPS = 1e-6
NEG_INF = -1e30

IN_SPLITS = (
    GROUP_W, HEAD_DIM, HEAD_DIM, IDX_HEADS * IDX_DIM, IDX_DIM, IDX_HEADS,
    GROUP_W, GROUP_W, GROUP_W, GROUP_W,
    N_GH * GLA_DK, N_GH * GLA_DK, GROUP_W, GLA_RANK, GROUP_W,
    2 * N_GH * DIFF_DK, 2 * N_GH * DIFF_DK, GROUP_W,
)
IN_COLS = sum(IN_SPLITS)

kernel_name = "chunk_causal_hybrid_head_group_trunk"


def rms_norm(x, g):
    xf = x.astype(jnp.float32)
    y = xf * lax.rsqrt(jnp.mean(xf * xf, axis=-1, keepdims=True) + EPS)
    return (y * g.astype(jnp.float32)).astype(x.dtype)


def head_norm(y, g, center, dtype):
    B, T, H, D = y.shape
    yf = y.astype(jnp.float32)
    if center:
        yf = yf - jnp.mean(yf, axis=-1, keepdims=True)
    yf = yf * lax.rsqrt(jnp.mean(yf * yf, axis=-1, keepdims=True) + EPS)
    yf = yf * g.astype(jnp.float32).reshape(H, D)
    return yf.reshape(B, T, H * D).astype(dtype)


def rope(x, pos, rot_dims, theta):
    half = rot_dims // 2
    inv = theta ** (-jnp.arange(half, dtype=jnp.float32) / half)
    ang = pos.astype(jnp.float32)[..., None] * inv
    cos = jnp.cos(ang)[:, :, None, :].astype(x.dtype)
    sin = jnp.sin(ang)[:, :, None, :].astype(x.dtype)
    x1, x2, xp = x[..., :half], x[..., half:rot_dims], x[..., rot_dims:]
    return jnp.concatenate([x1 * cos - x2 * sin, x1 * sin + x2 * cos, xp], axis=-1)


def split_blocks(a, size):
    B, T = a.shape[:2]
    return jnp.moveaxis(a.reshape((B, T // size, size) + a.shape[2:]), 1, 0)


def merge_blocks(a):
    N, B, S = a.shape[:3]
    return jnp.moveaxis(a, 0, 1).reshape((B, N * S) + a.shape[3:])


def dsa_mixer(q, k, v, qi, ki, wi, pos):
    B, T = q.shape[:2]
    topk = min(TOPK_MAX, T // 4)
    q = rope(q, pos, HEAD_DIM // ROPE_FRAC, ROPE_THETA) * (HEAD_DIM ** -0.5)
    k = rope(k[:, :, None], pos, HEAD_DIM // ROPE_FRAC, ROPE_THETA)[:, :, 0]
    qi = rope(qi, pos, IDX_DIM // ROPE_FRAC, ROPE_THETA)
    ki = rope(ki[:, :, None], pos, IDX_DIM // ROPE_FRAC, ROPE_THETA)[:, :, 0]
    wi = wi * (IDX_HEADS ** -0.5 * IDX_DIM ** -0.5)
    key_chunk = jnp.arange(T) // CHUNK
    gather = jax.vmap(lambda tab, idx: tab[idx])

    def block(args):
        qb, qib, wib, blk = args
        q_chunk = (blk * Q_BLOCK + jnp.arange(Q_BLOCK)) // CHUNK
        admissible = key_chunk[None, :] <= q_chunk[:, None]
        logits = jnp.einsum("bqhd,bsd->bqhs", qib, ki)
        score = jnp.einsum("bqh,bqhs->bqs", wib, jax.nn.relu(logits)).astype(jnp.float32)
        score = jnp.where(admissible[None], score, NEG_INF)
        _, sel = lax.top_k(score, topk)
        valid = (sel // CHUNK) <= q_chunk[None, :, None]
        k_sel = gather(k, sel)
        v_sel = gather(v, sel)
        s = jnp.einsum("bqhd,bqkd->bqhk", qb, k_sel).astype(jnp.float32)
        s = jnp.where(valid[:, :, None, :], s, NEG_INF)
        p = jax.nn.softmax(s, axis=-1).astype(v.dtype)
        return jnp.einsum("bqhk,bqkd->bqhd", p, v_sel)

    out = lax.map(block, (split_blocks(q, Q_BLOCK), split_blocks(qi, Q_BLOCK),
                          split_blocks(wi, Q_BLOCK), jnp.arange(T // Q_BLOCK)))
    return merge_blocks(out)


def retention_mixer(q, k, v, pos):
    B, T, H, D = q.shape
    f32 = jnp.float32
    q = (rope(q, pos, HEAD_DIM, RET_THETA) * (HEAD_DIM ** -0.5)).astype(f32)
    k = rope(k, pos, HEAD_DIM, RET_THETA).astype(f32)
    v = v.astype(f32)
    log_g = jnp.log(1.0 - 2.0 ** (-5.0 - jnp.arange(H, dtype=f32)))
    i = jnp.arange(CHUNK, dtype=f32)
    d_intra = jnp.exp(jnp.abs(i[:, None] - i[None, :])[None] * log_g[:, None, None])
    xi = jnp.exp((i + 1.0)[:, None] * log_g[None, :])[None, :, :, None]
    zeta = jnp.exp((CHUNK - 1.0 - i)[:, None] * log_g[None, :])[None, :, :, None]
    g_chunk = jnp.exp(CHUNK * log_g)[None, :, None, None]

    def step(R, xs):
        qc, kc, vc = xs
        s = jnp.einsum("bihd,bjhd->bhij", qc, kc) * d_intra[None]
        intra = jnp.einsum("bhij,bjhd->bihd", s, vc)
        inter = jnp.einsum("bihd,bhde->bihe", qc, R) * xi
        R = g_chunk * R + jnp.einsum("bjhd,bjhe->bhde", kc * zeta, vc)
        return R, intra + inter

    R0 = jnp.zeros((B, H, D, D), f32)
    _, out = lax.scan(step, R0, (split_blocks(q, CHUNK), split_blocks(k, CHUNK),
                                 split_blocks(v, CHUNK)))
    return merge_blocks(out)


def gla_mixer(q, k, v, a_low, w_a2, b_a):
    B, T, H, DV = v.shape
    f32 = jnp.float32
    log_a = jax.nn.log_sigmoid((jnp.einsum("btr,rk->btk", a_low, w_a2) + b_a).astype(f32)) / GLA_TAU
    log_a = log_a.reshape(B, T, H, GLA_DK)
    q = (q * (GLA_DK ** -0.5)).astype(f32)
    k = k.astype(f32)
    v = v.astype(f32)

    def step(S, xs):
        qc, kc, vc, lac = xs
        b = jnp.cumsum(lac, axis=1)
        decay = jnp.exp(-jnp.abs(b[:, :, None] - b[:, None, :]))
        A = jnp.einsum("bthk,btshk,bshk->bhts", qc, decay, kc)
        intra = jnp.einsum("bhts,bshd->bthd", A, vc)
        inter = jnp.einsum("bthk,bhkd->bthd", qc * jnp.exp(b), S)
        b_last = b[:, -1]
        S = jnp.exp(b_last)[..., None] * S + jnp.einsum(
            "bshk,bshd->bhkd", kc * jnp.exp(b_last[:, None] - b), vc)
        return S, intra + inter

    S0 = jnp.zeros((B, H, GLA_DK, DV), f32)
    _, out = lax.scan(step, S0, (split_blocks(q, CHUNK), split_blocks(k, CHUNK),
                                 split_blocks(v, CHUNK), split_blocks(log_a, CHUNK)))
    return merge_blocks(out)


def diff_mixer(q, k, v, pos, lam_q1, lam_k1, lam_q2, lam_k2, lam_init):
    B, T = q.shape[:2]
    f32 = jnp.float32
    q = rope(q, pos, DIFF_DK // ROPE_FRAC, ROPE_THETA) * (DIFF_DK ** -0.5)
    k = rope(k, pos, DIFF_DK // ROPE_FRAC, ROPE_THETA)
    q = q.reshape(B, T, N_GH, 2, DIFF_DK)
    k = k.reshape(B, T, N_GH, 2, DIFF_DK)
    lam = (jnp.exp(jnp.sum(lam_q1.astype(f32) * lam_k1.astype(f32)))
           - jnp.exp(jnp.sum(lam_q2.astype(f32) * lam_k2.astype(f32))) + lam_init)
    key_chunk = jnp.arange(T) // CHUNK

    def block(args):
        qb, blk = args
        q_chunk = (blk * Q_BLOCK + jnp.arange(Q_BLOCK)) // CHUNK
        mask = key_chunk[None, :] <= q_chunk[:, None]
        s = jnp.einsum("bqhcd,bshcd->bhcqs", qb, k).astype(f32)
        s = jnp.where(mask[None, None, None], s, NEG_INF)
        p = jax.nn.softmax(s, axis=-1)
        w = p[:, :, 0] - lam * p[:, :, 1]
        return jnp.einsum("bhqs,bshd->bqhd", w.astype(v.dtype), v)

    out = lax.map(block, (split_blocks(q, Q_BLOCK), jnp.arange(T // Q_BLOCK)))
    return merge_blocks(out)


def hybrid_layer(x, cond, pos, layer_idx, mod_w, mod_b, attn_pre_g, attn_post_g,
                 mlp_pre_g, mlp_post_g, w_in, gla_wa2, gla_ba, lam_q1, lam_k1,
                 lam_q2, lam_k2, head_norm_g, w_out, mlp_w1, mlp_w2):
    B, T, _ = x.shape
    dt = x.dtype
    mod = (cond @ mod_w + mod_b)[:, None, :]
    shift1, scale1, gate1, shift2, scale2, gate2 = jnp.split(mod, 6, axis=-1)

    h = rms_norm(x, attn_pre_g) * (1.0 + scale1) + shift1
    proj = h @ w_in
    (a_q, a_k, a_v, a_qi, a_ki, a_wi, b_q, b_k, b_v, b_g,
     c_q, c_k, c_v, c_a, c_g, d_q, d_k, d_v) = jnp.split(
        proj, np.cumsum(IN_SPLITS)[:-1].tolist(), axis=-1)
    heads = lambda a, d: a.reshape(B, T, -1, d)
    g_a, g_b, g_c, g_d = jnp.split(head_norm_g, 4)
    lam_init = 0.8 - 0.6 * math.exp(-0.3 * layer_idx)

    y_a = head_norm(dsa_mixer(heads(a_q, HEAD_DIM), a_k, a_v, heads(a_qi, IDX_DIM),
                              a_ki, a_wi, pos), g_a, False, dt)
    y_b = jax.nn.silu(b_g) * head_norm(retention_mixer(
        heads(b_q, HEAD_DIM), heads(b_k, HEAD_DIM), heads(b_v, HEAD_DIM), pos), g_b, True, dt)
    y_c = jax.nn.silu(c_g) * head_norm(gla_mixer(
        heads(c_q, GLA_DK), heads(c_k, GLA_DK), heads(c_v, HEAD_DIM), c_a, gla_wa2, gla_ba),
        g_c, True, dt)
    y_d = head_norm(diff_mixer(heads(d_q, DIFF_DK), heads(d_k, DIFF_DK), heads(d_v, HEAD_DIM),
                               pos, lam_q1, lam_k1, lam_q2, lam_k2, lam_init),
                    g_d, False, dt) * (1.0 - lam_init)
    y = jnp.concatenate([y_a, y_b, y_c, y_d], axis=-1) @ w_out
    x = x + gate1 * rms_norm(y, attn_post_g)

    h = rms_norm(x, mlp_pre_g) * (1.0 + scale2) + shift2
    y = jnp.square(jax.nn.relu(h @ mlp_w1)) @ mlp_w2
    return x + gate2 * rms_norm(y, mlp_post_g)


def setup_inputs(seed: int = 0) -> dict:
    key = jax.random.key(seed)
    ks = jax.random.split(key, 20)
    nrm = lambda k, shape, s: jax.random.normal(k, shape, jnp.float32) * s
    gain = lambda k, shape: 1.0 + 0.02 * jax.random.normal(k, shape, jnp.float32)
    x = nrm(ks[0], (BATCH, SEQ, D_MODEL), 1.0)
    c = nrm(ks[1], (BATCH, D_MODEL), 1.0)
    offset = jax.random.randint(ks[2], (BATCH, 1), 0, 64, dtype=jnp.int32) * CHUNK
    positions = offset + jnp.arange(SEQ, dtype=jnp.int32)[None, :]
    return {
        "x": x,
        "c": c,
        "positions": positions,
        "mod_w": nrm(ks[3], (DEPTH, D_MODEL, 6 * D_MODEL), 0.5 * D_MODEL ** -0.5),
        "mod_b": nrm(ks[4], (DEPTH, 6 * D_MODEL), 0.01),
        "attn_pre_g": gain(ks[5], (DEPTH, D_MODEL)),
        "attn_post_g": gain(ks[6], (DEPTH, D_MODEL)),
        "mlp_pre_g": gain(ks[7], (DEPTH, D_MODEL)),
        "mlp_post_g": gain(ks[8], (DEPTH, D_MODEL)),
        "w_in": nrm(ks[9], (DEPTH, D_MODEL, IN_COLS), D_MODEL ** -0.5),
        "gla_wa2": nrm(ks[10], (DEPTH, GLA_RANK, N_GH * GLA_DK), GLA_RANK ** -0.5),
        "gla_ba": nrm(ks[11], (DEPTH, N_GH * GLA_DK), 0.1),
        "lam_q1": nrm(ks[12], (DEPTH, DIFF_DK), 0.1),
        "lam_k1": nrm(ks[13], (DEPTH, DIFF_DK), 0.1),
        "lam_q2": nrm(ks[14], (DEPTH, DIFF_DK), 0.1),
        "lam_k2": nrm(ks[15], (DEPTH, DIFF_DK), 0.1),
        "head_norm_g": gain(ks[16], (DEPTH, D_MIX)),
        "w_out": nrm(ks[17], (DEPTH, D_MIX, D_MODEL), D_MIX ** -0.5),
        "mlp_w1": nrm(ks[18], (DEPTH, D_MODEL, D_FF), D_MODEL ** -0.5),
        "mlp_w2": nrm(ks[19], (DEPTH, D_FF, D_MODEL), D_FF ** -0.5),
    }


def reference(x, c, positions, mod_w, mod_b, attn_pre_g, attn_post_g, mlp_pre_g,
              mlp_post_g, w_in, gla_wa2, gla_ba, lam_q1, lam_k1, lam_q2, lam_k2,
              head_norm_g, w_out, mlp_w1, mlp_w2):
    cond = jax.nn.silu(c)
    for l in range(DEPTH):
        x = hybrid_layer(x, cond, positions, l, mod_w[l], mod_b[l], attn_pre_g[l],
                         attn_post_g[l], mlp_pre_g[l], mlp_post_g[l], w_in[l],
                         gla_wa2[l], gla_ba[l], lam_q1[l], lam_k1[l], lam_q2[l],
                         lam_k2[l], head_norm_g[l], w_out[l], mlp_w1[l], mlp_w2[l])
    return x
```

```python
import functools
import math

import numpy as np
import jax
import jax.numpy as jnp
from jax import lax
from jax.experimental import pallas as pl
from jax.experimental.pallas import tpu as pltpu

F32 = jnp.float32
BF16 = jnp.bfloat16
I32 = jnp.int32

D_MODEL = 1024
CHUNK = 64
HEAD_DIM = 64
N_GH = 4
GROUP_W = N_GH * HEAD_DIM
IDX_HEADS = 8
IDX_DIM = 64
TOPK_MAX = 256
RET_THETA = 10000.0
GLA_DK = HEAD_DIM // 2
GLA_RANK = 16
GLA_TAU = 16.0
DIFF_DK = HEAD_DIM // 2
ROPE_THETA = 500000.0
ROPE_FRAC = 4
D_FF = 4 * D_MODEL
EPS = 1e-6
NEG_INF = -1e30

LANES = 128
VMEM_LIMIT = 56 * 1024 * 1024

_IN_SPLITS = (
    GROUP_W, HEAD_DIM, HEAD_DIM, IDX_HEADS * IDX_DIM, IDX_DIM, IDX_HEADS,
    GROUP_W, GROUP_W, GROUP_W, GROUP_W,
    N_GH * GLA_DK, N_GH * GLA_DK, GROUP_W, GLA_RANK, GROUP_W,
    2 * N_GH * DIFF_DK, 2 * N_GH * DIFF_DK, GROUP_W,
)
_IN_OFF = np.concatenate([[0], np.cumsum(_IN_SPLITS)]).tolist()
(_A_Q, _A_K, _A_V, _A_QI, _A_KI, _A_WI, _B_Q, _B_K, _B_V, _B_G,
 _C_Q, _C_K, _C_V, _C_A, _C_G, _D_Q, _D_K, _D_V) = range(18)

SEG_A = 0
SEG_W = 1024
SEG_B = 1152
SEG_G = 1920
SEG_C = 2432
SEG_D = 3072
W_COLS = 3840

NT_DIMS = (((1,), (1,)), ((), ()))
TN_DIMS = (((0,), (0,)), ((), ()))

TM = 512
TQ = 128
TK = 256
TB = 256


def _float_key(v):
    bits = int(np.array(v, np.float32).view(np.int32))
    return -(bits & 0x7FFFFFFF) if bits < 0 else bits


NEG_KEY = _float_key(NEG_INF)
INT_MIN = -(2 ** 31)


def _cparams(*sem):
    return pltpu.CompilerParams(dimension_semantics=sem, vmem_limit_bytes=VMEM_LIMIT)


def _resident(shape, index_map):
    return pl.BlockSpec(shape, index_map, pipeline_mode=pl.Buffered(1))


def _dot(a, b):
    return jnp.dot(a, b, preferred_element_type=F32)


def _dot_nt(a, b):
    return lax.dot_general(a, b, NT_DIMS, preferred_element_type=F32)


def _dot_tn(a, b):
    return lax.dot_general(a, b, TN_DIMS, preferred_element_type=F32)


def _dot_f32(a, b):
    return jnp.dot(a, b, preferred_element_type=F32, precision=lax.Precision.HIGHEST)


def _lane(shape):
    return lax.broadcasted_iota(I32, shape, len(shape) - 1)


def _silu(x):
    return x / (1.0 + jnp.exp(-x))


def _mod_kernel(c_ref, w_ref, b_ref, o_ref):
    cond = _silu(c_ref[...])
    o_ref[...] = _dot_f32(cond, w_ref[...]) + b_ref[...]


def _mod_call(c, mod_w, mod_b):
    B, D = c.shape
    n = mod_w.shape[1] // D
    return pl.pallas_call(
        _mod_kernel,
        out_shape=jax.ShapeDtypeStruct((B, n * D), F32),
        grid=(n,),
        in_specs=[pl.BlockSpec((B, D), lambda j: (0, 0)),
                  pl.BlockSpec((D, D), lambda j: (0, j)),
                  pl.BlockSpec((1, D), lambda j: (0, j))],
        out_specs=pl.BlockSpec((B, D), lambda j: (0, j)),
        compiler_params=_cparams("arbitrary"),
        name="mod",
    )(c, mod_w, mod_b.reshape(1, -1))


def _rope_group(xg, cos, sin, half, period):
    lo = (_lane((1, LANES)) % period) < half
    swapped = jnp.where(lo, pltpu.roll(xg, LANES - half, 1), pltpu.roll(xg, half, 1))
    return xg * cos + swapped * sin


def _in_kernel(x_ref, mod_ref, g_ref, w_ref, ca_ref, sa_ref, cb_ref, sb_ref, cd_ref, sd_ref,
               qa_ref, kva_ref, wa_ref, qkvb_ref, gate_ref, c_ref, d_ref):
    x = x_ref[...]
    h = x * lax.rsqrt(jnp.mean(x * x, axis=-1, keepdims=True) + EPS) * g_ref[...]
    h = h * (1.0 + mod_ref[0, 1:2, :]) + mod_ref[0, 0:1, :]
    hb = h.astype(BF16)

    def seg(off, width):
        return _dot(hb, w_ref[:, off:off + width])

    lane = _lane((1, LANES))
    low = lane < HEAD_DIM

    ca, sa = ca_ref[...], sa_ref[...]
    a = seg(SEG_A, 1024)
    half_a = HEAD_DIM // ROPE_FRAC // 2
    for gi in range(2):
        r = _rope_group(a[:, gi * LANES:(gi + 1) * LANES], ca, sa, half_a, HEAD_DIM) * (HEAD_DIM ** -0.5)
        rs = pltpu.roll(r, HEAD_DIM, 1)
        qa_ref[:, (2 * gi) * LANES:(2 * gi + 1) * LANES] = jnp.where(low, r, 0.0).astype(BF16)
        qa_ref[:, (2 * gi + 1) * LANES:(2 * gi + 2) * LANES] = jnp.where(low, rs, 0.0).astype(BF16)
    for gi in range(4):
        r = _rope_group(a[:, (2 + gi) * LANES:(3 + gi) * LANES], ca, sa, half_a, HEAD_DIM)
        rs = pltpu.roll(r, HEAD_DIM, 1)
        qa_ref[:, (4 + 2 * gi) * LANES:(5 + 2 * gi) * LANES] = jnp.where(low, 0.0, rs).astype(BF16)
        qa_ref[:, (5 + 2 * gi) * LANES:(6 + 2 * gi) * LANES] = jnp.where(low, 0.0, r).astype(BF16)
    kva_ref[:, 0:LANES] = _rope_group(a[:, 6 * LANES:7 * LANES], ca, sa, half_a, HEAD_DIM).astype(BF16)
    kva_ref[:, LANES:2 * LANES] = a[:, 7 * LANES:8 * LANES].astype(BF16)

    wa_ref[...] = seg(SEG_W, LANES)

    cb, sb = cb_ref[...], sb_ref[...]
    b = seg(SEG_B, 768)
    for gi in range(4):
        r = _rope_group(b[:, gi * LANES:(gi + 1) * LANES], cb, sb, HEAD_DIM // 2, HEAD_DIM)
        if gi < 2:
            r = r * (HEAD_DIM ** -0.5)
        qkvb_ref[:, gi * LANES:(gi + 1) * LANES] = r.astype(BF16)
    qkvb_ref[:, 512:768] = b[:, 512:768].astype(BF16)

    gate_ref[...] = seg(SEG_G, 512)
    c_ref[...] = seg(SEG_C, 640)

    cd, sd = cd_ref[...], sd_ref[...]
    d = seg(SEG_D, 768)
    half_d = DIFF_DK // ROPE_FRAC // 2
    for gi in range(4):
        r = _rope_group(d[:, gi * LANES:(gi + 1) * LANES], cd, sd, half_d, DIFF_DK)
        if gi < 2:
            r = r * (DIFF_DK ** -0.5)
        d_ref[:, gi * LANES:(gi + 1) * LANES] = r.astype(BF16)
    d_ref[:, 512:768] = d[:, 512:768].astype(BF16)


def _in_call(x2, mod3, g, w_p, tabs, T):
    N, D = x2.shape
    nt = T // TM
    row = lambda i: (i, 0)
    tab_spec = pl.BlockSpec((TM, LANES), row)
    outs = [(1536, BF16), (256, BF16), (LANES, F32), (768, BF16), (512, F32), (640, F32), (768, BF16)]
    return pl.pallas_call(
        _in_kernel,
        out_shape=[jax.ShapeDtypeStruct((N, w), dt) for w, dt in outs],
        grid=(N // TM,),
        in_specs=[pl.BlockSpec((TM, D), row),
                  pl.BlockSpec((1, 6, D), lambda i: (i // nt, 0, 0)),
                  _resident((1, D), lambda i: (0, 0)),
                  _resident((D, W_COLS), lambda i: (0, 0))] + [tab_spec] * 6,
        out_specs=[pl.BlockSpec((TM, w), row) for w, _ in outs],
        compiler_params=_cparams("arbitrary"),
        name="in_proj",
    )(x2, mod3, g, w_p, *tabs)


def _pair_head_norm(o, gain, center):
    low = _lane((1, LANES)) < HEAD_DIM
    inv = 1.0 / HEAD_DIM

    def seg_mean(v):
        m0 = jnp.sum(jnp.where(low, v, 0.0), axis=1, keepdims=True)
        m1 = jnp.sum(jnp.where(low, 0.0, v), axis=1, keepdims=True)
        return jnp.where(low, m0, m1) * inv

    if center:
        o = o - seg_mean(o)
    return o * lax.rsqrt(seg_mean(o * o) + EPS) * gain


def _flash_tile(s, v, m, l, acc):
    m_new = jnp.maximum(m, jnp.max(s, axis=1, keepdims=True))
    alpha = jnp.exp(m - m_new)
    p = jnp.exp(s - m_new)
    l = alpha * l + jnp.sum(p, axis=1, keepdims=True)
    acc = alpha * acc + _dot(p.astype(BF16), v)
    return m_new, l, acc


def _dsa_kernel(q_ref, w_ref, kv_ref, g_ref, o_ref, bias_ref, key_ref, *, topk):
    q0 = pl.program_id(1) * TQ
    nkt = (q0 + TQ + TK - 1) // TK
    qchunk = (q0 + lax.broadcasted_iota(I32, (TQ, 1), 0)) // CHUNK
    wts = w_ref[...] * (IDX_HEADS ** -0.5 * IDX_DIM ** -0.5)

    def score_tile(kt, carry):
        k0 = pl.multiple_of(kt * TK, TK)
        kk = kv_ref[pl.ds(k0, TK), 0:LANES]
        acc = jnp.zeros((TQ, TK), F32)
        for h in range(IDX_HEADS):
            logits = _dot_nt(q_ref[:, (4 + h) * LANES:(5 + h) * LANES], kk)
            acc = acc + wts[:, h:h + 1] * jnp.maximum(logits, 0.0)
        kchunk = (k0 + _lane((1, TK))) // CHUNK
        acc = jnp.where(kchunk <= qchunk, acc, NEG_INF)
        bits = pltpu.bitcast(acc, I32)
        sign = bits >> 31
        key_ref[:, pl.ds(k0, TK)] = ((bits & 0x7FFFFFFF) ^ sign) - sign
        return carry

    lax.fori_loop(0, nkt, score_tile, 0)

    def count(pred):
        def body(kt, acc):
            k0 = pl.multiple_of(kt * TK, TK)
            hit = jnp.where(pred(key_ref[:, pl.ds(k0, TK)], k0), 1.0, 0.0)
            for j in range(TK // LANES):
                acc = acc + hit[:, j * LANES:(j + 1) * LANES]
            return acc
        acc = lax.fori_loop(0, nkt, body, jnp.zeros((TQ, LANES), F32))
        return jnp.sum(acc, axis=1, keepdims=True)

    kf = float(topk)
    c0 = count(lambda keys, k0: keys >= 0)
    thr = jnp.where(c0 >= kf, 0, INT_MIN).astype(I32)
    cnt = jnp.where(c0 >= kf, c0, (nkt * TK).astype(F32))

    def search(it, carry):
        thr, cnt = carry
        cand = thr + (jnp.int32(1) << (30 - it))
        c = count(lambda keys, k0: keys >= cand)
        ok = c >= kf
        return jnp.where(ok, cand, thr), jnp.where(ok, c, cnt)

    thr, cnt = lax.fori_loop(0, 31, search, (thr, cnt))
    floor_key = jnp.int32(NEG_KEY + 1)
    tied = jnp.max(jnp.where((cnt > kf) & (thr >= floor_key), 1.0, 0.0)) > 0.0

    @pl.when(jnp.logical_not(tied))
    def _():
        lim = jnp.maximum(thr, floor_key)

        def body(kt, carry):
            k0 = pl.multiple_of(kt * TK, TK)
            bias_ref[:, pl.ds(k0, TK)] = jnp.where(key_ref[:, pl.ds(k0, TK)] >= lim, 0.0, NEG_INF)
            return carry
        lax.fori_loop(0, nkt, body, 0)

    @pl.when(tied)
    def _():
        need = kf - count(lambda keys, k0: keys > thr)

        def idx(k0):
            return k0 + _lane((1, TK))

        def search_idx(it, lim):
            cand = lim + (jnp.int32(1) << (14 - it))
            c = count(lambda keys, k0: (keys == thr) & (idx(k0) < cand))
            return jnp.where(c <= need, cand, lim)

        lim = lax.fori_loop(0, 15, search_idx, jnp.zeros((TQ, 1), I32))

        def body(kt, carry):
            k0 = pl.multiple_of(kt * TK, TK)
            keys = key_ref[:, pl.ds(k0, TK)]
            sel = ((keys > thr) | ((keys == thr) & (idx(k0) < lim))) & (keys >= floor_key)
            bias_ref[:, pl.ds(k0, TK)] = jnp.where(sel, 0.0, NEG_INF)
            return carry
        lax.fori_loop(0, nkt, body, 0)

    outs = []
    for h in range(N_GH):
        qh = q_ref[:, h * LANES:(h + 1) * LANES]

        def att_tile(kt, carry, qh=qh):
            k0 = pl.multiple_of(kt * TK, TK)
            s = _dot_nt(qh, kv_ref[pl.ds(k0, TK), 0:LANES]) + bias_ref[:, pl.ds(k0, TK)]
            return _flash_tile(s, kv_ref[pl.ds(k0, TK), LANES:2 * LANES], *carry)

        init = (jnp.full((TQ, 1), NEG_INF, F32), jnp.zeros((TQ, 1), F32), jnp.zeros((TQ, LANES), F32))
        m, l, acc = lax.fori_loop(0, nkt, att_tile, init)
        o = acc / l
        ms = jnp.sum(o * o, axis=1, keepdims=True) * (1.0 / HEAD_DIM)
        outs.append(o * lax.rsqrt(ms + EPS) * g_ref[h:h + 1, :])
    for gi in range(2):
        pair = outs[2 * gi] + pltpu.roll(outs[2 * gi + 1], HEAD_DIM, 1)
        o_ref[:, gi * LANES:(gi + 1) * LANES] = pair.astype(BF16)


def _dsa_call(qa, wa, kva, g_heads, B, T):
    nq = T // TQ
    topk = min(TOPK_MAX, T // 4)
    return pl.pallas_call(
        functools.partial(_dsa_kernel, topk=topk),
        out_shape=jax.ShapeDtypeStruct((B * T, GROUP_W), BF16),
        grid=(B, nq),
        in_specs=[pl.BlockSpec((TQ, 1536), lambda b, i: (b * nq + i, 0)),
                  pl.BlockSpec((TQ, LANES), lambda b, i: (b * nq + i, 0)),
                  pl.BlockSpec((T, 256), lambda b, i: (b, 0)),
                  pl.BlockSpec((N_GH, LANES), lambda b, i: (0, 0))],
        out_specs=pl.BlockSpec((TQ, GROUP_W), lambda b, i: (b * nq + i, 0)),
        scratch_shapes=[pltpu.VMEM((TQ, T), F32), pltpu.VMEM((TQ, T), I32)],
        compiler_params=_cparams("arbitrary", "arbitrary"),
        name="dsa",
    )(qa, wa, kva, g_heads)


def _diff_kernel(q_ref, k_ref, v_ref, lam_ref, g_ref, o_ref, *, lam_init):
    i = pl.program_id(1)
    q0 = i * TQ
    nfull = (q0 + CHUNK) // TK
    qchunk = (q0 + lax.broadcasted_iota(I32, (TQ, 1), 0)) // CHUNK
    lv = lam_ref[...]
    lam = (jnp.exp(jnp.sum(lv[0:1] * lv[1:2], axis=1, keepdims=True))
           - jnp.exp(jnp.sum(lv[2:3] * lv[3:4], axis=1, keepdims=True)) + lam_init)
    lane = _lane((1, LANES))
    low = lane < HEAD_DIM
    heads = []
    for h in range(N_GH):
        g = h // 2
        parts = []
        for c in range(2):
            sub = (2 * h + c) % 4
            qg = q_ref[:, g * LANES:(g + 1) * LANES]
            qm = jnp.where((lane // DIFF_DK) == sub, qg, jnp.zeros_like(qg))

            def tile(kt, carry, masked, qm=qm, g=g):
                k0 = pl.multiple_of(kt * TK, TK)
                s = _dot_nt(qm, k_ref[pl.ds(k0, TK), g * LANES:(g + 1) * LANES])
                if masked:
                    kchunk = (k0 + _lane((1, TK))) // CHUNK
                    s = jnp.where(kchunk <= qchunk, s, NEG_INF)
                return _flash_tile(s, v_ref[pl.ds(k0, TK), g * LANES:(g + 1) * LANES], *carry)

            init = (jnp.full((TQ, 1), NEG_INF, F32), jnp.zeros((TQ, 1), F32),
                    jnp.zeros((TQ, LANES), F32))
            carry = lax.fori_loop(0, nfull, functools.partial(tile, masked=False), init)
            m, l, acc = tile(nfull, carry, True)
            parts.append(acc / l)
        heads.append(parts[0] - lam * parts[1])
    for gi in range(2):
        o = jnp.where(low, heads[2 * gi], heads[2 * gi + 1])
        y = _pair_head_norm(o, g_ref[:, gi * LANES:(gi + 1) * LANES], False) * (1.0 - lam_init)
        o_ref[:, gi * LANES:(gi + 1) * LANES] = y.astype(BF16)


def _diff_call(dqkv, lam_vecs, g_d, lam_init, B, T):
    nq = T // TQ
    return pl.pallas_call(
        functools.partial(_diff_kernel, lam_init=lam_init),
        out_shape=jax.ShapeDtypeStruct((B * T, GROUP_W), BF16),
        grid=(B, nq),
        in_specs=[pl.BlockSpec((TQ, 256), lambda b, i: (b * nq + i, 0)),
                  pl.BlockSpec((T, 256), lambda b, i: (b, 1)),
                  pl.BlockSpec((T, 256), lambda b, i: (b, 2)),
                  pl.BlockSpec((4, LANES), lambda b, i: (0, 0)),
                  pl.BlockSpec((1, GROUP_W), lambda b, i: (0, 0))],
        out_specs=pl.BlockSpec((TQ, GROUP_W), lambda b, i: (b * nq + i, 0)),
        compiler_params=_cparams("arbitrary", "arbitrary"),
        name="diff",
    )(dqkv, dqkv, dqkv, lam_vecs, g_d)


def _ret_tables():
    h = np.arange(N_GH, dtype=np.float32)
    log_g = jnp.log(1.0 - 2.0 ** (-5.0 - jnp.asarray(h)))
    t = np.arange(TB)
    same = (t[:, None] // CHUNK) == (t[None, :] // CHUNK)
    earlier = (t[None, :] // CHUNK) < (t[:, None] // CHUNK)
    dist = jnp.asarray(np.where(same, np.abs(t[:, None] - t[None, :]), t[:, None] - t[None, :]), F32)
    dmat = jnp.where(jnp.asarray(same | earlier)[None],
                     jnp.exp(dist[None] * log_g[:, None, None]), 0.0)
    tt = jnp.asarray(t, F32)
    xi = jnp.exp((tt + 1.0)[None, :] * log_g[:, None])
    zeta = jnp.exp((TB - 1.0 - tt)[None, :] * log_g[:, None])
    gblk = jnp.exp(TB * log_g)

    def lanes(tab):
        return jnp.repeat(tab.reshape(2, 2, TB), HEAD_DIM, axis=1).transpose(0, 2, 1)

    r = np.arange(LANES)
    blockdiag = (r[:, None] // HEAD_DIM) == (r[None, :] // HEAD_DIM)
    gb = jnp.where(jnp.asarray(blockdiag)[None],
                   jnp.repeat(gblk.reshape(2, 2), HEAD_DIM, axis=1)[:, :, None], 0.0)
    return dmat, lanes(xi), lanes(zeta), gb


def _ret_kernel(qkv_ref, gate_ref, dmat_ref, xi_ref, zeta_ref, gb_ref, g_ref, o_ref, r_ref):
    @pl.when(pl.program_id(1) == 0)
    def _():
        r_ref[...] = jnp.zeros_like(r_ref)

    lane = _lane((1, LANES))
    low = lane < HEAD_DIM
    r_idx = lax.broadcasted_iota(I32, (LANES, LANES), 0) // HEAD_DIM
    c_idx = lax.broadcasted_iota(I32, (LANES, LANES), 1) // HEAD_DIM
    for g in range(2):
        qg = qkv_ref[:, g * LANES:(g + 1) * LANES]
        kg = qkv_ref[:, 256 + g * LANES:256 + (g + 1) * LANES]
        vg = qkv_ref[:, 512 + g * LANES:512 + (g + 1) * LANES]
        state = r_ref[g]
        inter = _dot(qg, state.astype(BF16)) * xi_ref[g]
        parts = []
        for hh in range(2):
            qm = jnp.where(low, qg, jnp.zeros_like(qg)) if hh == 0 else jnp.where(low, jnp.zeros_like(qg), qg)
            s = _dot_nt(qm, kg) * dmat_ref[2 * g + hh]
            parts.append(_dot(s.astype(BF16), vg))
        o = jnp.where(low, parts[0], parts[1]) + inter
        kz = (kg.astype(F32) * zeta_ref[g]).astype(BF16)
        upd = jnp.where(r_idx == c_idx, _dot_tn(kz, vg), 0.0)
        r_ref[g] = state * gb_ref[g] + upd
        y = _pair_head_norm(o, g_ref[:, g * LANES:(g + 1) * LANES], True)
        o_ref[:, g * LANES:(g + 1) * LANES] = (_silu(gate_ref[:, g * LANES:(g + 1) * LANES]) * y).astype(BF16)


def _ret_call(qkvb, gates, tabs, g_b, B, T):
    nb = T // TB
    dmat, xi, zeta, gb = tabs
    const3 = lambda b, i: (0, 0, 0)
    return pl.pallas_call(
        _ret_kernel,
        out_shape=jax.ShapeDtypeStruct((B * T, GROUP_W), BF16),
        grid=(B, nb),
        in_specs=[pl.BlockSpec((TB, 768), lambda b, i: (b * nb + i, 0)),
                  pl.BlockSpec((TB, GROUP_W), lambda b, i: (b * nb + i, 0)),
                  pl.BlockSpec((N_GH, TB, TB), const3),
                  pl.BlockSpec((2, TB, LANES), const3),
                  pl.BlockSpec((2, TB, LANES), const3),
                  pl.BlockSpec((2, LANES, LANES), const3),
                  pl.BlockSpec((1, GROUP_W), lambda b, i: (0, 0))],
        out_specs=pl.BlockSpec((TB, GROUP_W), lambda b, i: (b * nb + i, 0)),
        scratch_shapes=[pltpu.VMEM((2, LANES, LANES), F32)],
        compiler_params=_cparams("arbitrary", "arbitrary"),
        name="ret",
    )(qkvb, gates, dmat, xi, zeta, gb, g_b)


def _gla_kernel(c_ref, gate_ref, wa2_ref, ba_ref, g_ref, o_ref, s_ref, b_scr, k_scr, v_scr):
    @pl.when(pl.program_id(1) == 0)
    def _():
        s_ref[...] = jnp.zeros_like(s_ref)

    hmask = (lax.broadcasted_iota(I32, (LANES, GROUP_W), 0) // GLA_DK
             == lax.broadcasted_iota(I32, (LANES, GROUP_W), 1) // HEAD_DIM)
    hsum = jnp.where(hmask, 1.0, 0.0).astype(BF16)
    tri = jnp.where(lax.broadcasted_iota(I32, (CHUNK, CHUNK), 0)
                    >= lax.broadcasted_iota(I32, (CHUNK, CHUNK), 1), 1.0, 0.0)

    z = _dot_f32(c_ref[:, 512:640], wa2_ref[...]) + ba_ref[...]
    log_a = (jnp.minimum(z, 0.0) - jnp.log(1.0 + jnp.exp(-jnp.abs(z)))) * (1.0 / GLA_TAU)
    outs = []
    for c in range(TB // CHUNK):
        rows = slice(c * CHUNK, (c + 1) * CHUNK)
        b = _dot_f32(tri, log_a[rows])
        q = c_ref[rows, 0:128] * (GLA_DK ** -0.5)
        k = c_ref[rows, 128:256]
        v = c_ref[rows, 256:512]
        state = s_ref[...]
        inter = _dot((q * jnp.exp(b)).astype(BF16), state.astype(BF16))
        b_scr[...] = b
        k_scr[...] = k
        v_scr[...] = v

        def key_step(s, acc, b=b, q=q):
            decay = jnp.exp(-jnp.abs(b - b_scr[pl.ds(s, 1), :]))
            w = (q * decay * k_scr[pl.ds(s, 1), :]).astype(BF16)
            return acc + _dot(w, hsum) * v_scr[pl.ds(s, 1), :]

        intra = lax.fori_loop(0, CHUNK, key_step, jnp.zeros((CHUNK, GROUP_W), F32), unroll=8)
        outs.append(intra + inter)
        b_last = b[CHUNK - 1:CHUNK, :]
        kd = (k * jnp.exp(b_last - b)).astype(BF16)
        upd = _dot_tn(kd, v.astype(BF16))
        scale = jnp.transpose(jnp.broadcast_to(jnp.exp(b_last), (LANES, LANES)))
        s_ref[...] = jnp.where(hmask, state * jnp.concatenate([scale, scale], axis=1) + upd, 0.0)
    o = jnp.concatenate(outs, axis=0)
    for gi in range(2):
        y = _pair_head_norm(o[:, gi * LANES:(gi + 1) * LANES], g_ref[:, gi * LANES:(gi + 1) * LANES], True)
        o_ref[:, gi * LANES:(gi + 1) * LANES] = (_silu(gate_ref[:, gi * LANES:(gi + 1) * LANES]) * y).astype(BF16)


def _gla_call(cc, gates, wa2_p, ba, g_c, B, T):
    nb = T // TB
    return pl.pallas_call(
        _gla_kernel,
        out_shape=jax.ShapeDtypeStruct((B * T, GROUP_W), BF16),
        grid=(B, nb),
        in_specs=[pl.BlockSpec((TB, 640), lambda b, i: (b * nb + i, 0)),
                  pl.BlockSpec((TB, GROUP_W), lambda b, i: (b * nb + i, 1)),
                  pl.BlockSpec((LANES, LANES), lambda b, i: (0, 0)),
                  pl.BlockSpec((1, LANES), lambda b, i: (0, 0)),
                  pl.BlockSpec((1, GROUP_W), lambda b, i: (0, 0))],
        out_specs=pl.BlockSpec((TB, GROUP_W), lambda b, i: (b * nb + i, 0)),
        scratch_shapes=[pltpu.VMEM((LANES, GROUP_W), F32), pltpu.VMEM((CHUNK, LANES), F32),
                        pltpu.VMEM((CHUNK, LANES), F32), pltpu.VMEM((CHUNK, GROUP_W), F32)],
        compiler_params=_cparams("arbitrary", "arbitrary"),
        name="gla",
    )(cc, gates, wa2_p, ba, g_c)


def _rms(y, g):
    return y * lax.rsqrt(jnp.mean(y * y, axis=-1, keepdims=True) + EPS) * g


def _out_mlp_kernel(x_ref, ya_ref, yb_ref, yc_ref, yd_ref, mod_ref, wo_ref, w1_ref, w2_ref,
                    gpost_ref, gpre_ref, gpost2_ref, o_ref):
    y = _dot(ya_ref[...], wo_ref[0:256, :])
    y = y + _dot(yb_ref[...], wo_ref[256:512, :])
    y = y + _dot(yc_ref[...], wo_ref[512:768, :])
    y = y + _dot(yd_ref[...], wo_ref[768:1024, :])
    x = x_ref[...] + mod_ref[0, 2:3, :] * _rms(y, gpost_ref[...])
    h = _rms(x, gpre_ref[...]) * (1.0 + mod_ref[0, 4:5, :]) + mod_ref[0, 3:4, :]
    hb = h.astype(BF16)
    acc = jnp.zeros(x.shape, F32)
    fc = 1024
    for f in range(D_FF // fc):
        u = jnp.maximum(_dot(hb, w1_ref[:, f * fc:(f + 1) * fc]), 0.0)
        acc = acc + _dot((u * u).astype(BF16), w2_ref[f * fc:(f + 1) * fc, :])
    o_ref[...] = x + mod_ref[0, 5:6, :] * _rms(acc, gpost2_ref[...])


def _out_mlp_call(x2, ys, mod3, w_out, w1, w2, g_post, g_pre2, g_post2, T):
    N, D = x2.shape
    nt = T // TM
    row = lambda i: (i, 0)
    const = lambda i: (0, 0)
    return pl.pallas_call(
        _out_mlp_kernel,
        out_shape=jax.ShapeDtypeStruct((N, D), F32),
        grid=(N // TM,),
        in_specs=[pl.BlockSpec((TM, D), row)] + [pl.BlockSpec((TM, GROUP_W), row)] * 4
                 + [pl.BlockSpec((1, 6, D), lambda i: (i // nt, 0, 0)),
                    _resident((D, D), const), _resident((D, D_FF), const), _resident((D_FF, D), const),
                    _resident((1, D), const), _resident((1, D), const), _resident((1, D), const)],
        out_specs=pl.BlockSpec((TM, D), row),
        compiler_params=_cparams("arbitrary"),
        name="out_mlp",
    )(x2, *ys, mod3, w_out, w1, w2, g_post, g_pre2, g_post2)


def _relayout_w_in(w_in):
    col = lambda k: w_in[:, _IN_OFF[k]:_IN_OFF[k + 1]]
    zeros = lambda n: jnp.zeros((w_in.shape[0], n), w_in.dtype)
    parts = [col(_A_Q), col(_A_QI), col(_A_K), col(_A_KI), col(_A_V), zeros(64),
             col(_A_WI), zeros(120),
             col(_B_Q), col(_B_K), col(_B_V),
             col(_B_G), col(_C_G),
             col(_C_Q), col(_C_K), col(_C_V), col(_C_A), zeros(112),
             col(_D_Q), col(_D_K), col(_D_V)]
    return jnp.concatenate(parts, axis=1).astype(BF16)


def _rope_tables(pos, rot, period, theta):
    half = rot // 2
    inv = theta ** (-jnp.arange(half, dtype=F32) / half)
    ang = pos.astype(F32)[..., None] * inv
    cos, sin = jnp.cos(ang), jnp.sin(ang)
    pad = ang.shape[:-1] + (period - rot,)
    ct = jnp.concatenate([cos, cos, jnp.ones(pad, F32)], axis=-1)
    st = jnp.concatenate([-sin, sin, jnp.zeros(pad, F32)], axis=-1)
    reps = LANES // period
    return (jnp.tile(ct, (1, 1, reps)).reshape(-1, LANES), jnp.tile(st, (1, 1, reps)).reshape(-1, LANES))


def _layer(x2, cond, tabs, ret_tabs, layer_idx, B, T, mod_w, mod_b, attn_pre_g, attn_post_g,
           mlp_pre_g, mlp_post_g, w_in, gla_wa2, gla_ba, lam_q1, lam_k1, lam_q2, lam_k2,
           head_norm_g, w_out, mlp_w1, mlp_w2):
    D = D_MODEL
    mod3 = _mod_call(cond, mod_w, mod_b).reshape(B, 6, D)
    qa, kva, wa, qkvb, gates, cc, dqkv = _in_call(
        x2, mod3, attn_pre_g.reshape(1, D), _relayout_w_in(w_in), tabs, T)

    g_a, g_b, g_c, g_d = jnp.split(head_norm_g, 4)
    g_heads = jnp.concatenate([g_a.reshape(N_GH, HEAD_DIM), jnp.zeros((N_GH, HEAD_DIM), F32)], axis=1)
    y_a = _dsa_call(qa, wa, kva, g_heads, B, T)
    y_b = _ret_call(qkvb, gates, ret_tabs, g_b.reshape(1, GROUP_W), B, T)
    wa2_p = jnp.zeros((LANES, LANES), F32).at[:GLA_RANK].set(gla_wa2)
    y_c = _gla_call(cc, gates, wa2_p, gla_ba.reshape(1, LANES), g_c.reshape(1, GROUP_W), B, T)
    lam_init = 0.8 - 0.6 * math.exp(-0.3 * layer_idx)
    lam_vecs = jnp.zeros((4, LANES), F32).at[:, :DIFF_DK].set(jnp.stack([lam_q1, lam_k1, lam_q2, lam_k2]))
    y_d = _diff_call(dqkv, lam_vecs, g_d.reshape(1, GROUP_W), lam_init, B, T)

    return _out_mlp_call(x2, (y_a, y_b, y_c, y_d), mod3, w_out.astype(BF16), mlp_w1.astype(BF16),
                         mlp_w2.astype(BF16), attn_post_g.reshape(1, D), mlp_pre_g.reshape(1, D),
                         mlp_post_g.reshape(1, D), T)


def kernel(x, c, positions, mod_w, mod_b, attn_pre_g, attn_post_g, mlp_pre_g, mlp_post_g, w_in,
           gla_wa2, gla_ba, lam_q1, lam_k1, lam_q2, lam_k2, head_norm_g, w_out, mlp_w1, mlp_w2):
    B, T, D = x.shape
    assert D == D_MODEL and T % TM == 0 and T % TB == 0
    tabs = (_rope_tables(positions, HEAD_DIM // ROPE_FRAC, HEAD_DIM, ROPE_THETA)
            + _rope_tables(positions, HEAD_DIM, HEAD_DIM, RET_THETA)
            + _rope_tables(positions, DIFF_DK // ROPE_FRAC, DIFF_DK, ROPE_THETA))
    ret_tabs = _ret_tables()
    x2 = x.reshape(B * T, D)
    for l in range(mod_w.shape[0]):
        x2 = _layer(x2, c, tabs, ret_tabs, l, B, T, mod_w[l], mod_b[l], attn_pre_g[l], attn_post_g[l],
                    mlp_pre_g[l], mlp_post_g[l], w_in[l], gla_wa2[l], gla_ba[l], lam_q1[l], lam_k1[l],
                    lam_q2[l], lam_k2[l], head_norm_g[l], w_out[l], mlp_w1[l], mlp_w2[l])
    return x2.reshape(B, T, D)
```

```python
import functools
import math

import numpy as np
import jax
import jax.numpy as jnp
from jax import lax
from jax.experimental import pallas as pl
from jax.experimental.pallas import tpu as pltpu

F32 = jnp.float32
BF16 = jnp.bfloat16
I32 = jnp.int32

D_MODEL = 1024
CHUNK = 64
HEAD_DIM = 64
N_GH = 4
GROUP_W = N_GH * HEAD_DIM
IDX_HEADS = 8
IDX_DIM = 64
TOPK_MAX = 256
RET_THETA = 10000.0
GLA_DK = HEAD_DIM // 2
GLA_RANK = 16
GLA_TAU = 16.0
DIFF_DK = HEAD_DIM // 2
ROPE_THETA = 500000.0
ROPE_FRAC = 4
D_FF = 4 * D_MODEL
EPS = 1e-6
NEG_INF = -1e30

LANES = 128
VMEM_LIMIT = 56 * 1024 * 1024

_IN_SPLITS = (
    GROUP_W, HEAD_DIM, HEAD_DIM, IDX_HEADS * IDX_DIM, IDX_DIM, IDX_HEADS,
    GROUP_W, GROUP_W, GROUP_W, GROUP_W,
    N_GH * GLA_DK, N_GH * GLA_DK, GROUP_W, GLA_RANK, GROUP_W,
    2 * N_GH * DIFF_DK, 2 * N_GH * DIFF_DK, GROUP_W,
)
_IN_OFF = np.concatenate([[0], np.cumsum(_IN_SPLITS)]).tolist()
(_A_Q, _A_K, _A_V, _A_QI, _A_KI, _A_WI, _B_Q, _B_K, _B_V, _B_G,
 _C_Q, _C_K, _C_V, _C_A, _C_G, _D_Q, _D_K, _D_V) = range(18)

SEG_A = 0
SEG_W = 1024
SEG_B = 1152
SEG_G = 1920
SEG_C = 2432
SEG_D = 3072
W_COLS = 3840

NT_DIMS = (((1,), (1,)), ((), ()))
TN_DIMS = (((0,), (0,)), ((), ()))

TM = 512
TQ = 256
TK = 256
TK_SCORE = 128
TB = 256


def _float_key(v):
    bits = int(np.array(v, np.float32).view(np.int32))
    return -(bits & 0x7FFFFFFF) if bits < 0 else bits


NEG_KEY = _float_key(NEG_INF)
INT_MIN = -(2 ** 31)


def _cparams(*sem):
    return pltpu.CompilerParams(dimension_semantics=sem, vmem_limit_bytes=VMEM_LIMIT)


def _resident(shape, index_map):
    return pl.BlockSpec(shape, index_map, pipeline_mode=pl.Buffered(1))


def _dot(a, b):
    return jnp.dot(a, b, preferred_element_type=F32)


def _dot_nt(a, b):
    return lax.dot_general(a, b, NT_DIMS, preferred_element_type=F32)


def _dot_tn(a, b):
    return lax.dot_general(a, b, TN_DIMS, preferred_element_type=F32)


def _dot_f32(a, b):
    return jnp.dot(a, b, preferred_element_type=F32, precision=lax.Precision.HIGHEST)


def _lane(shape):
    return lax.broadcasted_iota(I32, shape, len(shape) - 1)


def _silu(x):
    return x / (1.0 + jnp.exp(-x))


def _mod_kernel(c_ref, w_ref, b_ref, o_ref):
    cond = _silu(c_ref[...])
    o_ref[...] = _dot_f32(cond, w_ref[...]) + b_ref[...]


def _mod_call(c, mod_w, mod_b):
    B, D = c.shape
    n = mod_w.shape[1] // D
    return pl.pallas_call(
        _mod_kernel,
        out_shape=jax.ShapeDtypeStruct((B, n * D), F32),
        grid=(n,),
        in_specs=[pl.BlockSpec((B, D), lambda j: (0, 0)),
                  pl.BlockSpec((D, D), lambda j: (0, j)),
                  pl.BlockSpec((1, D), lambda j: (0, j))],
        out_specs=pl.BlockSpec((B, D), lambda j: (0, j)),
        compiler_params=_cparams("arbitrary"),
        name="mod",
    )(c, mod_w, mod_b.reshape(1, -1))


def _rope_group(xg, cos, sin, half, period):
    lo = (_lane((1, LANES)) % period) < half
    swapped = jnp.where(lo, pltpu.roll(xg, LANES - half, 1), pltpu.roll(xg, half, 1))
    return xg * cos + swapped * sin


def _in_kernel(x_ref, mod_ref, g_ref, w_ref, ca_ref, sa_ref, cb_ref, sb_ref, cd_ref, sd_ref,
               qa_ref, kva_ref, wa_ref, qkvb_ref, gate_ref, c_ref, d_ref):
    x = x_ref[...]
    h = x * lax.rsqrt(jnp.mean(x * x, axis=-1, keepdims=True) + EPS) * g_ref[...]
    h = h * (1.0 + mod_ref[0, 1:2, :]) + mod_ref[0, 0:1, :]
    hb = h.astype(BF16)

    def seg(off, width):
        return _dot(hb, w_ref[:, off:off + width])

    lane = _lane((1, LANES))
    low = lane < HEAD_DIM

    ca, sa = ca_ref[...], sa_ref[...]
    a = seg(SEG_A, 1024)
    half_a = HEAD_DIM // ROPE_FRAC // 2
    for gi in range(2):
        r = _rope_group(a[:, gi * LANES:(gi + 1) * LANES], ca, sa, half_a, HEAD_DIM) * (HEAD_DIM ** -0.5)
        rs = pltpu.roll(r, HEAD_DIM, 1)
        qa_ref[:, (2 * gi) * LANES:(2 * gi + 1) * LANES] = jnp.where(low, r, 0.0).astype(BF16)
        qa_ref[:, (2 * gi + 1) * LANES:(2 * gi + 2) * LANES] = jnp.where(low, rs, 0.0).astype(BF16)
    for gi in range(4):
        r = _rope_group(a[:, (2 + gi) * LANES:(3 + gi) * LANES], ca, sa, half_a, HEAD_DIM)
        rs = pltpu.roll(r, HEAD_DIM, 1)
        qa_ref[:, (4 + 2 * gi) * LANES:(5 + 2 * gi) * LANES] = jnp.where(low, 0.0, rs).astype(BF16)
        qa_ref[:, (5 + 2 * gi) * LANES:(6 + 2 * gi) * LANES] = jnp.where(low, 0.0, r).astype(BF16)
    kva_ref[:, 0:LANES] = _rope_group(a[:, 6 * LANES:7 * LANES], ca, sa, half_a, HEAD_DIM).astype(BF16)
    kva_ref[:, LANES:2 * LANES] = a[:, 7 * LANES:8 * LANES].astype(BF16)

    wa_ref[...] = seg(SEG_W, LANES)

    cb, sb = cb_ref[...], sb_ref[...]
    b = seg(SEG_B, 768)
    for gi in range(4):
        r = _rope_group(b[:, gi * LANES:(gi + 1) * LANES], cb, sb, HEAD_DIM // 2, HEAD_DIM)
        if gi < 2:
            r = r * (HEAD_DIM ** -0.5)
        qkvb_ref[:, gi * LANES:(gi + 1) * LANES] = r.astype(BF16)
    qkvb_ref[:, 512:768] = b[:, 512:768].astype(BF16)

    gate_ref[...] = seg(SEG_G, 512)
    c_ref[...] = seg(SEG_C, 640)

    cd, sd = cd_ref[...], sd_ref[...]
    d = seg(SEG_D, 768)
    half_d = DIFF_DK // ROPE_FRAC // 2
    for gi in range(4):
        r = _rope_group(d[:, gi * LANES:(gi + 1) * LANES], cd, sd, half_d, DIFF_DK)
        if gi < 2:
            r = r * (DIFF_DK ** -0.5)
        d_ref[:, gi * LANES:(gi + 1) * LANES] = r.astype(BF16)
    d_ref[:, 512:768] = d[:, 512:768].astype(BF16)


def _in_call(x2, mod3, g, w_p, tabs, T):
    N, D = x2.shape
    nt = T // TM
    row = lambda i: (i, 0)
    tab_spec = pl.BlockSpec((TM, LANES), row)
    outs = [(1536, BF16), (256, BF16), (LANES, F32), (768, BF16), (512, F32), (640, F32), (768, BF16)]
    return pl.pallas_call(
        _in_kernel,
        out_shape=[jax.ShapeDtypeStruct((N, w), dt) for w, dt in outs],
        grid=(N // TM,),
        in_specs=[pl.BlockSpec((TM, D), row),
                  pl.BlockSpec((1, 6, D), lambda i: (i // nt, 0, 0)),
                  _resident((1, D), lambda i: (0, 0)),
                  _resident((D, W_COLS), lambda i: (0, 0))] + [tab_spec] * 6,
        out_specs=[pl.BlockSpec((TM, w), row) for w, _ in outs],
        compiler_params=_cparams("arbitrary"),
        name="in_proj",
    )(x2, mod3, g, w_p, *tabs)


def _pair_head_norm(o, gain, center):
    low = _lane((1, LANES)) < HEAD_DIM
    inv = 1.0 / HEAD_DIM

    def seg_mean(v):
        m0 = jnp.sum(jnp.where(low, v, 0.0), axis=1, keepdims=True)
        m1 = jnp.sum(jnp.where(low, 0.0, v), axis=1, keepdims=True)
        return jnp.where(low, m0, m1) * inv

    if center:
        o = o - seg_mean(o)
    return o * lax.rsqrt(seg_mean(o * o) + EPS) * gain


class _Flash:
    def __init__(self, n, s_ref, p_ref, acc_ref, st_ref):
        self.n, self.s, self.p, self.acc, self.st = n, s_ref, p_ref, acc_ref, st_ref

    @staticmethod
    def scratch(n):
        return [pltpu.VMEM((n, TK, TQ), F32), pltpu.VMEM((n, TK, TQ), BF16),
                pltpu.VMEM((n, HEAD_DIM, TQ), F32), pltpu.VMEM((4, n, TQ), F32)]

    def _row(self, k, i):
        return self.st.at[k, i:i + 1, :]

    def init(self):
        self.st[0] = jnp.full((self.n, TQ), NEG_INF, F32)
        self.st[1] = jnp.zeros((self.n, TQ), F32)
        self.acc[...] = jnp.zeros_like(self.acc)

    def scores(self, i, s):
        self.s[i] = s
        self._row(2, i)[...] = jnp.max(s, axis=0, keepdims=True)

    def probs(self, i):
        m_old = self._row(0, i)[...]
        m_new = jnp.maximum(m_old, self._row(2, i)[...])
        alpha = jnp.exp(m_old - m_new)
        p = jnp.exp(self.s[i] - m_new)
        self._row(1, i)[...] = alpha * self._row(1, i)[...] + jnp.sum(p, axis=0, keepdims=True)
        self._row(0, i)[...] = m_new
        self._row(3, i)[...] = alpha
        self.p[i] = p.astype(BF16)

    def values(self, i, vt):
        self.acc[i] = self._row(3, i)[...] * self.acc[i] + _dot(vt, self.p[i])

    def result(self, i):
        return self.acc[i] / self._row(1, i)[...]


def _fold8(x):
    parts = [x[r:r + 8] for r in range(0, x.shape[0], 8)]
    while len(parts) > 1:
        parts = [a + b for a, b in zip(parts[0::2], parts[1::2])]
    return parts[0]


def _head_norm_t(o, gain):
    ms = jnp.mean(o * o, axis=0, keepdims=True)
    return o * lax.rsqrt(ms + EPS) * gain


def _dsa_kernel(q_ref, w_ref, kv_ref, vt_ref, g_ref, o_ref, bias_ref, key_ref, *flash_refs, topk):
    q0 = pl.program_id(1) * TQ
    nkt = (q0 + TQ) // TK
    qchunk = (q0 + _lane((1, TQ))) // CHUNK
    wts = jnp.transpose(w_ref[...]) * (IDX_HEADS ** -0.5 * IDX_DIM ** -0.5)

    def score_tile(kt, carry):
        k0 = pl.multiple_of(kt * TK_SCORE, TK_SCORE)
        kk = kv_ref[pl.ds(k0, TK_SCORE), 0:LANES]
        acc = jnp.zeros((TK_SCORE, TQ), F32)
        for h in range(IDX_HEADS):
            logits = _dot_nt(kk, q_ref[:, (4 + h) * LANES:(5 + h) * LANES])
            acc = acc + jnp.maximum(logits, 0.0) * wts[h:h + 1, :]
        kchunk = (k0 + lax.broadcasted_iota(I32, (TK_SCORE, 1), 0)) // CHUNK
        acc = jnp.where(kchunk <= qchunk, acc, NEG_INF)
        bits = pltpu.bitcast(acc, I32)
        sign = bits >> 31
        key_ref[pl.ds(k0, TK_SCORE), :] = ((bits & 0x7FFFFFFF) ^ sign) - sign
        return carry

    lax.fori_loop(0, nkt * (TK // TK_SCORE), score_tile, 0)

    def count(pred):
        def body(kt, acc):
            k0 = pl.multiple_of(kt * TK, TK)
            hit = jnp.where(pred(key_ref[pl.ds(k0, TK), :], k0), 1.0, 0.0)
            return acc + _fold8(hit)
        acc = lax.fori_loop(0, nkt, body, jnp.zeros((8, TQ), F32))
        return jnp.sum(acc, axis=0, keepdims=True)

    kf = float(topk)
    c0 = count(lambda keys, k0: keys >= 0)
    thr = jnp.where(c0 >= kf, 0, INT_MIN).astype(I32)
    cnt = jnp.where(c0 >= kf, c0, (nkt * TK).astype(F32))

    def search(it, carry):
        thr, cnt = carry
        cand = thr + (jnp.int32(1) << (30 - it))
        c = count(lambda keys, k0: keys >= cand)
        ok = c >= kf
        return jnp.where(ok, cand, thr), jnp.where(ok, c, cnt)

    thr, cnt = lax.fori_loop(0, 31, search, (thr, cnt))
    floor_key = jnp.int32(NEG_KEY + 1)
    tied = jnp.max(jnp.where((cnt > kf) & (thr >= floor_key), 1.0, 0.0)) > 0.0

    @pl.when(jnp.logical_not(tied))
    def _():
        lim = jnp.maximum(thr, floor_key)

        def body(kt, carry):
            k0 = pl.multiple_of(kt * TK, TK)
            bias_ref[pl.ds(k0, TK), :] = jnp.where(key_ref[pl.ds(k0, TK), :] >= lim, 0.0, NEG_INF)
            return carry
        lax.fori_loop(0, nkt, body, 0)

    @pl.when(tied)
    def _():
        need = kf - count(lambda keys, k0: keys > thr)

        def idx(k0):
            return k0 + lax.broadcasted_iota(I32, (TK, 1), 0)

        def search_idx(it, lim):
            cand = lim + (jnp.int32(1) << (14 - it))
            c = count(lambda keys, k0: (keys == thr) & (idx(k0) < cand))
            return jnp.where(c <= need, cand, lim)

        lim = lax.fori_loop(0, 15, search_idx, jnp.zeros((1, TQ), I32))

        def body(kt, carry):
            k0 = pl.multiple_of(kt * TK, TK)
            keys = key_ref[pl.ds(k0, TK), :]
            sel = ((keys > thr) | ((keys == thr) & (idx(k0) < lim))) & (keys >= floor_key)
            bias_ref[pl.ds(k0, TK), :] = jnp.where(sel, 0.0, NEG_INF)
            return carry
        lax.fori_loop(0, nkt, body, 0)

    flash = _Flash(N_GH, *flash_refs)
    flash.init()

    def att_tile(kt, carry):
        k0 = pl.multiple_of(kt * TK, TK)
        kk = kv_ref[pl.ds(k0, TK), 0:LANES]
        bias = bias_ref[pl.ds(k0, TK), :]
        for h in range(N_GH):
            flash.scores(h, _dot_nt(kk, q_ref[:, h * LANES:(h + 1) * LANES]) + bias)
        for h in range(N_GH):
            flash.probs(h)
        for h in range(N_GH):
            flash.values(h, vt_ref[:, pl.ds(k0, TK)])
        return carry

    lax.fori_loop(0, nkt, att_tile, 0)
    y = jnp.concatenate([_head_norm_t(flash.result(h), g_ref[h * HEAD_DIM:(h + 1) * HEAD_DIM, :])
                         for h in range(N_GH)], axis=0)
    o_ref[...] = jnp.transpose(y).astype(BF16)


def _dsa_call(qa, wa, kva, vta, g_t, B, T):
    nq = T // TQ
    topk = min(TOPK_MAX, T // 4)
    return pl.pallas_call(
        functools.partial(_dsa_kernel, topk=topk),
        out_shape=jax.ShapeDtypeStruct((B * T, GROUP_W), BF16),
        grid=(B, nq),
        in_specs=[pl.BlockSpec((TQ, 1536), lambda b, i: (b * nq + i, 0)),
                  pl.BlockSpec((TQ, LANES), lambda b, i: (b * nq + i, 0)),
                  pl.BlockSpec((T, 256), lambda b, i: (b, 0)),
                  pl.BlockSpec((HEAD_DIM, T), lambda b, i: (b, 0)),
                  pl.BlockSpec((GROUP_W, TQ), lambda b, i: (0, 0))],
        out_specs=pl.BlockSpec((TQ, GROUP_W), lambda b, i: (b * nq + i, 0)),
        scratch_shapes=[pltpu.VMEM((T, TQ), F32), pltpu.VMEM((T, TQ), I32)] + _Flash.scratch(N_GH),
        compiler_params=_cparams("arbitrary", "arbitrary"),
        name="dsa",
    )(qa, wa, kva, vta, g_t)


def _diff_kernel(q_ref, k_ref, vt_ref, lam_ref, g_ref, o_ref, qm_ref, *flash_refs, lam_init):
    q0 = pl.program_id(1) * TQ
    nfull = (q0 + CHUNK) // TK
    qchunk = (q0 + _lane((1, TQ))) // CHUNK
    lv = lam_ref[...]
    lam = (jnp.exp(jnp.sum(lv[0:1] * lv[1:2], axis=1, keepdims=True))
           - jnp.exp(jnp.sum(lv[2:3] * lv[3:4], axis=1, keepdims=True)) + lam_init)
    lane = _lane((1, LANES))
    n_maps = 2 * N_GH
    for mi in range(n_maps):
        qg = q_ref[:, (mi // 4) * LANES:(mi // 4 + 1) * LANES]
        qm_ref[mi] = jnp.where((lane // DIFF_DK) == mi % 4, qg, jnp.zeros_like(qg))

    flash = _Flash(n_maps, *flash_refs)
    flash.init()

    def tile(kt, carry, masked):
        k0 = pl.multiple_of(kt * TK, TK)
        if masked:
            kchunk = (k0 + lax.broadcasted_iota(I32, (TK, 1), 0)) // CHUNK
            admissible = kchunk <= qchunk
        for mi in range(n_maps):
            g = mi // 4
            s = _dot_nt(k_ref[pl.ds(k0, TK), g * LANES:(g + 1) * LANES], qm_ref[mi])
            flash.scores(mi, jnp.where(admissible, s, NEG_INF) if masked else s)
        for mi in range(n_maps):
            flash.probs(mi)
        for mi in range(n_maps):
            h = mi // 2
            flash.values(mi, vt_ref[h * HEAD_DIM:(h + 1) * HEAD_DIM, pl.ds(k0, TK)])
        return carry

    lax.fori_loop(0, nfull, functools.partial(tile, masked=False), 0)
    tile(nfull, 0, True)
    outs = []
    for h in range(N_GH):
        o = flash.result(2 * h) - lam * flash.result(2 * h + 1)
        outs.append(_head_norm_t(o, g_ref[h * HEAD_DIM:(h + 1) * HEAD_DIM, :]) * (1.0 - lam_init))
    o_ref[...] = jnp.transpose(jnp.concatenate(outs, axis=0)).astype(BF16)


def _diff_call(dqkv, vtd, lam_vecs, g_t, lam_init, B, T):
    nq = T // TQ
    return pl.pallas_call(
        functools.partial(_diff_kernel, lam_init=lam_init),
        out_shape=jax.ShapeDtypeStruct((B * T, GROUP_W), BF16),
        grid=(B, nq),
        in_specs=[pl.BlockSpec((TQ, 256), lambda b, i: (b * nq + i, 0)),
                  pl.BlockSpec((T, 256), lambda b, i: (b, 1)),
                  pl.BlockSpec((GROUP_W, T), lambda b, i: (b, 0)),
                  pl.BlockSpec((4, LANES), lambda b, i: (0, 0)),
                  pl.BlockSpec((GROUP_W, TQ), lambda b, i: (0, 0))],
        out_specs=pl.BlockSpec((TQ, GROUP_W), lambda b, i: (b * nq + i, 0)),
        scratch_shapes=[pltpu.VMEM((2 * N_GH, TQ, LANES), BF16)] + _Flash.scratch(2 * N_GH),
        compiler_params=_cparams("arbitrary", "arbitrary"),
        name="diff",
    )(dqkv, dqkv, vtd, lam_vecs, g_t)


def _ret_tables():
    h = np.arange(N_GH, dtype=np.float32)
    log_g = jnp.log(1.0 - 2.0 ** (-5.0 - jnp.asarray(h)))
    t = np.arange(TB)
    same = (t[:, None] // CHUNK) == (t[None, :] // CHUNK)
    earlier = (t[None, :] // CHUNK) < (t[:, None] // CHUNK)
    dist = jnp.asarray(np.where(same, np.abs(t[:, None] - t[None, :]), t[:, None] - t[None, :]), F32)
    dmat = jnp.where(jnp.asarray(same | earlier)[None],
                     jnp.exp(dist[None] * log_g[:, None, None]), 0.0)
    tt = jnp.asarray(t, F32)
    xi = jnp.exp((tt + 1.0)[None, :] * log_g[:, None])
    zeta = jnp.exp((TB - 1.0 - tt)[None, :] * log_g[:, None])
    gblk = jnp.exp(TB * log_g)

    def lanes(tab):
        return jnp.repeat(tab.reshape(2, 2, TB), HEAD_DIM, axis=1).transpose(0, 2, 1)

    r = np.arange(LANES)
    blockdiag = (r[:, None] // HEAD_DIM) == (r[None, :] // HEAD_DIM)
    gb = jnp.where(jnp.asarray(blockdiag)[None],
                   jnp.repeat(gblk.reshape(2, 2), HEAD_DIM, axis=1)[:, :, None], 0.0)
    return dmat, lanes(xi), lanes(zeta), gb


def _ret_kernel(qkv_ref, gate_ref, dmat_ref, xi_ref, zeta_ref, gb_ref, g_ref, o_ref, r_ref):
    @pl.when(pl.program_id(1) == 0)
    def _():
        r_ref[...] = jnp.zeros_like(r_ref)

    lane = _lane((1, LANES))
    low = lane < HEAD_DIM
    r_idx = lax.broadcasted_iota(I32, (LANES, LANES), 0) // HEAD_DIM
    c_idx = lax.broadcasted_iota(I32, (LANES, LANES), 1) // HEAD_DIM
    for g in range(2):
        qg = qkv_ref[:, g * LANES:(g + 1) * LANES]
        kg = qkv_ref[:, 256 + g * LANES:256 + (g + 1) * LANES]
        vg = qkv_ref[:, 512 + g * LANES:512 + (g + 1) * LANES]
        state = r_ref[g]
        inter = _dot(qg, state.astype(BF16)) * xi_ref[g]
        parts = []
        for hh in range(2):
            qm = jnp.where(low, qg, jnp.zeros_like(qg)) if hh == 0 else jnp.where(low, jnp.zeros_like(qg), qg)
            s = _dot_nt(qm, kg) * dmat_ref[2 * g + hh]
            parts.append(_dot(s.astype(BF16), vg))
        o = jnp.where(low, parts[0], parts[1]) + inter
        kz = (kg.astype(F32) * zeta_ref[g]).astype(BF16)
        upd = jnp.where(r_idx == c_idx, _dot_tn(kz, vg), 0.0)
        r_ref[g] = state * gb_ref[g] + upd
        y = _pair_head_norm(o, g_ref[:, g * LANES:(g + 1) * LANES], True)
        o_ref[:, g * LANES:(g + 1) * LANES] = (_silu(gate_ref[:, g * LANES:(g + 1) * LANES]) * y).astype(BF16)


def _ret_call(qkvb, gates, tabs, g_b, B, T):
    nb = T // TB
    dmat, xi, zeta, gb = tabs
    const3 = lambda b, i: (0, 0, 0)
    return pl.pallas_call(
        _ret_kernel,
        out_shape=jax.ShapeDtypeStruct((B * T, GROUP_W), BF16),
        grid=(B, nb),
        in_specs=[pl.BlockSpec((TB, 768), lambda b, i: (b * nb + i, 0)),
                  pl.BlockSpec((TB, GROUP_W), lambda b, i: (b * nb + i, 0)),
                  pl.BlockSpec((N_GH, TB, TB), const3),
                  pl.BlockSpec((2, TB, LANES), const3),
                  pl.BlockSpec((2, TB, LANES), const3),
                  pl.BlockSpec((2, LANES, LANES), const3),
                  pl.BlockSpec((1, GROUP_W), lambda b, i: (0, 0))],
        out_specs=pl.BlockSpec((TB, GROUP_W), lambda b, i: (b * nb + i, 0)),
        scratch_shapes=[pltpu.VMEM((2, LANES, LANES), F32)],
        compiler_params=_cparams("arbitrary", "arbitrary"),
        name="ret",
    )(qkvb, gates, dmat, xi, zeta, gb, g_b)


def _gla_kernel(c_ref, gate_ref, wa2_ref, ba_ref, g_ref, o_ref, s_ref, b_scr, k_scr, v_scr):
    @pl.when(pl.program_id(1) == 0)
    def _():
        s_ref[...] = jnp.zeros_like(s_ref)

    hmask = (lax.broadcasted_iota(I32, (LANES, GROUP_W), 0) // GLA_DK
             == lax.broadcasted_iota(I32, (LANES, GROUP_W), 1) // HEAD_DIM)
    hsum = jnp.where(hmask, 1.0, 0.0).astype(BF16)
    tri = jnp.where(lax.broadcasted_iota(I32, (CHUNK, CHUNK), 0)
                    >= lax.broadcasted_iota(I32, (CHUNK, CHUNK), 1), 1.0, 0.0)

    z = _dot_f32(c_ref[:, 512:640], wa2_ref[...]) + ba_ref[...]
    log_a = (jnp.minimum(z, 0.0) - jnp.log(1.0 + jnp.exp(-jnp.abs(z)))) * (1.0 / GLA_TAU)
    outs = []
    for c in range(TB // CHUNK):
        rows = slice(c * CHUNK, (c + 1) * CHUNK)
        b = _dot_f32(tri, log_a[rows])
        q = c_ref[rows, 0:128] * (GLA_DK ** -0.5)
        k = c_ref[rows, 128:256]
        v = c_ref[rows, 256:512]
        state = s_ref[...]
        inter = _dot((q * jnp.exp(b)).astype(BF16), state.astype(BF16))
        b_scr[...] = b
        k_scr[...] = k
        v_scr[...] = v

        def key_step(s, acc, b=b, q=q):
            decay = jnp.exp(-jnp.abs(b - b_scr[pl.ds(s, 1), :]))
            w = (q * decay * k_scr[pl.ds(s, 1), :]).astype(BF16)
            return acc + _dot(w, hsum) * v_scr[pl.ds(s, 1), :]

        intra = lax.fori_loop(0, CHUNK, key_step, jnp.zeros((CHUNK, GROUP_W), F32), unroll=8)
        outs.append(intra + inter)
        b_last = b[CHUNK - 1:CHUNK, :]
        kd = (k * jnp.exp(b_last - b)).astype(BF16)
        upd = _dot_tn(kd, v.astype(BF16))
        scale = jnp.transpose(jnp.broadcast_to(jnp.exp(b_last), (LANES, LANES)))
        s_ref[...] = jnp.where(hmask, state * jnp.concatenate([scale, scale], axis=1) + upd, 0.0)
    o = jnp.concatenate(outs, axis=0)
    for gi in range(2):
        y = _pair_head_norm(o[:, gi * LANES:(gi + 1) * LANES], g_ref[:, gi * LANES:(gi + 1) * LANES], True)
        o_ref[:, gi * LANES:(gi + 1) * LANES] = (_silu(gate_ref[:, gi * LANES:(gi + 1) * LANES]) * y).astype(BF16)


def _gla_call(cc, gates, wa2_p, ba, g_c, B, T):
    nb = T // TB
    return pl.pallas_call(
        _gla_kernel,
        out_shape=jax.ShapeDtypeStruct((B * T, GROUP_W), BF16),
        grid=(B, nb),
        in_specs=[pl.BlockSpec((TB, 640), lambda b, i: (b * nb + i, 0)),
                  pl.BlockSpec((TB, GROUP_W), lambda b, i: (b * nb + i, 1)),
                  pl.BlockSpec((LANES, LANES), lambda b, i: (0, 0)),
                  pl.BlockSpec((1, LANES), lambda b, i: (0, 0)),
                  pl.BlockSpec((1, GROUP_W), lambda b, i: (0, 0))],
        out_specs=pl.BlockSpec((TB, GROUP_W), lambda b, i: (b * nb + i, 0)),
        scratch_shapes=[pltpu.VMEM((LANES, GROUP_W), F32), pltpu.VMEM((CHUNK, LANES), F32),
                        pltpu.VMEM((CHUNK, LANES), F32), pltpu.VMEM((CHUNK, GROUP_W), F32)],
        compiler_params=_cparams("arbitrary", "arbitrary"),
        name="gla",
    )(cc, gates, wa2_p, ba, g_c)


def _rms(y, g):
    return y * lax.rsqrt(jnp.mean(y * y, axis=-1, keepdims=True) + EPS) * g


def _out_mlp_kernel(x_ref, ya_ref, yb_ref, yc_ref, yd_ref, mod_ref, wo_ref, w1_ref, w2_ref,
                    gpost_ref, gpre_ref, gpost2_ref, o_ref):
    y = _dot(ya_ref[...], wo_ref[0:256, :])
    y = y + _dot(yb_ref[...], wo_ref[256:512, :])
    y = y + _dot(yc_ref[...], wo_ref[512:768, :])
    y = y + _dot(yd_ref[...], wo_ref[768:1024, :])
    x = x_ref[...] + mod_ref[0, 2:3, :] * _rms(y, gpost_ref[...])
    h = _rms(x, gpre_ref[...]) * (1.0 + mod_ref[0, 4:5, :]) + mod_ref[0, 3:4, :]
    hb = h.astype(BF16)
    acc = jnp.zeros(x.shape, F32)
    fc = 1024
    for f in range(D_FF // fc):
        u = jnp.maximum(_dot(hb, w1_ref[:, f * fc:(f + 1) * fc]), 0.0)
        acc = acc + _dot((u * u).astype(BF16), w2_ref[f * fc:(f + 1) * fc, :])
    o_ref[...] = x + mod_ref[0, 5:6, :] * _rms(acc, gpost2_ref[...])


def _out_mlp_call(x2, ys, mod3, w_out, w1, w2, g_post, g_pre2, g_post2, T):
    N, D = x2.shape
    nt = T // TM
    row = lambda i: (i, 0)
    const = lambda i: (0, 0)
    return pl.pallas_call(
        _out_mlp_kernel,
        out_shape=jax.ShapeDtypeStruct((N, D), F32),
        grid=(N // TM,),
        in_specs=[pl.BlockSpec((TM, D), row)] + [pl.BlockSpec((TM, GROUP_W), row)] * 4
                 + [pl.BlockSpec((1, 6, D), lambda i: (i // nt, 0, 0)),
                    _resident((D, D), const), _resident((D, D_FF), const), _resident((D_FF, D), const),
                    _resident((1, D), const), _resident((1, D), const), _resident((1, D), const)],
        out_specs=pl.BlockSpec((TM, D), row),
        compiler_params=_cparams("arbitrary"),
        name="out_mlp",
    )(x2, *ys, mod3, w_out, w1, w2, g_post, g_pre2, g_post2)


def _relayout_w_in(w_in):
    col = lambda k: w_in[:, _IN_OFF[k]:_IN_OFF[k + 1]]
    zeros = lambda n: jnp.zeros((w_in.shape[0], n), w_in.dtype)
    parts = [col(_A_Q), col(_A_QI), col(_A_K), col(_A_KI), col(_A_V), zeros(64),
             col(_A_WI), zeros(120),
             col(_B_Q), col(_B_K), col(_B_V),
             col(_B_G), col(_C_G),
             col(_C_Q), col(_C_K), col(_C_V), col(_C_A), zeros(112),
             col(_D_Q), col(_D_K), col(_D_V)]
    return jnp.concatenate(parts, axis=1).astype(BF16)


def _rope_tables(pos, rot, period, theta):
    half = rot // 2
    inv = theta ** (-jnp.arange(half, dtype=F32) / half)
    ang = pos.astype(F32)[..., None] * inv
    cos, sin = jnp.cos(ang), jnp.sin(ang)
    pad = ang.shape[:-1] + (period - rot,)
    ct = jnp.concatenate([cos, cos, jnp.ones(pad, F32)], axis=-1)
    st = jnp.concatenate([-sin, sin, jnp.zeros(pad, F32)], axis=-1)
    reps = LANES // period
    return (jnp.tile(ct, (1, 1, reps)).reshape(-1, LANES), jnp.tile(st, (1, 1, reps)).reshape(-1, LANES))


def _layer(x2, cond, tabs, ret_tabs, layer_idx, B, T, mod_w, mod_b, attn_pre_g, attn_post_g,
           mlp_pre_g, mlp_post_g, w_in, gla_wa2, gla_ba, lam_q1, lam_k1, lam_q2, lam_k2,
           head_norm_g, w_out, mlp_w1, mlp_w2):
    D = D_MODEL
    mod3 = _mod_call(cond, mod_w, mod_b).reshape(B, 6, D)
    qa, kva, wa, qkvb, gates, cc, dqkv = _in_call(
        x2, mod3, attn_pre_g.reshape(1, D), _relayout_w_in(w_in), tabs, T)

    g_a, g_b, g_c, g_d = jnp.split(head_norm_g, 4)
    gain_t = lambda g: jnp.broadcast_to(g[:, None], (GROUP_W, TQ))
    values_t = lambda v: jnp.swapaxes(v.reshape(B, T, -1), 1, 2).reshape(-1, T)
    y_a = _dsa_call(qa, wa, kva, values_t(kva[:, LANES:LANES + HEAD_DIM]), gain_t(g_a), B, T)
    y_b = _ret_call(qkvb, gates, ret_tabs, g_b.reshape(1, GROUP_W), B, T)
    wa2_p = jnp.zeros((LANES, LANES), F32).at[:GLA_RANK].set(gla_wa2)
    y_c = _gla_call(cc, gates, wa2_p, gla_ba.reshape(1, LANES), g_c.reshape(1, GROUP_W), B, T)
    lam_init = 0.8 - 0.6 * math.exp(-0.3 * layer_idx)
    lam_vecs = jnp.zeros((4, LANES), F32).at[:, :DIFF_DK].set(jnp.stack([lam_q1, lam_k1, lam_q2, lam_k2]))
    y_d = _diff_call(dqkv, values_t(dqkv[:, 512:768]), lam_vecs, gain_t(g_d), lam_init, B, T)

    return _out_mlp_call(x2, (y_a, y_b, y_c, y_d), mod3, w_out.astype(BF16), mlp_w1.astype(BF16),
                         mlp_w2.astype(BF16), attn_post_g.reshape(1, D), mlp_pre_g.reshape(1, D),
                         mlp_post_g.reshape(1, D), T)


def kernel(x, c, positions, mod_w, mod_b, attn_pre_g, attn_post_g, mlp_pre_g, mlp_post_g, w_in,
           gla_wa2, gla_ba, lam_q1, lam_k1, lam_q2, lam_k2, head_norm_g, w_out, mlp_w1, mlp_w2):
    B, T, D = x.shape
    assert D == D_MODEL and T % TM == 0 and T % TB == 0
    tabs = (_rope_tables(positions, HEAD_DIM // ROPE_FRAC, HEAD_DIM, ROPE_THETA)
            + _rope_tables(positions, HEAD_DIM, HEAD_DIM, RET_THETA)
            + _rope_tables(positions, DIFF_DK // ROPE_FRAC, DIFF_DK, ROPE_THETA))
    ret_tabs = _ret_tables()
    x2 = x.reshape(B * T, D)
    for l in range(mod_w.shape[0]):
        x2 = _layer(x2, c, tabs, ret_tabs, l, B, T, mod_w[l], mod_b[l], attn_pre_g[l], attn_post_g[l],
                    mlp_pre_g[l], mlp_post_g[l], w_in[l], gla_wa2[l], gla_ba[l], lam_q1[l], lam_k1[l],
                    lam_q2[l], lam_k2[l], head_norm_g[l], w_out[l], mlp_w1[l], mlp_w2[l])
    return x2.reshape(B, T, D)
```

```python
import functools
import math

import numpy as np
import jax
import jax.numpy as jnp
from jax import lax
from jax.experimental import pallas as pl
from jax.experimental.pallas import tpu as pltpu

F32 = jnp.float32
BF16 = jnp.bfloat16
I32 = jnp.int32

D_MODEL = 1024
CHUNK = 64
HEAD_DIM = 64
N_GH = 4
GROUP_W = N_GH * HEAD_DIM
IDX_HEADS = 8
IDX_DIM = 64
TOPK_MAX = 256
RET_THETA = 10000.0
GLA_DK = HEAD_DIM // 2
GLA_RANK = 16
GLA_TAU = 16.0
DIFF_DK = HEAD_DIM // 2
ROPE_THETA = 500000.0
ROPE_FRAC = 4
D_FF = 4 * D_MODEL
EPS = 1e-6
NEG_INF = -1e30

LANES = 128
VMEM_LIMIT = 56 * 1024 * 1024

_IN_SPLITS = (
    GROUP_W, HEAD_DIM, HEAD_DIM, IDX_HEADS * IDX_DIM, IDX_DIM, IDX_HEADS,
    GROUP_W, GROUP_W, GROUP_W, GROUP_W,
    N_GH * GLA_DK, N_GH * GLA_DK, GROUP_W, GLA_RANK, GROUP_W,
    2 * N_GH * DIFF_DK, 2 * N_GH * DIFF_DK, GROUP_W,
)
_IN_OFF = np.concatenate([[0], np.cumsum(_IN_SPLITS)]).tolist()
(_A_Q, _A_K, _A_V, _A_QI, _A_KI, _A_WI, _B_Q, _B_K, _B_V, _B_G,
 _C_Q, _C_K, _C_V, _C_A, _C_G, _D_Q, _D_K, _D_V) = range(18)

SEG_A = 0
SEG_W = 1024
SEG_B = 1152
SEG_G = 1920
SEG_C = 2432
SEG_D = 3072
W_COLS = 3840

NT_DIMS = (((1,), (1,)), ((), ()))
TN_DIMS = (((0,), (0,)), ((), ()))

TM = 512
TQ = 256
TK = 256
FLASH_SLAB = 32
LOG2E = math.log2(math.e)
TB = 256


def _float_key(v):
    bits = int(np.array(v, np.float32).view(np.int32))
    return -(bits & 0x7FFFFFFF) if bits < 0 else bits


NEG_KEY = _float_key(NEG_INF)
INT_MIN = -(2 ** 31)


def _cparams(*sem):
    return pltpu.CompilerParams(dimension_semantics=sem, vmem_limit_bytes=VMEM_LIMIT)


def _resident(shape, index_map):
    return pl.BlockSpec(shape, index_map, pipeline_mode=pl.Buffered(1))


def _dot(a, b):
    return jnp.dot(a, b, preferred_element_type=F32)


def _dot_nt(a, b):
    return lax.dot_general(a, b, NT_DIMS, preferred_element_type=F32)


def _dot_tn(a, b):
    return lax.dot_general(a, b, TN_DIMS, preferred_element_type=F32)


def _dot_f32(a, b):
    return jnp.dot(a, b, preferred_element_type=F32, precision=lax.Precision.HIGHEST)


def _lane(shape):
    return lax.broadcasted_iota(I32, shape, len(shape) - 1)


def _silu(x):
    return x / (1.0 + jnp.exp(-x))


def _mod_kernel(c_ref, w_ref, b_ref, o_ref):
    cond = _silu(c_ref[...])
    o_ref[...] = _dot_f32(cond, w_ref[...]) + b_ref[...]


def _mod_call(c, mod_w, mod_b):
    B, D = c.shape
    n = mod_w.shape[1] // D
    return pl.pallas_call(
        _mod_kernel,
        out_shape=jax.ShapeDtypeStruct((B, n * D), F32),
        grid=(n,),
        in_specs=[pl.BlockSpec((B, D), lambda j: (0, 0)),
                  pl.BlockSpec((D, D), lambda j: (0, j)),
                  pl.BlockSpec((1, D), lambda j: (0, j))],
        out_specs=pl.BlockSpec((B, D), lambda j: (0, j)),
        compiler_params=_cparams("arbitrary"),
        name="mod",
    )(c, mod_w, mod_b.reshape(1, -1))


def _rope_group(xg, cos, sin, half, period):
    lo = (_lane((1, LANES)) % period) < half
    swapped = jnp.where(lo, pltpu.roll(xg, LANES - half, 1), pltpu.roll(xg, half, 1))
    return xg * cos + swapped * sin


def _in_kernel(x_ref, mod_ref, g_ref, w_ref, ca_ref, sa_ref, cb_ref, sb_ref, cd_ref, sd_ref,
               qa_ref, kva_ref, wa_ref, qkvb_ref, gate_ref, c_ref, d_ref):
    x = x_ref[...]
    h = x * lax.rsqrt(jnp.mean(x * x, axis=-1, keepdims=True) + EPS) * g_ref[...]
    h = h * (1.0 + mod_ref[0, 1:2, :]) + mod_ref[0, 0:1, :]
    hb = h.astype(BF16)

    def seg(off, width):
        return _dot(hb, w_ref[:, off:off + width])

    lane = _lane((1, LANES))
    low = lane < HEAD_DIM

    ca, sa = ca_ref[...], sa_ref[...]
    a = seg(SEG_A, 1024)
    half_a = HEAD_DIM // ROPE_FRAC // 2
    for gi in range(2):
        r = _rope_group(a[:, gi * LANES:(gi + 1) * LANES], ca, sa, half_a, HEAD_DIM) * (HEAD_DIM ** -0.5 * LOG2E)
        rs = pltpu.roll(r, HEAD_DIM, 1)
        qa_ref[:, (2 * gi) * LANES:(2 * gi + 1) * LANES] = jnp.where(low, r, 0.0).astype(BF16)
        qa_ref[:, (2 * gi + 1) * LANES:(2 * gi + 2) * LANES] = jnp.where(low, rs, 0.0).astype(BF16)
    for gi in range(4):
        r = _rope_group(a[:, (2 + gi) * LANES:(3 + gi) * LANES], ca, sa, half_a, HEAD_DIM)
        rs = pltpu.roll(r, HEAD_DIM, 1)
        qa_ref[:, (4 + 2 * gi) * LANES:(5 + 2 * gi) * LANES] = jnp.where(low, 0.0, rs).astype(BF16)
        qa_ref[:, (5 + 2 * gi) * LANES:(6 + 2 * gi) * LANES] = jnp.where(low, 0.0, r).astype(BF16)
    kva_ref[:, 0:LANES] = _rope_group(a[:, 6 * LANES:7 * LANES], ca, sa, half_a, HEAD_DIM).astype(BF16)
    kva_ref[:, LANES:2 * LANES] = a[:, 7 * LANES:8 * LANES].astype(BF16)

    wa_ref[...] = seg(SEG_W, LANES)

    cb, sb = cb_ref[...], sb_ref[...]
    b = seg(SEG_B, 768)
    for gi in range(4):
        r = _rope_group(b[:, gi * LANES:(gi + 1) * LANES], cb, sb, HEAD_DIM // 2, HEAD_DIM)
        if gi < 2:
            r = r * (HEAD_DIM ** -0.5)
        qkvb_ref[:, gi * LANES:(gi + 1) * LANES] = r.astype(BF16)
    qkvb_ref[:, 512:768] = b[:, 512:768].astype(BF16)

    gate_ref[...] = seg(SEG_G, 512)
    c_ref[...] = seg(SEG_C, 640)

    cd, sd = cd_ref[...], sd_ref[...]
    d = seg(SEG_D, 768)
    half_d = DIFF_DK // ROPE_FRAC // 2
    for gi in range(4):
        r = _rope_group(d[:, gi * LANES:(gi + 1) * LANES], cd, sd, half_d, DIFF_DK)
        if gi < 2:
            r = r * (DIFF_DK ** -0.5 * LOG2E)
        d_ref[:, gi * LANES:(gi + 1) * LANES] = r.astype(BF16)
    d_ref[:, 512:768] = d[:, 512:768].astype(BF16)


def _in_call(x2, mod3, g, w_p, tabs, T):
    N, D = x2.shape
    nt = T // TM
    row = lambda i: (i, 0)
    tab_spec = pl.BlockSpec((TM, LANES), row)
    outs = [(1536, BF16), (256, BF16), (LANES, F32), (768, BF16), (512, F32), (640, F32), (768, BF16)]
    return pl.pallas_call(
        _in_kernel,
        out_shape=[jax.ShapeDtypeStruct((N, w), dt) for w, dt in outs],
        grid=(N // TM,),
        in_specs=[pl.BlockSpec((TM, D), row),
                  pl.BlockSpec((1, 6, D), lambda i: (i // nt, 0, 0)),
                  _resident((1, D), lambda i: (0, 0)),
                  _resident((D, W_COLS), lambda i: (0, 0))] + [tab_spec] * 6,
        out_specs=[pl.BlockSpec((TM, w), row) for w, _ in outs],
        compiler_params=_cparams("arbitrary"),
        name="in_proj",
    )(x2, mod3, g, w_p, *tabs)


def _pair_head_norm(o, gain, center):
    low = _lane((1, LANES)) < HEAD_DIM
    inv = 1.0 / HEAD_DIM

    def seg_mean(v):
        m0 = jnp.sum(jnp.where(low, v, 0.0), axis=1, keepdims=True)
        m1 = jnp.sum(jnp.where(low, 0.0, v), axis=1, keepdims=True)
        return jnp.where(low, m0, m1) * inv

    if center:
        o = o - seg_mean(o)
    return o * lax.rsqrt(seg_mean(o * o) + EPS) * gain


class _Flash:
    def __init__(self, n, s_ref, p_ref, acc_ref, st_ref):
        self.n, self.s, self.p, self.acc, self.st = n, s_ref, p_ref, acc_ref, st_ref

    @staticmethod
    def scratch(n):
        return [pltpu.VMEM((n, TK, TQ), F32), pltpu.VMEM((n, TK, TQ), BF16),
                pltpu.VMEM((n, HEAD_DIM, TQ), F32), pltpu.VMEM((4, n, TQ), F32)]

    def _row(self, k, i):
        return self.st.at[k, i:i + 1, :]

    def init(self):
        self.st[0] = jnp.full((self.n, TQ), NEG_INF, F32)
        self.st[1] = jnp.zeros((self.n, TQ), F32)
        self.acc[...] = jnp.zeros_like(self.acc)

    def scores(self, i, s):
        self.s[i] = s
        self._row(2, i)[...] = jnp.max(_fold8(s, jnp.maximum), axis=0, keepdims=True)

    def probs(self, i):
        m_old = self._row(0, i)[...]
        m_new = jnp.maximum(m_old, self._row(2, i)[...])
        alpha = jnp.exp2(m_old - m_new)
        psum = jnp.zeros((8, TQ), F32)
        for r in range(0, TK, FLASH_SLAB):
            p = jnp.exp2(self.s[i, r:r + FLASH_SLAB, :] - m_new)
            psum = psum + _fold8(p)
            self.p[i, r:r + FLASH_SLAB, :] = p.astype(BF16)
        self._row(1, i)[...] = alpha * self._row(1, i)[...] + jnp.sum(psum, axis=0, keepdims=True)
        self._row(0, i)[...] = m_new
        self._row(3, i)[...] = alpha

    def values(self, i, vt):
        self.acc[i] = self._row(3, i)[...] * self.acc[i] + _dot(vt, self.p[i])

    def result(self, i):
        return self.acc[i] / self._row(1, i)[...]

    def pipeline(self, first, n_rest, score, value):
        score(first, True)

        def body(t, carry):
            for i in range(self.n):
                self.probs(i)
            score(t, False)
            value(jnp.where(t == 0, first, t - 1))
            return carry

        lax.fori_loop(0, n_rest, body, 0)
        for i in range(self.n):
            self.probs(i)
        value(jnp.where(n_rest == 0, first, n_rest - 1))


def _fold8(x, op=jnp.add):
    parts = [x[r:r + 8] for r in range(0, x.shape[0], 8)]
    while len(parts) > 1:
        parts = [op(a, b) for a, b in zip(parts[0::2], parts[1::2])]
    return parts[0]


def _head_norm_t(o, gain):
    ms = jnp.mean(o * o, axis=0, keepdims=True)
    return o * lax.rsqrt(ms + EPS) * gain


def _dsa_kernel(q_ref, w_ref, kv_ref, vt_ref, g_ref, o_ref, bias_ref, key_ref, *flash_refs, topk):
    q0 = pl.program_id(1) * TQ
    nkt = (q0 + TQ) // TK
    qchunk = (q0 + _lane((1, TQ))) // CHUNK
    wts = jnp.transpose(w_ref[...]) * (IDX_HEADS ** -0.5 * IDX_DIM ** -0.5)

    def score_tile(kt, carry):
        k0 = pl.multiple_of(kt * TK, TK)
        kk = kv_ref[pl.ds(k0, TK), 0:LANES]
        tile = bias_ref.at[pl.ds(k0, TK), :]
        for h in range(IDX_HEADS):
            term = jnp.maximum(_dot_nt(kk, q_ref[:, (4 + h) * LANES:(5 + h) * LANES]), 0.0) * wts[h:h + 1, :]
            if h == 0:
                tile[...] = term
            elif h < IDX_HEADS - 1:
                tile[...] += term
        kchunk = (k0 + lax.broadcasted_iota(I32, (TK, 1), 0)) // CHUNK
        acc = jnp.where(kchunk <= qchunk, tile[...] + term, NEG_INF)
        bits = pltpu.bitcast(acc, I32)
        sign = bits >> 31
        key_ref[pl.ds(k0, TK), :] = ((bits & 0x7FFFFFFF) ^ sign) - sign
        return carry

    lax.fori_loop(0, nkt, score_tile, 0)

    def count(pred):
        def body(kt, acc):
            k0 = pl.multiple_of(kt * TK, TK)
            hit = jnp.where(pred(key_ref[pl.ds(k0, TK), :], k0), 1.0, 0.0)
            return acc + _fold8(hit)
        acc = lax.fori_loop(0, nkt, body, jnp.zeros((8, TQ), F32))
        return jnp.sum(acc, axis=0, keepdims=True)

    kf = float(topk)
    c0 = count(lambda keys, k0: keys >= 0)
    thr = jnp.where(c0 >= kf, 0, INT_MIN).astype(I32)
    cnt = jnp.where(c0 >= kf, c0, (nkt * TK).astype(F32))

    def search(it, carry):
        thr, cnt = carry
        cand = thr + (jnp.int32(1) << (30 - it))
        c = count(lambda keys, k0: keys >= cand)
        ok = c >= kf
        return jnp.where(ok, cand, thr), jnp.where(ok, c, cnt)

    thr, cnt = lax.fori_loop(0, 31, search, (thr, cnt))
    floor_key = jnp.int32(NEG_KEY + 1)
    tied = jnp.max(jnp.where((cnt > kf) & (thr >= floor_key), 1.0, 0.0)) > 0.0

    @pl.when(jnp.logical_not(tied))
    def _():
        lim = jnp.maximum(thr, floor_key)

        def body(kt, carry):
            k0 = pl.multiple_of(kt * TK, TK)
            bias_ref[pl.ds(k0, TK), :] = jnp.where(key_ref[pl.ds(k0, TK), :] >= lim, 0.0, NEG_INF)
            return carry
        lax.fori_loop(0, nkt, body, 0)

    @pl.when(tied)
    def _():
        need = kf - count(lambda keys, k0: keys > thr)

        def idx(k0):
            return k0 + lax.broadcasted_iota(I32, (TK, 1), 0)

        def search_idx(it, lim):
            cand = lim + (jnp.int32(1) << (14 - it))
            c = count(lambda keys, k0: (keys == thr) & (idx(k0) < cand))
            return jnp.where(c <= need, cand, lim)

        lim = lax.fori_loop(0, 15, search_idx, jnp.zeros((1, TQ), I32))

        def body(kt, carry):
            k0 = pl.multiple_of(kt * TK, TK)
            keys = key_ref[pl.ds(k0, TK), :]
            sel = ((keys > thr) | ((keys == thr) & (idx(k0) < lim))) & (keys >= floor_key)
            bias_ref[pl.ds(k0, TK), :] = jnp.where(sel, 0.0, NEG_INF)
            return carry
        lax.fori_loop(0, nkt, body, 0)

    flash = _Flash(N_GH, *flash_refs)
    flash.init()

    def score(kt, is_first):
        k0 = pl.multiple_of(kt * TK, TK)
        kk = kv_ref[pl.ds(k0, TK), 0:LANES]
        bias = bias_ref[pl.ds(k0, TK), :]
        for h in range(N_GH):
            flash.scores(h, _dot_nt(kk, q_ref[:, h * LANES:(h + 1) * LANES]) + bias)

    def value(kt):
        k0 = pl.multiple_of(kt * TK, TK)
        for h in range(N_GH):
            flash.values(h, vt_ref[:, pl.ds(k0, TK)])

    flash.pipeline(nkt - 1, nkt - 1, score, value)
    y = jnp.concatenate([_head_norm_t(flash.result(h), g_ref[h * HEAD_DIM:(h + 1) * HEAD_DIM, :])
                         for h in range(N_GH)], axis=0)
    o_ref[...] = jnp.transpose(y).astype(BF16)


def _dsa_call(qa, wa, kva, vta, g_t, B, T):
    nq = T // TQ
    topk = min(TOPK_MAX, T // 4)
    return pl.pallas_call(
        functools.partial(_dsa_kernel, topk=topk),
        out_shape=jax.ShapeDtypeStruct((B * T, GROUP_W), BF16),
        grid=(B, nq),
        in_specs=[pl.BlockSpec((TQ, 1536), lambda b, i: (b * nq + i, 0)),
                  pl.BlockSpec((TQ, LANES), lambda b, i: (b * nq + i, 0)),
                  pl.BlockSpec((T, 256), lambda b, i: (b, 0)),
                  pl.BlockSpec((HEAD_DIM, T), lambda b, i: (b, 0)),
                  pl.BlockSpec((GROUP_W, TQ), lambda b, i: (0, 0))],
        out_specs=pl.BlockSpec((TQ, GROUP_W), lambda b, i: (b * nq + i, 0)),
        scratch_shapes=[pltpu.VMEM((T, TQ), F32), pltpu.VMEM((T, TQ), I32)] + _Flash.scratch(N_GH),
        compiler_params=_cparams("arbitrary", "arbitrary"),
        name="dsa",
    )(qa, wa, kva, vta, g_t)


def _diff_kernel(q_ref, k_ref, vt_ref, lam_ref, g_ref, o_ref, qm_ref, *flash_refs, lam_init):
    q0 = pl.program_id(1) * TQ
    nfull = (q0 + CHUNK) // TK
    qchunk = (q0 + _lane((1, TQ))) // CHUNK
    lv = lam_ref[...]
    lam = (jnp.exp(jnp.sum(lv[0:1] * lv[1:2], axis=1, keepdims=True))
           - jnp.exp(jnp.sum(lv[2:3] * lv[3:4], axis=1, keepdims=True)) + lam_init)
    lane = _lane((1, LANES))
    n_maps = 2 * N_GH
    for mi in range(n_maps):
        qg = q_ref[:, (mi // 4) * LANES:(mi // 4 + 1) * LANES]
        qm_ref[mi] = jnp.where((lane // DIFF_DK) == mi % 4, qg, jnp.zeros_like(qg))

    flash = _Flash(n_maps, *flash_refs)
    flash.init()

    def score(kt, masked):
        k0 = pl.multiple_of(kt * TK, TK)
        if masked:
            kchunk = (k0 + lax.broadcasted_iota(I32, (TK, 1), 0)) // CHUNK
            admissible = kchunk <= qchunk
        for mi in range(n_maps):
            g = mi // 4
            s = _dot_nt(k_ref[pl.ds(k0, TK), g * LANES:(g + 1) * LANES], qm_ref[mi])
            flash.scores(mi, jnp.where(admissible, s, NEG_INF) if masked else s)

    def value(kt):
        k0 = pl.multiple_of(kt * TK, TK)
        for mi in range(n_maps):
            h = mi // 2
            flash.values(mi, vt_ref[h * HEAD_DIM:(h + 1) * HEAD_DIM, pl.ds(k0, TK)])

    flash.pipeline(nfull, nfull, score, value)
    outs = []
    for h in range(N_GH):
        o = flash.result(2 * h) - lam * flash.result(2 * h + 1)
        outs.append(_head_norm_t(o, g_ref[h * HEAD_DIM:(h + 1) * HEAD_DIM, :]) * (1.0 - lam_init))
    o_ref[...] = jnp.transpose(jnp.concatenate(outs, axis=0)).astype(BF16)


def _diff_call(dqkv, vtd, lam_vecs, g_t, lam_init, B, T):
    nq = T // TQ
    return pl.pallas_call(
        functools.partial(_diff_kernel, lam_init=lam_init),
        out_shape=jax.ShapeDtypeStruct((B * T, GROUP_W), BF16),
        grid=(B, nq),
        in_specs=[pl.BlockSpec((TQ, 256), lambda b, i: (b * nq + i, 0)),
                  pl.BlockSpec((T, 256), lambda b, i: (b, 1)),
                  pl.BlockSpec((GROUP_W, T), lambda b, i: (b, 0)),
                  pl.BlockSpec((4, LANES), lambda b, i: (0, 0)),
                  pl.BlockSpec((GROUP_W, TQ), lambda b, i: (0, 0))],
        out_specs=pl.BlockSpec((TQ, GROUP_W), lambda b, i: (b * nq + i, 0)),
        scratch_shapes=[pltpu.VMEM((2 * N_GH, TQ, LANES), BF16)] + _Flash.scratch(2 * N_GH),
        compiler_params=_cparams("arbitrary", "arbitrary"),
        name="diff",
    )(dqkv, dqkv, vtd, lam_vecs, g_t)


def _ret_tables():
    h = np.arange(N_GH, dtype=np.float32)
    log_g = jnp.log(1.0 - 2.0 ** (-5.0 - jnp.asarray(h)))
    t = np.arange(TB)
    same = (t[:, None] // CHUNK) == (t[None, :] // CHUNK)
    earlier = (t[None, :] // CHUNK) < (t[:, None] // CHUNK)
    dist = jnp.asarray(np.where(same, np.abs(t[:, None] - t[None, :]), t[:, None] - t[None, :]), F32)
    dmat = jnp.where(jnp.asarray(same | earlier)[None],
                     jnp.exp(dist[None] * log_g[:, None, None]), 0.0)
    tt = jnp.asarray(t, F32)
    xi = jnp.exp((tt + 1.0)[None, :] * log_g[:, None])
    zeta = jnp.exp((TB - 1.0 - tt)[None, :] * log_g[:, None])
    gblk = jnp.exp(TB * log_g)

    def lanes(tab):
        return jnp.repeat(tab.reshape(2, 2, TB), HEAD_DIM, axis=1).transpose(0, 2, 1)

    r = np.arange(LANES)
    blockdiag = (r[:, None] // HEAD_DIM) == (r[None, :] // HEAD_DIM)
    gb = jnp.where(jnp.asarray(blockdiag)[None],
                   jnp.repeat(gblk.reshape(2, 2), HEAD_DIM, axis=1)[:, :, None], 0.0)
    return dmat, lanes(xi), lanes(zeta), gb


def _ret_kernel(qkv_ref, gate_ref, dmat_ref, xi_ref, zeta_ref, gb_ref, g_ref, o_ref, r_ref):
    @pl.when(pl.program_id(1) == 0)
    def _():
        r_ref[...] = jnp.zeros_like(r_ref)

    lane = _lane((1, LANES))
    low = lane < HEAD_DIM
    r_idx = lax.broadcasted_iota(I32, (LANES, LANES), 0) // HEAD_DIM
    c_idx = lax.broadcasted_iota(I32, (LANES, LANES), 1) // HEAD_DIM
    for g in range(2):
        qg = qkv_ref[:, g * LANES:(g + 1) * LANES]
        kg = qkv_ref[:, 256 + g * LANES:256 + (g + 1) * LANES]
        vg = qkv_ref[:, 512 + g * LANES:512 + (g + 1) * LANES]
        state = r_ref[g]
        inter = _dot(qg, state.astype(BF16)) * xi_ref[g]
        parts = []
        for hh in range(2):
            qm = jnp.where(low, qg, jnp.zeros_like(qg)) if hh == 0 else jnp.where(low, jnp.zeros_like(qg), qg)
            s = _dot_nt(qm, kg) * dmat_ref[2 * g + hh]
            parts.append(_dot(s.astype(BF16), vg))
        o = jnp.where(low, parts[0], parts[1]) + inter
        kz = (kg.astype(F32) * zeta_ref[g]).astype(BF16)
        upd = jnp.where(r_idx == c_idx, _dot_tn(kz, vg), 0.0)
        r_ref[g] = state * gb_ref[g] + upd
        y = _pair_head_norm(o, g_ref[:, g * LANES:(g + 1) * LANES], True)
        o_ref[:, g * LANES:(g + 1) * LANES] = (_silu(gate_ref[:, g * LANES:(g + 1) * LANES]) * y).astype(BF16)


def _ret_call(qkvb, gates, tabs, g_b, B, T):
    nb = T // TB
    dmat, xi, zeta, gb = tabs
    const3 = lambda b, i: (0, 0, 0)
    return pl.pallas_call(
        _ret_kernel,
        out_shape=jax.ShapeDtypeStruct((B * T, GROUP_W), BF16),
        grid=(B, nb),
        in_specs=[pl.BlockSpec((TB, 768), lambda b, i: (b * nb + i, 0)),
                  pl.BlockSpec((TB, GROUP_W), lambda b, i: (b * nb + i, 0)),
                  pl.BlockSpec((N_GH, TB, TB), const3),
                  pl.BlockSpec((2, TB, LANES), const3),
                  pl.BlockSpec((2, TB, LANES), const3),
                  pl.BlockSpec((2, LANES, LANES), const3),
                  pl.BlockSpec((1, GROUP_W), lambda b, i: (0, 0))],
        out_specs=pl.BlockSpec((TB, GROUP_W), lambda b, i: (b * nb + i, 0)),
        scratch_shapes=[pltpu.VMEM((2, LANES, LANES), F32)],
        compiler_params=_cparams("arbitrary", "arbitrary"),
        name="ret",
    )(qkvb, gates, dmat, xi, zeta, gb, g_b)


def _gla_kernel(c_ref, gate_ref, wa2_ref, ba_ref, g_ref, o_ref, s_ref, b_scr, k_scr, v_scr):
    @pl.when(pl.program_id(1) == 0)
    def _():
        s_ref[...] = jnp.zeros_like(s_ref)

    hmask = (lax.broadcasted_iota(I32, (LANES, GROUP_W), 0) // GLA_DK
             == lax.broadcasted_iota(I32, (LANES, GROUP_W), 1) // HEAD_DIM)
    hsum = jnp.where(hmask, 1.0, 0.0).astype(BF16)
    tri = jnp.where(lax.broadcasted_iota(I32, (CHUNK, CHUNK), 0)
                    >= lax.broadcasted_iota(I32, (CHUNK, CHUNK), 1), 1.0, 0.0)

    z = _dot_f32(c_ref[:, 512:640], wa2_ref[...]) + ba_ref[...]
    log_a = (jnp.minimum(z, 0.0) - jnp.log(1.0 + jnp.exp(-jnp.abs(z)))) * (1.0 / GLA_TAU)
    outs = []
    for c in range(TB // CHUNK):
        rows = slice(c * CHUNK, (c + 1) * CHUNK)
        b = _dot_f32(tri, log_a[rows])
        q = c_ref[rows, 0:128] * (GLA_DK ** -0.5)
        k = c_ref[rows, 128:256]
        v = c_ref[rows, 256:512]
        state = s_ref[...]
        inter = _dot((q * jnp.exp(b)).astype(BF16), state.astype(BF16))
        b_scr[...] = b
        k_scr[...] = k
        v_scr[...] = v

        def key_step(s, acc, b=b, q=q):
            decay = jnp.exp(-jnp.abs(b - b_scr[pl.ds(s, 1), :]))
            w = (q * decay * k_scr[pl.ds(s, 1), :]).astype(BF16)
            return acc + _dot(w, hsum) * v_scr[pl.ds(s, 1), :]

        intra = lax.fori_loop(0, CHUNK, key_step, jnp.zeros((CHUNK, GROUP_W), F32), unroll=16)
        outs.append(intra + inter)
        b_last = b[CHUNK - 1:CHUNK, :]
        kd = (k * jnp.exp(b_last - b)).astype(BF16)
        upd = _dot_tn(kd, v.astype(BF16))
        scale = jnp.transpose(jnp.broadcast_to(jnp.exp(b_last), (LANES, LANES)))
        s_ref[...] = jnp.where(hmask, state * jnp.concatenate([scale, scale], axis=1) + upd, 0.0)
    o = jnp.concatenate(outs, axis=0)
    for gi in range(2):
        y = _pair_head_norm(o[:, gi * LANES:(gi + 1) * LANES], g_ref[:, gi * LANES:(gi + 1) * LANES], True)
        o_ref[:, gi * LANES:(gi + 1) * LANES] = (_silu(gate_ref[:, gi * LANES:(gi + 1) * LANES]) * y).astype(BF16)


def _gla_call(cc, gates, wa2_p, ba, g_c, B, T):
    nb = T // TB
    return pl.pallas_call(
        _gla_kernel,
        out_shape=jax.ShapeDtypeStruct((B * T, GROUP_W), BF16),
        grid=(B, nb),
        in_specs=[pl.BlockSpec((TB, 640), lambda b, i: (b * nb + i, 0)),
                  pl.BlockSpec((TB, GROUP_W), lambda b, i: (b * nb + i, 1)),
                  pl.BlockSpec((LANES, LANES), lambda b, i: (0, 0)),
                  pl.BlockSpec((1, LANES), lambda b, i: (0, 0)),
                  pl.BlockSpec((1, GROUP_W), lambda b, i: (0, 0))],
        out_specs=pl.BlockSpec((TB, GROUP_W), lambda b, i: (b * nb + i, 0)),
        scratch_shapes=[pltpu.VMEM((LANES, GROUP_W), F32), pltpu.VMEM((CHUNK, LANES), F32),
                        pltpu.VMEM((CHUNK, LANES), F32), pltpu.VMEM((CHUNK, GROUP_W), F32)],
        compiler_params=_cparams("arbitrary", "arbitrary"),
        name="gla",
    )(cc, gates, wa2_p, ba, g_c)


def _rms(y, g):
    return y * lax.rsqrt(jnp.mean(y * y, axis=-1, keepdims=True) + EPS) * g


def _out_mlp_kernel(x_ref, ya_ref, yb_ref, yc_ref, yd_ref, mod_ref, wo_ref, w1_ref, w2_ref,
                    gpost_ref, gpre_ref, gpost2_ref, o_ref):
    y = _dot(ya_ref[...], wo_ref[0:256, :])
    y = y + _dot(yb_ref[...], wo_ref[256:512, :])
    y = y + _dot(yc_ref[...], wo_ref[512:768, :])
    y = y + _dot(yd_ref[...], wo_ref[768:1024, :])
    x = x_ref[...] + mod_ref[0, 2:3, :] * _rms(y, gpost_ref[...])
    h = _rms(x, gpre_ref[...]) * (1.0 + mod_ref[0, 4:5, :]) + mod_ref[0, 3:4, :]
    hb = h.astype(BF16)
    acc = jnp.zeros(x.shape, F32)
    fc = 1024
    for f in range(D_FF // fc):
        u = jnp.maximum(_dot(hb, w1_ref[:, f * fc:(f + 1) * fc]), 0.0)
        acc = acc + _dot((u * u).astype(BF16), w2_ref[f * fc:(f + 1) * fc, :])
    o_ref[...] = x + mod_ref[0, 5:6, :] * _rms(acc, gpost2_ref[...])


def _out_mlp_call(x2, ys, mod3, w_out, w1, w2, g_post, g_pre2, g_post2, T):
    N, D = x2.shape
    nt = T // TM
    row = lambda i: (i, 0)
    const = lambda i: (0, 0)
    return pl.pallas_call(
        _out_mlp_kernel,
        out_shape=jax.ShapeDtypeStruct((N, D), F32),
        grid=(N // TM,),
        in_specs=[pl.BlockSpec((TM, D), row)] + [pl.BlockSpec((TM, GROUP_W), row)] * 4
                 + [pl.BlockSpec((1, 6, D), lambda i: (i // nt, 0, 0)),
                    _resident((D, D), const), _resident((D, D_FF), const), _resident((D_FF, D), const),
                    _resident((1, D), const), _resident((1, D), const), _resident((1, D), const)],
        out_specs=pl.BlockSpec((TM, D), row),
        compiler_params=_cparams("arbitrary"),
        name="out_mlp",
    )(x2, *ys, mod3, w_out, w1, w2, g_post, g_pre2, g_post2)


def _relayout_w_in(w_in):
    col = lambda k: w_in[:, _IN_OFF[k]:_IN_OFF[k + 1]]
    zeros = lambda n: jnp.zeros((w_in.shape[0], n), w_in.dtype)
    parts = [col(_A_Q), col(_A_QI), col(_A_K), col(_A_KI), col(_A_V), zeros(64),
             col(_A_WI), zeros(120),
             col(_B_Q), col(_B_K), col(_B_V),
             col(_B_G), col(_C_G),
             col(_C_Q), col(_C_K), col(_C_V), col(_C_A), zeros(112),
             col(_D_Q), col(_D_K), col(_D_V)]
    return jnp.concatenate(parts, axis=1).astype(BF16)


def _rope_tables(pos, rot, period, theta):
    half = rot // 2
    inv = theta ** (-jnp.arange(half, dtype=F32) / half)
    ang = pos.astype(F32)[..., None] * inv
    cos, sin = jnp.cos(ang), jnp.sin(ang)
    pad = ang.shape[:-1] + (period - rot,)
    ct = jnp.concatenate([cos, cos, jnp.ones(pad, F32)], axis=-1)
    st = jnp.concatenate([-sin, sin, jnp.zeros(pad, F32)], axis=-1)
    reps = LANES // period
    return (jnp.tile(ct, (1, 1, reps)).reshape(-1, LANES), jnp.tile(st, (1, 1, reps)).reshape(-1, LANES))


def _layer(x2, cond, tabs, ret_tabs, layer_idx, B, T, mod_w, mod_b, attn_pre_g, attn_post_g,
           mlp_pre_g, mlp_post_g, w_in, gla_wa2, gla_ba, lam_q1, lam_k1, lam_q2, lam_k2,
           head_norm_g, w_out, mlp_w1, mlp_w2):
    D = D_MODEL
    mod3 = _mod_call(cond, mod_w, mod_b).reshape(B, 6, D)
    qa, kva, wa, qkvb, gates, cc, dqkv = _in_call(
        x2, mod3, attn_pre_g.reshape(1, D), _relayout_w_in(w_in), tabs, T)

    g_a, g_b, g_c, g_d = jnp.split(head_norm_g, 4)
    gain_t = lambda g: jnp.broadcast_to(g[:, None], (GROUP_W, TQ))
    values_t = lambda v: jnp.swapaxes(v.reshape(B, T, -1), 1, 2).reshape(-1, T)
    y_a = _dsa_call(qa, wa, kva, values_t(kva[:, LANES:LANES + HEAD_DIM]), gain_t(g_a), B, T)
    y_b = _ret_call(qkvb, gates, ret_tabs, g_b.reshape(1, GROUP_W), B, T)
    wa2_p = jnp.zeros((LANES, LANES), F32).at[:GLA_RANK].set(gla_wa2)
    y_c = _gla_call(cc, gates, wa2_p, gla_ba.reshape(1, LANES), g_c.reshape(1, GROUP_W), B, T)
    lam_init = 0.8 - 0.6 * math.exp(-0.3 * layer_idx)
    lam_vecs = jnp.zeros((4, LANES), F32).at[:, :DIFF_DK].set(jnp.stack([lam_q1, lam_k1, lam_q2, lam_k2]))
    y_d = _diff_call(dqkv, values_t(dqkv[:, 512:768]), lam_vecs, gain_t(g_d), lam_init, B, T)

    return _out_mlp_call(x2, (y_a, y_b, y_c, y_d), mod3, w_out.astype(BF16), mlp_w1.astype(BF16),
                         mlp_w2.astype(BF16), attn_post_g.reshape(1, D), mlp_pre_g.reshape(1, D),
                         mlp_post_g.reshape(1, D), T)


def kernel(x, c, positions, mod_w, mod_b, attn_pre_g, attn_post_g, mlp_pre_g, mlp_post_g, w_in,
           gla_wa2, gla_ba, lam_q1, lam_k1, lam_q2, lam_k2, head_norm_g, w_out, mlp_w1, mlp_w2):
    B, T, D = x.shape
    assert D == D_MODEL and T % TM == 0 and T % TB == 0
    tabs = (_rope_tables(positions, HEAD_DIM // ROPE_FRAC, HEAD_DIM, ROPE_THETA)
            + _rope_tables(positions, HEAD_DIM, HEAD_DIM, RET_THETA)
            + _rope_tables(positions, DIFF_DK // ROPE_FRAC, DIFF_DK, ROPE_THETA))
    ret_tabs = _ret_tables()
    x2 = x.reshape(B * T, D)
    for l in range(mod_w.shape[0]):
        x2 = _layer(x2, c, tabs, ret_tabs, l, B, T, mod_w[l], mod_b[l], attn_pre_g[l], attn_post_g[l],
                    mlp_pre_g[l], mlp_post_g[l], w_in[l], gla_wa2[l], gla_ba[l], lam_q1[l], lam_k1[l],
                    lam_q2[l], lam_k2[l], head_norm_g[l], w_out[l], mlp_w1[l], mlp_w2[l])
    return x2.reshape(B, T, D)
```

```python
import functools
import math

import numpy as np
import jax
import jax.numpy as jnp
from jax import lax
from jax.experimental import pallas as pl
from jax.experimental.pallas import tpu as pltpu

F32 = jnp.float32
BF16 = jnp.bfloat16
I32 = jnp.int32
I16 = jnp.int16

D_MODEL = 1024
CHUNK = 64
HEAD_DIM = 64
N_GH = 4
GROUP_W = N_GH * HEAD_DIM
IDX_HEADS = 8
IDX_DIM = 64
TOPK_MAX = 256
RET_THETA = 10000.0
GLA_DK = HEAD_DIM // 2
GLA_RANK = 16
GLA_TAU = 16.0
DIFF_DK = HEAD_DIM // 2
ROPE_THETA = 500000.0
ROPE_FRAC = 4
D_FF = 4 * D_MODEL
EPS = 1e-6
NEG_INF = -1e30

LANES = 128
VMEM_LIMIT = 56 * 1024 * 1024

_IN_SPLITS = (
    GROUP_W, HEAD_DIM, HEAD_DIM, IDX_HEADS * IDX_DIM, IDX_DIM, IDX_HEADS,
    GROUP_W, GROUP_W, GROUP_W, GROUP_W,
    N_GH * GLA_DK, N_GH * GLA_DK, GROUP_W, GLA_RANK, GROUP_W,
    2 * N_GH * DIFF_DK, 2 * N_GH * DIFF_DK, GROUP_W,
)
_IN_OFF = np.concatenate([[0], np.cumsum(_IN_SPLITS)]).tolist()
(_A_Q, _A_K, _A_V, _A_QI, _A_KI, _A_WI, _B_Q, _B_K, _B_V, _B_G,
 _C_Q, _C_K, _C_V, _C_A, _C_G, _D_Q, _D_K, _D_V) = range(18)

SEG_A = 0
SEG_W = 1024
SEG_B = 1152
SEG_G = 1920
SEG_C = 2432
SEG_D = 3072
W_COLS = 3840

NT_DIMS = (((1,), (1,)), ((), ()))
TN_DIMS = (((0,), (0,)), ((), ()))

TM = 512
TQ = 256
TK = 256
FLASH_SLAB = 32
LOG2E = math.log2(math.e)
TB = 256


def _float_key(v):
    bits = int(np.array(v, np.float32).view(np.int32))
    return -(bits & 0x7FFFFFFF) if bits < 0 else bits


NEG_KEY = _float_key(NEG_INF)
INT_MIN = -(2 ** 31)


def _cparams(*sem):
    return pltpu.CompilerParams(dimension_semantics=sem, vmem_limit_bytes=VMEM_LIMIT)


def _resident(shape, index_map):
    return pl.BlockSpec(shape, index_map, pipeline_mode=pl.Buffered(1))


def _dot(a, b):
    return jnp.dot(a, b, preferred_element_type=F32)


def _dot_nt(a, b):
    return lax.dot_general(a, b, NT_DIMS, preferred_element_type=F32)


def _dot_tn(a, b):
    return lax.dot_general(a, b, TN_DIMS, preferred_element_type=F32)


def _dot_f32(a, b):
    return jnp.dot(a, b, preferred_element_type=F32, precision=lax.Precision.HIGHEST)


def _lane(shape):
    return lax.broadcasted_iota(I32, shape, len(shape) - 1)


def _silu(x):
    return x / (1.0 + jnp.exp(-x))


def _mod_kernel(c_ref, w_ref, b_ref, o_ref):
    cond = _silu(c_ref[...])
    o_ref[...] = _dot_f32(cond, w_ref[...]) + b_ref[...]


def _mod_call(c, mod_w, mod_b):
    B, D = c.shape
    n = mod_w.shape[1] // D
    return pl.pallas_call(
        _mod_kernel,
        out_shape=jax.ShapeDtypeStruct((B, n * D), F32),
        grid=(n,),
        in_specs=[pl.BlockSpec((B, D), lambda j: (0, 0)),
                  pl.BlockSpec((D, D), lambda j: (0, j)),
                  pl.BlockSpec((1, D), lambda j: (0, j))],
        out_specs=pl.BlockSpec((B, D), lambda j: (0, j)),
        compiler_params=_cparams("arbitrary"),
        name="mod",
    )(c, mod_w, mod_b.reshape(1, -1))


def _rope_group(xg, cos, sin, half, period):
    lo = (_lane((1, LANES)) % period) < half
    swapped = jnp.where(lo, pltpu.roll(xg, LANES - half, 1), pltpu.roll(xg, half, 1))
    return xg * cos + swapped * sin


def _in_kernel(x_ref, mod_ref, g_ref, w_ref, ca_ref, sa_ref, cb_ref, sb_ref, cd_ref, sd_ref,
               qa_ref, kva_ref, wa_ref, qkvb_ref, gate_ref, c_ref, d_ref):
    x = x_ref[...]
    h = x * lax.rsqrt(jnp.mean(x * x, axis=-1, keepdims=True) + EPS) * g_ref[...]
    h = h * (1.0 + mod_ref[0, 1:2, :]) + mod_ref[0, 0:1, :]
    hb = h.astype(BF16)

    def seg(off, width):
        return _dot(hb, w_ref[:, off:off + width])

    lane = _lane((1, LANES))
    low = lane < HEAD_DIM

    ca, sa = ca_ref[...], sa_ref[...]
    a = seg(SEG_A, 1024)
    half_a = HEAD_DIM // ROPE_FRAC // 2
    for gi in range(2):
        r = _rope_group(a[:, gi * LANES:(gi + 1) * LANES], ca, sa, half_a, HEAD_DIM) * (HEAD_DIM ** -0.5 * LOG2E)
        rs = pltpu.roll(r, HEAD_DIM, 1)
        qa_ref[:, (2 * gi) * LANES:(2 * gi + 1) * LANES] = jnp.where(low, r, 0.0).astype(BF16)
        qa_ref[:, (2 * gi + 1) * LANES:(2 * gi + 2) * LANES] = jnp.where(low, rs, 0.0).astype(BF16)
    for gi in range(4):
        r = _rope_group(a[:, (2 + gi) * LANES:(3 + gi) * LANES], ca, sa, half_a, HEAD_DIM)
        rs = pltpu.roll(r, HEAD_DIM, 1)
        qa_ref[:, (4 + 2 * gi) * LANES:(5 + 2 * gi) * LANES] = jnp.where(low, 0.0, rs).astype(BF16)
        qa_ref[:, (5 + 2 * gi) * LANES:(6 + 2 * gi) * LANES] = jnp.where(low, 0.0, r).astype(BF16)
    kva_ref[:, 0:LANES] = _rope_group(a[:, 6 * LANES:7 * LANES], ca, sa, half_a, HEAD_DIM).astype(BF16)
    kva_ref[:, LANES:2 * LANES] = a[:, 7 * LANES:8 * LANES].astype(BF16)

    wa_ref[...] = seg(SEG_W, LANES)

    cb, sb = cb_ref[...], sb_ref[...]
    b = seg(SEG_B, 768)
    for gi in range(4):
        r = _rope_group(b[:, gi * LANES:(gi + 1) * LANES], cb, sb, HEAD_DIM // 2, HEAD_DIM)
        if gi < 2:
            r = r * (HEAD_DIM ** -0.5)
        qkvb_ref[:, gi * LANES:(gi + 1) * LANES] = r.astype(BF16)
    qkvb_ref[:, 512:768] = b[:, 512:768].astype(BF16)

    gate_ref[...] = seg(SEG_G, 512)
    c_ref[...] = seg(SEG_C, 640)

    cd, sd = cd_ref[...], sd_ref[...]
    d = seg(SEG_D, 768)
    half_d = DIFF_DK // ROPE_FRAC // 2
    for gi in range(4):
        r = _rope_group(d[:, gi * LANES:(gi + 1) * LANES], cd, sd, half_d, DIFF_DK)
        if gi < 2:
            r = r * (DIFF_DK ** -0.5 * LOG2E)
        d_ref[:, gi * LANES:(gi + 1) * LANES] = r.astype(BF16)
    d_ref[:, 512:768] = d[:, 512:768].astype(BF16)


def _in_call(x2, mod3, g, w_p, tabs, T):
    N, D = x2.shape
    nt = T // TM
    row = lambda i: (i, 0)
    tab_spec = pl.BlockSpec((TM, LANES), row)
    outs = [(1536, BF16), (256, BF16), (LANES, F32), (768, BF16), (512, F32), (640, F32), (768, BF16)]
    return pl.pallas_call(
        _in_kernel,
        out_shape=[jax.ShapeDtypeStruct((N, w), dt) for w, dt in outs],
        grid=(N // TM,),
        in_specs=[pl.BlockSpec((TM, D), row),
                  pl.BlockSpec((1, 6, D), lambda i: (i // nt, 0, 0)),
                  _resident((1, D), lambda i: (0, 0)),
                  _resident((D, W_COLS), lambda i: (0, 0))] + [tab_spec] * 6,
        out_specs=[pl.BlockSpec((TM, w), row) for w, _ in outs],
        compiler_params=_cparams("arbitrary"),
        name="in_proj",
    )(x2, mod3, g, w_p, *tabs)


def _pair_head_norm(o, gain, center):
    low = _lane((1, LANES)) < HEAD_DIM
    inv = 1.0 / HEAD_DIM

    def seg_mean(v):
        m0 = jnp.sum(jnp.where(low, v, 0.0), axis=1, keepdims=True)
        m1 = jnp.sum(jnp.where(low, 0.0, v), axis=1, keepdims=True)
        return jnp.where(low, m0, m1) * inv

    if center:
        o = o - seg_mean(o)
    return o * lax.rsqrt(seg_mean(o * o) + EPS) * gain


class _Flash:
    def __init__(self, n, s_ref, p_ref, acc_ref, st_ref):
        self.n, self.s, self.p, self.acc, self.st = n, s_ref, p_ref, acc_ref, st_ref

    @staticmethod
    def scratch(n):
        return [pltpu.VMEM((n, TK, TQ), F32), pltpu.VMEM((n, TK, TQ), BF16),
                pltpu.VMEM((n, HEAD_DIM, TQ), F32), pltpu.VMEM((4, n, TQ), F32)]

    def _row(self, k, i):
        return self.st.at[k, i:i + 1, :]

    def init(self):
        self.st[0] = jnp.full((self.n, TQ), NEG_INF, F32)
        self.st[1] = jnp.zeros((self.n, TQ), F32)
        self.acc[...] = jnp.zeros_like(self.acc)

    def scores(self, i, s):
        self.s[i] = s
        self._row(2, i)[...] = jnp.max(_fold8(s, jnp.maximum), axis=0, keepdims=True)

    def probs(self, i):
        m_old = self._row(0, i)[...]
        m_new = jnp.maximum(m_old, self._row(2, i)[...])
        alpha = jnp.exp2(m_old - m_new)
        psum = jnp.zeros((8, TQ), F32)
        for r in range(0, TK, FLASH_SLAB):
            p = jnp.exp2(self.s[i, r:r + FLASH_SLAB, :] - m_new)
            psum = psum + _fold8(p)
            self.p[i, r:r + FLASH_SLAB, :] = p.astype(BF16)
        self._row(1, i)[...] = alpha * self._row(1, i)[...] + jnp.sum(psum, axis=0, keepdims=True)
        self._row(0, i)[...] = m_new
        self._row(3, i)[...] = alpha

    def values(self, i, vt):
        self.acc[i] = self._row(3, i)[...] * self.acc[i] + _dot(vt, self.p[i])

    def result(self, i):
        return self.acc[i] / self._row(1, i)[...]

    def pipeline(self, first, n_rest, score, value):
        score(first, True)

        def body(t, carry):
            for i in range(self.n):
                self.probs(i)
            score(t, False)
            value(jnp.where(t == 0, first, t - 1))
            return carry

        lax.fori_loop(0, n_rest, body, 0)
        for i in range(self.n):
            self.probs(i)
        value(jnp.where(n_rest == 0, first, n_rest - 1))


def _fold8(x, op=jnp.add, rows=8):
    parts = [x[r:r + rows] for r in range(0, x.shape[0], rows)]
    while len(parts) > 1:
        parts = [op(a, b) for a, b in zip(parts[0::2], parts[1::2])]
    return parts[0]


def _head_norm_t(o, gain):
    ms = jnp.mean(o * o, axis=0, keepdims=True)
    return o * lax.rsqrt(ms + EPS) * gain


def _dsa_kernel(q_ref, w_ref, kv_ref, vt_ref, g_ref, o_ref, bias_ref, key_ref, hi_ref, lo_ref, *flash_refs, topk):
    q0 = pl.program_id(1) * TQ
    nkt = (q0 + TQ) // TK
    qchunk = (q0 + _lane((1, TQ))) // CHUNK
    wts = jnp.transpose(w_ref[...]) * (IDX_HEADS ** -0.5 * IDX_DIM ** -0.5)

    def score_tile(kt, carry):
        k0 = pl.multiple_of(kt * TK, TK)
        kk = kv_ref[pl.ds(k0, TK), 0:LANES]
        tile = bias_ref.at[pl.ds(k0, TK), :]
        for h in range(IDX_HEADS):
            term = jnp.maximum(_dot_nt(kk, q_ref[:, (4 + h) * LANES:(5 + h) * LANES]), 0.0) * wts[h:h + 1, :]
            if h == 0:
                tile[...] = term
            elif h < IDX_HEADS - 1:
                tile[...] += term
        kchunk = (k0 + lax.broadcasted_iota(I32, (TK, 1), 0)) // CHUNK
        acc = jnp.where(kchunk <= qchunk, tile[...] + term, NEG_INF)
        bits = pltpu.bitcast(acc, I32)
        sign = bits >> 31
        key = ((bits & 0x7FFFFFFF) ^ sign) - sign
        key_ref[pl.ds(k0, TK), :] = key
        hi_ref[pl.ds(k0, TK), :] = (key >> 16).astype(I16)
        lo_ref[pl.ds(k0, TK), :] = (key ^ 0x8000).astype(I16)
        return carry

    lax.fori_loop(0, nkt, score_tile, 0)

    kf = float(topk)

    def count16(ref, cand):
        def body(kt, acc):
            k0 = pl.multiple_of(kt * TK, TK)
            hit = jnp.where(ref[pl.ds(k0, TK), :] >= cand, jnp.ones((TK, TQ), I16), jnp.zeros((TK, TQ), I16))
            return acc + _fold8(hit, rows=16)
        acc = lax.fori_loop(0, nkt, body, jnp.zeros((16, TQ), I16))
        return jnp.sum(acc.astype(F32), axis=0, keepdims=True)

    def search16(ref, cnt_all):
        c0 = count16(ref, jnp.zeros((1, TQ), I16))
        t = jnp.where(c0 >= kf, 0, -32768).astype(I32)
        cnt = jnp.where(c0 >= kf, c0, cnt_all)

        def step(it, carry):
            t, cnt = carry
            cand = t + (jnp.int32(1) << (14 - it))
            c = count16(ref, cand.astype(I16))
            ok = c >= kf
            return jnp.where(ok, cand, t), jnp.where(ok, c, cnt)

        return lax.fori_loop(0, 15, step, (t, cnt))

    def count(pred):
        def body(kt, acc):
            k0 = pl.multiple_of(kt * TK, TK)
            hit = jnp.where(pred(key_ref[pl.ds(k0, TK), :], k0), 1.0, 0.0)
            return acc + _fold8(hit)
        acc = lax.fori_loop(0, nkt, body, jnp.zeros((8, TQ), F32))
        return jnp.sum(acc, axis=0, keepdims=True)

    t_hi, cnt_hi = search16(hi_ref, (nkt * TK).astype(F32))
    t_hi16 = t_hi.astype(I16)

    def low_digit_tile(kt, carry):
        k0 = pl.multiple_of(kt * TK, TK)
        hi = hi_ref[pl.ds(k0, TK), :]
        in_bucket = jnp.where(hi == t_hi16, lo_ref[pl.ds(k0, TK), :], jnp.full((TK, TQ), -32768, I16))
        hi_ref[pl.ds(k0, TK), :] = jnp.where(hi > t_hi16, jnp.full((TK, TQ), 32767, I16), in_bucket)
        return carry

    lax.fori_loop(0, nkt, low_digit_tile, 0)
    t_lo, cnt = search16(hi_ref, cnt_hi)
    thr = t_hi * 65536 + (t_lo + 32768)
    floor_key = jnp.int32(NEG_KEY + 1)
    tied = jnp.max(jnp.where((cnt > kf) & (thr >= floor_key), 1.0, 0.0)) > 0.0

    @pl.when(jnp.logical_not(tied))
    def _():
        lim = jnp.maximum(thr, floor_key)

        def body(kt, carry):
            k0 = pl.multiple_of(kt * TK, TK)
            bias_ref[pl.ds(k0, TK), :] = jnp.where(key_ref[pl.ds(k0, TK), :] >= lim, 0.0, NEG_INF)
            return carry
        lax.fori_loop(0, nkt, body, 0)

    @pl.when(tied)
    def _():
        need = kf - count(lambda keys, k0: keys > thr)

        def idx(k0):
            return k0 + lax.broadcasted_iota(I32, (TK, 1), 0)

        def search_idx(it, lim):
            cand = lim + (jnp.int32(1) << (14 - it))
            c = count(lambda keys, k0: (keys == thr) & (idx(k0) < cand))
            return jnp.where(c <= need, cand, lim)

        lim = lax.fori_loop(0, 15, search_idx, jnp.zeros((1, TQ), I32))

        def body(kt, carry):
            k0 = pl.multiple_of(kt * TK, TK)
            keys = key_ref[pl.ds(k0, TK), :]
            sel = ((keys > thr) | ((keys == thr) & (idx(k0) < lim))) & (keys >= floor_key)
            bias_ref[pl.ds(k0, TK), :] = jnp.where(sel, 0.0, NEG_INF)
            return carry
        lax.fori_loop(0, nkt, body, 0)

    flash = _Flash(N_GH, *flash_refs)
    flash.init()

    def score(kt, is_first):
        k0 = pl.multiple_of(kt * TK, TK)
        kk = kv_ref[pl.ds(k0, TK), 0:LANES]
        bias = bias_ref[pl.ds(k0, TK), :]
        for h in range(N_GH):
            flash.scores(h, _dot_nt(kk, q_ref[:, h * LANES:(h + 1) * LANES]) + bias)

    def value(kt):
        k0 = pl.multiple_of(kt * TK, TK)
        for h in range(N_GH):
            flash.values(h, vt_ref[:, pl.ds(k0, TK)])

    flash.pipeline(nkt - 1, nkt - 1, score, value)
    y = jnp.concatenate([_head_norm_t(flash.result(h), g_ref[h * HEAD_DIM:(h + 1) * HEAD_DIM, :])
                         for h in range(N_GH)], axis=0)
    o_ref[...] = jnp.transpose(y).astype(BF16)


def _dsa_call(qa, wa, kva, vta, g_t, B, T):
    nq = T // TQ
    topk = min(TOPK_MAX, T // 4)
    return pl.pallas_call(
        functools.partial(_dsa_kernel, topk=topk),
        out_shape=jax.ShapeDtypeStruct((B * T, GROUP_W), BF16),
        grid=(B, nq),
        in_specs=[pl.BlockSpec((TQ, 1536), lambda b, i: (b * nq + i, 0)),
                  pl.BlockSpec((TQ, LANES), lambda b, i: (b * nq + i, 0)),
                  pl.BlockSpec((T, 256), lambda b, i: (b, 0)),
                  pl.BlockSpec((HEAD_DIM, T), lambda b, i: (b, 0)),
                  pl.BlockSpec((GROUP_W, TQ), lambda b, i: (0, 0))],
        out_specs=pl.BlockSpec((TQ, GROUP_W), lambda b, i: (b * nq + i, 0)),
        scratch_shapes=[pltpu.VMEM((T, TQ), F32), pltpu.VMEM((T, TQ), I32), pltpu.VMEM((T, TQ), I16),
                        pltpu.VMEM((T, TQ), I16)] + _Flash.scratch(N_GH),
        compiler_params=_cparams("arbitrary", "arbitrary"),
        name="dsa",
    )(qa, wa, kva, vta, g_t)


def _diff_kernel(q_ref, k_ref, vt_ref, lam_ref, g_ref, o_ref, qm_ref, *flash_refs, lam_init):
    q0 = pl.program_id(1) * TQ
    nfull = (q0 + CHUNK) // TK
    qchunk = (q0 + _lane((1, TQ))) // CHUNK
    lv = lam_ref[...]
    lam = (jnp.exp(jnp.sum(lv[0:1] * lv[1:2], axis=1, keepdims=True))
           - jnp.exp(jnp.sum(lv[2:3] * lv[3:4], axis=1, keepdims=True)) + lam_init)
    lane = _lane((1, LANES))
    n_maps = 2 * N_GH
    for mi in range(n_maps):
        qg = q_ref[:, (mi // 4) * LANES:(mi // 4 + 1) * LANES]
        qm_ref[mi] = jnp.where((lane // DIFF_DK) == mi % 4, qg, jnp.zeros_like(qg))

    flash = _Flash(n_maps, *flash_refs)
    flash.init()

    def score(kt, masked):
        k0 = pl.multiple_of(kt * TK, TK)
        if masked:
            kchunk = (k0 + lax.broadcasted_iota(I32, (TK, 1), 0)) // CHUNK
            admissible = kchunk <= qchunk
        for mi in range(n_maps):
            g = mi // 4
            s = _dot_nt(k_ref[pl.ds(k0, TK), g * LANES:(g + 1) * LANES], qm_ref[mi])
            flash.scores(mi, jnp.where(admissible, s, NEG_INF) if masked else s)

    def value(kt):
        k0 = pl.multiple_of(kt * TK, TK)
        for mi in range(n_maps):
            h = mi // 2
            flash.values(mi, vt_ref[h * HEAD_DIM:(h + 1) * HEAD_DIM, pl.ds(k0, TK)])

    flash.pipeline(nfull, nfull, score, value)
    outs = []
    for h in range(N_GH):
        o = flash.result(2 * h) - lam * flash.result(2 * h + 1)
        outs.append(_head_norm_t(o, g_ref[h * HEAD_DIM:(h + 1) * HEAD_DIM, :]) * (1.0 - lam_init))
    o_ref[...] = jnp.transpose(jnp.concatenate(outs, axis=0)).astype(BF16)


def _diff_call(dqkv, vtd, lam_vecs, g_t, lam_init, B, T):
    nq = T // TQ
    return pl.pallas_call(
        functools.partial(_diff_kernel, lam_init=lam_init),
        out_shape=jax.ShapeDtypeStruct((B * T, GROUP_W), BF16),
        grid=(B, nq),
        in_specs=[pl.BlockSpec((TQ, 256), lambda b, i: (b * nq + i, 0)),
                  pl.BlockSpec((T, 256), lambda b, i: (b, 1)),
                  pl.BlockSpec((GROUP_W, T), lambda b, i: (b, 0)),
                  pl.BlockSpec((4, LANES), lambda b, i: (0, 0)),
                  pl.BlockSpec((GROUP_W, TQ), lambda b, i: (0, 0))],
        out_specs=pl.BlockSpec((TQ, GROUP_W), lambda b, i: (b * nq + i, 0)),
        scratch_shapes=[pltpu.VMEM((2 * N_GH, TQ, LANES), BF16)] + _Flash.scratch(2 * N_GH),
        compiler_params=_cparams("arbitrary", "arbitrary"),
        name="diff",
    )(dqkv, dqkv, vtd, lam_vecs, g_t)


def _ret_tables():
    h = np.arange(N_GH, dtype=np.float32)
    log_g = jnp.log(1.0 - 2.0 ** (-5.0 - jnp.asarray(h)))
    t = np.arange(TB)
    same = (t[:, None] // CHUNK) == (t[None, :] // CHUNK)
    earlier = (t[None, :] // CHUNK) < (t[:, None] // CHUNK)
    dist = jnp.asarray(np.where(same, np.abs(t[:, None] - t[None, :]), t[:, None] - t[None, :]), F32)
    dmat = jnp.where(jnp.asarray(same | earlier)[None],
                     jnp.exp(dist[None] * log_g[:, None, None]), 0.0)
    tt = jnp.asarray(t, F32)
    xi = jnp.exp((tt + 1.0)[None, :] * log_g[:, None])
    zeta = jnp.exp((TB - 1.0 - tt)[None, :] * log_g[:, None])
    gblk = jnp.exp(TB * log_g)

    def lanes(tab):
        return jnp.repeat(tab.reshape(2, 2, TB), HEAD_DIM, axis=1).transpose(0, 2, 1)

    r = np.arange(LANES)
    blockdiag = (r[:, None] // HEAD_DIM) == (r[None, :] // HEAD_DIM)
    gb = jnp.where(jnp.asarray(blockdiag)[None],
                   jnp.repeat(gblk.reshape(2, 2), HEAD_DIM, axis=1)[:, :, None], 0.0)
    return dmat, lanes(xi), lanes(zeta), gb


def _ret_kernel(qkv_ref, gate_ref, dmat_ref, xi_ref, zeta_ref, gb_ref, g_ref, o_ref, r_ref):
    @pl.when(pl.program_id(1) == 0)
    def _():
        r_ref[...] = jnp.zeros_like(r_ref)

    lane = _lane((1, LANES))
    low = lane < HEAD_DIM
    r_idx = lax.broadcasted_iota(I32, (LANES, LANES), 0) // HEAD_DIM
    c_idx = lax.broadcasted_iota(I32, (LANES, LANES), 1) // HEAD_DIM
    for g in range(2):
        qg = qkv_ref[:, g * LANES:(g + 1) * LANES]
        kg = qkv_ref[:, 256 + g * LANES:256 + (g + 1) * LANES]
        vg = qkv_ref[:, 512 + g * LANES:512 + (g + 1) * LANES]
        state = r_ref[g]
        inter = _dot(qg, state.astype(BF16)) * xi_ref[g]
        parts = []
        for hh in range(2):
            qm = jnp.where(low, qg, jnp.zeros_like(qg)) if hh == 0 else jnp.where(low, jnp.zeros_like(qg), qg)
            s = _dot_nt(qm, kg) * dmat_ref[2 * g + hh]
            parts.append(_dot(s.astype(BF16), vg))
        o = jnp.where(low, parts[0], parts[1]) + inter
        kz = (kg.astype(F32) * zeta_ref[g]).astype(BF16)
        upd = jnp.where(r_idx == c_idx, _dot_tn(kz, vg), 0.0)
        r_ref[g] = state * gb_ref[g] + upd
        y = _pair_head_norm(o, g_ref[:, g * LANES:(g + 1) * LANES], True)
        o_ref[:, g * LANES:(g + 1) * LANES] = (_silu(gate_ref[:, g * LANES:(g + 1) * LANES]) * y).astype(BF16)


def _ret_call(qkvb, gates, tabs, g_b, B, T):
    nb = T // TB
    dmat, xi, zeta, gb = tabs
    const3 = lambda b, i: (0, 0, 0)
    return pl.pallas_call(
        _ret_kernel,
        out_shape=jax.ShapeDtypeStruct((B * T, GROUP_W), BF16),
        grid=(B, nb),
        in_specs=[pl.BlockSpec((TB, 768), lambda b, i: (b * nb + i, 0)),
                  pl.BlockSpec((TB, GROUP_W), lambda b, i: (b * nb + i, 0)),
                  pl.BlockSpec((N_GH, TB, TB), const3),
                  pl.BlockSpec((2, TB, LANES), const3),
                  pl.BlockSpec((2, TB, LANES), const3),
                  pl.BlockSpec((2, LANES, LANES), const3),
                  pl.BlockSpec((1, GROUP_W), lambda b, i: (0, 0))],
        out_specs=pl.BlockSpec((TB, GROUP_W), lambda b, i: (b * nb + i, 0)),
        scratch_shapes=[pltpu.VMEM((2, LANES, LANES), F32)],
        compiler_params=_cparams("arbitrary", "arbitrary"),
        name="ret",
    )(qkvb, gates, dmat, xi, zeta, gb, g_b)


def _gla_kernel(c_ref, gate_ref, wa2_ref, ba_ref, g_ref, o_ref, s_ref, b_scr, k_scr, v_scr):
    @pl.when(pl.program_id(1) == 0)
    def _():
        s_ref[...] = jnp.zeros_like(s_ref)

    hmask = (lax.broadcasted_iota(I32, (LANES, GROUP_W), 0) // GLA_DK
             == lax.broadcasted_iota(I32, (LANES, GROUP_W), 1) // HEAD_DIM)
    hsum = jnp.where(hmask, 1.0, 0.0).astype(BF16)
    tri = jnp.where(lax.broadcasted_iota(I32, (CHUNK, CHUNK), 0)
                    >= lax.broadcasted_iota(I32, (CHUNK, CHUNK), 1), 1.0, 0.0)

    z = _dot_f32(c_ref[:, 512:640], wa2_ref[...]) + ba_ref[...]
    log_a = (jnp.minimum(z, 0.0) - jnp.log(1.0 + jnp.exp(-jnp.abs(z)))) * (1.0 / GLA_TAU)
    outs = []
    for c in range(TB // CHUNK):
        rows = slice(c * CHUNK, (c + 1) * CHUNK)
        b = _dot_f32(tri, log_a[rows])
        q = c_ref[rows, 0:128] * (GLA_DK ** -0.5)
        k = c_ref[rows, 128:256]
        v = c_ref[rows, 256:512]
        state = s_ref[...]
        inter = _dot((q * jnp.exp(b)).astype(BF16), state.astype(BF16))
        b_scr[...] = b
        k_scr[...] = k
        v_scr[...] = v

        def key_step(s, acc, b=b, q=q):
            decay = jnp.exp(-jnp.abs(b - b_scr[pl.ds(s, 1), :]))
            w = (q * decay * k_scr[pl.ds(s, 1), :]).astype(BF16)
            return acc + _dot(w, hsum) * v_scr[pl.ds(s, 1), :]

        intra = lax.fori_loop(0, CHUNK, key_step, jnp.zeros((CHUNK, GROUP_W), F32), unroll=16)
        outs.append(intra + inter)
        b_last = b[CHUNK - 1:CHUNK, :]
        kd = (k * jnp.exp(b_last - b)).astype(BF16)
        upd = _dot_tn(kd, v.astype(BF16))
        scale = jnp.transpose(jnp.broadcast_to(jnp.exp(b_last), (LANES, LANES)))
        s_ref[...] = jnp.where(hmask, state * jnp.concatenate([scale, scale], axis=1) + upd, 0.0)
    o = jnp.concatenate(outs, axis=0)
    for gi in range(2):
        y = _pair_head_norm(o[:, gi * LANES:(gi + 1) * LANES], g_ref[:, gi * LANES:(gi + 1) * LANES], True)
        o_ref[:, gi * LANES:(gi + 1) * LANES] = (_silu(gate_ref[:, gi * LANES:(gi + 1) * LANES]) * y).astype(BF16)


def _gla_call(cc, gates, wa2_p, ba, g_c, B, T):
    nb = T // TB
    return pl.pallas_call(
        _gla_kernel,
        out_shape=jax.ShapeDtypeStruct((B * T, GROUP_W), BF16),
        grid=(B, nb),
        in_specs=[pl.BlockSpec((TB, 640), lambda b, i: (b * nb + i, 0)),
                  pl.BlockSpec((TB, GROUP_W), lambda b, i: (b * nb + i, 1)),
                  pl.BlockSpec((LANES, LANES), lambda b, i: (0, 0)),
                  pl.BlockSpec((1, LANES), lambda b, i: (0, 0)),
                  pl.BlockSpec((1, GROUP_W), lambda b, i: (0, 0))],
        out_specs=pl.BlockSpec((TB, GROUP_W), lambda b, i: (b * nb + i, 0)),
        scratch_shapes=[pltpu.VMEM((LANES, GROUP_W), F32), pltpu.VMEM((CHUNK, LANES), F32),
                        pltpu.VMEM((CHUNK, LANES), F32), pltpu.VMEM((CHUNK, GROUP_W), F32)],
        compiler_params=_cparams("arbitrary", "arbitrary"),
        name="gla",
    )(cc, gates, wa2_p, ba, g_c)


def _rms(y, g):
    return y * lax.rsqrt(jnp.mean(y * y, axis=-1, keepdims=True) + EPS) * g


def _out_mlp_kernel(x_ref, ya_ref, yb_ref, yc_ref, yd_ref, mod_ref, wo_ref, w1_ref, w2_ref,
                    gpost_ref, gpre_ref, gpost2_ref, o_ref):
    y = _dot(ya_ref[...], wo_ref[0:256, :])
    y = y + _dot(yb_ref[...], wo_ref[256:512, :])
    y = y + _dot(yc_ref[...], wo_ref[512:768, :])
    y = y + _dot(yd_ref[...], wo_ref[768:1024, :])
    x = x_ref[...] + mod_ref[0, 2:3, :] * _rms(y, gpost_ref[...])
    h = _rms(x, gpre_ref[...]) * (1.0 + mod_ref[0, 4:5, :]) + mod_ref[0, 3:4, :]
    hb = h.astype(BF16)
    acc = jnp.zeros(x.shape, F32)
    fc = 1024
    for f in range(D_FF // fc):
        u = jnp.maximum(_dot(hb, w1_ref[:, f * fc:(f + 1) * fc]), 0.0)
        acc = acc + _dot((u * u).astype(BF16), w2_ref[f * fc:(f + 1) * fc, :])
    o_ref[...] = x + mod_ref[0, 5:6, :] * _rms(acc, gpost2_ref[...])


def _out_mlp_call(x2, ys, mod3, w_out, w1, w2, g_post, g_pre2, g_post2, T):
    N, D = x2.shape
    nt = T // TM
    row = lambda i: (i, 0)
    const = lambda i: (0, 0)
    return pl.pallas_call(
        _out_mlp_kernel,
        out_shape=jax.ShapeDtypeStruct((N, D), F32),
        grid=(N // TM,),
        in_specs=[pl.BlockSpec((TM, D), row)] + [pl.BlockSpec((TM, GROUP_W), row)] * 4
                 + [pl.BlockSpec((1, 6, D), lambda i: (i // nt, 0, 0)),
                    _resident((D, D), const), _resident((D, D_FF), const), _resident((D_FF, D), const),
                    _resident((1, D), const), _resident((1, D), const), _resident((1, D), const)],
        out_specs=pl.BlockSpec((TM, D), row),
        compiler_params=_cparams("arbitrary"),
        name="out_mlp",
    )(x2, *ys, mod3, w_out, w1, w2, g_post, g_pre2, g_post2)


def _relayout_w_in(w_in):
    col = lambda k: w_in[:, _IN_OFF[k]:_IN_OFF[k + 1]]
    zeros = lambda n: jnp.zeros((w_in.shape[0], n), w_in.dtype)
    parts = [col(_A_Q), col(_A_QI), col(_A_K), col(_A_KI), col(_A_V), zeros(64),
             col(_A_WI), zeros(120),
             col(_B_Q), col(_B_K), col(_B_V),
             col(_B_G), col(_C_G),
             col(_C_Q), col(_C_K), col(_C_V), col(_C_A), zeros(112),
             col(_D_Q), col(_D_K), col(_D_V)]
    return jnp.concatenate(parts, axis=1).astype(BF16)


def _rope_tables(pos, rot, period, theta):
    half = rot // 2
    inv = theta ** (-jnp.arange(half, dtype=F32) / half)
    ang = pos.astype(F32)[..., None] * inv
    cos, sin = jnp.cos(ang), jnp.sin(ang)
    pad = ang.shape[:-1] + (period - rot,)
    ct = jnp.concatenate([cos, cos, jnp.ones(pad, F32)], axis=-1)
    st = jnp.concatenate([-sin, sin, jnp.zeros(pad, F32)], axis=-1)
    reps = LANES // period
    return (jnp.tile(ct, (1, 1, reps)).reshape(-1, LANES), jnp.tile(st, (1, 1, reps)).reshape(-1, LANES))


def _layer(x2, cond, tabs, ret_tabs, layer_idx, B, T, mod_w, mod_b, attn_pre_g, attn_post_g,
           mlp_pre_g, mlp_post_g, w_in, gla_wa2, gla_ba, lam_q1, lam_k1, lam_q2, lam_k2,
           head_norm_g, w_out, mlp_w1, mlp_w2):
    D = D_MODEL
    mod3 = _mod_call(cond, mod_w, mod_b).reshape(B, 6, D)
    qa, kva, wa, qkvb, gates, cc, dqkv = _in_call(
        x2, mod3, attn_pre_g.reshape(1, D), _relayout_w_in(w_in), tabs, T)

    g_a, g_b, g_c, g_d = jnp.split(head_norm_g, 4)
    gain_t = lambda g: jnp.broadcast_to(g[:, None], (GROUP_W, TQ))
    values_t = lambda v: jnp.swapaxes(v.reshape(B, T, -1), 1, 2).reshape(-1, T)
    y_a = _dsa_call(qa, wa, kva, values_t(kva[:, LANES:LANES + HEAD_DIM]), gain_t(g_a), B, T)
    y_b = _ret_call(qkvb, gates, ret_tabs, g_b.reshape(1, GROUP_W), B, T)
    wa2_p = jnp.zeros((LANES, LANES), F32).at[:GLA_RANK].set(gla_wa2)
    y_c = _gla_call(cc, gates, wa2_p, gla_ba.reshape(1, LANES), g_c.reshape(1, GROUP_W), B, T)
    lam_init = 0.8 - 0.6 * math.exp(-0.3 * layer_idx)
    lam_vecs = jnp.zeros((4, LANES), F32).at[:, :DIFF_DK].set(jnp.stack([lam_q1, lam_k1, lam_q2, lam_k2]))
    y_d = _diff_call(dqkv, values_t(dqkv[:, 512:768]), lam_vecs, gain_t(g_d), lam_init, B, T)

    return _out_mlp_call(x2, (y_a, y_b, y_c, y_d), mod3, w_out.astype(BF16), mlp_w1.astype(BF16),
                         mlp_w2.astype(BF16), attn_post_g.reshape(1, D), mlp_pre_g.reshape(1, D),
                         mlp_post_g.reshape(1, D), T)


def kernel(x, c, positions, mod_w, mod_b, attn_pre_g, attn_post_g, mlp_pre_g, mlp_post_g, w_in,
           gla_wa2, gla_ba, lam_q1, lam_k1, lam_q2, lam_k2, head_norm_g, w_out, mlp_w1, mlp_w2):
    B, T, D = x.shape
    assert D == D_MODEL and T % TM == 0 and T % TB == 0
    tabs = (_rope_tables(positions, HEAD_DIM // ROPE_FRAC, HEAD_DIM, ROPE_THETA)
            + _rope_tables(positions, HEAD_DIM, HEAD_DIM, RET_THETA)
            + _rope_tables(positions, DIFF_DK // ROPE_FRAC, DIFF_DK, ROPE_THETA))
    ret_tabs = _ret_tables()
    x2 = x.reshape(B * T, D)
    for l in range(mod_w.shape[0]):
        x2 = _layer(x2, c, tabs, ret_tabs, l, B, T, mod_w[l], mod_b[l], attn_pre_g[l], attn_post_g[l],
                    mlp_pre_g[l], mlp_post_g[l], w_in[l], gla_wa2[l], gla_ba[l], lam_q1[l], lam_k1[l],
                    lam_q2[l], lam_k2[l], head_norm_g[l], w_out[l], mlp_w1[l], mlp_w2[l])
    return x2.reshape(B, T, D)
```

```python
import functools
import math

import numpy as np
import jax
import jax.numpy as jnp
from jax import lax
from jax.experimental import pallas as pl
from jax.experimental.pallas import tpu as pltpu

F32 = jnp.float32
BF16 = jnp.bfloat16
I32 = jnp.int32
I16 = jnp.int16

D_MODEL = 1024
CHUNK = 64
HEAD_DIM = 64
N_GH = 4
GROUP_W = N_GH * HEAD_DIM
IDX_HEADS = 8
IDX_DIM = 64
TOPK_MAX = 256
RET_THETA = 10000.0
GLA_DK = HEAD_DIM // 2
GLA_RANK = 16
GLA_TAU = 16.0
DIFF_DK = HEAD_DIM // 2
ROPE_THETA = 500000.0
ROPE_FRAC = 4
D_FF = 4 * D_MODEL
EPS = 1e-6
NEG_INF = -1e30

LANES = 128
VMEM_LIMIT = 56 * 1024 * 1024

_IN_SPLITS = (
    GROUP_W, HEAD_DIM, HEAD_DIM, IDX_HEADS * IDX_DIM, IDX_DIM, IDX_HEADS,
    GROUP_W, GROUP_W, GROUP_W, GROUP_W,
    N_GH * GLA_DK, N_GH * GLA_DK, GROUP_W, GLA_RANK, GROUP_W,
    2 * N_GH * DIFF_DK, 2 * N_GH * DIFF_DK, GROUP_W,
)
_IN_OFF = np.concatenate([[0], np.cumsum(_IN_SPLITS)]).tolist()
(_A_Q, _A_K, _A_V, _A_QI, _A_KI, _A_WI, _B_Q, _B_K, _B_V, _B_G,
 _C_Q, _C_K, _C_V, _C_A, _C_G, _D_Q, _D_K, _D_V) = range(18)

SEG_A = 0
SEG_W = 1024
SEG_B = 1152
SEG_G = 1920
SEG_C = 2432
SEG_D = 3072
W_COLS = 3840

NT_DIMS = (((1,), (1,)), ((), ()))
TN_DIMS = (((0,), (0,)), ((), ()))

TM = 512
TQ = 256
TK = 256
FLASH_SLAB = 32
VT_ROWS = HEAD_DIM + 16
LOG2E = math.log2(math.e)
TB = 256


def _float_key(v):
    bits = int(np.array(v, np.float32).view(np.int32))
    return -(bits & 0x7FFFFFFF) if bits < 0 else bits


NEG_KEY = _float_key(NEG_INF)
INT_MIN = -(2 ** 31)


def _cparams(*sem):
    return pltpu.CompilerParams(dimension_semantics=sem, vmem_limit_bytes=VMEM_LIMIT)


def _resident(shape, index_map):
    return pl.BlockSpec(shape, index_map, pipeline_mode=pl.Buffered(1))


def _dot(a, b):
    return jnp.dot(a, b, preferred_element_type=F32)


def _dot_nt(a, b):
    return lax.dot_general(a, b, NT_DIMS, preferred_element_type=F32)


def _dot_tn(a, b):
    return lax.dot_general(a, b, TN_DIMS, preferred_element_type=F32)


def _dot_f32(a, b):
    return jnp.dot(a, b, preferred_element_type=F32, precision=lax.Precision.HIGHEST)


def _split_bf16(x, n):
    parts = []
    for _ in range(n):
        p = x.astype(BF16)
        parts.append(p)
        x = x - p.astype(F32)
    return parts


def _dot_split(a, b):
    (a0, a1), (b0, b1) = _split_bf16(a, 2), _split_bf16(b, 2)
    return _dot(a0, b0) + (_dot(a0, b1) + _dot(a1, b0))


def _lane(shape):
    return lax.broadcasted_iota(I32, shape, len(shape) - 1)


def _silu(x):
    return x / (1.0 + jnp.exp(-x))


def _mod_kernel(c_ref, w_ref, b_ref, o_ref):
    cond = _silu(c_ref[...])
    o_ref[...] = _dot_f32(cond, w_ref[...]) + b_ref[...]


def _mod_call(c, mod_w, mod_b):
    B, D = c.shape
    n = mod_w.shape[1] // D
    return pl.pallas_call(
        _mod_kernel,
        out_shape=jax.ShapeDtypeStruct((B, n * D), F32),
        grid=(n,),
        in_specs=[pl.BlockSpec((B, D), lambda j: (0, 0)),
                  pl.BlockSpec((D, D), lambda j: (0, j)),
                  pl.BlockSpec((1, D), lambda j: (0, j))],
        out_specs=pl.BlockSpec((B, D), lambda j: (0, j)),
        compiler_params=_cparams("arbitrary"),
        name="mod",
    )(c, mod_w, mod_b.reshape(1, -1))


def _rope_group(xg, cos, sin, half, period):
    lo = (_lane((1, LANES)) % period) < half
    swapped = jnp.where(lo, pltpu.roll(xg, LANES - half, 1), pltpu.roll(xg, half, 1))
    return xg * cos + swapped * sin


def _in_kernel(x_ref, mod_ref, g_ref, w_ref, ca_ref, sa_ref, cb_ref, sb_ref, cd_ref, sd_ref,
               qa_ref, kva_ref, wa_ref, qkvb_ref, gate_ref, c_ref, d_ref):
    x = x_ref[...]
    h = x * lax.rsqrt(jnp.mean(x * x, axis=-1, keepdims=True) + EPS) * g_ref[...]
    h = h * (1.0 + mod_ref[0, 1:2, :]) + mod_ref[0, 0:1, :]
    hb = h.astype(BF16)

    def seg(off, width):
        return _dot(hb, w_ref[:, off:off + width])

    lane = _lane((1, LANES))
    low = lane < HEAD_DIM

    ca, sa = ca_ref[...], sa_ref[...]
    a = seg(SEG_A, 1024)
    half_a = HEAD_DIM // ROPE_FRAC // 2
    for gi in range(2):
        r = _rope_group(a[:, gi * LANES:(gi + 1) * LANES], ca, sa, half_a, HEAD_DIM) * (HEAD_DIM ** -0.5 * LOG2E)
        rs = pltpu.roll(r, HEAD_DIM, 1)
        qa_ref[:, (2 * gi) * LANES:(2 * gi + 1) * LANES] = jnp.where(low, r, 0.0).astype(BF16)
        qa_ref[:, (2 * gi + 1) * LANES:(2 * gi + 2) * LANES] = jnp.where(low, rs, 0.0).astype(BF16)
    for gi in range(4):
        r = _rope_group(a[:, (2 + gi) * LANES:(3 + gi) * LANES], ca, sa, half_a, HEAD_DIM)
        rs = pltpu.roll(r, HEAD_DIM, 1)
        qa_ref[:, (4 + 2 * gi) * LANES:(5 + 2 * gi) * LANES] = jnp.where(low, 0.0, rs).astype(BF16)
        qa_ref[:, (5 + 2 * gi) * LANES:(6 + 2 * gi) * LANES] = jnp.where(low, 0.0, r).astype(BF16)
    kva_ref[:, 0:LANES] = _rope_group(a[:, 6 * LANES:7 * LANES], ca, sa, half_a, HEAD_DIM).astype(BF16)
    kva_ref[:, LANES:2 * LANES] = a[:, 7 * LANES:8 * LANES].astype(BF16)

    wa_ref[...] = seg(SEG_W, LANES)

    cb, sb = cb_ref[...], sb_ref[...]
    b = seg(SEG_B, 768)
    for gi in range(4):
        r = _rope_group(b[:, gi * LANES:(gi + 1) * LANES], cb, sb, HEAD_DIM // 2, HEAD_DIM)
        if gi < 2:
            r = r * (HEAD_DIM ** -0.5)
        qkvb_ref[:, gi * LANES:(gi + 1) * LANES] = r.astype(BF16)
    qkvb_ref[:, 512:768] = b[:, 512:768].astype(BF16)

    gate_ref[...] = seg(SEG_G, 512)
    c_ref[...] = seg(SEG_C, 640)

    cd, sd = cd_ref[...], sd_ref[...]
    d = seg(SEG_D, 768)
    half_d = DIFF_DK // ROPE_FRAC // 2
    for gi in range(4):
        r = _rope_group(d[:, gi * LANES:(gi + 1) * LANES], cd, sd, half_d, DIFF_DK)
        if gi < 2:
            r = r * (DIFF_DK ** -0.5 * LOG2E)
        d_ref[:, gi * LANES:(gi + 1) * LANES] = r.astype(BF16)
    d_ref[:, 512:768] = d[:, 512:768].astype(BF16)


def _in_call(x2, mod3, g, w_p, tabs, T):
    N, D = x2.shape
    nt = T // TM
    row = lambda i: (i, 0)
    tab_spec = pl.BlockSpec((TM, LANES), row)
    outs = [(1536, BF16), (256, BF16), (LANES, F32), (768, BF16), (512, F32), (640, F32), (768, BF16)]
    return pl.pallas_call(
        _in_kernel,
        out_shape=[jax.ShapeDtypeStruct((N, w), dt) for w, dt in outs],
        grid=(N // TM,),
        in_specs=[pl.BlockSpec((TM, D), row),
                  pl.BlockSpec((1, 6, D), lambda i: (i // nt, 0, 0)),
                  _resident((1, D), lambda i: (0, 0)),
                  _resident((D, W_COLS), lambda i: (0, 0))] + [tab_spec] * 6,
        out_specs=[pl.BlockSpec((TM, w), row) for w, _ in outs],
        compiler_params=_cparams("arbitrary"),
        name="in_proj",
    )(x2, mod3, g, w_p, *tabs)


def _pair_head_norm(o, gain, center):
    low = _lane((1, LANES)) < HEAD_DIM
    inv = 1.0 / HEAD_DIM

    def seg_mean(v):
        m0 = jnp.sum(jnp.where(low, v, 0.0), axis=1, keepdims=True)
        m1 = jnp.sum(jnp.where(low, 0.0, v), axis=1, keepdims=True)
        return jnp.where(low, m0, m1) * inv

    if center:
        o = o - seg_mean(o)
    return o * lax.rsqrt(seg_mean(o * o) + EPS) * gain


class _Flash:
    def __init__(self, n, s_ref, p_ref, acc_ref, st_ref):
        self.n, self.s, self.p, self.acc, self.st = n, s_ref, p_ref, acc_ref, st_ref

    @staticmethod
    def scratch(n):
        return [pltpu.VMEM((n, TK, TQ), F32), pltpu.VMEM((n, TK, TQ), BF16),
                pltpu.VMEM((n, VT_ROWS, TQ), F32), pltpu.VMEM((3, n, TQ), F32)]

    def _row(self, k, i):
        return self.st.at[k, i:i + 1, :]

    def init(self):
        self.st[0] = jnp.full((self.n, TQ), NEG_INF, F32)
        self.acc[...] = jnp.zeros_like(self.acc)

    def scores(self, i, s):
        self.s[i] = s
        self._row(1, i)[...] = jnp.max(_fold8(s, jnp.maximum), axis=0, keepdims=True)

    def probs(self, i):
        m_old = self._row(0, i)[...]
        m_new = jnp.maximum(m_old, self._row(1, i)[...])
        for r in range(0, TK, FLASH_SLAB):
            self.p[i, r:r + FLASH_SLAB, :] = jnp.exp2(self.s[i, r:r + FLASH_SLAB, :] - m_new).astype(BF16)
        self._row(0, i)[...] = m_new
        self._row(2, i)[...] = jnp.exp2(m_old - m_new)

    def values(self, i, vt):
        self.acc[i] = self._row(2, i)[...] * self.acc[i] + _dot(vt, self.p[i])

    def result(self, i):
        return self.acc[i, 0:HEAD_DIM, :] / self.acc[i, HEAD_DIM:HEAD_DIM + 1, :]

    def pipeline(self, first, n_rest, score, value):
        score(first, True)

        def body(t, carry):
            for i in range(self.n):
                self.probs(i)
            score(t, False)
            value(jnp.where(t == 0, first, t - 1))
            return carry

        lax.fori_loop(0, n_rest, body, 0)
        for i in range(self.n):
            self.probs(i)
        value(jnp.where(n_rest == 0, first, n_rest - 1))


def _fold8(x, op=jnp.add, rows=8):
    parts = [x[r:r + rows] for r in range(0, x.shape[0], rows)]
    while len(parts) > 1:
        parts = [op(a, b) for a, b in zip(parts[0::2], parts[1::2])]
    return parts[0]


def _head_norm_t(o, gain):
    ms = jnp.mean(o * o, axis=0, keepdims=True)
    return o * lax.rsqrt(ms + EPS) * gain


def _dsa_kernel(q_ref, w_ref, kv_ref, vt_ref, g_ref, o_ref, bias_ref, key_ref, hi_ref, lo_ref, *flash_refs, topk):
    q0 = pl.program_id(1) * TQ
    nkt = (q0 + TQ) // TK
    qchunk = (q0 + _lane((1, TQ))) // CHUNK
    wts = jnp.transpose(w_ref[...]) * (IDX_HEADS ** -0.5 * IDX_DIM ** -0.5)

    def score_tile(kt, carry):
        k0 = pl.multiple_of(kt * TK, TK)
        kk = kv_ref[pl.ds(k0, TK), 0:LANES]
        tile = bias_ref.at[pl.ds(k0, TK), :]
        for h in range(IDX_HEADS):
            term = jnp.maximum(_dot_nt(kk, q_ref[:, (4 + h) * LANES:(5 + h) * LANES]), 0.0) * wts[h:h + 1, :]
            if h == 0:
                tile[...] = term
            elif h < IDX_HEADS - 1:
                tile[...] += term
        kchunk = (k0 + lax.broadcasted_iota(I32, (TK, 1), 0)) // CHUNK
        acc = jnp.where(kchunk <= qchunk, tile[...] + term, NEG_INF)
        bits = pltpu.bitcast(acc, I32)
        sign = bits >> 31
        key = ((bits & 0x7FFFFFFF) ^ sign) - sign
        key_ref[pl.ds(k0, TK), :] = key
        hi_ref[pl.ds(k0, TK), :] = (key >> 16).astype(I16)
        lo_ref[pl.ds(k0, TK), :] = (key ^ 0x8000).astype(I16)
        return carry

    lax.fori_loop(0, nkt, score_tile, 0)

    kf = float(topk)

    def count16(ref, cand):
        def body(kt, acc):
            k0 = pl.multiple_of(kt * TK, TK)
            hit = jnp.where(ref[pl.ds(k0, TK), :] >= cand, jnp.ones((TK, TQ), I16), jnp.zeros((TK, TQ), I16))
            return acc + _fold8(hit, rows=16)
        acc = lax.fori_loop(0, nkt, body, jnp.zeros((16, TQ), I16))
        return jnp.sum(acc.astype(F32), axis=0, keepdims=True)

    def search16(ref, cnt_all):
        c0 = count16(ref, jnp.zeros((1, TQ), I16))
        t = jnp.where(c0 >= kf, 0, -32768).astype(I32)
        cnt = jnp.where(c0 >= kf, c0, cnt_all)

        def step(it, carry):
            t, cnt = carry
            cand = t + (jnp.int32(1) << (14 - it))
            c = count16(ref, cand.astype(I16))
            ok = c >= kf
            return jnp.where(ok, cand, t), jnp.where(ok, c, cnt)

        return lax.fori_loop(0, 15, step, (t, cnt))

    def count(pred):
        def body(kt, acc):
            k0 = pl.multiple_of(kt * TK, TK)
            hit = jnp.where(pred(key_ref[pl.ds(k0, TK), :], k0), 1.0, 0.0)
            return acc + _fold8(hit)
        acc = lax.fori_loop(0, nkt, body, jnp.zeros((8, TQ), F32))
        return jnp.sum(acc, axis=0, keepdims=True)

    t_hi, cnt_hi = search16(hi_ref, (nkt * TK).astype(F32))
    t_hi16 = t_hi.astype(I16)

    def low_digit_tile(kt, carry):
        k0 = pl.multiple_of(kt * TK, TK)
        hi = hi_ref[pl.ds(k0, TK), :]
        in_bucket = jnp.where(hi == t_hi16, lo_ref[pl.ds(k0, TK), :], jnp.full((TK, TQ), -32768, I16))
        hi_ref[pl.ds(k0, TK), :] = jnp.where(hi > t_hi16, jnp.full((TK, TQ), 32767, I16), in_bucket)
        return carry

    lax.fori_loop(0, nkt, low_digit_tile, 0)
    t_lo, cnt = search16(hi_ref, cnt_hi)
    thr = t_hi * 65536 + (t_lo + 32768)
    floor_key = jnp.int32(NEG_KEY + 1)
    tied = jnp.max(jnp.where((cnt > kf) & (thr >= floor_key), 1.0, 0.0)) > 0.0

    @pl.when(jnp.logical_not(tied))
    def _():
        lim = jnp.maximum(thr, floor_key)

        def body(kt, carry):
            k0 = pl.multiple_of(kt * TK, TK)
            bias_ref[pl.ds(k0, TK), :] = jnp.where(key_ref[pl.ds(k0, TK), :] >= lim, 0.0, NEG_INF)
            return carry
        lax.fori_loop(0, nkt, body, 0)

    @pl.when(tied)
    def _():
        need = kf - count(lambda keys, k0: keys > thr)

        def idx(k0):
            return k0 + lax.broadcasted_iota(I32, (TK, 1), 0)

        def search_idx(it, lim):
            cand = lim + (jnp.int32(1) << (14 - it))
            c = count(lambda keys, k0: (keys == thr) & (idx(k0) < cand))
            return jnp.where(c <= need, cand, lim)

        lim = lax.fori_loop(0, 15, search_idx, jnp.zeros((1, TQ), I32))

        def body(kt, carry):
            k0 = pl.multiple_of(kt * TK, TK)
            keys = key_ref[pl.ds(k0, TK), :]
            sel = ((keys > thr) | ((keys == thr) & (idx(k0) < lim))) & (keys >= floor_key)
            bias_ref[pl.ds(k0, TK), :] = jnp.where(sel, 0.0, NEG_INF)
            return carry
        lax.fori_loop(0, nkt, body, 0)

    flash = _Flash(N_GH, *flash_refs)
    flash.init()

    def score(kt, is_first):
        k0 = pl.multiple_of(kt * TK, TK)
        kk = kv_ref[pl.ds(k0, TK), 0:LANES]
        bias = bias_ref[pl.ds(k0, TK), :]
        for h in range(N_GH):
            flash.scores(h, _dot_nt(kk, q_ref[:, h * LANES:(h + 1) * LANES]) + bias)

    def value(kt):
        k0 = pl.multiple_of(kt * TK, TK)
        for h in range(N_GH):
            flash.values(h, vt_ref[:, pl.ds(k0, TK)])

    flash.pipeline(nkt - 1, nkt - 1, score, value)
    y = jnp.concatenate([_head_norm_t(flash.result(h), g_ref[h * HEAD_DIM:(h + 1) * HEAD_DIM, :])
                         for h in range(N_GH)], axis=0)
    o_ref[...] = jnp.transpose(y).astype(BF16)


def _dsa_call(qa, wa, kva, vta, g_t, B, T):
    nq = T // TQ
    topk = min(TOPK_MAX, T // 4)
    return pl.pallas_call(
        functools.partial(_dsa_kernel, topk=topk),
        out_shape=jax.ShapeDtypeStruct((B * T, GROUP_W), BF16),
        grid=(B, nq),
        in_specs=[pl.BlockSpec((TQ, 1536), lambda b, i: (b * nq + i, 0)),
                  pl.BlockSpec((TQ, LANES), lambda b, i: (b * nq + i, 0)),
                  pl.BlockSpec((T, 256), lambda b, i: (b, 0)),
                  pl.BlockSpec((VT_ROWS, T), lambda b, i: (b, 0)),
                  pl.BlockSpec((GROUP_W, TQ), lambda b, i: (0, 0))],
        out_specs=pl.BlockSpec((TQ, GROUP_W), lambda b, i: (b * nq + i, 0)),
        scratch_shapes=[pltpu.VMEM((T, TQ), F32), pltpu.VMEM((T, TQ), I32), pltpu.VMEM((T, TQ), I16),
                        pltpu.VMEM((T, TQ), I16)] + _Flash.scratch(N_GH),
        compiler_params=_cparams("arbitrary", "arbitrary"),
        name="dsa",
    )(qa, wa, kva, vta, g_t)


def _diff_kernel(q_ref, k_ref, vt_ref, lam_ref, g_ref, o_ref, qm_ref, *flash_refs, lam_init):
    q0 = pl.program_id(1) * TQ
    nfull = (q0 + CHUNK) // TK
    qchunk = (q0 + _lane((1, TQ))) // CHUNK
    lv = lam_ref[...]
    lam = (jnp.exp(jnp.sum(lv[0:1] * lv[1:2], axis=1, keepdims=True))
           - jnp.exp(jnp.sum(lv[2:3] * lv[3:4], axis=1, keepdims=True)) + lam_init)
    lane = _lane((1, LANES))
    n_maps = 2 * N_GH
    for mi in range(n_maps):
        qg = q_ref[:, (mi // 4) * LANES:(mi // 4 + 1) * LANES]
        qm_ref[mi] = jnp.where((lane // DIFF_DK) == mi % 4, qg, jnp.zeros_like(qg))

    flash = _Flash(n_maps, *flash_refs)
    flash.init()

    def score(kt, masked):
        k0 = pl.multiple_of(kt * TK, TK)
        if masked:
            kchunk = (k0 + lax.broadcasted_iota(I32, (TK, 1), 0)) // CHUNK
            admissible = kchunk <= qchunk
        for mi in range(n_maps):
            g = mi // 4
            s = _dot_nt(k_ref[pl.ds(k0, TK), g * LANES:(g + 1) * LANES], qm_ref[mi])
            flash.scores(mi, jnp.where(admissible, s, NEG_INF) if masked else s)

    def value(kt):
        k0 = pl.multiple_of(kt * TK, TK)
        for mi in range(n_maps):
            h = mi // 2
            flash.values(mi, vt_ref[h * VT_ROWS:(h + 1) * VT_ROWS, pl.ds(k0, TK)])

    flash.pipeline(nfull, nfull, score, value)
    outs = []
    for h in range(N_GH):
        o = flash.result(2 * h) - lam * flash.result(2 * h + 1)
        outs.append(_head_norm_t(o, g_ref[h * HEAD_DIM:(h + 1) * HEAD_DIM, :]) * (1.0 - lam_init))
    o_ref[...] = jnp.transpose(jnp.concatenate(outs, axis=0)).astype(BF16)


def _diff_call(dqkv, vtd, lam_vecs, g_t, lam_init, B, T):
    nq = T // TQ
    return pl.pallas_call(
        functools.partial(_diff_kernel, lam_init=lam_init),
        out_shape=jax.ShapeDtypeStruct((B * T, GROUP_W), BF16),
        grid=(B, nq),
        in_specs=[pl.BlockSpec((TQ, 256), lambda b, i: (b * nq + i, 0)),
                  pl.BlockSpec((T, 256), lambda b, i: (b, 1)),
                  pl.BlockSpec((N_GH * VT_ROWS, T), lambda b, i: (b, 0)),
                  pl.BlockSpec((4, LANES), lambda b, i: (0, 0)),
                  pl.BlockSpec((GROUP_W, TQ), lambda b, i: (0, 0))],
        out_specs=pl.BlockSpec((TQ, GROUP_W), lambda b, i: (b * nq + i, 0)),
        scratch_shapes=[pltpu.VMEM((2 * N_GH, TQ, LANES), BF16)] + _Flash.scratch(2 * N_GH),
        compiler_params=_cparams("arbitrary", "arbitrary"),
        name="diff",
    )(dqkv, dqkv, vtd, lam_vecs, g_t)


def _ret_tables():
    h = np.arange(N_GH, dtype=np.float32)
    log_g = jnp.log(1.0 - 2.0 ** (-5.0 - jnp.asarray(h)))
    t = np.arange(TB)
    same = (t[:, None] // CHUNK) == (t[None, :] // CHUNK)
    earlier = (t[None, :] // CHUNK) < (t[:, None] // CHUNK)
    dist = jnp.asarray(np.where(same, np.abs(t[:, None] - t[None, :]), t[:, None] - t[None, :]), F32)
    dmat = jnp.where(jnp.asarray(same | earlier)[None],
                     jnp.exp(dist[None] * log_g[:, None, None]), 0.0)
    tt = jnp.asarray(t, F32)
    xi = jnp.exp((tt + 1.0)[None, :] * log_g[:, None])
    zeta = jnp.exp((TB - 1.0 - tt)[None, :] * log_g[:, None])
    gblk = jnp.exp(TB * log_g)

    def lanes(tab):
        return jnp.repeat(tab.reshape(2, 2, TB), HEAD_DIM, axis=1).transpose(0, 2, 1)

    r = np.arange(LANES)
    blockdiag = (r[:, None] // HEAD_DIM) == (r[None, :] // HEAD_DIM)
    gb = jnp.where(jnp.asarray(blockdiag)[None],
                   jnp.repeat(gblk.reshape(2, 2), HEAD_DIM, axis=1)[:, :, None], 0.0)
    return dmat, lanes(xi), lanes(zeta), gb


def _ret_kernel(qkv_ref, gate_ref, dmat_ref, xi_ref, zeta_ref, gb_ref, g_ref, o_ref, r_ref):
    @pl.when(pl.program_id(1) == 0)
    def _():
        r_ref[...] = jnp.zeros_like(r_ref)

    lane = _lane((1, LANES))
    low = lane < HEAD_DIM
    r_idx = lax.broadcasted_iota(I32, (LANES, LANES), 0) // HEAD_DIM
    c_idx = lax.broadcasted_iota(I32, (LANES, LANES), 1) // HEAD_DIM
    for g in range(2):
        qg = qkv_ref[:, g * LANES:(g + 1) * LANES]
        kg = qkv_ref[:, 256 + g * LANES:256 + (g + 1) * LANES]
        vg = qkv_ref[:, 512 + g * LANES:512 + (g + 1) * LANES]
        state = r_ref[g]
        inter = _dot(qg, state.astype(BF16)) * xi_ref[g]
        parts = []
        for hh in range(2):
            qm = jnp.where(low, qg, jnp.zeros_like(qg)) if hh == 0 else jnp.where(low, jnp.zeros_like(qg), qg)
            s = _dot_nt(qm, kg) * dmat_ref[2 * g + hh]
            parts.append(_dot(s.astype(BF16), vg))
        o = jnp.where(low, parts[0], parts[1]) + inter
        kz = (kg.astype(F32) * zeta_ref[g]).astype(BF16)
        upd = jnp.where(r_idx == c_idx, _dot_tn(kz, vg), 0.0)
        r_ref[g] = state * gb_ref[g] + upd
        y = _pair_head_norm(o, g_ref[:, g * LANES:(g + 1) * LANES], True)
        o_ref[:, g * LANES:(g + 1) * LANES] = (_silu(gate_ref[:, g * LANES:(g + 1) * LANES]) * y).astype(BF16)


def _ret_call(qkvb, gates, tabs, g_b, B, T):
    nb = T // TB
    dmat, xi, zeta, gb = tabs
    const3 = lambda b, i: (0, 0, 0)
    return pl.pallas_call(
        _ret_kernel,
        out_shape=jax.ShapeDtypeStruct((B * T, GROUP_W), BF16),
        grid=(B, nb),
        in_specs=[pl.BlockSpec((TB, 768), lambda b, i: (b * nb + i, 0)),
                  pl.BlockSpec((TB, GROUP_W), lambda b, i: (b * nb + i, 0)),
                  pl.BlockSpec((N_GH, TB, TB), const3),
                  pl.BlockSpec((2, TB, LANES), const3),
                  pl.BlockSpec((2, TB, LANES), const3),
                  pl.BlockSpec((2, LANES, LANES), const3),
                  pl.BlockSpec((1, GROUP_W), lambda b, i: (0, 0))],
        out_specs=pl.BlockSpec((TB, GROUP_W), lambda b, i: (b * nb + i, 0)),
        scratch_shapes=[pltpu.VMEM((2, LANES, LANES), F32)],
        compiler_params=_cparams("arbitrary", "arbitrary"),
        name="ret",
    )(qkvb, gates, dmat, xi, zeta, gb, g_b)


def _gla_kernel(c_ref, gate_ref, wa2_ref, ba_ref, g_ref, o_ref, s_ref, b_scr, q_scr, k_scr, v_scr, acc_scr):
    @pl.when(pl.program_id(1) == 0)
    def _():
        s_ref[...] = jnp.zeros_like(s_ref)

    n_chunks = TB // CHUNK
    hmask = (lax.broadcasted_iota(I32, (LANES, GROUP_W), 0) // GLA_DK
             == lax.broadcasted_iota(I32, (LANES, GROUP_W), 1) // HEAD_DIM)
    hsum = jnp.where(hmask, 1.0, 0.0).astype(BF16)
    r_idx = lax.broadcasted_iota(I32, (TB, TB), 0)
    c_idx = lax.broadcasted_iota(I32, (TB, TB), 1)
    tri = jnp.where((r_idx >= c_idx) & (r_idx // CHUNK == c_idx // CHUNK), 1.0, 0.0).astype(BF16)

    z = _dot_split(c_ref[:, 512:640], wa2_ref[...]) + ba_ref[...]
    log_a = (jnp.minimum(z, 0.0) - jnp.log(1.0 + jnp.exp(-jnp.abs(z)))) * (1.0 / GLA_TAU)
    b = sum(_dot(tri, piece) for piece in _split_bf16(log_a, 3))
    q = c_ref[:, 0:128] * (GLA_DK ** -0.5)
    b_scr[...] = b
    q_scr[...] = q
    k_scr[...] = c_ref[:, 128:256]
    v_scr[...] = c_ref[:, 256:512]

    b_last = [b[(c + 1) * CHUNK - 1:(c + 1) * CHUNK, :] for c in range(n_chunks)]
    b_end = jnp.concatenate([jnp.broadcast_to(bl, (CHUNK, LANES)) for bl in b_last], axis=0)
    kd = (c_ref[:, 128:256] * jnp.exp(b_end - b)).astype(BF16)
    qe = (q * jnp.exp(b)).astype(BF16)
    vb = c_ref[:, 256:512].astype(BF16)
    state = s_ref[...]
    inters = []
    for c in range(n_chunks):
        rows = slice(c * CHUNK, (c + 1) * CHUNK)
        inters.append(_dot(qe[rows], state.astype(BF16)))
        scale = jnp.transpose(jnp.broadcast_to(jnp.exp(b_last[c]), (LANES, LANES)))
        state = jnp.where(hmask, state * jnp.concatenate([scale, scale], axis=1) + _dot_tn(kd[rows], vb[rows]), 0.0)
    s_ref[...] = state

    acc_scr[...] = jnp.concatenate(inters, axis=0)

    def key_step(s, carry):
        for c in range(n_chunks):
            rows = slice(c * CHUNK, (c + 1) * CHUNK)
            r = c * CHUNK + s
            decay = jnp.exp(-jnp.abs(b_scr[rows, :] - b_scr[pl.ds(r, 1), :]))
            w = (q_scr[rows, :] * decay * k_scr[pl.ds(r, 1), :]).astype(BF16)
            acc_scr[rows, :] += _dot(w, hsum) * v_scr[pl.ds(r, 1), :]
        return carry

    lax.fori_loop(0, CHUNK, key_step, 0, unroll=4)
    o = acc_scr[...]
    for gi in range(2):
        y = _pair_head_norm(o[:, gi * LANES:(gi + 1) * LANES], g_ref[:, gi * LANES:(gi + 1) * LANES], True)
        o_ref[:, gi * LANES:(gi + 1) * LANES] = (_silu(gate_ref[:, gi * LANES:(gi + 1) * LANES]) * y).astype(BF16)


def _gla_call(cc, gates, wa2_p, ba, g_c, B, T):
    nb = T // TB
    return pl.pallas_call(
        _gla_kernel,
        out_shape=jax.ShapeDtypeStruct((B * T, GROUP_W), BF16),
        grid=(B, nb),
        in_specs=[pl.BlockSpec((TB, 640), lambda b, i: (b * nb + i, 0)),
                  pl.BlockSpec((TB, GROUP_W), lambda b, i: (b * nb + i, 1)),
                  pl.BlockSpec((LANES, LANES), lambda b, i: (0, 0)),
                  pl.BlockSpec((1, LANES), lambda b, i: (0, 0)),
                  pl.BlockSpec((1, GROUP_W), lambda b, i: (0, 0))],
        out_specs=pl.BlockSpec((TB, GROUP_W), lambda b, i: (b * nb + i, 0)),
        scratch_shapes=[pltpu.VMEM((LANES, GROUP_W), F32), pltpu.VMEM((TB, LANES), F32),
                        pltpu.VMEM((TB, LANES), F32), pltpu.VMEM((TB, LANES), F32),
                        pltpu.VMEM((TB, GROUP_W), F32), pltpu.VMEM((TB, GROUP_W), F32)],
        compiler_params=_cparams("arbitrary", "arbitrary"),
        name="gla",
    )(cc, gates, wa2_p, ba, g_c)


def _rms(y, g):
    return y * lax.rsqrt(jnp.mean(y * y, axis=-1, keepdims=True) + EPS) * g


def _out_mlp_kernel(x_ref, ya_ref, yb_ref, yc_ref, yd_ref, mod_ref, wo_ref, w1_ref, w2_ref,
                    gpost_ref, gpre_ref, gpost2_ref, o_ref):
    y = _dot(ya_ref[...], wo_ref[0:256, :])
    y = y + _dot(yb_ref[...], wo_ref[256:512, :])
    y = y + _dot(yc_ref[...], wo_ref[512:768, :])
    y = y + _dot(yd_ref[...], wo_ref[768:1024, :])
    x = x_ref[...] + mod_ref[0, 2:3, :] * _rms(y, gpost_ref[...])
    h = _rms(x, gpre_ref[...]) * (1.0 + mod_ref[0, 4:5, :]) + mod_ref[0, 3:4, :]
    hb = h.astype(BF16)
    acc = jnp.zeros(x.shape, F32)
    fc = 1024
    for f in range(D_FF // fc):
        u = jnp.maximum(_dot(hb, w1_ref[:, f * fc:(f + 1) * fc]), 0.0)
        acc = acc + _dot((u * u).astype(BF16), w2_ref[f * fc:(f + 1) * fc, :])
    o_ref[...] = x + mod_ref[0, 5:6, :] * _rms(acc, gpost2_ref[...])


def _out_mlp_call(x2, ys, mod3, w_out, w1, w2, g_post, g_pre2, g_post2, T):
    N, D = x2.shape
    nt = T // TM
    row = lambda i: (i, 0)
    const = lambda i: (0, 0)
    return pl.pallas_call(
        _out_mlp_kernel,
        out_shape=jax.ShapeDtypeStruct((N, D), F32),
        grid=(N // TM,),
        in_specs=[pl.BlockSpec((TM, D), row)] + [pl.BlockSpec((TM, GROUP_W), row)] * 4
                 + [pl.BlockSpec((1, 6, D), lambda i: (i // nt, 0, 0)),
                    _resident((D, D), const), _resident((D, D_FF), const), _resident((D_FF, D), const),
                    _resident((1, D), const), _resident((1, D), const), _resident((1, D), const)],
        out_specs=pl.BlockSpec((TM, D), row),
        compiler_params=_cparams("arbitrary"),
        name="out_mlp",
    )(x2, *ys, mod3, w_out, w1, w2, g_post, g_pre2, g_post2)


def _relayout_w_in(w_in):
    col = lambda k: w_in[:, _IN_OFF[k]:_IN_OFF[k + 1]]
    zeros = lambda n: jnp.zeros((w_in.shape[0], n), w_in.dtype)
    parts = [col(_A_Q), col(_A_QI), col(_A_K), col(_A_KI), col(_A_V), zeros(64),
             col(_A_WI), zeros(120),
             col(_B_Q), col(_B_K), col(_B_V),
             col(_B_G), col(_C_G),
             col(_C_Q), col(_C_K), col(_C_V), col(_C_A), zeros(112),
             col(_D_Q), col(_D_K), col(_D_V)]
    return jnp.concatenate(parts, axis=1).astype(BF16)


def _rope_tables(pos, rot, period, theta):
    half = rot // 2
    inv = theta ** (-jnp.arange(half, dtype=F32) / half)
    ang = pos.astype(F32)[..., None] * inv
    cos, sin = jnp.cos(ang), jnp.sin(ang)
    pad = ang.shape[:-1] + (period - rot,)
    ct = jnp.concatenate([cos, cos, jnp.ones(pad, F32)], axis=-1)
    st = jnp.concatenate([-sin, sin, jnp.zeros(pad, F32)], axis=-1)
    reps = LANES // period
    return (jnp.tile(ct, (1, 1, reps)).reshape(-1, LANES), jnp.tile(st, (1, 1, reps)).reshape(-1, LANES))


def _values_t(v, B, T):
    heads = v.shape[1] // HEAD_DIM
    vt = jnp.transpose(v.reshape(B, T, heads, HEAD_DIM), (0, 2, 3, 1))
    ones = jnp.ones((B, heads, VT_ROWS - HEAD_DIM, T), v.dtype)
    return jnp.concatenate([vt, ones], axis=2).reshape(-1, T)


def _layer(x2, cond, tabs, ret_tabs, layer_idx, B, T, mod_w, mod_b, attn_pre_g, attn_post_g,
           mlp_pre_g, mlp_post_g, w_in, gla_wa2, gla_ba, lam_q1, lam_k1, lam_q2, lam_k2,
           head_norm_g, w_out, mlp_w1, mlp_w2):
    D = D_MODEL
    mod3 = _mod_call(cond, mod_w, mod_b).reshape(B, 6, D)
    qa, kva, wa, qkvb, gates, cc, dqkv = _in_call(
        x2, mod3, attn_pre_g.reshape(1, D), _relayout_w_in(w_in), tabs, T)

    g_a, g_b, g_c, g_d = jnp.split(head_norm_g, 4)
    gain_t = lambda g: jnp.broadcast_to(g[:, None], (GROUP_W, TQ))
    values_t = lambda v: _values_t(v, B, T)
    y_a = _dsa_call(qa, wa, kva, values_t(kva[:, LANES:LANES + HEAD_DIM]), gain_t(g_a), B, T)
    y_b = _ret_call(qkvb, gates, ret_tabs, g_b.reshape(1, GROUP_W), B, T)
    wa2_p = jnp.zeros((LANES, LANES), F32).at[:GLA_RANK].set(gla_wa2)
    y_c = _gla_call(cc, gates, wa2_p, gla_ba.reshape(1, LANES), g_c.reshape(1, GROUP_W), B, T)
    lam_init = 0.8 - 0.6 * math.exp(-0.3 * layer_idx)
    lam_vecs = jnp.zeros((4, LANES), F32).at[:, :DIFF_DK].set(jnp.stack([lam_q1, lam_k1, lam_q2, lam_k2]))
    y_d = _diff_call(dqkv, values_t(dqkv[:, 512:768]), lam_vecs, gain_t(g_d), lam_init, B, T)

    return _out_mlp_call(x2, (y_a, y_b, y_c, y_d), mod3, w_out.astype(BF16), mlp_w1.astype(BF16),
                         mlp_w2.astype(BF16), attn_post_g.reshape(1, D), mlp_pre_g.reshape(1, D),
                         mlp_post_g.reshape(1, D), T)


def kernel(x, c, positions, mod_w, mod_b, attn_pre_g, attn_post_g, mlp_pre_g, mlp_post_g, w_in,
           gla_wa2, gla_ba, lam_q1, lam_k1, lam_q2, lam_k2, head_norm_g, w_out, mlp_w1, mlp_w2):
    B, T, D = x.shape
    assert D == D_MODEL and T % TM == 0 and T % TB == 0
    tabs = (_rope_tables(positions, HEAD_DIM // ROPE_FRAC, HEAD_DIM, ROPE_THETA)
            + _rope_tables(positions, HEAD_DIM, HEAD_DIM, RET_THETA)
            + _rope_tables(positions, DIFF_DK // ROPE_FRAC, DIFF_DK, ROPE_THETA))
    ret_tabs = _ret_tables()
    x2 = x.reshape(B * T, D)
    for l in range(mod_w.shape[0]):
        x2 = _layer(x2, c, tabs, ret_tabs, l, B, T, mod_w[l], mod_b[l], attn_pre_g[l], attn_post_g[l],
                    mlp_pre_g[l], mlp_post_g[l], w_in[l], gla_wa2[l], gla_ba[l], lam_q1[l], lam_k1[l],
                    lam_q2[l], lam_k2[l], head_norm_g[l], w_out[l], mlp_w1[l], mlp_w2[l])
    return x2.reshape(B, T, D)
```

```python
import functools
import math

import numpy as np
import jax
import jax.numpy as jnp
from jax import lax
from jax.experimental import pallas as pl
from jax.experimental.pallas import tpu as pltpu

F32 = jnp.float32
BF16 = jnp.bfloat16
I32 = jnp.int32
I16 = jnp.int16

D_MODEL = 1024
CHUNK = 64
HEAD_DIM = 64
N_GH = 4
GROUP_W = N_GH * HEAD_DIM
IDX_HEADS = 8
IDX_DIM = 64
TOPK_MAX = 256
RET_THETA = 10000.0
GLA_DK = HEAD_DIM // 2
GLA_RANK = 16
GLA_TAU = 16.0
DIFF_DK = HEAD_DIM // 2
ROPE_THETA = 500000.0
ROPE_FRAC = 4
D_FF = 4 * D_MODEL
EPS = 1e-6
NEG_INF = -1e30

LANES = 128
VMEM_LIMIT = 56 * 1024 * 1024

_IN_SPLITS = (
    GROUP_W, HEAD_DIM, HEAD_DIM, IDX_HEADS * IDX_DIM, IDX_DIM, IDX_HEADS,
    GROUP_W, GROUP_W, GROUP_W, GROUP_W,
    N_GH * GLA_DK, N_GH * GLA_DK, GROUP_W, GLA_RANK, GROUP_W,
    2 * N_GH * DIFF_DK, 2 * N_GH * DIFF_DK, GROUP_W,
)
_IN_OFF = np.concatenate([[0], np.cumsum(_IN_SPLITS)]).tolist()
(_A_Q, _A_K, _A_V, _A_QI, _A_KI, _A_WI, _B_Q, _B_K, _B_V, _B_G,
 _C_Q, _C_K, _C_V, _C_A, _C_G, _D_Q, _D_K, _D_V) = range(18)

SEG_A = 0
SEG_W = 1024
SEG_B = 1152
SEG_G = 1920
SEG_C = 2432
SEG_D = 3072
W_COLS = 3840

NT_DIMS = (((1,), (1,)), ((), ()))
TN_DIMS = (((0,), (0,)), ((), ()))

TM = 512
TQ = 256
TK = 256
FLASH_SLAB = 32
VT_ROWS = HEAD_DIM + 16
LOG2E = math.log2(math.e)
TB = 256


def _float_key(v):
    bits = int(np.array(v, np.float32).view(np.int32))
    return -(bits & 0x7FFFFFFF) if bits < 0 else bits


NEG_KEY = _float_key(NEG_INF)
INT_MIN = -(2 ** 31)


def _cparams(*sem):
    return pltpu.CompilerParams(dimension_semantics=sem, vmem_limit_bytes=VMEM_LIMIT)


def _resident(shape, index_map):
    return pl.BlockSpec(shape, index_map, pipeline_mode=pl.Buffered(1))


def _dot(a, b):
    return jnp.dot(a, b, preferred_element_type=F32)


def _dot_nt(a, b):
    return lax.dot_general(a, b, NT_DIMS, preferred_element_type=F32)


def _dot_tn(a, b):
    return lax.dot_general(a, b, TN_DIMS, preferred_element_type=F32)


def _dot_f32(a, b):
    return jnp.dot(a, b, preferred_element_type=F32, precision=lax.Precision.HIGHEST)


def _split_bf16(x, n):
    parts = []
    for _ in range(n):
        p = x.astype(BF16)
        parts.append(p)
        x = x - p.astype(F32)
    return parts


def _dot_split(a, b):
    (a0, a1), (b0, b1) = _split_bf16(a, 2), _split_bf16(b, 2)
    return _dot(a0, b0) + (_dot(a0, b1) + _dot(a1, b0))


def _lane(shape):
    return lax.broadcasted_iota(I32, shape, len(shape) - 1)


def _silu(x):
    return x / (1.0 + jnp.exp(-x))


def _mod_kernel(c_ref, w_ref, b_ref, o_ref):
    cond = _silu(c_ref[...])
    o_ref[...] = _dot_f32(cond, w_ref[...]) + b_ref[...]


def _mod_call(c, mod_w, mod_b):
    B, D = c.shape
    n = mod_w.shape[1] // D
    return pl.pallas_call(
        _mod_kernel,
        out_shape=jax.ShapeDtypeStruct((B, n * D), F32),
        grid=(n,),
        in_specs=[pl.BlockSpec((B, D), lambda j: (0, 0)),
                  pl.BlockSpec((D, D), lambda j: (0, j)),
                  pl.BlockSpec((1, D), lambda j: (0, j))],
        out_specs=pl.BlockSpec((B, D), lambda j: (0, j)),
        compiler_params=_cparams("arbitrary"),
        name="mod",
    )(c, mod_w, mod_b.reshape(1, -1))


def _rope_group(xg, cos, sin, half, period):
    lo = (_lane((1, LANES)) % period) < half
    swapped = jnp.where(lo, pltpu.roll(xg, LANES - half, 1), pltpu.roll(xg, half, 1))
    return xg * cos + swapped * sin


def _in_kernel(x_ref, mod_ref, g_ref, w_ref, ca_ref, sa_ref, cb_ref, sb_ref, cd_ref, sd_ref,
               qa_ref, kva_ref, wa_ref, qkvb_ref, gate_ref, c_ref, d_ref):
    x = x_ref[...]
    h = x * lax.rsqrt(jnp.mean(x * x, axis=-1, keepdims=True) + EPS) * g_ref[...]
    h = h * (1.0 + mod_ref[0, 1:2, :]) + mod_ref[0, 0:1, :]
    hb = h.astype(BF16)

    def seg(off, width):
        return _dot(hb, w_ref[:, off:off + width])

    lane = _lane((1, LANES))
    low = lane < HEAD_DIM

    ca, sa = ca_ref[...], sa_ref[...]
    a = seg(SEG_A, 1024)
    half_a = HEAD_DIM // ROPE_FRAC // 2
    for gi in range(2):
        r = _rope_group(a[:, gi * LANES:(gi + 1) * LANES], ca, sa, half_a, HEAD_DIM) * (HEAD_DIM ** -0.5 * LOG2E)
        rs = pltpu.roll(r, HEAD_DIM, 1)
        qa_ref[:, (2 * gi) * LANES:(2 * gi + 1) * LANES] = jnp.where(low, r, 0.0).astype(BF16)
        qa_ref[:, (2 * gi + 1) * LANES:(2 * gi + 2) * LANES] = jnp.where(low, rs, 0.0).astype(BF16)
    for gi in range(4):
        r = _rope_group(a[:, (2 + gi) * LANES:(3 + gi) * LANES], ca, sa, half_a, HEAD_DIM)
        rs = pltpu.roll(r, HEAD_DIM, 1)
        qa_ref[:, (4 + 2 * gi) * LANES:(5 + 2 * gi) * LANES] = jnp.where(low, 0.0, rs).astype(BF16)
        qa_ref[:, (5 + 2 * gi) * LANES:(6 + 2 * gi) * LANES] = jnp.where(low, 0.0, r).astype(BF16)
    kva_ref[:, 0:LANES] = _rope_group(a[:, 6 * LANES:7 * LANES], ca, sa, half_a, HEAD_DIM).astype(BF16)
    kva_ref[:, LANES:2 * LANES] = a[:, 7 * LANES:8 * LANES].astype(BF16)

    wa_ref[...] = seg(SEG_W, LANES)

    cb, sb = cb_ref[...], sb_ref[...]
    b = seg(SEG_B, 768)
    for gi in range(4):
        r = _rope_group(b[:, gi * LANES:(gi + 1) * LANES], cb, sb, HEAD_DIM // 2, HEAD_DIM)
        if gi < 2:
            r = r * (HEAD_DIM ** -0.5)
        qkvb_ref[:, gi * LANES:(gi + 1) * LANES] = r.astype(BF16)
    qkvb_ref[:, 512:768] = b[:, 512:768].astype(BF16)

    gate_ref[...] = seg(SEG_G, 512)
    c_ref[...] = seg(SEG_C, 640)

    cd, sd = cd_ref[...], sd_ref[...]
    d = seg(SEG_D, 768)
    half_d = DIFF_DK // ROPE_FRAC // 2
    for gi in range(4):
        r = _rope_group(d[:, gi * LANES:(gi + 1) * LANES], cd, sd, half_d, DIFF_DK)
        if gi < 2:
            r = r * (DIFF_DK ** -0.5 * LOG2E)
        d_ref[:, gi * LANES:(gi + 1) * LANES] = r.astype(BF16)
    d_ref[:, 512:768] = d[:, 512:768].astype(BF16)


def _in_call(x2, mod3, g, w_p, tabs, T):
    N, D = x2.shape
    nt = T // TM
    row = lambda i: (i, 0)
    tab_spec = pl.BlockSpec((TM, LANES), row)
    outs = [(1536, BF16), (256, BF16), (LANES, F32), (768, BF16), (512, F32), (640, F32), (768, BF16)]
    return pl.pallas_call(
        _in_kernel,
        out_shape=[jax.ShapeDtypeStruct((N, w), dt) for w, dt in outs],
        grid=(N // TM,),
        in_specs=[pl.BlockSpec((TM, D), row),
                  pl.BlockSpec((1, 6, D), lambda i: (i // nt, 0, 0)),
                  _resident((1, D), lambda i: (0, 0)),
                  _resident((D, W_COLS), lambda i: (0, 0))] + [tab_spec] * 6,
        out_specs=[pl.BlockSpec((TM, w), row) for w, _ in outs],
        compiler_params=_cparams("arbitrary"),
        name="in_proj",
    )(x2, mod3, g, w_p, *tabs)


def _pair_head_norm(o, gain, center):
    low = _lane((1, LANES)) < HEAD_DIM
    inv = 1.0 / HEAD_DIM

    def seg_mean(v):
        m0 = jnp.sum(jnp.where(low, v, 0.0), axis=1, keepdims=True)
        m1 = jnp.sum(jnp.where(low, 0.0, v), axis=1, keepdims=True)
        return jnp.where(low, m0, m1) * inv

    if center:
        o = o - seg_mean(o)
    return o * lax.rsqrt(seg_mean(o * o) + EPS) * gain


class _Flash:
    def __init__(self, n, s_ref, p_ref, acc_ref, st_ref):
        self.n, self.s, self.p, self.acc, self.st = n, s_ref, p_ref, acc_ref, st_ref

    @staticmethod
    def scratch(n):
        return [pltpu.VMEM((n, TK, TQ), F32), pltpu.VMEM((n, TK, TQ), BF16),
                pltpu.VMEM((n, VT_ROWS, TQ), F32), pltpu.VMEM((3, n, TQ), F32)]

    def _row(self, k, i):
        return self.st.at[k, i:i + 1, :]

    def init(self):
        self.st[0] = jnp.full((self.n, TQ), NEG_INF, F32)
        self.st[2] = jnp.ones((self.n, TQ), F32)
        self.acc[...] = jnp.zeros_like(self.acc)
        self.p[...] = jnp.zeros_like(self.p)

    def scores(self, i, s):
        self.s[i] = s
        self._row(1, i)[...] = jnp.max(_fold8(s, jnp.maximum), axis=0, keepdims=True)

    def probs(self, i):
        m_old = self._row(0, i)[...]
        m_new = jnp.maximum(m_old, self._row(1, i)[...])
        for r in range(0, TK, FLASH_SLAB):
            self.p[i, r:r + FLASH_SLAB, :] = jnp.exp2(self.s[i, r:r + FLASH_SLAB, :] - m_new).astype(BF16)
        self._row(0, i)[...] = m_new
        self._row(2, i)[...] = jnp.exp2(m_old - m_new)

    def values(self, i, vt):
        self.acc[i] = self._row(2, i)[...] * self.acc[i] + _dot(vt, self.p[i])

    def result(self, i):
        return self.acc[i, 0:HEAD_DIM, :] / self.acc[i, HEAD_DIM:HEAD_DIM + 1, :]

    def pipeline(self, first, n_rest, score, value):
        score(first, True)

        def body(t, carry):
            value(jnp.where(t <= 1, first, t - 2))
            for i in range(self.n):
                self.probs(i)
            score(t, False)
            return carry

        lax.fori_loop(0, n_rest, body, 0)
        value(jnp.where(n_rest <= 1, first, n_rest - 2))
        for i in range(self.n):
            self.probs(i)
        value(jnp.where(n_rest == 0, first, n_rest - 1))


def _fold8(x, op=jnp.add, rows=8):
    parts = [x[r:r + rows] for r in range(0, x.shape[0], rows)]
    while len(parts) > 1:
        parts = [op(a, b) for a, b in zip(parts[0::2], parts[1::2])]
    return parts[0]


def _head_norm_t(o, gain):
    ms = jnp.mean(o * o, axis=0, keepdims=True)
    return o * lax.rsqrt(ms + EPS) * gain


def _dsa_kernel(q_ref, w_ref, kv_ref, vt_ref, g_ref, o_ref, bias_ref, key_ref, hi_ref, lo_ref, *flash_refs, topk):
    q0 = pl.program_id(1) * TQ
    nkt = (q0 + TQ) // TK
    qchunk = (q0 + _lane((1, TQ))) // CHUNK
    wts = jnp.transpose(w_ref[...]) * (IDX_HEADS ** -0.5 * IDX_DIM ** -0.5)

    def score_tile(kt, carry):
        k0 = pl.multiple_of(kt * TK, TK)
        kk = kv_ref[pl.ds(k0, TK), 0:LANES]
        tile = bias_ref.at[pl.ds(k0, TK), :]
        for h in range(IDX_HEADS):
            term = jnp.maximum(_dot_nt(kk, q_ref[:, (4 + h) * LANES:(5 + h) * LANES]), 0.0) * wts[h:h + 1, :]
            if h == 0:
                tile[...] = term
            elif h < IDX_HEADS - 1:
                tile[...] += term
        kchunk = (k0 + lax.broadcasted_iota(I32, (TK, 1), 0)) // CHUNK
        acc = jnp.where(kchunk <= qchunk, tile[...] + term, NEG_INF)
        bits = pltpu.bitcast(acc, I32)
        sign = bits >> 31
        key = ((bits & 0x7FFFFFFF) ^ sign) - sign
        key_ref[pl.ds(k0, TK), :] = key
        hi_ref[pl.ds(k0, TK), :] = (key >> 16).astype(I16)
        lo_ref[pl.ds(k0, TK), :] = (key ^ 0x8000).astype(I16)
        return carry

    lax.fori_loop(0, nkt, score_tile, 0)

    @pl.when(nkt % 2 == 1)
    def _():
        pad0 = pl.multiple_of(nkt * TK, TK)
        hi_ref[pl.ds(pad0, TK), :] = jnp.full((TK, TQ), -32768, I16)
        lo_ref[pl.ds(pad0, TK), :] = jnp.full((TK, TQ), -32768, I16)

    kf = float(topk)
    n_pairs = (nkt + 1) // 2

    def count16(ref, cand):
        def body(kp, acc):
            k0 = pl.multiple_of(kp * (2 * TK), 2 * TK)
            hit = jnp.where(ref[pl.ds(k0, 2 * TK), :] >= cand,
                            jnp.ones((2 * TK, TQ), I16), jnp.zeros((2 * TK, TQ), I16))
            return acc + _fold8(hit, rows=16)
        acc = lax.fori_loop(0, n_pairs, body, jnp.zeros((16, TQ), I16))
        return jnp.sum(acc.astype(F32), axis=0, keepdims=True)

    def search16(ref, cnt_all):
        c0 = count16(ref, jnp.zeros((1, TQ), I16))
        t = jnp.where(c0 >= kf, 0, -32768).astype(I32)
        cnt = jnp.where(c0 >= kf, c0, cnt_all)

        def step(it, carry):
            t, cnt = carry
            cand = t + (jnp.int32(1) << (14 - it))
            c = count16(ref, cand.astype(I16))
            ok = c >= kf
            return jnp.where(ok, cand, t), jnp.where(ok, c, cnt)

        return lax.fori_loop(0, 15, step, (t, cnt))

    def count(pred):
        def body(kt, acc):
            k0 = pl.multiple_of(kt * TK, TK)
            hit = jnp.where(pred(key_ref[pl.ds(k0, TK), :], k0), 1.0, 0.0)
            return acc + _fold8(hit)
        acc = lax.fori_loop(0, nkt, body, jnp.zeros((8, TQ), F32))
        return jnp.sum(acc, axis=0, keepdims=True)

    t_hi, cnt_hi = search16(hi_ref, (nkt * TK).astype(F32))
    t_hi16 = t_hi.astype(I16)

    def low_digit_tile(kp, carry):
        k0 = pl.multiple_of(kp * (2 * TK), 2 * TK)
        hi = hi_ref[pl.ds(k0, 2 * TK), :]
        in_bucket = jnp.where(hi == t_hi16, lo_ref[pl.ds(k0, 2 * TK), :], jnp.full((2 * TK, TQ), -32768, I16))
        hi_ref[pl.ds(k0, 2 * TK), :] = jnp.where(hi > t_hi16, jnp.full((2 * TK, TQ), 32767, I16), in_bucket)
        return carry

    lax.fori_loop(0, n_pairs, low_digit_tile, 0)
    t_lo, cnt = search16(hi_ref, cnt_hi)
    thr = t_hi * 65536 + (t_lo + 32768)
    floor_key = jnp.int32(NEG_KEY + 1)
    tied = jnp.max(jnp.where((cnt > kf) & (thr >= floor_key), 1.0, 0.0)) > 0.0

    @pl.when(jnp.logical_not(tied))
    def _():
        lim = jnp.maximum(thr, floor_key)

        def body(kt, carry):
            k0 = pl.multiple_of(kt * TK, TK)
            bias_ref[pl.ds(k0, TK), :] = jnp.where(key_ref[pl.ds(k0, TK), :] >= lim, 0.0, NEG_INF)
            return carry
        lax.fori_loop(0, nkt, body, 0)

    @pl.when(tied)
    def _():
        need = kf - count(lambda keys, k0: keys > thr)

        def idx(k0):
            return k0 + lax.broadcasted_iota(I32, (TK, 1), 0)

        def search_idx(it, lim):
            cand = lim + (jnp.int32(1) << (14 - it))
            c = count(lambda keys, k0: (keys == thr) & (idx(k0) < cand))
            return jnp.where(c <= need, cand, lim)

        lim = lax.fori_loop(0, 15, search_idx, jnp.zeros((1, TQ), I32))

        def body(kt, carry):
            k0 = pl.multiple_of(kt * TK, TK)
            keys = key_ref[pl.ds(k0, TK), :]
            sel = ((keys > thr) | ((keys == thr) & (idx(k0) < lim))) & (keys >= floor_key)
            bias_ref[pl.ds(k0, TK), :] = jnp.where(sel, 0.0, NEG_INF)
            return carry
        lax.fori_loop(0, nkt, body, 0)

    flash = _Flash(N_GH, *flash_refs)
    flash.init()

    def score(kt, is_first):
        k0 = pl.multiple_of(kt * TK, TK)
        kk = kv_ref[pl.ds(k0, TK), 0:LANES]
        bias = bias_ref[pl.ds(k0, TK), :]
        for h in range(N_GH):
            flash.scores(h, _dot_nt(kk, q_ref[:, h * LANES:(h + 1) * LANES]) + bias)

    def value(kt):
        k0 = pl.multiple_of(kt * TK, TK)
        for h in range(N_GH):
            flash.values(h, vt_ref[:, pl.ds(k0, TK)])

    flash.pipeline(nkt - 1, nkt - 1, score, value)
    y = jnp.concatenate([_head_norm_t(flash.result(h), g_ref[h * HEAD_DIM:(h + 1) * HEAD_DIM, :])
                         for h in range(N_GH)], axis=0)
    o_ref[...] = jnp.transpose(y).astype(BF16)


def _dsa_call(qa, wa, kva, vta, g_t, B, T):
    nq = T // TQ
    topk = min(TOPK_MAX, T // 4)
    return pl.pallas_call(
        functools.partial(_dsa_kernel, topk=topk),
        out_shape=jax.ShapeDtypeStruct((B * T, GROUP_W), BF16),
        grid=(B, nq),
        in_specs=[pl.BlockSpec((TQ, 1536), lambda b, i: (b * nq + i, 0)),
                  pl.BlockSpec((TQ, LANES), lambda b, i: (b * nq + i, 0)),
                  pl.BlockSpec((T, 256), lambda b, i: (b, 0)),
                  pl.BlockSpec((VT_ROWS, T), lambda b, i: (b, 0)),
                  pl.BlockSpec((GROUP_W, TQ), lambda b, i: (0, 0))],
        out_specs=pl.BlockSpec((TQ, GROUP_W), lambda b, i: (b * nq + i, 0)),
        scratch_shapes=[pltpu.VMEM((T, TQ), F32), pltpu.VMEM((T, TQ), I32), pltpu.VMEM((T, TQ), I16),
                        pltpu.VMEM((T, TQ), I16)] + _Flash.scratch(N_GH),
        compiler_params=_cparams("arbitrary", "arbitrary"),
        name="dsa",
    )(qa, wa, kva, vta, g_t)


def _diff_kernel(q_ref, k_ref, vt_ref, lam_ref, g_ref, o_ref, qm_ref, *flash_refs, lam_init):
    q0 = pl.program_id(1) * TQ
    nfull = (q0 + CHUNK) // TK
    qchunk = (q0 + _lane((1, TQ))) // CHUNK
    lv = lam_ref[...]
    lam = (jnp.exp(jnp.sum(lv[0:1] * lv[1:2], axis=1, keepdims=True))
           - jnp.exp(jnp.sum(lv[2:3] * lv[3:4], axis=1, keepdims=True)) + lam_init)
    lane = _lane((1, LANES))
    n_maps = 2 * N_GH
    for mi in range(n_maps):
        qg = q_ref[:, (mi // 4) * LANES:(mi // 4 + 1) * LANES]
        qm_ref[mi] = jnp.where((lane // DIFF_DK) == mi % 4, qg, jnp.zeros_like(qg))

    flash = _Flash(n_maps, *flash_refs)
    flash.init()

    def score(kt, masked):
        k0 = pl.multiple_of(kt * TK, TK)
        if masked:
            kchunk = (k0 + lax.broadcasted_iota(I32, (TK, 1), 0)) // CHUNK
            admissible = kchunk <= qchunk
        for mi in range(n_maps):
            g = mi // 4
            s = _dot_nt(k_ref[pl.ds(k0, TK), g * LANES:(g + 1) * LANES], qm_ref[mi])
            flash.scores(mi, jnp.where(admissible, s, NEG_INF) if masked else s)

    def value(kt):
        k0 = pl.multiple_of(kt * TK, TK)
        for mi in range(n_maps):
            h = mi // 2
            flash.values(mi, vt_ref[h * VT_ROWS:(h + 1) * VT_ROWS, pl.ds(k0, TK)])

    flash.pipeline(nfull, nfull, score, value)
    outs = []
    for h in range(N_GH):
        o = flash.result(2 * h) - lam * flash.result(2 * h + 1)
        outs.append(_head_norm_t(o, g_ref[h * HEAD_DIM:(h + 1) * HEAD_DIM, :]) * (1.0 - lam_init))
    o_ref[...] = jnp.transpose(jnp.concatenate(outs, axis=0)).astype(BF16)


def _diff_call(dqkv, vtd, lam_vecs, g_t, lam_init, B, T):
    nq = T // TQ
    return pl.pallas_call(
        functools.partial(_diff_kernel, lam_init=lam_init),
        out_shape=jax.ShapeDtypeStruct((B * T, GROUP_W), BF16),
        grid=(B, nq),
        in_specs=[pl.BlockSpec((TQ, 256), lambda b, i: (b * nq + i, 0)),
                  pl.BlockSpec((T, 256), lambda b, i: (b, 1)),
                  pl.BlockSpec((N_GH * VT_ROWS, T), lambda b, i: (b, 0)),
                  pl.BlockSpec((4, LANES), lambda b, i: (0, 0)),
                  pl.BlockSpec((GROUP_W, TQ), lambda b, i: (0, 0))],
        out_specs=pl.BlockSpec((TQ, GROUP_W), lambda b, i: (b * nq + i, 0)),
        scratch_shapes=[pltpu.VMEM((2 * N_GH, TQ, LANES), BF16)] + _Flash.scratch(2 * N_GH),
        compiler_params=_cparams("arbitrary", "arbitrary"),
        name="diff",
    )(dqkv, dqkv, vtd, lam_vecs, g_t)


def _ret_tables():
    h = np.arange(N_GH, dtype=np.float32)
    log_g = jnp.log(1.0 - 2.0 ** (-5.0 - jnp.asarray(h)))
    t = np.arange(TB)
    same = (t[:, None] // CHUNK) == (t[None, :] // CHUNK)
    earlier = (t[None, :] // CHUNK) < (t[:, None] // CHUNK)
    dist = jnp.asarray(np.where(same, np.abs(t[:, None] - t[None, :]), t[:, None] - t[None, :]), F32)
    dmat = jnp.where(jnp.asarray(same | earlier)[None],
                     jnp.exp(dist[None] * log_g[:, None, None]), 0.0)
    tt = jnp.asarray(t, F32)
    xi = jnp.exp((tt + 1.0)[None, :] * log_g[:, None])
    zeta = jnp.exp((TB - 1.0 - tt)[None, :] * log_g[:, None])
    gblk = jnp.exp(TB * log_g)

    def lanes(tab):
        return jnp.repeat(tab.reshape(2, 2, TB), HEAD_DIM, axis=1).transpose(0, 2, 1)

    r = np.arange(LANES)
    blockdiag = (r[:, None] // HEAD_DIM) == (r[None, :] // HEAD_DIM)
    gb = jnp.where(jnp.asarray(blockdiag)[None],
                   jnp.repeat(gblk.reshape(2, 2), HEAD_DIM, axis=1)[:, :, None], 0.0)
    return dmat, lanes(xi), lanes(zeta), gb


def _ret_kernel(qkv_ref, gate_ref, dmat_ref, xi_ref, zeta_ref, gb_ref, g_ref, o_ref, r_ref):
    @pl.when(pl.program_id(1) == 0)
    def _():
        r_ref[...] = jnp.zeros_like(r_ref)

    lane = _lane((1, LANES))
    low = lane < HEAD_DIM
    r_idx = lax.broadcasted_iota(I32, (LANES, LANES), 0) // HEAD_DIM
    c_idx = lax.broadcasted_iota(I32, (LANES, LANES), 1) // HEAD_DIM
    for g in range(2):
        qg = qkv_ref[:, g * LANES:(g + 1) * LANES]
        kg = qkv_ref[:, 256 + g * LANES:256 + (g + 1) * LANES]
        vg = qkv_ref[:, 512 + g * LANES:512 + (g + 1) * LANES]
        state = r_ref[g]
        inter = _dot(qg, state.astype(BF16)) * xi_ref[g]
        parts = []
        for hh in range(2):
            qm = jnp.where(low, qg, jnp.zeros_like(qg)) if hh == 0 else jnp.where(low, jnp.zeros_like(qg), qg)
            s = _dot_nt(qm, kg) * dmat_ref[2 * g + hh]
            parts.append(_dot(s.astype(BF16), vg))
        o = jnp.where(low, parts[0], parts[1]) + inter
        kz = (kg.astype(F32) * zeta_ref[g]).astype(BF16)
        upd = jnp.where(r_idx == c_idx, _dot_tn(kz, vg), 0.0)
        r_ref[g] = state * gb_ref[g] + upd
        y = _pair_head_norm(o, g_ref[:, g * LANES:(g + 1) * LANES], True)
        o_ref[:, g * LANES:(g + 1) * LANES] = (_silu(gate_ref[:, g * LANES:(g + 1) * LANES]) * y).astype(BF16)


def _ret_call(qkvb, gates, tabs, g_b, B, T):
    nb = T // TB
    dmat, xi, zeta, gb = tabs
    const3 = lambda b, i: (0, 0, 0)
    return pl.pallas_call(
        _ret_kernel,
        out_shape=jax.ShapeDtypeStruct((B * T, GROUP_W), BF16),
        grid=(B, nb),
        in_specs=[pl.BlockSpec((TB, 768), lambda b, i: (b * nb + i, 0)),
                  pl.BlockSpec((TB, GROUP_W), lambda b, i: (b * nb + i, 0)),
                  pl.BlockSpec((N_GH, TB, TB), const3),
                  pl.BlockSpec((2, TB, LANES), const3),
                  pl.BlockSpec((2, TB, LANES), const3),
                  pl.BlockSpec((2, LANES, LANES), const3),
                  pl.BlockSpec((1, GROUP_W), lambda b, i: (0, 0))],
        out_specs=pl.BlockSpec((TB, GROUP_W), lambda b, i: (b * nb + i, 0)),
        scratch_shapes=[pltpu.VMEM((2, LANES, LANES), F32)],
        compiler_params=_cparams("arbitrary", "arbitrary"),
        name="ret",
    )(qkvb, gates, dmat, xi, zeta, gb, g_b)


def _gla_kernel(c_ref, gate_ref, wa2_ref, ba_ref, g_ref, o_ref, s_ref, b_scr, q_scr, k_scr, v_scr, acc_scr):
    @pl.when(pl.program_id(1) == 0)
    def _():
        s_ref[...] = jnp.zeros_like(s_ref)

    n_chunks = TB // CHUNK
    hmask = (lax.broadcasted_iota(I32, (LANES, GROUP_W), 0) // GLA_DK
             == lax.broadcasted_iota(I32, (LANES, GROUP_W), 1) // HEAD_DIM)
    hsum = jnp.where(hmask, 1.0, 0.0).astype(BF16)
    r_idx = lax.broadcasted_iota(I32, (TB, TB), 0)
    c_idx = lax.broadcasted_iota(I32, (TB, TB), 1)
    tri = jnp.where((r_idx >= c_idx) & (r_idx // CHUNK == c_idx // CHUNK), 1.0, 0.0).astype(BF16)

    z = _dot_split(c_ref[:, 512:640], wa2_ref[...]) + ba_ref[...]
    log_a = (jnp.minimum(z, 0.0) - jnp.log(1.0 + jnp.exp(-jnp.abs(z)))) * (1.0 / GLA_TAU)
    b = sum(_dot(tri, piece) for piece in _split_bf16(log_a, 3))
    q = c_ref[:, 0:128] * (GLA_DK ** -0.5)
    b_scr[...] = b
    q_scr[...] = q
    k_scr[...] = c_ref[:, 128:256]
    v_scr[...] = c_ref[:, 256:512]

    b_last = [b[(c + 1) * CHUNK - 1:(c + 1) * CHUNK, :] for c in range(n_chunks)]
    b_end = jnp.concatenate([jnp.broadcast_to(bl, (CHUNK, LANES)) for bl in b_last], axis=0)
    kd = (c_ref[:, 128:256] * jnp.exp(b_end - b)).astype(BF16)
    qe = (q * jnp.exp(b)).astype(BF16)
    vb = c_ref[:, 256:512].astype(BF16)
    state = s_ref[...]
    inters = []
    for c in range(n_chunks):
        rows = slice(c * CHUNK, (c + 1) * CHUNK)
        inters.append(_dot(qe[rows], state.astype(BF16)))
        scale = jnp.transpose(jnp.broadcast_to(jnp.exp(b_last[c]), (LANES, LANES)))
        state = jnp.where(hmask, state * jnp.concatenate([scale, scale], axis=1) + _dot_tn(kd[rows], vb[rows]), 0.0)
    s_ref[...] = state

    acc_scr[...] = jnp.concatenate(inters, axis=0)

    def key_step(s, carry):
        for c in range(n_chunks):
            rows = slice(c * CHUNK, (c + 1) * CHUNK)
            r = c * CHUNK + s
            decay = jnp.exp(-jnp.abs(b_scr[rows, :] - b_scr[pl.ds(r, 1), :]))
            w = (q_scr[rows, :] * decay * k_scr[pl.ds(r, 1), :]).astype(BF16)
            acc_scr[rows, :] += _dot(w, hsum) * v_scr[pl.ds(r, 1), :]
        return carry

    lax.fori_loop(0, CHUNK, key_step, 0, unroll=4)
    o = acc_scr[...]
    for gi in range(2):
        y = _pair_head_norm(o[:, gi * LANES:(gi + 1) * LANES], g_ref[:, gi * LANES:(gi + 1) * LANES], True)
        o_ref[:, gi * LANES:(gi + 1) * LANES] = (_silu(gate_ref[:, gi * LANES:(gi + 1) * LANES]) * y).astype(BF16)


def _gla_call(cc, gates, wa2_p, ba, g_c, B, T):
    nb = T // TB
    return pl.pallas_call(
        _gla_kernel,
        out_shape=jax.ShapeDtypeStruct((B * T, GROUP_W), BF16),
        grid=(B, nb),
        in_specs=[pl.BlockSpec((TB, 640), lambda b, i: (b * nb + i, 0)),
                  pl.BlockSpec((TB, GROUP_W), lambda b, i: (b * nb + i, 1)),
                  pl.BlockSpec((LANES, LANES), lambda b, i: (0, 0)),
                  pl.BlockSpec((1, LANES), lambda b, i: (0, 0)),
                  pl.BlockSpec((1, GROUP_W), lambda b, i: (0, 0))],
        out_specs=pl.BlockSpec((TB, GROUP_W), lambda b, i: (b * nb + i, 0)),
        scratch_shapes=[pltpu.VMEM((LANES, GROUP_W), F32), pltpu.VMEM((TB, LANES), F32),
                        pltpu.VMEM((TB, LANES), F32), pltpu.VMEM((TB, LANES), F32),
                        pltpu.VMEM((TB, GROUP_W), F32), pltpu.VMEM((TB, GROUP_W), F32)],
        compiler_params=_cparams("arbitrary", "arbitrary"),
        name="gla",
    )(cc, gates, wa2_p, ba, g_c)


def _rms(y, g):
    return y * lax.rsqrt(jnp.mean(y * y, axis=-1, keepdims=True) + EPS) * g


def _out_mlp_kernel(x_ref, ya_ref, yb_ref, yc_ref, yd_ref, mod_ref, wo_ref, w1_ref, w2_ref,
                    gpost_ref, gpre_ref, gpost2_ref, o_ref):
    y = _dot(ya_ref[...], wo_ref[0:256, :])
    y = y + _dot(yb_ref[...], wo_ref[256:512, :])
    y = y + _dot(yc_ref[...], wo_ref[512:768, :])
    y = y + _dot(yd_ref[...], wo_ref[768:1024, :])
    x = x_ref[...] + mod_ref[0, 2:3, :] * _rms(y, gpost_ref[...])
    h = _rms(x, gpre_ref[...]) * (1.0 + mod_ref[0, 4:5, :]) + mod_ref[0, 3:4, :]
    hb = h.astype(BF16)
    acc = jnp.zeros(x.shape, F32)
    fc = 1024
    for f in range(D_FF // fc):
        u = jnp.maximum(_dot(hb, w1_ref[:, f * fc:(f + 1) * fc]), 0.0)
        acc = acc + _dot((u * u).astype(BF16), w2_ref[f * fc:(f + 1) * fc, :])
    o_ref[...] = x + mod_ref[0, 5:6, :] * _rms(acc, gpost2_ref[...])


def _out_mlp_call(x2, ys, mod3, w_out, w1, w2, g_post, g_pre2, g_post2, T):
    N, D = x2.shape
    nt = T // TM
    row = lambda i: (i, 0)
    const = lambda i: (0, 0)
    return pl.pallas_call(
        _out_mlp_kernel,
        out_shape=jax.ShapeDtypeStruct((N, D), F32),
        grid=(N // TM,),
        in_specs=[pl.BlockSpec((TM, D), row)] + [pl.BlockSpec((TM, GROUP_W), row)] * 4
                 + [pl.BlockSpec((1, 6, D), lambda i: (i // nt, 0, 0)),
                    _resident((D, D), const), _resident((D, D_FF), const), _resident((D_FF, D), const),
                    _resident((1, D), const), _resident((1, D), const), _resident((1, D), const)],
        out_specs=pl.BlockSpec((TM, D), row),
        compiler_params=_cparams("arbitrary"),
        name="out_mlp",
    )(x2, *ys, mod3, w_out, w1, w2, g_post, g_pre2, g_post2)


def _relayout_w_in(w_in):
    col = lambda k: w_in[:, _IN_OFF[k]:_IN_OFF[k + 1]]
    zeros = lambda n: jnp.zeros((w_in.shape[0], n), w_in.dtype)
    parts = [col(_A_Q), col(_A_QI), col(_A_K), col(_A_KI), col(_A_V), zeros(64),
             col(_A_WI), zeros(120),
             col(_B_Q), col(_B_K), col(_B_V),
             col(_B_G), col(_C_G),
             col(_C_Q), col(_C_K), col(_C_V), col(_C_A), zeros(112),
             col(_D_Q), col(_D_K), col(_D_V)]
    return jnp.concatenate(parts, axis=1).astype(BF16)


def _rope_tables(pos, rot, period, theta):
    half = rot // 2
    inv = theta ** (-jnp.arange(half, dtype=F32) / half)
    ang = pos.astype(F32)[..., None] * inv
    cos, sin = jnp.cos(ang), jnp.sin(ang)
    pad = ang.shape[:-1] + (period - rot,)
    ct = jnp.concatenate([cos, cos, jnp.ones(pad, F32)], axis=-1)
    st = jnp.concatenate([-sin, sin, jnp.zeros(pad, F32)], axis=-1)
    reps = LANES // period
    return (jnp.tile(ct, (1, 1, reps)).reshape(-1, LANES), jnp.tile(st, (1, 1, reps)).reshape(-1, LANES))


def _values_t(v, B, T):
    heads = v.shape[1] // HEAD_DIM
    vt = jnp.transpose(v.reshape(B, T, heads, HEAD_DIM), (0, 2, 3, 1))
    ones = jnp.ones((B, heads, VT_ROWS - HEAD_DIM, T), v.dtype)
    return jnp.concatenate([vt, ones], axis=2).reshape(-1, T)


def _layer(x2, cond, tabs, ret_tabs, layer_idx, B, T, mod_w, mod_b, attn_pre_g, attn_post_g,
           mlp_pre_g, mlp_post_g, w_in, gla_wa2, gla_ba, lam_q1, lam_k1, lam_q2, lam_k2,
           head_norm_g, w_out, mlp_w1, mlp_w2):
    D = D_MODEL
    mod3 = _mod_call(cond, mod_w, mod_b).reshape(B, 6, D)
    qa, kva, wa, qkvb, gates, cc, dqkv = _in_call(
        x2, mod3, attn_pre_g.reshape(1, D), _relayout_w_in(w_in), tabs, T)

    g_a, g_b, g_c, g_d = jnp.split(head_norm_g, 4)
    gain_t = lambda g: jnp.broadcast_to(g[:, None], (GROUP_W, TQ))
    values_t = lambda v: _values_t(v, B, T)
    y_a = _dsa_call(qa, wa, kva, values_t(kva[:, LANES:LANES + HEAD_DIM]), gain_t(g_a), B, T)
    y_b = _ret_call(qkvb, gates, ret_tabs, g_b.reshape(1, GROUP_W), B, T)
    wa2_p = jnp.zeros((LANES, LANES), F32).at[:GLA_RANK].set(gla_wa2)
    y_c = _gla_call(cc, gates, wa2_p, gla_ba.reshape(1, LANES), g_c.reshape(1, GROUP_W), B, T)
    lam_init = 0.8 - 0.6 * math.exp(-0.3 * layer_idx)
    lam_vecs = jnp.zeros((4, LANES), F32).at[:, :DIFF_DK].set(jnp.stack([lam_q1, lam_k1, lam_q2, lam_k2]))
    y_d = _diff_call(dqkv, values_t(dqkv[:, 512:768]), lam_vecs, gain_t(g_d), lam_init, B, T)

    return _out_mlp_call(x2, (y_a, y_b, y_c, y_d), mod3, w_out.astype(BF16), mlp_w1.astype(BF16),
                         mlp_w2.astype(BF16), attn_post_g.reshape(1, D), mlp_pre_g.reshape(1, D),
                         mlp_post_g.reshape(1, D), T)


def kernel(x, c, positions, mod_w, mod_b, attn_pre_g, attn_post_g, mlp_pre_g, mlp_post_g, w_in,
           gla_wa2, gla_ba, lam_q1, lam_k1, lam_q2, lam_k2, head_norm_g, w_out, mlp_w1, mlp_w2):
    B, T, D = x.shape
    assert D == D_MODEL and T % TM == 0 and T % TB == 0
    tabs = (_rope_tables(positions, HEAD_DIM // ROPE_FRAC, HEAD_DIM, ROPE_THETA)
            + _rope_tables(positions, HEAD_DIM, HEAD_DIM, RET_THETA)
            + _rope_tables(positions, DIFF_DK // ROPE_FRAC, DIFF_DK, ROPE_THETA))
    ret_tabs = _ret_tables()
    x2 = x.reshape(B * T, D)
    for l in range(mod_w.shape[0]):
        x2 = _layer(x2, c, tabs, ret_tabs, l, B, T, mod_w[l], mod_b[l], attn_pre_g[l], attn_post_g[l],
                    mlp_pre_g[l], mlp_post_g[l], w_in[l], gla_wa2[l], gla_ba[l], lam_q1[l], lam_k1[l],
                    lam_q2[l], lam_k2[l], head_norm_g[l], w_out[l], mlp_w1[l], mlp_w2[l])
    return x2.reshape(B, T, D)
```

```python
import functools
import math

import numpy as np
import jax
import jax.numpy as jnp
from jax import lax
from jax.experimental import pallas as pl
from jax.experimental.pallas import tpu as pltpu

F32 = jnp.float32
BF16 = jnp.bfloat16
I32 = jnp.int32
I16 = jnp.int16

D_MODEL = 1024
CHUNK = 64
HEAD_DIM = 64
N_GH = 4
GROUP_W = N_GH * HEAD_DIM
IDX_HEADS = 8
IDX_DIM = 64
TOPK_MAX = 256
RET_THETA = 10000.0
GLA_DK = HEAD_DIM // 2
GLA_RANK = 16
GLA_TAU = 16.0
DIFF_DK = HEAD_DIM // 2
ROPE_THETA = 500000.0
ROPE_FRAC = 4
D_FF = 4 * D_MODEL
EPS = 1e-6
NEG_INF = -1e30

LANES = 128
VMEM_LIMIT = 56 * 1024 * 1024

_IN_SPLITS = (
    GROUP_W, HEAD_DIM, HEAD_DIM, IDX_HEADS * IDX_DIM, IDX_DIM, IDX_HEADS,
    GROUP_W, GROUP_W, GROUP_W, GROUP_W,
    N_GH * GLA_DK, N_GH * GLA_DK, GROUP_W, GLA_RANK, GROUP_W,
    2 * N_GH * DIFF_DK, 2 * N_GH * DIFF_DK, GROUP_W,
)
_IN_OFF = np.concatenate([[0], np.cumsum(_IN_SPLITS)]).tolist()
(_A_Q, _A_K, _A_V, _A_QI, _A_KI, _A_WI, _B_Q, _B_K, _B_V, _B_G,
 _C_Q, _C_K, _C_V, _C_A, _C_G, _D_Q, _D_K, _D_V) = range(18)

SEG_A = 0
SEG_W = 1024
SEG_B = 1152
SEG_G = 1920
SEG_C = 2432
SEG_D = 3072
W_COLS = 3840

NT_DIMS = (((1,), (1,)), ((), ()))
TN_DIMS = (((0,), (0,)), ((), ()))

TM = 512
TQ = 256
TK = 256
FLASH_SLAB = 32
VT_ROWS = HEAD_DIM + 16
LOG2E = math.log2(math.e)
TB = 256


def _float_key(v):
    bits = int(np.array(v, np.float32).view(np.int32))
    return -(bits & 0x7FFFFFFF) if bits < 0 else bits


NEG_KEY = _float_key(NEG_INF)
INT_MIN = -(2 ** 31)


def _cparams(*sem):
    return pltpu.CompilerParams(dimension_semantics=sem, vmem_limit_bytes=VMEM_LIMIT)


def _resident(shape, index_map):
    return pl.BlockSpec(shape, index_map, pipeline_mode=pl.Buffered(1))


def _dot(a, b):
    return jnp.dot(a, b, preferred_element_type=F32)


def _dot_nt(a, b):
    return lax.dot_general(a, b, NT_DIMS, preferred_element_type=F32)


def _dot_tn(a, b):
    return lax.dot_general(a, b, TN_DIMS, preferred_element_type=F32)


def _dot_f32(a, b):
    return jnp.dot(a, b, preferred_element_type=F32, precision=lax.Precision.HIGHEST)


def _split_bf16(x, n):
    parts = []
    for _ in range(n):
        p = x.astype(BF16)
        parts.append(p)
        x = x - p.astype(F32)
    return parts


def _dot_split(a, b):
    (a0, a1), (b0, b1) = _split_bf16(a, 2), _split_bf16(b, 2)
    return _dot(a0, b0) + (_dot(a0, b1) + _dot(a1, b0))


def _lane(shape):
    return lax.broadcasted_iota(I32, shape, len(shape) - 1)


def _silu(x):
    return x / (1.0 + jnp.exp(-x))


def _mod_kernel(c_ref, w_ref, b_ref, o_ref):
    cond = _silu(c_ref[...])
    o_ref[...] = _dot_f32(cond, w_ref[...]) + b_ref[...]


def _mod_call(c, mod_w, mod_b):
    B, D = c.shape
    n = mod_w.shape[1] // D
    return pl.pallas_call(
        _mod_kernel,
        out_shape=jax.ShapeDtypeStruct((B, n * D), F32),
        grid=(n,),
        in_specs=[pl.BlockSpec((B, D), lambda j: (0, 0)),
                  pl.BlockSpec((D, D), lambda j: (0, j)),
                  pl.BlockSpec((1, D), lambda j: (0, j))],
        out_specs=pl.BlockSpec((B, D), lambda j: (0, j)),
        compiler_params=_cparams("arbitrary"),
        name="mod",
    )(c, mod_w, mod_b.reshape(1, -1))


def _rope_group(xg, cos, sin, half, period):
    lo = (_lane((1, LANES)) % period) < half
    swapped = jnp.where(lo, pltpu.roll(xg, LANES - half, 1), pltpu.roll(xg, half, 1))
    return xg * cos + swapped * sin


def _in_kernel(x_ref, mod_ref, g_ref, w_ref, ca_ref, sa_ref, cb_ref, sb_ref, cd_ref, sd_ref,
               qa_ref, kva_ref, wa_ref, qkvb_ref, gate_ref, c_ref, d_ref):
    x = x_ref[...]
    h = x * lax.rsqrt(jnp.mean(x * x, axis=-1, keepdims=True) + EPS) * g_ref[...]
    h = h * (1.0 + mod_ref[0, 1:2, :]) + mod_ref[0, 0:1, :]
    hb = h.astype(BF16)

    def seg(off, width):
        return _dot(hb, w_ref[:, off:off + width])

    lane = _lane((1, LANES))
    low = lane < HEAD_DIM

    ca, sa = ca_ref[...], sa_ref[...]
    a = seg(SEG_A, 1024)
    half_a = HEAD_DIM // ROPE_FRAC // 2
    for gi in range(2):
        r = _rope_group(a[:, gi * LANES:(gi + 1) * LANES], ca, sa, half_a, HEAD_DIM) * (HEAD_DIM ** -0.5 * LOG2E)
        rs = pltpu.roll(r, HEAD_DIM, 1)
        qa_ref[:, (2 * gi) * LANES:(2 * gi + 1) * LANES] = jnp.where(low, r, 0.0).astype(BF16)
        qa_ref[:, (2 * gi + 1) * LANES:(2 * gi + 2) * LANES] = jnp.where(low, rs, 0.0).astype(BF16)
    for gi in range(4):
        r = _rope_group(a[:, (2 + gi) * LANES:(3 + gi) * LANES], ca, sa, half_a, HEAD_DIM)
        rs = pltpu.roll(r, HEAD_DIM, 1)
        qa_ref[:, (4 + 2 * gi) * LANES:(5 + 2 * gi) * LANES] = jnp.where(low, 0.0, rs).astype(BF16)
        qa_ref[:, (5 + 2 * gi) * LANES:(6 + 2 * gi) * LANES] = jnp.where(low, 0.0, r).astype(BF16)
    kva_ref[:, 0:LANES] = _rope_group(a[:, 6 * LANES:7 * LANES], ca, sa, half_a, HEAD_DIM).astype(BF16)
    kva_ref[:, LANES:2 * LANES] = a[:, 7 * LANES:8 * LANES].astype(BF16)

    wa_ref[...] = seg(SEG_W, LANES)

    cb, sb = cb_ref[...], sb_ref[...]
    b = seg(SEG_B, 768)
    for gi in range(4):
        r = _rope_group(b[:, gi * LANES:(gi + 1) * LANES], cb, sb, HEAD_DIM // 2, HEAD_DIM)
        if gi < 2:
            r = r * (HEAD_DIM ** -0.5)
        qkvb_ref[:, gi * LANES:(gi + 1) * LANES] = r.astype(BF16)
    qkvb_ref[:, 512:768] = b[:, 512:768].astype(BF16)

    gate_ref[...] = seg(SEG_G, 512)
    c_ref[...] = seg(SEG_C, 640)

    cd, sd = cd_ref[...], sd_ref[...]
    d = seg(SEG_D, 768)
    half_d = DIFF_DK // ROPE_FRAC // 2
    for gi in range(4):
        r = _rope_group(d[:, gi * LANES:(gi + 1) * LANES], cd, sd, half_d, DIFF_DK)
        if gi < 2:
            r = r * (DIFF_DK ** -0.5 * LOG2E)
        d_ref[:, gi * LANES:(gi + 1) * LANES] = r.astype(BF16)
    d_ref[:, 512:768] = d[:, 512:768].astype(BF16)


def _in_call(x2, mod3, g, w_p, tabs, T):
    N, D = x2.shape
    nt = T // TM
    row = lambda i: (i, 0)
    tab_spec = pl.BlockSpec((TM, LANES), row)
    outs = [(1536, BF16), (256, BF16), (LANES, F32), (768, BF16), (512, F32), (640, F32), (768, BF16)]
    return pl.pallas_call(
        _in_kernel,
        out_shape=[jax.ShapeDtypeStruct((N, w), dt) for w, dt in outs],
        grid=(N // TM,),
        in_specs=[pl.BlockSpec((TM, D), row),
                  pl.BlockSpec((1, 6, D), lambda i: (i // nt, 0, 0)),
                  _resident((1, D), lambda i: (0, 0)),
                  _resident((D, W_COLS), lambda i: (0, 0))] + [tab_spec] * 6,
        out_specs=[pl.BlockSpec((TM, w), row) for w, _ in outs],
        compiler_params=_cparams("arbitrary"),
        name="in_proj",
    )(x2, mod3, g, w_p, *tabs)


def _pair_head_norm(o, gain, center):
    low = _lane((1, LANES)) < HEAD_DIM
    inv = 1.0 / HEAD_DIM

    def seg_mean(v):
        m0 = jnp.sum(jnp.where(low, v, 0.0), axis=1, keepdims=True)
        m1 = jnp.sum(jnp.where(low, 0.0, v), axis=1, keepdims=True)
        return jnp.where(low, m0, m1) * inv

    if center:
        o = o - seg_mean(o)
    return o * lax.rsqrt(seg_mean(o * o) + EPS) * gain


class _Flash:
    def __init__(self, n, s_ref, p_ref, acc_ref, st_ref):
        self.n, self.s, self.p, self.acc, self.st = n, s_ref, p_ref, acc_ref, st_ref

    @staticmethod
    def scratch(n):
        return [pltpu.VMEM((n, TK, TQ), F32), pltpu.VMEM((n, TK, TQ), BF16),
                pltpu.VMEM((n, VT_ROWS, TQ), F32), pltpu.VMEM((3, n, TQ), F32)]

    def _row(self, k, i):
        return self.st.at[k, i:i + 1, :]

    def init(self):
        self.st[0] = jnp.full((self.n, TQ), NEG_INF, F32)
        self.acc[...] = jnp.zeros_like(self.acc)

    def scores(self, i, s):
        self.s[i] = s
        self._row(1, i)[...] = jnp.max(_fold8(s, jnp.maximum), axis=0, keepdims=True)

    def probs(self, i):
        m_old = self._row(0, i)[...]
        m_new = jnp.maximum(m_old, self._row(1, i)[...])
        for r in range(0, TK, FLASH_SLAB):
            self.p[i, r:r + FLASH_SLAB, :] = jnp.exp2(self.s[i, r:r + FLASH_SLAB, :] - m_new).astype(BF16)
        self._row(0, i)[...] = m_new
        self._row(2, i)[...] = jnp.exp2(m_old - m_new)

    def values(self, i, vt):
        self.acc[i] = self._row(2, i)[...] * self.acc[i] + _dot(vt, self.p[i])

    def result(self, i):
        return self.acc[i, 0:HEAD_DIM, :] / self.acc[i, HEAD_DIM:HEAD_DIM + 1, :]

    def pipeline(self, first, n_rest, score, value):
        score(first, True)

        def body(t, carry):
            for i in range(self.n):
                self.probs(i)
            score(t, False)
            value(jnp.where(t == 0, first, t - 1))
            return carry

        lax.fori_loop(0, n_rest, body, 0)
        for i in range(self.n):
            self.probs(i)
        value(jnp.where(n_rest == 0, first, n_rest - 1))


def _fold8(x, op=jnp.add, rows=8):
    parts = [x[r:r + rows] for r in range(0, x.shape[0], rows)]
    while len(parts) > 1:
        parts = [op(a, b) for a, b in zip(parts[0::2], parts[1::2])]
    return parts[0]


def _head_norm_t(o, gain):
    ms = jnp.mean(o * o, axis=0, keepdims=True)
    return o * lax.rsqrt(ms + EPS) * gain


def _dsa_kernel(q_ref, w_ref, kv_ref, vt_ref, g_ref, o_ref, bias_ref, key_ref, hi_ref, lo_ref, *flash_refs, topk):
    q0 = pl.program_id(1) * TQ
    nkt = (q0 + TQ) // TK
    qchunk = (q0 + _lane((1, TQ))) // CHUNK
    wts = jnp.transpose(w_ref[...]) * (IDX_HEADS ** -0.5 * IDX_DIM ** -0.5)

    def score_tile(kt, carry):
        k0 = pl.multiple_of(kt * TK, TK)
        kk = kv_ref[pl.ds(k0, TK), 0:LANES]
        tile = bias_ref.at[pl.ds(k0, TK), :]
        for h in range(IDX_HEADS):
            term = jnp.maximum(_dot_nt(kk, q_ref[:, (4 + h) * LANES:(5 + h) * LANES]), 0.0) * wts[h:h + 1, :]
            if h == 0:
                tile[...] = term
            elif h < IDX_HEADS - 1:
                tile[...] += term
        kchunk = (k0 + lax.broadcasted_iota(I32, (TK, 1), 0)) // CHUNK
        acc = jnp.where(kchunk <= qchunk, tile[...] + term, NEG_INF)
        bits = pltpu.bitcast(acc, I32)
        sign = bits >> 31
        key = ((bits & 0x7FFFFFFF) ^ sign) - sign
        key_ref[pl.ds(k0, TK), :] = key
        hi_ref[pl.ds(k0, TK), :] = (key >> 16).astype(I16)
        lo_ref[pl.ds(k0, TK), :] = (key ^ 0x8000).astype(I16)
        return carry

    lax.fori_loop(0, nkt, score_tile, 0)

    @pl.when(nkt % 2 == 1)
    def _():
        pad0 = pl.multiple_of(nkt * TK, TK)
        hi_ref[pl.ds(pad0, TK), :] = jnp.full((TK, TQ), -32768, I16)
        lo_ref[pl.ds(pad0, TK), :] = jnp.full((TK, TQ), -32768, I16)

    kf = float(topk)
    n_pairs = (nkt + 1) // 2

    def count16(ref, cand):
        def body(kp, acc):
            k0 = pl.multiple_of(kp * (2 * TK), 2 * TK)
            hit = jnp.where(ref[pl.ds(k0, 2 * TK), :] >= cand,
                            jnp.ones((2 * TK, TQ), I16), jnp.zeros((2 * TK, TQ), I16))
            return acc + _fold8(hit, rows=16)
        acc = lax.fori_loop(0, n_pairs, body, jnp.zeros((16, TQ), I16))
        return jnp.sum(acc.astype(F32), axis=0, keepdims=True)

    def search16(ref, cnt_all):
        c0 = count16(ref, jnp.zeros((1, TQ), I16))
        t = jnp.where(c0 >= kf, 0, -32768).astype(I32)
        cnt = jnp.where(c0 >= kf, c0, cnt_all)

        def step(it, carry):
            t, cnt = carry
            cand = t + (jnp.int32(1) << (14 - it))
            c = count16(ref, cand.astype(I16))
            ok = c >= kf
            return jnp.where(ok, cand, t), jnp.where(ok, c, cnt)

        return lax.fori_loop(0, 15, step, (t, cnt))

    def count(pred):
        def body(kt, acc):
            k0 = pl.multiple_of(kt * TK, TK)
            hit = jnp.where(pred(key_ref[pl.ds(k0, TK), :], k0), 1.0, 0.0)
            return acc + _fold8(hit)
        acc = lax.fori_loop(0, nkt, body, jnp.zeros((8, TQ), F32))
        return jnp.sum(acc, axis=0, keepdims=True)

    t_hi, cnt_hi = search16(hi_ref, (nkt * TK).astype(F32))
    t_hi16 = t_hi.astype(I16)

    def low_digit_tile(kp, carry):
        k0 = pl.multiple_of(kp * (2 * TK), 2 * TK)
        hi = hi_ref[pl.ds(k0, 2 * TK), :]
        in_bucket = jnp.where(hi == t_hi16, lo_ref[pl.ds(k0, 2 * TK), :], jnp.full((2 * TK, TQ), -32768, I16))
        hi_ref[pl.ds(k0, 2 * TK), :] = jnp.where(hi > t_hi16, jnp.full((2 * TK, TQ), 32767, I16), in_bucket)
        return carry

    lax.fori_loop(0, n_pairs, low_digit_tile, 0)
    t_lo, cnt = search16(hi_ref, cnt_hi)
    thr = t_hi * 65536 + (t_lo + 32768)
    floor_key = jnp.int32(NEG_KEY + 1)
    tied = jnp.max(jnp.where((cnt > kf) & (thr >= floor_key), 1.0, 0.0)) > 0.0

    @pl.when(jnp.logical_not(tied))
    def _():
        lim = jnp.maximum(thr, floor_key)

        def body(kt, carry):
            k0 = pl.multiple_of(kt * TK, TK)
            bias_ref[pl.ds(k0, TK), :] = jnp.where(key_ref[pl.ds(k0, TK), :] >= lim, 0.0, NEG_INF)
            return carry
        lax.fori_loop(0, nkt, body, 0)

    @pl.when(tied)
    def _():
        need = kf - count(lambda keys, k0: keys > thr)

        def idx(k0):
            return k0 + lax.broadcasted_iota(I32, (TK, 1), 0)

        def search_idx(it, lim):
            cand = lim + (jnp.int32(1) << (14 - it))
            c = count(lambda keys, k0: (keys == thr) & (idx(k0) < cand))
            return jnp.where(c <= need, cand, lim)

        lim = lax.fori_loop(0, 15, search_idx, jnp.zeros((1, TQ), I32))

        def body(kt, carry):
            k0 = pl.multiple_of(kt * TK, TK)
            keys = key_ref[pl.ds(k0, TK), :]
            sel = ((keys > thr) | ((keys == thr) & (idx(k0) < lim))) & (keys >= floor_key)
            bias_ref[pl.ds(k0, TK), :] = jnp.where(sel, 0.0, NEG_INF)
            return carry
        lax.fori_loop(0, nkt, body, 0)

    flash = _Flash(N_GH, *flash_refs)
    flash.init()

    def score(kt, is_first):
        k0 = pl.multiple_of(kt * TK, TK)
        kk = kv_ref[pl.ds(k0, TK), 0:LANES]
        bias = bias_ref[pl.ds(k0, TK), :]
        for h in range(N_GH):
            flash.scores(h, _dot_nt(kk, q_ref[:, h * LANES:(h + 1) * LANES]) + bias)

    def value(kt):
        k0 = pl.multiple_of(kt * TK, TK)
        for h in range(N_GH):
            flash.values(h, vt_ref[:, pl.ds(k0, TK)])

    flash.pipeline(nkt - 1, nkt - 1, score, value)
    y = jnp.concatenate([_head_norm_t(flash.result(h), g_ref[h * HEAD_DIM:(h + 1) * HEAD_DIM, :])
                         for h in range(N_GH)], axis=0)
    o_ref[...] = jnp.transpose(y).astype(BF16)


def _dsa_call(qa, wa, kva, vta, g_t, B, T):
    nq = T // TQ
    topk = min(TOPK_MAX, T // 4)
    return pl.pallas_call(
        functools.partial(_dsa_kernel, topk=topk),
        out_shape=jax.ShapeDtypeStruct((B * T, GROUP_W), BF16),
        grid=(B, nq),
        in_specs=[pl.BlockSpec((TQ, 1536), lambda b, i: (b * nq + i, 0)),
                  pl.BlockSpec((TQ, LANES), lambda b, i: (b * nq + i, 0)),
                  pl.BlockSpec((T, 256), lambda b, i: (b, 0)),
                  pl.BlockSpec((VT_ROWS, T), lambda b, i: (b, 0)),
                  pl.BlockSpec((GROUP_W, TQ), lambda b, i: (0, 0))],
        out_specs=pl.BlockSpec((TQ, GROUP_W), lambda b, i: (b * nq + i, 0)),
        scratch_shapes=[pltpu.VMEM((T, TQ), F32), pltpu.VMEM((T, TQ), I32), pltpu.VMEM((T, TQ), I16),
                        pltpu.VMEM((T, TQ), I16)] + _Flash.scratch(N_GH),
        compiler_params=_cparams("arbitrary", "arbitrary"),
        name="dsa",
    )(qa, wa, kva, vta, g_t)


def _diff_kernel(q_ref, k_ref, vt_ref, lam_ref, g_ref, o_ref, qm_ref, *flash_refs, lam_init):
    q0 = pl.program_id(1) * TQ
    nfull = (q0 + CHUNK) // TK
    qchunk = (q0 + _lane((1, TQ))) // CHUNK
    lv = lam_ref[...]
    lam = (jnp.exp(jnp.sum(lv[0:1] * lv[1:2], axis=1, keepdims=True))
           - jnp.exp(jnp.sum(lv[2:3] * lv[3:4], axis=1, keepdims=True)) + lam_init)
    n_maps = 2 * N_GH
    qt = jnp.transpose(q_ref[...].astype(F32))
    sub = lax.broadcasted_iota(I32, (LANES, 1), 0) // DIFF_DK
    for mi in range(n_maps):
        qg = qt[(mi // 4) * LANES:(mi // 4 + 1) * LANES, :]
        qm_ref[mi] = jnp.where(sub == mi % 4, qg, 0.0).astype(BF16)

    flash = _Flash(n_maps, *flash_refs)
    flash.init()

    def score(kt, masked):
        k0 = pl.multiple_of(kt * TK, TK)
        if masked:
            kchunk = (k0 + lax.broadcasted_iota(I32, (TK, 1), 0)) // CHUNK
            admissible = kchunk <= qchunk
        for mi in range(n_maps):
            g = mi // 4
            s = _dot(k_ref[pl.ds(k0, TK), g * LANES:(g + 1) * LANES], qm_ref[mi])
            flash.scores(mi, jnp.where(admissible, s, NEG_INF) if masked else s)

    def value(kt):
        k0 = pl.multiple_of(kt * TK, TK)
        for mi in range(n_maps):
            h = mi // 2
            flash.values(mi, vt_ref[h * VT_ROWS:(h + 1) * VT_ROWS, pl.ds(k0, TK)])

    flash.pipeline(nfull, nfull, score, value)
    outs = []
    for h in range(N_GH):
        o = flash.result(2 * h) - lam * flash.result(2 * h + 1)
        outs.append(_head_norm_t(o, g_ref[h * HEAD_DIM:(h + 1) * HEAD_DIM, :]) * (1.0 - lam_init))
    o_ref[...] = jnp.transpose(jnp.concatenate(outs, axis=0)).astype(BF16)


def _diff_call(dqkv, vtd, lam_vecs, g_t, lam_init, B, T):
    nq = T // TQ
    return pl.pallas_call(
        functools.partial(_diff_kernel, lam_init=lam_init),
        out_shape=jax.ShapeDtypeStruct((B * T, GROUP_W), BF16),
        grid=(B, nq),
        in_specs=[pl.BlockSpec((TQ, 256), lambda b, i: (b * nq + i, 0)),
                  pl.BlockSpec((T, 256), lambda b, i: (b, 1)),
                  pl.BlockSpec((N_GH * VT_ROWS, T), lambda b, i: (b, 0)),
                  pl.BlockSpec((4, LANES), lambda b, i: (0, 0)),
                  pl.BlockSpec((GROUP_W, TQ), lambda b, i: (0, 0))],
        out_specs=pl.BlockSpec((TQ, GROUP_W), lambda b, i: (b * nq + i, 0)),
        scratch_shapes=[pltpu.VMEM((2 * N_GH, LANES, TQ), BF16)] + _Flash.scratch(2 * N_GH),
        compiler_params=_cparams("arbitrary", "arbitrary"),
        name="diff",
    )(dqkv, dqkv, vtd, lam_vecs, g_t)


def _ret_tables():
    h = np.arange(N_GH, dtype=np.float32)
    log_g = jnp.log(1.0 - 2.0 ** (-5.0 - jnp.asarray(h)))
    t = np.arange(TB)
    same = (t[:, None] // CHUNK) == (t[None, :] // CHUNK)
    earlier = (t[None, :] // CHUNK) < (t[:, None] // CHUNK)
    dist = jnp.asarray(np.where(same, np.abs(t[:, None] - t[None, :]), t[:, None] - t[None, :]), F32)
    dmat = jnp.where(jnp.asarray(same | earlier)[None],
                     jnp.exp(dist[None] * log_g[:, None, None]), 0.0)
    tt = jnp.asarray(t, F32)
    xi = jnp.exp((tt + 1.0)[None, :] * log_g[:, None])
    zeta = jnp.exp((TB - 1.0 - tt)[None, :] * log_g[:, None])
    gblk = jnp.exp(TB * log_g)

    def lanes(tab):
        return jnp.repeat(tab.reshape(2, 2, TB), HEAD_DIM, axis=1).transpose(0, 2, 1)

    r = np.arange(LANES)
    blockdiag = (r[:, None] // HEAD_DIM) == (r[None, :] // HEAD_DIM)
    gb = jnp.where(jnp.asarray(blockdiag)[None],
                   jnp.repeat(gblk.reshape(2, 2), HEAD_DIM, axis=1)[:, :, None], 0.0)
    return dmat, lanes(xi), lanes(zeta), gb


def _ret_kernel(qkv_ref, gate_ref, dmat_ref, xi_ref, zeta_ref, gb_ref, g_ref, o_ref, r_ref):
    @pl.when(pl.program_id(1) == 0)
    def _():
        r_ref[...] = jnp.zeros_like(r_ref)

    lane = _lane((1, LANES))
    low = lane < HEAD_DIM
    r_idx = lax.broadcasted_iota(I32, (LANES, LANES), 0) // HEAD_DIM
    c_idx = lax.broadcasted_iota(I32, (LANES, LANES), 1) // HEAD_DIM
    for g in range(2):
        qg = qkv_ref[:, g * LANES:(g + 1) * LANES]
        kg = qkv_ref[:, 256 + g * LANES:256 + (g + 1) * LANES]
        vg = qkv_ref[:, 512 + g * LANES:512 + (g + 1) * LANES]
        state = r_ref[g]
        inter = _dot(qg, state.astype(BF16)) * xi_ref[g]
        parts = []
        for hh in range(2):
            qm = jnp.where(low, qg, jnp.zeros_like(qg)) if hh == 0 else jnp.where(low, jnp.zeros_like(qg), qg)
            s = _dot_nt(qm, kg) * dmat_ref[2 * g + hh]
            parts.append(_dot(s.astype(BF16), vg))
        o = jnp.where(low, parts[0], parts[1]) + inter
        kz = (kg.astype(F32) * zeta_ref[g]).astype(BF16)
        upd = jnp.where(r_idx == c_idx, _dot_tn(kz, vg), 0.0)
        r_ref[g] = state * gb_ref[g] + upd
        y = _pair_head_norm(o, g_ref[:, g * LANES:(g + 1) * LANES], True)
        o_ref[:, g * LANES:(g + 1) * LANES] = (_silu(gate_ref[:, g * LANES:(g + 1) * LANES]) * y).astype(BF16)


def _ret_call(qkvb, gates, tabs, g_b, B, T):
    nb = T // TB
    dmat, xi, zeta, gb = tabs
    const3 = lambda b, i: (0, 0, 0)
    return pl.pallas_call(
        _ret_kernel,
        out_shape=jax.ShapeDtypeStruct((B * T, GROUP_W), BF16),
        grid=(B, nb),
        in_specs=[pl.BlockSpec((TB, 768), lambda b, i: (b * nb + i, 0)),
                  pl.BlockSpec((TB, GROUP_W), lambda b, i: (b * nb + i, 0)),
                  pl.BlockSpec((N_GH, TB, TB), const3),
                  pl.BlockSpec((2, TB, LANES), const3),
                  pl.BlockSpec((2, TB, LANES), const3),
                  pl.BlockSpec((2, LANES, LANES), const3),
                  pl.BlockSpec((1, GROUP_W), lambda b, i: (0, 0))],
        out_specs=pl.BlockSpec((TB, GROUP_W), lambda b, i: (b * nb + i, 0)),
        scratch_shapes=[pltpu.VMEM((2, LANES, LANES), F32)],
        compiler_params=_cparams("arbitrary", "arbitrary"),
        name="ret",
    )(qkvb, gates, dmat, xi, zeta, gb, g_b)


def _gla_kernel(c_ref, gate_ref, wa2_ref, ba_ref, g_ref, o_ref, s_ref, b_scr, q_scr, k_scr, v_scr, acc_scr):
    @pl.when(pl.program_id(1) == 0)
    def _():
        s_ref[...] = jnp.zeros_like(s_ref)

    n_chunks = TB // CHUNK
    hmask = (lax.broadcasted_iota(I32, (LANES, GROUP_W), 0) // GLA_DK
             == lax.broadcasted_iota(I32, (LANES, GROUP_W), 1) // HEAD_DIM)
    hsum = jnp.where(hmask, 1.0, 0.0).astype(BF16)
    r_idx = lax.broadcasted_iota(I32, (TB, TB), 0)
    c_idx = lax.broadcasted_iota(I32, (TB, TB), 1)
    tri = jnp.where((r_idx >= c_idx) & (r_idx // CHUNK == c_idx // CHUNK), 1.0, 0.0).astype(BF16)

    z = _dot_split(c_ref[:, 512:640], wa2_ref[...]) + ba_ref[...]
    log_a = (jnp.minimum(z, 0.0) - jnp.log(1.0 + jnp.exp(-jnp.abs(z)))) * (1.0 / GLA_TAU)
    b = sum(_dot(tri, piece) for piece in _split_bf16(log_a, 3))
    q = c_ref[:, 0:128] * (GLA_DK ** -0.5)
    b_scr[...] = b
    q_scr[...] = q
    k_scr[...] = c_ref[:, 128:256]
    v_scr[...] = c_ref[:, 256:512]

    b_last = [b[(c + 1) * CHUNK - 1:(c + 1) * CHUNK, :] for c in range(n_chunks)]
    b_end = jnp.concatenate([jnp.broadcast_to(bl, (CHUNK, LANES)) for bl in b_last], axis=0)
    kd = (c_ref[:, 128:256] * jnp.exp(b_end - b)).astype(BF16)
    qe = (q * jnp.exp(b)).astype(BF16)
    vb = c_ref[:, 256:512].astype(BF16)
    state = s_ref[...]
    inters = []
    for c in range(n_chunks):
        rows = slice(c * CHUNK, (c + 1) * CHUNK)
        inters.append(_dot(qe[rows], state.astype(BF16)))
        scale = jnp.transpose(jnp.broadcast_to(jnp.exp(b_last[c]), (LANES, LANES)))
        state = jnp.where(hmask, state * jnp.concatenate([scale, scale], axis=1) + _dot_tn(kd[rows], vb[rows]), 0.0)
    s_ref[...] = state

    acc_scr[...] = jnp.concatenate(inters, axis=0)

    def key_step(s, carry):
        for c in range(n_chunks):
            rows = slice(c * CHUNK, (c + 1) * CHUNK)
            r = c * CHUNK + s
            decay = jnp.exp(-jnp.abs(b_scr[rows, :] - b_scr[pl.ds(r, 1), :]))
            w = (q_scr[rows, :] * decay * k_scr[pl.ds(r, 1), :]).astype(BF16)
            acc_scr[rows, :] += _dot(w, hsum) * v_scr[pl.ds(r, 1), :]
        return carry

    lax.fori_loop(0, CHUNK, key_step, 0, unroll=4)
    o = acc_scr[...]
    for gi in range(2):
        y = _pair_head_norm(o[:, gi * LANES:(gi + 1) * LANES], g_ref[:, gi * LANES:(gi + 1) * LANES], True)
        o_ref[:, gi * LANES:(gi + 1) * LANES] = (_silu(gate_ref[:, gi * LANES:(gi + 1) * LANES]) * y).astype(BF16)


def _gla_call(cc, gates, wa2_p, ba, g_c, B, T):
    nb = T // TB
    return pl.pallas_call(
        _gla_kernel,
        out_shape=jax.ShapeDtypeStruct((B * T, GROUP_W), BF16),
        grid=(B, nb),
        in_specs=[pl.BlockSpec((TB, 640), lambda b, i: (b * nb + i, 0)),
                  pl.BlockSpec((TB, GROUP_W), lambda b, i: (b * nb + i, 1)),
                  pl.BlockSpec((LANES, LANES), lambda b, i: (0, 0)),
                  pl.BlockSpec((1, LANES), lambda b, i: (0, 0)),
                  pl.BlockSpec((1, GROUP_W), lambda b, i: (0, 0))],
        out_specs=pl.BlockSpec((TB, GROUP_W), lambda b, i: (b * nb + i, 0)),
        scratch_shapes=[pltpu.VMEM((LANES, GROUP_W), F32), pltpu.VMEM((TB, LANES), F32),
                        pltpu.VMEM((TB, LANES), F32), pltpu.VMEM((TB, LANES), F32),
                        pltpu.VMEM((TB, GROUP_W), F32), pltpu.VMEM((TB, GROUP_W), F32)],
        compiler_params=_cparams("arbitrary", "arbitrary"),
        name="gla",
    )(cc, gates, wa2_p, ba, g_c)


def _rms(y, g):
    return y * lax.rsqrt(jnp.mean(y * y, axis=-1, keepdims=True) + EPS) * g


def _out_mlp_kernel(x_ref, ya_ref, yb_ref, yc_ref, yd_ref, mod_ref, wo_ref, w1_ref, w2_ref,
                    gpost_ref, gpre_ref, gpost2_ref, o_ref):
    y = _dot(ya_ref[...], wo_ref[0:256, :])
    y = y + _dot(yb_ref[...], wo_ref[256:512, :])
    y = y + _dot(yc_ref[...], wo_ref[512:768, :])
    y = y + _dot(yd_ref[...], wo_ref[768:1024, :])
    x = x_ref[...] + mod_ref[0, 2:3, :] * _rms(y, gpost_ref[...])
    h = _rms(x, gpre_ref[...]) * (1.0 + mod_ref[0, 4:5, :]) + mod_ref[0, 3:4, :]
    hb = h.astype(BF16)
    acc = jnp.zeros(x.shape, F32)
    fc = 1024
    for f in range(D_FF // fc):
        u = jnp.maximum(_dot(hb, w1_ref[:, f * fc:(f + 1) * fc]), 0.0)
        acc = acc + _dot((u * u).astype(BF16), w2_ref[f * fc:(f + 1) * fc, :])
    o_ref[...] = x + mod_ref[0, 5:6, :] * _rms(acc, gpost2_ref[...])


def _out_mlp_call(x2, ys, mod3, w_out, w1, w2, g_post, g_pre2, g_post2, T):
    N, D = x2.shape
    nt = T // TM
    row = lambda i: (i, 0)
    const = lambda i: (0, 0)
    return pl.pallas_call(
        _out_mlp_kernel,
        out_shape=jax.ShapeDtypeStruct((N, D), F32),
        grid=(N // TM,),
        in_specs=[pl.BlockSpec((TM, D), row)] + [pl.BlockSpec((TM, GROUP_W), row)] * 4
                 + [pl.BlockSpec((1, 6, D), lambda i: (i // nt, 0, 0)),
                    _resident((D, D), const), _resident((D, D_FF), const), _resident((D_FF, D), const),
                    _resident((1, D), const), _resident((1, D), const), _resident((1, D), const)],
        out_specs=pl.BlockSpec((TM, D), row),
        compiler_params=_cparams("arbitrary"),
        name="out_mlp",
    )(x2, *ys, mod3, w_out, w1, w2, g_post, g_pre2, g_post2)


def _relayout_w_in(w_in):
    col = lambda k: w_in[:, _IN_OFF[k]:_IN_OFF[k + 1]]
    zeros = lambda n: jnp.zeros((w_in.shape[0], n), w_in.dtype)
    parts = [col(_A_Q), col(_A_QI), col(_A_K), col(_A_KI), col(_A_V), zeros(64),
             col(_A_WI), zeros(120),
             col(_B_Q), col(_B_K), col(_B_V),
             col(_B_G), col(_C_G),
             col(_C_Q), col(_C_K), col(_C_V), col(_C_A), zeros(112),
             col(_D_Q), col(_D_K), col(_D_V)]
    return jnp.concatenate(parts, axis=1).astype(BF16)


def _rope_tables(pos, rot, period, theta):
    half = rot // 2
    inv = theta ** (-jnp.arange(half, dtype=F32) / half)
    ang = pos.astype(F32)[..., None] * inv
    cos, sin = jnp.cos(ang), jnp.sin(ang)
    pad = ang.shape[:-1] + (period - rot,)
    ct = jnp.concatenate([cos, cos, jnp.ones(pad, F32)], axis=-1)
    st = jnp.concatenate([-sin, sin, jnp.zeros(pad, F32)], axis=-1)
    reps = LANES // period
    return (jnp.tile(ct, (1, 1, reps)).reshape(-1, LANES), jnp.tile(st, (1, 1, reps)).reshape(-1, LANES))


def _values_t(v, B, T):
    heads = v.shape[1] // HEAD_DIM
    vt = jnp.transpose(v.reshape(B, T, heads, HEAD_DIM), (0, 2, 3, 1))
    ones = jnp.ones((B, heads, VT_ROWS - HEAD_DIM, T), v.dtype)
    return jnp.concatenate([vt, ones], axis=2).reshape(-1, T)


def _layer(x2, cond, tabs, ret_tabs, layer_idx, B, T, mod_w, mod_b, attn_pre_g, attn_post_g,
           mlp_pre_g, mlp_post_g, w_in, gla_wa2, gla_ba, lam_q1, lam_k1, lam_q2, lam_k2,
           head_norm_g, w_out, mlp_w1, mlp_w2):
    D = D_MODEL
    mod3 = _mod_call(cond, mod_w, mod_b).reshape(B, 6, D)
    qa, kva, wa, qkvb, gates, cc, dqkv = _in_call(
        x2, mod3, attn_pre_g.reshape(1, D), _relayout_w_in(w_in), tabs, T)

    g_a, g_b, g_c, g_d = jnp.split(head_norm_g, 4)
    gain_t = lambda g: jnp.broadcast_to(g[:, None], (GROUP_W, TQ))
    values_t = lambda v: _values_t(v, B, T)
    y_a = _dsa_call(qa, wa, kva, values_t(kva[:, LANES:LANES + HEAD_DIM]), gain_t(g_a), B, T)
    y_b = _ret_call(qkvb, gates, ret_tabs, g_b.reshape(1, GROUP_W), B, T)
    wa2_p = jnp.zeros((LANES, LANES), F32).at[:GLA_RANK].set(gla_wa2)
    y_c = _gla_call(cc, gates, wa2_p, gla_ba.reshape(1, LANES), g_c.reshape(1, GROUP_W), B, T)
    lam_init = 0.8 - 0.6 * math.exp(-0.3 * layer_idx)
    lam_vecs = jnp.zeros((4, LANES), F32).at[:, :DIFF_DK].set(jnp.stack([lam_q1, lam_k1, lam_q2, lam_k2]))
    y_d = _diff_call(dqkv, values_t(dqkv[:, 512:768]), lam_vecs, gain_t(g_d), lam_init, B, T)

    return _out_mlp_call(x2, (y_a, y_b, y_c, y_d), mod3, w_out.astype(BF16), mlp_w1.astype(BF16),
                         mlp_w2.astype(BF16), attn_post_g.reshape(1, D), mlp_pre_g.reshape(1, D),
                         mlp_post_g.reshape(1, D), T)


def kernel(x, c, positions, mod_w, mod_b, attn_pre_g, attn_post_g, mlp_pre_g, mlp_post_g, w_in,
           gla_wa2, gla_ba, lam_q1, lam_k1, lam_q2, lam_k2, head_norm_g, w_out, mlp_w1, mlp_w2):
    B, T, D = x.shape
    assert D == D_MODEL and T % TM == 0 and T % TB == 0
    tabs = (_rope_tables(positions, HEAD_DIM // ROPE_FRAC, HEAD_DIM, ROPE_THETA)
            + _rope_tables(positions, HEAD_DIM, HEAD_DIM, RET_THETA)
            + _rope_tables(positions, DIFF_DK // ROPE_FRAC, DIFF_DK, ROPE_THETA))
    ret_tabs = _ret_tables()
    x2 = x.reshape(B * T, D)
    for l in range(mod_w.shape[0]):
        x2 = _layer(x2, c, tabs, ret_tabs, l, B, T, mod_w[l], mod_b[l], attn_pre_g[l], attn_post_g[l],
                    mlp_pre_g[l], mlp_post_g[l], w_in[l], gla_wa2[l], gla_ba[l], lam_q1[l], lam_k1[l],
                    lam_q2[l], lam_k2[l], head_norm_g[l], w_out[l], mlp_w1[l], mlp_w2[l])
    return x2.reshape(B, T, D)
```

```python
import functools
import math

import numpy as np
import jax
import jax.numpy as jnp
from jax import lax
from jax.experimental import pallas as pl
from jax.experimental.pallas import tpu as pltpu

F32 = jnp.float32
BF16 = jnp.bfloat16
I32 = jnp.int32
I16 = jnp.int16

D_MODEL = 1024
CHUNK = 64
HEAD_DIM = 64
N_GH = 4
GROUP_W = N_GH * HEAD_DIM
IDX_HEADS = 8
IDX_DIM = 64
TOPK_MAX = 256
RET_THETA = 10000.0
GLA_DK = HEAD_DIM // 2
GLA_RANK = 16
GLA_TAU = 16.0
DIFF_DK = HEAD_DIM // 2
ROPE_THETA = 500000.0
ROPE_FRAC = 4
D_FF = 4 * D_MODEL
EPS = 1e-6
NEG_INF = -1e30

LANES = 128
VMEM_LIMIT = 56 * 1024 * 1024

_IN_SPLITS = (
    GROUP_W, HEAD_DIM, HEAD_DIM, IDX_HEADS * IDX_DIM, IDX_DIM, IDX_HEADS,
    GROUP_W, GROUP_W, GROUP_W, GROUP_W,
    N_GH * GLA_DK, N_GH * GLA_DK, GROUP_W, GLA_RANK, GROUP_W,
    2 * N_GH * DIFF_DK, 2 * N_GH * DIFF_DK, GROUP_W,
)
_IN_OFF = np.concatenate([[0], np.cumsum(_IN_SPLITS)]).tolist()
(_A_Q, _A_K, _A_V, _A_QI, _A_KI, _A_WI, _B_Q, _B_K, _B_V, _B_G,
 _C_Q, _C_K, _C_V, _C_A, _C_G, _D_Q, _D_K, _D_V) = range(18)

SEG_A = 0
SEG_W = 1024
SEG_B = 1152
SEG_G = 1920
SEG_C = 2432
SEG_D = 3072
W_COLS = 3840

NT_DIMS = (((1,), (1,)), ((), ()))
TN_DIMS = (((0,), (0,)), ((), ()))

TM = 512
TQ = 256
TK = 256
FLASH_SLAB = 32
VT_ROWS = HEAD_DIM + 16
LOG2E = math.log2(math.e)
TB = 256


def _float_key(v):
    bits = int(np.array(v, np.float32).view(np.int32))
    return -(bits & 0x7FFFFFFF) if bits < 0 else bits


NEG_KEY = _float_key(NEG_INF)
INT_MIN = -(2 ** 31)


def _cparams(*sem):
    return pltpu.CompilerParams(dimension_semantics=sem, vmem_limit_bytes=VMEM_LIMIT)


def _resident(shape, index_map):
    return pl.BlockSpec(shape, index_map, pipeline_mode=pl.Buffered(1))


def _dot(a, b):
    return jnp.dot(a, b, preferred_element_type=F32)


def _dot_nt(a, b):
    return lax.dot_general(a, b, NT_DIMS, preferred_element_type=F32)


def _dot_tn(a, b):
    return lax.dot_general(a, b, TN_DIMS, preferred_element_type=F32)


def _dot_f32(a, b):
    return jnp.dot(a, b, preferred_element_type=F32, precision=lax.Precision.HIGHEST)


def _split_bf16(x, n):
    parts = []
    for _ in range(n):
        p = x.astype(BF16)
        parts.append(p)
        x = x - p.astype(F32)
    return parts


def _dot_split(a, b):
    (a0, a1), (b0, b1) = _split_bf16(a, 2), _split_bf16(b, 2)
    return _dot(a0, b0) + (_dot(a0, b1) + _dot(a1, b0))


def _lane(shape):
    return lax.broadcasted_iota(I32, shape, len(shape) - 1)


def _silu(x):
    return x / (1.0 + jnp.exp(-x))


def _mod_kernel(c_ref, w_ref, b_ref, o_ref):
    cond = _silu(c_ref[...])
    o_ref[...] = _dot_f32(cond, w_ref[...]) + b_ref[...]


def _mod_call(c, mod_w, mod_b):
    B, D = c.shape
    n = mod_w.shape[1] // D
    return pl.pallas_call(
        _mod_kernel,
        out_shape=jax.ShapeDtypeStruct((B, n * D), F32),
        grid=(n,),
        in_specs=[pl.BlockSpec((B, D), lambda j: (0, 0)),
                  pl.BlockSpec((D, D), lambda j: (0, j)),
                  pl.BlockSpec((1, D), lambda j: (0, j))],
        out_specs=pl.BlockSpec((B, D), lambda j: (0, j)),
        compiler_params=_cparams("arbitrary"),
        name="mod",
    )(c, mod_w, mod_b.reshape(1, -1))


def _rope_group(xg, cos, sin, half, period):
    lo = (_lane((1, LANES)) % period) < half
    swapped = jnp.where(lo, pltpu.roll(xg, LANES - half, 1), pltpu.roll(xg, half, 1))
    return xg * cos + swapped * sin


def _in_kernel(x_ref, mod_ref, g_ref, w_ref, ca_ref, sa_ref, cb_ref, sb_ref, cd_ref, sd_ref,
               qa_ref, kva_ref, wa_ref, qkvb_ref, gate_ref, c_ref, d_ref, vta_ref, vtd_ref):
    x = x_ref[...]
    h = x * lax.rsqrt(jnp.mean(x * x, axis=-1, keepdims=True) + EPS) * g_ref[...]
    h = h * (1.0 + mod_ref[0, 1:2, :]) + mod_ref[0, 0:1, :]
    hb = h.astype(BF16)

    def seg(off, width):
        return _dot(hb, w_ref[:, off:off + width])

    lane = _lane((1, LANES))
    low = lane < HEAD_DIM

    ca, sa = ca_ref[...], sa_ref[...]
    a = seg(SEG_A, 1024)
    half_a = HEAD_DIM // ROPE_FRAC // 2
    for gi in range(2):
        r = _rope_group(a[:, gi * LANES:(gi + 1) * LANES], ca, sa, half_a, HEAD_DIM) * (HEAD_DIM ** -0.5 * LOG2E)
        rs = pltpu.roll(r, HEAD_DIM, 1)
        qa_ref[:, (2 * gi) * LANES:(2 * gi + 1) * LANES] = jnp.where(low, r, 0.0).astype(BF16)
        qa_ref[:, (2 * gi + 1) * LANES:(2 * gi + 2) * LANES] = jnp.where(low, rs, 0.0).astype(BF16)
    for gi in range(4):
        r = _rope_group(a[:, (2 + gi) * LANES:(3 + gi) * LANES], ca, sa, half_a, HEAD_DIM)
        rs = pltpu.roll(r, HEAD_DIM, 1)
        qa_ref[:, (4 + 2 * gi) * LANES:(5 + 2 * gi) * LANES] = jnp.where(low, 0.0, rs).astype(BF16)
        qa_ref[:, (5 + 2 * gi) * LANES:(6 + 2 * gi) * LANES] = jnp.where(low, 0.0, r).astype(BF16)
    kva_ref[:, 0:LANES] = _rope_group(a[:, 6 * LANES:7 * LANES], ca, sa, half_a, HEAD_DIM).astype(BF16)
    kva_ref[:, LANES:2 * LANES] = a[:, 7 * LANES:8 * LANES].astype(BF16)
    ones_rows = jnp.ones((VT_ROWS - HEAD_DIM, TM), BF16)
    vta_ref[0:HEAD_DIM, :] = jnp.transpose(a[:, 7 * LANES:8 * LANES])[0:HEAD_DIM, :].astype(BF16)
    vta_ref[HEAD_DIM:VT_ROWS, :] = ones_rows

    wa_ref[...] = seg(SEG_W, LANES)

    cb, sb = cb_ref[...], sb_ref[...]
    b = seg(SEG_B, 768)
    for gi in range(4):
        r = _rope_group(b[:, gi * LANES:(gi + 1) * LANES], cb, sb, HEAD_DIM // 2, HEAD_DIM)
        if gi < 2:
            r = r * (HEAD_DIM ** -0.5)
        qkvb_ref[:, gi * LANES:(gi + 1) * LANES] = r.astype(BF16)
    qkvb_ref[:, 512:768] = b[:, 512:768].astype(BF16)

    gate_ref[...] = seg(SEG_G, 512)
    c_ref[...] = seg(SEG_C, 640)

    cd, sd = cd_ref[...], sd_ref[...]
    d = seg(SEG_D, 768)
    half_d = DIFF_DK // ROPE_FRAC // 2
    for gi in range(4):
        r = _rope_group(d[:, gi * LANES:(gi + 1) * LANES], cd, sd, half_d, DIFF_DK)
        if gi < 2:
            r = r * (DIFF_DK ** -0.5 * LOG2E)
        d_ref[:, gi * LANES:(gi + 1) * LANES] = r.astype(BF16)
    d_ref[:, 512:768] = d[:, 512:768].astype(BF16)
    vt = jnp.transpose(d[:, 512:768])
    for h in range(N_GH):
        vtd_ref[h * VT_ROWS:h * VT_ROWS + HEAD_DIM, :] = vt[h * HEAD_DIM:(h + 1) * HEAD_DIM, :].astype(BF16)
        vtd_ref[h * VT_ROWS + HEAD_DIM:(h + 1) * VT_ROWS, :] = ones_rows


def _in_call(x2, mod3, g, w_p, tabs, T):
    N, D = x2.shape
    nt = T // TM
    row = lambda i: (i, 0)
    tab_spec = pl.BlockSpec((TM, LANES), row)
    outs = [(1536, BF16), (256, BF16), (LANES, F32), (768, BF16), (512, F32), (640, F32), (768, BF16)]
    vt_rows = (VT_ROWS, N_GH * VT_ROWS)
    vt_shapes = [jax.ShapeDtypeStruct((N // T * r, T), BF16) for r in vt_rows]
    vt_specs = [pl.BlockSpec((r, TM), lambda i: (i // nt, i % nt)) for r in vt_rows]
    return pl.pallas_call(
        _in_kernel,
        out_shape=[jax.ShapeDtypeStruct((N, w), dt) for w, dt in outs] + vt_shapes,
        grid=(N // TM,),
        in_specs=[pl.BlockSpec((TM, D), row),
                  pl.BlockSpec((1, 6, D), lambda i: (i // nt, 0, 0)),
                  _resident((1, D), lambda i: (0, 0)),
                  _resident((D, W_COLS), lambda i: (0, 0))] + [tab_spec] * 6,
        out_specs=[pl.BlockSpec((TM, w), row) for w, _ in outs] + vt_specs,
        compiler_params=_cparams("arbitrary"),
        name="in_proj",
    )(x2, mod3, g, w_p, *tabs)


def _pair_head_norm(o, gain, center):
    low = _lane((1, LANES)) < HEAD_DIM
    inv = 1.0 / HEAD_DIM

    def seg_mean(v):
        m0 = jnp.sum(jnp.where(low, v, 0.0), axis=1, keepdims=True)
        m1 = jnp.sum(jnp.where(low, 0.0, v), axis=1, keepdims=True)
        return jnp.where(low, m0, m1) * inv

    if center:
        o = o - seg_mean(o)
    return o * lax.rsqrt(seg_mean(o * o) + EPS) * gain


class _Flash:
    def __init__(self, n, s_ref, p_ref, acc_ref, st_ref):
        self.n, self.s, self.p, self.acc, self.st = n, s_ref, p_ref, acc_ref, st_ref

    @staticmethod
    def scratch(n):
        return [pltpu.VMEM((n, TK, TQ), F32), pltpu.VMEM((n, TK, TQ), BF16),
                pltpu.VMEM((n, VT_ROWS, TQ), F32), pltpu.VMEM((3, n, TQ), F32)]

    def _row(self, k, i):
        return self.st.at[k, i:i + 1, :]

    def init(self):
        self.st[0] = jnp.full((self.n, TQ), NEG_INF, F32)
        self.acc[...] = jnp.zeros_like(self.acc)

    def scores(self, i, s):
        self.s[i] = s
        self._row(1, i)[...] = jnp.max(_fold8(s, jnp.maximum), axis=0, keepdims=True)

    def probs(self, i):
        m_old = self._row(0, i)[...]
        m_new = jnp.maximum(m_old, self._row(1, i)[...])
        for r in range(0, TK, FLASH_SLAB):
            self.p[i, r:r + FLASH_SLAB, :] = jnp.exp2(self.s[i, r:r + FLASH_SLAB, :] - m_new).astype(BF16)
        self._row(0, i)[...] = m_new
        self._row(2, i)[...] = jnp.exp2(m_old - m_new)

    def values(self, i, vt):
        self.acc[i] = self._row(2, i)[...] * self.acc[i] + _dot(vt, self.p[i])

    def result(self, i):
        return self.acc[i, 0:HEAD_DIM, :] / self.acc[i, HEAD_DIM:HEAD_DIM + 1, :]

    def pipeline(self, first, n_rest, score, value):
        score(first, True)

        def body(t, carry):
            for i in range(self.n):
                self.probs(i)
            score(t, False)
            value(jnp.where(t == 0, first, t - 1))
            return carry

        lax.fori_loop(0, n_rest, body, 0)
        for i in range(self.n):
            self.probs(i)
        value(jnp.where(n_rest == 0, first, n_rest - 1))


def _fold8(x, op=jnp.add, rows=8):
    parts = [x[r:r + rows] for r in range(0, x.shape[0], rows)]
    while len(parts) > 1:
        parts = [op(a, b) for a, b in zip(parts[0::2], parts[1::2])]
    return parts[0]


def _head_norm_t(o, gain):
    ms = jnp.mean(o * o, axis=0, keepdims=True)
    return o * lax.rsqrt(ms + EPS) * gain


def _dsa_kernel(q_ref, w_ref, kv_ref, vt_ref, g_ref, o_ref, bias_ref, key_ref, hi_ref, lo_ref, *flash_refs, topk):
    q0 = pl.program_id(1) * TQ
    nkt = (q0 + TQ) // TK
    qchunk = (q0 + _lane((1, TQ))) // CHUNK
    wts = jnp.transpose(w_ref[...]) * (IDX_HEADS ** -0.5 * IDX_DIM ** -0.5)

    def score_tile(kt, carry):
        k0 = pl.multiple_of(kt * TK, TK)
        kk = kv_ref[pl.ds(k0, TK), 0:LANES]
        tile = bias_ref.at[pl.ds(k0, TK), :]
        for h in range(IDX_HEADS):
            term = jnp.maximum(_dot_nt(kk, q_ref[:, (4 + h) * LANES:(5 + h) * LANES]), 0.0) * wts[h:h + 1, :]
            if h == 0:
                tile[...] = term
            elif h < IDX_HEADS - 1:
                tile[...] += term
        kchunk = (k0 + lax.broadcasted_iota(I32, (TK, 1), 0)) // CHUNK
        acc = jnp.where(kchunk <= qchunk, tile[...] + term, NEG_INF)
        bits = pltpu.bitcast(acc, I32)
        sign = bits >> 31
        key = ((bits & 0x7FFFFFFF) ^ sign) - sign
        key_ref[pl.ds(k0, TK), :] = key
        hi_ref[pl.ds(k0, TK), :] = (key >> 16).astype(I16)
        lo_ref[pl.ds(k0, TK), :] = (key ^ 0x8000).astype(I16)
        return carry

    lax.fori_loop(0, nkt, score_tile, 0)

    @pl.when(nkt % 2 == 1)
    def _():
        pad0 = pl.multiple_of(nkt * TK, TK)
        hi_ref[pl.ds(pad0, TK), :] = jnp.full((TK, TQ), -32768, I16)
        lo_ref[pl.ds(pad0, TK), :] = jnp.full((TK, TQ), -32768, I16)

    kf = float(topk)
    n_pairs = (nkt + 1) // 2

    def count16(ref, cand):
        def body(kp, acc):
            k0 = pl.multiple_of(kp * (2 * TK), 2 * TK)
            hit = jnp.where(ref[pl.ds(k0, 2 * TK), :] >= cand,
                            jnp.ones((2 * TK, TQ), I16), jnp.zeros((2 * TK, TQ), I16))
            return acc + _fold8(hit, rows=16)
        acc = lax.fori_loop(0, n_pairs, body, jnp.zeros((16, TQ), I16))
        return jnp.sum(acc.astype(F32), axis=0, keepdims=True)

    def search16(ref, cnt_all):
        c0 = count16(ref, jnp.zeros((1, TQ), I16))
        t = jnp.where(c0 >= kf, 0, -32768).astype(I32)
        cnt = jnp.where(c0 >= kf, c0, cnt_all)

        def step(it, carry):
            t, cnt = carry
            cand = t + (jnp.int32(1) << (14 - it))
            c = count16(ref, cand.astype(I16))
            ok = c >= kf
            return jnp.where(ok, cand, t), jnp.where(ok, c, cnt)

        return lax.fori_loop(0, 15, step, (t, cnt))

    def count(pred):
        def body(kt, acc):
            k0 = pl.multiple_of(kt * TK, TK)
            hit = jnp.where(pred(key_ref[pl.ds(k0, TK), :], k0), 1.0, 0.0)
            return acc + _fold8(hit)
        acc = lax.fori_loop(0, nkt, body, jnp.zeros((8, TQ), F32))
        return jnp.sum(acc, axis=0, keepdims=True)

    t_hi, cnt_hi = search16(hi_ref, (nkt * TK).astype(F32))
    t_hi16 = t_hi.astype(I16)

    def low_digit_tile(kp, carry):
        k0 = pl.multiple_of(kp * (2 * TK), 2 * TK)
        hi = hi_ref[pl.ds(k0, 2 * TK), :]
        in_bucket = jnp.where(hi == t_hi16, lo_ref[pl.ds(k0, 2 * TK), :], jnp.full((2 * TK, TQ), -32768, I16))
        hi_ref[pl.ds(k0, 2 * TK), :] = jnp.where(hi > t_hi16, jnp.full((2 * TK, TQ), 32767, I16), in_bucket)
        return carry

    lax.fori_loop(0, n_pairs, low_digit_tile, 0)
    t_lo, cnt = search16(hi_ref, cnt_hi)
    thr = t_hi * 65536 + (t_lo + 32768)
    floor_key = jnp.int32(NEG_KEY + 1)
    tied = jnp.max(jnp.where((cnt > kf) & (thr >= floor_key), 1.0, 0.0)) > 0.0

    @pl.when(jnp.logical_not(tied))
    def _():
        lim = jnp.maximum(thr, floor_key)

        def body(kt, carry):
            k0 = pl.multiple_of(kt * TK, TK)
            bias_ref[pl.ds(k0, TK), :] = jnp.where(key_ref[pl.ds(k0, TK), :] >= lim, 0.0, NEG_INF)
            return carry
        lax.fori_loop(0, nkt, body, 0)

    @pl.when(tied)
    def _():
        need = kf - count(lambda keys, k0: keys > thr)

        def idx(k0):
            return k0 + lax.broadcasted_iota(I32, (TK, 1), 0)

        def search_idx(it, lim):
            cand = lim + (jnp.int32(1) << (14 - it))
            c = count(lambda keys, k0: (keys == thr) & (idx(k0) < cand))
            return jnp.where(c <= need, cand, lim)

        lim = lax.fori_loop(0, 15, search_idx, jnp.zeros((1, TQ), I32))

        def body(kt, carry):
            k0 = pl.multiple_of(kt * TK, TK)
            keys = key_ref[pl.ds(k0, TK), :]
            sel = ((keys > thr) | ((keys == thr) & (idx(k0) < lim))) & (keys >= floor_key)
            bias_ref[pl.ds(k0, TK), :] = jnp.where(sel, 0.0, NEG_INF)
            return carry
        lax.fori_loop(0, nkt, body, 0)

    flash = _Flash(N_GH, *flash_refs)
    flash.init()

    def score(kt, is_first):
        k0 = pl.multiple_of(kt * TK, TK)
        kk = kv_ref[pl.ds(k0, TK), 0:LANES]
        bias = bias_ref[pl.ds(k0, TK), :]
        for h in range(N_GH):
            flash.scores(h, _dot_nt(kk, q_ref[:, h * LANES:(h + 1) * LANES]) + bias)

    def value(kt):
        k0 = pl.multiple_of(kt * TK, TK)
        for h in range(N_GH):
            flash.values(h, vt_ref[:, pl.ds(k0, TK)])

    flash.pipeline(nkt - 1, nkt - 1, score, value)
    y = jnp.concatenate([_head_norm_t(flash.result(h), g_ref[h * HEAD_DIM:(h + 1) * HEAD_DIM, :])
                         for h in range(N_GH)], axis=0)
    o_ref[...] = jnp.transpose(y).astype(BF16)


def _dsa_call(qa, wa, kva, vta, g_t, B, T):
    nq = T // TQ
    topk = min(TOPK_MAX, T // 4)
    return pl.pallas_call(
        functools.partial(_dsa_kernel, topk=topk),
        out_shape=jax.ShapeDtypeStruct((B * T, GROUP_W), BF16),
        grid=(B, nq),
        in_specs=[pl.BlockSpec((TQ, 1536), lambda b, i: (b * nq + i, 0)),
                  pl.BlockSpec((TQ, LANES), lambda b, i: (b * nq + i, 0)),
                  pl.BlockSpec((T, 256), lambda b, i: (b, 0)),
                  pl.BlockSpec((VT_ROWS, T), lambda b, i: (b, 0)),
                  pl.BlockSpec((GROUP_W, TQ), lambda b, i: (0, 0))],
        out_specs=pl.BlockSpec((TQ, GROUP_W), lambda b, i: (b * nq + i, 0)),
        scratch_shapes=[pltpu.VMEM((T, TQ), F32), pltpu.VMEM((T, TQ), I32), pltpu.VMEM((T, TQ), I16),
                        pltpu.VMEM((T, TQ), I16)] + _Flash.scratch(N_GH),
        compiler_params=_cparams("arbitrary", "arbitrary"),
        name="dsa",
    )(qa, wa, kva, vta, g_t)


def _diff_kernel(q_ref, k_ref, vt_ref, lam_ref, g_ref, o_ref, qm_ref, *flash_refs, lam_init):
    q0 = pl.program_id(1) * TQ
    nfull = (q0 + CHUNK) // TK
    qchunk = (q0 + _lane((1, TQ))) // CHUNK
    lv = lam_ref[...]
    lam = (jnp.exp(jnp.sum(lv[0:1] * lv[1:2], axis=1, keepdims=True))
           - jnp.exp(jnp.sum(lv[2:3] * lv[3:4], axis=1, keepdims=True)) + lam_init)
    n_maps = 2 * N_GH
    qt = jnp.transpose(q_ref[...].astype(F32))
    sub = lax.broadcasted_iota(I32, (LANES, 1), 0) // DIFF_DK
    for mi in range(n_maps):
        qg = qt[(mi // 4) * LANES:(mi // 4 + 1) * LANES, :]
        qm_ref[mi] = jnp.where(sub == mi % 4, qg, 0.0).astype(BF16)

    flash = _Flash(n_maps, *flash_refs)
    flash.init()

    def score(kt, masked):
        k0 = pl.multiple_of(kt * TK, TK)
        if masked:
            kchunk = (k0 + lax.broadcasted_iota(I32, (TK, 1), 0)) // CHUNK
            admissible = kchunk <= qchunk
        for mi in range(n_maps):
            g = mi // 4
            s = _dot(k_ref[pl.ds(k0, TK), g * LANES:(g + 1) * LANES], qm_ref[mi])
            flash.scores(mi, jnp.where(admissible, s, NEG_INF) if masked else s)

    def value(kt):
        k0 = pl.multiple_of(kt * TK, TK)
        for mi in range(n_maps):
            h = mi // 2
            flash.values(mi, vt_ref[h * VT_ROWS:(h + 1) * VT_ROWS, pl.ds(k0, TK)])

    flash.pipeline(nfull, nfull, score, value)
    outs = []
    for h in range(N_GH):
        o = flash.result(2 * h) - lam * flash.result(2 * h + 1)
        outs.append(_head_norm_t(o, g_ref[h * HEAD_DIM:(h + 1) * HEAD_DIM, :]) * (1.0 - lam_init))
    o_ref[...] = jnp.transpose(jnp.concatenate(outs, axis=0)).astype(BF16)


def _diff_call(dqkv, vtd, lam_vecs, g_t, lam_init, B, T):
    nq = T // TQ
    return pl.pallas_call(
        functools.partial(_diff_kernel, lam_init=lam_init),
        out_shape=jax.ShapeDtypeStruct((B * T, GROUP_W), BF16),
        grid=(B, nq),
        in_specs=[pl.BlockSpec((TQ, 256), lambda b, i: (b * nq + i, 0)),
                  pl.BlockSpec((T, 256), lambda b, i: (b, 1)),
                  pl.BlockSpec((N_GH * VT_ROWS, T), lambda b, i: (b, 0)),
                  pl.BlockSpec((4, LANES), lambda b, i: (0, 0)),
                  pl.BlockSpec((GROUP_W, TQ), lambda b, i: (0, 0))],
        out_specs=pl.BlockSpec((TQ, GROUP_W), lambda b, i: (b * nq + i, 0)),
        scratch_shapes=[pltpu.VMEM((2 * N_GH, LANES, TQ), BF16)] + _Flash.scratch(2 * N_GH),
        compiler_params=_cparams("arbitrary", "arbitrary"),
        name="diff",
    )(dqkv, dqkv, vtd, lam_vecs, g_t)


def _ret_tables():
    h = np.arange(N_GH, dtype=np.float32)
    log_g = jnp.log(1.0 - 2.0 ** (-5.0 - jnp.asarray(h)))
    t = np.arange(TB)
    same = (t[:, None] // CHUNK) == (t[None, :] // CHUNK)
    earlier = (t[None, :] // CHUNK) < (t[:, None] // CHUNK)
    dist = jnp.asarray(np.where(same, np.abs(t[:, None] - t[None, :]), t[:, None] - t[None, :]), F32)
    dmat = jnp.where(jnp.asarray(same | earlier)[None],
                     jnp.exp(dist[None] * log_g[:, None, None]), 0.0)
    tt = jnp.asarray(t, F32)
    xi = jnp.exp((tt + 1.0)[None, :] * log_g[:, None])
    zeta = jnp.exp((TB - 1.0 - tt)[None, :] * log_g[:, None])
    gblk = jnp.exp(TB * log_g)

    def lanes(tab):
        return jnp.repeat(tab.reshape(2, 2, TB), HEAD_DIM, axis=1).transpose(0, 2, 1)

    r = np.arange(LANES)
    blockdiag = (r[:, None] // HEAD_DIM) == (r[None, :] // HEAD_DIM)
    gb = jnp.where(jnp.asarray(blockdiag)[None],
                   jnp.repeat(gblk.reshape(2, 2), HEAD_DIM, axis=1)[:, :, None], 0.0)
    return dmat, lanes(xi), lanes(zeta), gb


def _ret_kernel(qkv_ref, gate_ref, dmat_ref, xi_ref, zeta_ref, gb_ref, g_ref, o_ref, r_ref):
    @pl.when(pl.program_id(1) == 0)
    def _():
        r_ref[...] = jnp.zeros_like(r_ref)

    lane = _lane((1, LANES))
    low = lane < HEAD_DIM
    r_idx = lax.broadcasted_iota(I32, (LANES, LANES), 0) // HEAD_DIM
    c_idx = lax.broadcasted_iota(I32, (LANES, LANES), 1) // HEAD_DIM
    for g in range(2):
        qg = qkv_ref[:, g * LANES:(g + 1) * LANES]
        kg = qkv_ref[:, 256 + g * LANES:256 + (g + 1) * LANES]
        vg = qkv_ref[:, 512 + g * LANES:512 + (g + 1) * LANES]
        state = r_ref[g]
        inter = _dot(qg, state.astype(BF16)) * xi_ref[g]
        parts = []
        for hh in range(2):
            qm = jnp.where(low, qg, jnp.zeros_like(qg)) if hh == 0 else jnp.where(low, jnp.zeros_like(qg), qg)
            s = _dot_nt(qm, kg) * dmat_ref[2 * g + hh]
            parts.append(_dot(s.astype(BF16), vg))
        o = jnp.where(low, parts[0], parts[1]) + inter
        kz = (kg.astype(F32) * zeta_ref[g]).astype(BF16)
        upd = jnp.where(r_idx == c_idx, _dot_tn(kz, vg), 0.0)
        r_ref[g] = state * gb_ref[g] + upd
        y = _pair_head_norm(o, g_ref[:, g * LANES:(g + 1) * LANES], True)
        o_ref[:, g * LANES:(g + 1) * LANES] = (_silu(gate_ref[:, g * LANES:(g + 1) * LANES]) * y).astype(BF16)


def _ret_call(qkvb, gates, tabs, g_b, B, T):
    nb = T // TB
    dmat, xi, zeta, gb = tabs
    const3 = lambda b, i: (0, 0, 0)
    return pl.pallas_call(
        _ret_kernel,
        out_shape=jax.ShapeDtypeStruct((B * T, GROUP_W), BF16),
        grid=(B, nb),
        in_specs=[pl.BlockSpec((TB, 768), lambda b, i: (b * nb + i, 0)),
                  pl.BlockSpec((TB, GROUP_W), lambda b, i: (b * nb + i, 0)),
                  pl.BlockSpec((N_GH, TB, TB), const3),
                  pl.BlockSpec((2, TB, LANES), const3),
                  pl.BlockSpec((2, TB, LANES), const3),
                  pl.BlockSpec((2, LANES, LANES), const3),
                  pl.BlockSpec((1, GROUP_W), lambda b, i: (0, 0))],
        out_specs=pl.BlockSpec((TB, GROUP_W), lambda b, i: (b * nb + i, 0)),
        scratch_shapes=[pltpu.VMEM((2, LANES, LANES), F32)],
        compiler_params=_cparams("arbitrary", "arbitrary"),
        name="ret",
    )(qkvb, gates, dmat, xi, zeta, gb, g_b)


def _gla_kernel(c_ref, gate_ref, wa2_ref, ba_ref, g_ref, o_ref, s_ref, b_scr, q_scr, k_scr, v_scr, acc_scr):
    @pl.when(pl.program_id(1) == 0)
    def _():
        s_ref[...] = jnp.zeros_like(s_ref)

    n_chunks = TB // CHUNK
    hmask = (lax.broadcasted_iota(I32, (LANES, GROUP_W), 0) // GLA_DK
             == lax.broadcasted_iota(I32, (LANES, GROUP_W), 1) // HEAD_DIM)
    hsum = jnp.where(hmask, 1.0, 0.0).astype(BF16)
    r_idx = lax.broadcasted_iota(I32, (TB, TB), 0)
    c_idx = lax.broadcasted_iota(I32, (TB, TB), 1)
    tri = jnp.where((r_idx >= c_idx) & (r_idx // CHUNK == c_idx // CHUNK), 1.0, 0.0).astype(BF16)

    z = _dot_split(c_ref[:, 512:640], wa2_ref[...]) + ba_ref[...]
    log_a = (jnp.minimum(z, 0.0) - jnp.log(1.0 + jnp.exp(-jnp.abs(z)))) * (1.0 / GLA_TAU)
    b = sum(_dot(tri, piece) for piece in _split_bf16(log_a, 3))
    q = c_ref[:, 0:128] * (GLA_DK ** -0.5)
    b_scr[...] = b * LOG2E
    q_scr[...] = q
    k_scr[...] = c_ref[:, 128:256]
    v_scr[...] = c_ref[:, 256:512]

    b_last = [b[(c + 1) * CHUNK - 1:(c + 1) * CHUNK, :] for c in range(n_chunks)]
    b_end = jnp.concatenate([jnp.broadcast_to(bl, (CHUNK, LANES)) for bl in b_last], axis=0)
    kd = (c_ref[:, 128:256] * jnp.exp(b_end - b)).astype(BF16)
    qe = (q * jnp.exp(b)).astype(BF16)
    vb = c_ref[:, 256:512].astype(BF16)
    state = s_ref[...]
    inters = []
    for c in range(n_chunks):
        rows = slice(c * CHUNK, (c + 1) * CHUNK)
        inters.append(_dot(qe[rows], state.astype(BF16)))
        scale = jnp.transpose(jnp.broadcast_to(jnp.exp(b_last[c]), (LANES, LANES)))
        state = jnp.where(hmask, state * jnp.concatenate([scale, scale], axis=1) + _dot_tn(kd[rows], vb[rows]), 0.0)
    s_ref[...] = state

    acc_scr[...] = jnp.concatenate(inters, axis=0)

    keys_per_step = 16

    def key_step(s4, carry):
        for c in range(n_chunks):
            rows = slice(c * CHUNK, (c + 1) * CHUNK)
            part = None
            for u in range(keys_per_step):
                r = c * CHUNK + s4 * keys_per_step + u
                gap = pltpu.bitcast(b_scr[rows, :] - b_scr[pl.ds(r, 1), :], I32)
                decay = jnp.exp2(pltpu.bitcast(gap | jnp.int32(INT_MIN), F32))
                w = (q_scr[rows, :] * decay * k_scr[pl.ds(r, 1), :]).astype(BF16)
                term = _dot(w, hsum) * v_scr[pl.ds(r, 1), :]
                part = term if part is None else part + term
            acc_scr[rows, :] += part
        return carry

    lax.fori_loop(0, CHUNK // keys_per_step, key_step, 0)
    o = acc_scr[...]
    for gi in range(2):
        y = _pair_head_norm(o[:, gi * LANES:(gi + 1) * LANES], g_ref[:, gi * LANES:(gi + 1) * LANES], True)
        o_ref[:, gi * LANES:(gi + 1) * LANES] = (_silu(gate_ref[:, gi * LANES:(gi + 1) * LANES]) * y).astype(BF16)


def _gla_call(cc, gates, wa2_p, ba, g_c, B, T):
    nb = T // TB
    return pl.pallas_call(
        _gla_kernel,
        out_shape=jax.ShapeDtypeStruct((B * T, GROUP_W), BF16),
        grid=(B, nb),
        in_specs=[pl.BlockSpec((TB, 640), lambda b, i: (b * nb + i, 0)),
                  pl.BlockSpec((TB, GROUP_W), lambda b, i: (b * nb + i, 1)),
                  pl.BlockSpec((LANES, LANES), lambda b, i: (0, 0)),
                  pl.BlockSpec((1, LANES), lambda b, i: (0, 0)),
                  pl.BlockSpec((1, GROUP_W), lambda b, i: (0, 0))],
        out_specs=pl.BlockSpec((TB, GROUP_W), lambda b, i: (b * nb + i, 0)),
        scratch_shapes=[pltpu.VMEM((LANES, GROUP_W), F32), pltpu.VMEM((TB, LANES), F32),
                        pltpu.VMEM((TB, LANES), F32), pltpu.VMEM((TB, LANES), F32),
                        pltpu.VMEM((TB, GROUP_W), F32), pltpu.VMEM((TB, GROUP_W), F32)],
        compiler_params=_cparams("arbitrary", "arbitrary"),
        name="gla",
    )(cc, gates, wa2_p, ba, g_c)


def _rms(y, g):
    return y * lax.rsqrt(jnp.mean(y * y, axis=-1, keepdims=True) + EPS) * g


def _out_mlp_kernel(x_ref, ya_ref, yb_ref, yc_ref, yd_ref, mod_ref, wo_ref, w1_ref, w2_ref,
                    gpost_ref, gpre_ref, gpost2_ref, o_ref):
    y = _dot(ya_ref[...], wo_ref[0:256, :])
    y = y + _dot(yb_ref[...], wo_ref[256:512, :])
    y = y + _dot(yc_ref[...], wo_ref[512:768, :])
    y = y + _dot(yd_ref[...], wo_ref[768:1024, :])
    x = x_ref[...] + mod_ref[0, 2:3, :] * _rms(y, gpost_ref[...])
    h = _rms(x, gpre_ref[...]) * (1.0 + mod_ref[0, 4:5, :]) + mod_ref[0, 3:4, :]
    hb = h.astype(BF16)
    acc = jnp.zeros(x.shape, F32)
    fc = 1024
    for f in range(D_FF // fc):
        u = jnp.maximum(_dot(hb, w1_ref[:, f * fc:(f + 1) * fc]), 0.0)
        acc = acc + _dot((u * u).astype(BF16), w2_ref[f * fc:(f + 1) * fc, :])
    o_ref[...] = x + mod_ref[0, 5:6, :] * _rms(acc, gpost2_ref[...])


def _out_mlp_call(x2, ys, mod3, w_out, w1, w2, g_post, g_pre2, g_post2, T):
    N, D = x2.shape
    nt = T // TM
    row = lambda i: (i, 0)
    const = lambda i: (0, 0)
    return pl.pallas_call(
        _out_mlp_kernel,
        out_shape=jax.ShapeDtypeStruct((N, D), F32),
        grid=(N // TM,),
        in_specs=[pl.BlockSpec((TM, D), row)] + [pl.BlockSpec((TM, GROUP_W), row)] * 4
                 + [pl.BlockSpec((1, 6, D), lambda i: (i // nt, 0, 0)),
                    _resident((D, D), const), _resident((D, D_FF), const), _resident((D_FF, D), const),
                    _resident((1, D), const), _resident((1, D), const), _resident((1, D), const)],
        out_specs=pl.BlockSpec((TM, D), row),
        compiler_params=_cparams("arbitrary"),
        name="out_mlp",
    )(x2, *ys, mod3, w_out, w1, w2, g_post, g_pre2, g_post2)


def _relayout_w_in(w_in):
    col = lambda k: w_in[:, _IN_OFF[k]:_IN_OFF[k + 1]]
    zeros = lambda n: jnp.zeros((w_in.shape[0], n), w_in.dtype)
    parts = [col(_A_Q), col(_A_QI), col(_A_K), col(_A_KI), col(_A_V), zeros(64),
             col(_A_WI), zeros(120),
             col(_B_Q), col(_B_K), col(_B_V),
             col(_B_G), col(_C_G),
             col(_C_Q), col(_C_K), col(_C_V), col(_C_A), zeros(112),
             col(_D_Q), col(_D_K), col(_D_V)]
    return jnp.concatenate(parts, axis=1).astype(BF16)


def _rope_tables(pos, rot, period, theta):
    half = rot // 2
    inv = theta ** (-jnp.arange(half, dtype=F32) / half)
    ang = pos.astype(F32)[..., None] * inv
    cos, sin = jnp.cos(ang), jnp.sin(ang)
    pad = ang.shape[:-1] + (period - rot,)
    ct = jnp.concatenate([cos, cos, jnp.ones(pad, F32)], axis=-1)
    st = jnp.concatenate([-sin, sin, jnp.zeros(pad, F32)], axis=-1)
    reps = LANES // period
    return (jnp.tile(ct, (1, 1, reps)).reshape(-1, LANES), jnp.tile(st, (1, 1, reps)).reshape(-1, LANES))


def _layer(x2, cond, tabs, ret_tabs, layer_idx, B, T, mod_w, mod_b, attn_pre_g, attn_post_g,
           mlp_pre_g, mlp_post_g, w_in, gla_wa2, gla_ba, lam_q1, lam_k1, lam_q2, lam_k2,
           head_norm_g, w_out, mlp_w1, mlp_w2):
    D = D_MODEL
    mod3 = _mod_call(cond, mod_w, mod_b).reshape(B, 6, D)
    qa, kva, wa, qkvb, gates, cc, dqkv, vta, vtd = _in_call(
        x2, mod3, attn_pre_g.reshape(1, D), _relayout_w_in(w_in), tabs, T)

    g_a, g_b, g_c, g_d = jnp.split(head_norm_g, 4)
    gain_t = lambda g: jnp.broadcast_to(g[:, None], (GROUP_W, TQ))
    y_a = _dsa_call(qa, wa, kva, vta, gain_t(g_a), B, T)
    y_b = _ret_call(qkvb, gates, ret_tabs, g_b.reshape(1, GROUP_W), B, T)
    wa2_p = jnp.zeros((LANES, LANES), F32).at[:GLA_RANK].set(gla_wa2)
    y_c = _gla_call(cc, gates, wa2_p, gla_ba.reshape(1, LANES), g_c.reshape(1, GROUP_W), B, T)
    lam_init = 0.8 - 0.6 * math.exp(-0.3 * layer_idx)
    lam_vecs = jnp.zeros((4, LANES), F32).at[:, :DIFF_DK].set(jnp.stack([lam_q1, lam_k1, lam_q2, lam_k2]))
    y_d = _diff_call(dqkv, vtd, lam_vecs, gain_t(g_d), lam_init, B, T)

    return _out_mlp_call(x2, (y_a, y_b, y_c, y_d), mod3, w_out.astype(BF16), mlp_w1.astype(BF16),
                         mlp_w2.astype(BF16), attn_post_g.reshape(1, D), mlp_pre_g.reshape(1, D),
                         mlp_post_g.reshape(1, D), T)


def kernel(x, c, positions, mod_w, mod_b, attn_pre_g, attn_post_g, mlp_pre_g, mlp_post_g, w_in,
           gla_wa2, gla_ba, lam_q1, lam_k1, lam_q2, lam_k2, head_norm_g, w_out, mlp_w1, mlp_w2):
    B, T, D = x.shape
    assert D == D_MODEL and T % TM == 0 and T % TB == 0
    tabs = (_rope_tables(positions, HEAD_DIM // ROPE_FRAC, HEAD_DIM, ROPE_THETA)
            + _rope_tables(positions, HEAD_DIM, HEAD_DIM, RET_THETA)
            + _rope_tables(positions, DIFF_DK // ROPE_FRAC, DIFF_DK, ROPE_THETA))
    ret_tabs = _ret_tables()
    x2 = x.reshape(B * T, D)
    for l in range(mod_w.shape[0]):
        x2 = _layer(x2, c, tabs, ret_tabs, l, B, T, mod_w[l], mod_b[l], attn_pre_g[l], attn_post_g[l],
                    mlp_pre_g[l], mlp_post_g[l], w_in[l], gla_wa2[l], gla_ba[l], lam_q1[l], lam_k1[l],
                    lam_q2[l], lam_k2[l], head_norm_g[l], w_out[l], mlp_w1[l], mlp_w2[l])
    return x2.reshape(B, T, D)
```

```python
import functools
import math

import numpy as np
import jax
import jax.numpy as jnp
from jax import lax
from jax.experimental import pallas as pl
from jax.experimental.pallas import tpu as pltpu

F32 = jnp.float32
BF16 = jnp.bfloat16
I32 = jnp.int32
I16 = jnp.int16

D_MODEL = 1024
CHUNK = 64
HEAD_DIM = 64
N_GH = 4
GROUP_W = N_GH * HEAD_DIM
IDX_HEADS = 8
IDX_DIM = 64
TOPK_MAX = 256
RET_THETA = 10000.0
GLA_DK = HEAD_DIM // 2
GLA_RANK = 16
GLA_TAU = 16.0
DIFF_DK = HEAD_DIM // 2
ROPE_THETA = 500000.0
ROPE_FRAC = 4
D_FF = 4 * D_MODEL
EPS = 1e-6
NEG_INF = -1e30

LANES = 128
VMEM_LIMIT = 56 * 1024 * 1024

_IN_SPLITS = (
    GROUP_W, HEAD_DIM, HEAD_DIM, IDX_HEADS * IDX_DIM, IDX_DIM, IDX_HEADS,
    GROUP_W, GROUP_W, GROUP_W, GROUP_W,
    N_GH * GLA_DK, N_GH * GLA_DK, GROUP_W, GLA_RANK, GROUP_W,
    2 * N_GH * DIFF_DK, 2 * N_GH * DIFF_DK, GROUP_W,
)
_IN_OFF = np.concatenate([[0], np.cumsum(_IN_SPLITS)]).tolist()
(_A_Q, _A_K, _A_V, _A_QI, _A_KI, _A_WI, _B_Q, _B_K, _B_V, _B_G,
 _C_Q, _C_K, _C_V, _C_A, _C_G, _D_Q, _D_K, _D_V) = range(18)

SEG_A = 0
SEG_W = 1024
SEG_B = 1152
SEG_G = 1920
SEG_C = 2432
SEG_D = 3072
W_COLS = 3840

NT_DIMS = (((1,), (1,)), ((), ()))
TN_DIMS = (((0,), (0,)), ((), ()))

TM = 512
TQ = 256
TK = 512
FLASH_SLAB = 32
VT_ROWS = HEAD_DIM + 16
LOG2E = math.log2(math.e)
TB = 256


def _float_key(v):
    bits = int(np.array(v, np.float32).view(np.int32))
    return -(bits & 0x7FFFFFFF) if bits < 0 else bits


NEG_KEY = _float_key(NEG_INF)
INT_MIN = -(2 ** 31)


def _cparams(*sem):
    return pltpu.CompilerParams(dimension_semantics=sem, vmem_limit_bytes=VMEM_LIMIT)


def _resident(shape, index_map):
    return pl.BlockSpec(shape, index_map, pipeline_mode=pl.Buffered(1))


def _dot(a, b):
    return jnp.dot(a, b, preferred_element_type=F32)


def _dot_nt(a, b):
    return lax.dot_general(a, b, NT_DIMS, preferred_element_type=F32)


def _dot_tn(a, b):
    return lax.dot_general(a, b, TN_DIMS, preferred_element_type=F32)


def _dot_f32(a, b):
    return jnp.dot(a, b, preferred_element_type=F32, precision=lax.Precision.HIGHEST)


def _split_bf16(x, n):
    parts = []
    for _ in range(n):
        p = x.astype(BF16)
        parts.append(p)
        x = x - p.astype(F32)
    return parts


def _dot_split(a, b):
    (a0, a1), (b0, b1) = _split_bf16(a, 2), _split_bf16(b, 2)
    return _dot(a0, b0) + (_dot(a0, b1) + _dot(a1, b0))


def _lane(shape):
    return lax.broadcasted_iota(I32, shape, len(shape) - 1)


def _silu(x):
    return x / (1.0 + jnp.exp(-x))


def _mod_kernel(c_ref, w_ref, b_ref, o_ref):
    cond = _silu(c_ref[...])
    o_ref[...] = _dot_f32(cond, w_ref[...]) + b_ref[...]


def _mod_call(c, mod_w, mod_b):
    B, D = c.shape
    n = mod_w.shape[1] // D
    return pl.pallas_call(
        _mod_kernel,
        out_shape=jax.ShapeDtypeStruct((B, n * D), F32),
        grid=(n,),
        in_specs=[pl.BlockSpec((B, D), lambda j: (0, 0)),
                  pl.BlockSpec((D, D), lambda j: (0, j)),
                  pl.BlockSpec((1, D), lambda j: (0, j))],
        out_specs=pl.BlockSpec((B, D), lambda j: (0, j)),
        compiler_params=_cparams("arbitrary"),
        name="mod",
    )(c, mod_w, mod_b.reshape(1, -1))


def _rope_group(xg, cos, sin, half, period):
    lo = (_lane((1, LANES)) % period) < half
    swapped = jnp.where(lo, pltpu.roll(xg, LANES - half, 1), pltpu.roll(xg, half, 1))
    return xg * cos + swapped * sin


def _in_kernel(x_ref, mod_ref, g_ref, w_ref, ca_ref, sa_ref, cb_ref, sb_ref, cd_ref, sd_ref,
               qa_ref, kva_ref, wa_ref, qkvb_ref, gate_ref, c_ref, d_ref, vta_ref, vtd_ref):
    x = x_ref[...]
    h = x * lax.rsqrt(jnp.mean(x * x, axis=-1, keepdims=True) + EPS) * g_ref[...]
    h = h * (1.0 + mod_ref[0, 1:2, :]) + mod_ref[0, 0:1, :]
    hb = h.astype(BF16)

    def seg(off, width):
        return _dot(hb, w_ref[:, off:off + width])

    lane = _lane((1, LANES))
    low = lane < HEAD_DIM

    ca, sa = ca_ref[...], sa_ref[...]
    a = seg(SEG_A, 1024)
    half_a = HEAD_DIM // ROPE_FRAC // 2
    for gi in range(2):
        r = _rope_group(a[:, gi * LANES:(gi + 1) * LANES], ca, sa, half_a, HEAD_DIM) * (HEAD_DIM ** -0.5 * LOG2E)
        rs = pltpu.roll(r, HEAD_DIM, 1)
        qa_ref[:, (2 * gi) * LANES:(2 * gi + 1) * LANES] = jnp.where(low, r, 0.0).astype(BF16)
        qa_ref[:, (2 * gi + 1) * LANES:(2 * gi + 2) * LANES] = jnp.where(low, rs, 0.0).astype(BF16)
    for gi in range(4):
        r = _rope_group(a[:, (2 + gi) * LANES:(3 + gi) * LANES], ca, sa, half_a, HEAD_DIM)
        rs = pltpu.roll(r, HEAD_DIM, 1)
        qa_ref[:, (4 + 2 * gi) * LANES:(5 + 2 * gi) * LANES] = jnp.where(low, 0.0, rs).astype(BF16)
        qa_ref[:, (5 + 2 * gi) * LANES:(6 + 2 * gi) * LANES] = jnp.where(low, 0.0, r).astype(BF16)
    kva_ref[:, 0:LANES] = _rope_group(a[:, 6 * LANES:7 * LANES], ca, sa, half_a, HEAD_DIM).astype(BF16)
    kva_ref[:, LANES:2 * LANES] = a[:, 7 * LANES:8 * LANES].astype(BF16)
    ones_rows = jnp.ones((VT_ROWS - HEAD_DIM, TM), BF16)
    vta_ref[0:HEAD_DIM, :] = jnp.transpose(a[:, 7 * LANES:8 * LANES])[0:HEAD_DIM, :].astype(BF16)
    vta_ref[HEAD_DIM:VT_ROWS, :] = ones_rows

    wa_ref[...] = seg(SEG_W, LANES)

    cb, sb = cb_ref[...], sb_ref[...]
    b = seg(SEG_B, 768)
    for gi in range(4):
        r = _rope_group(b[:, gi * LANES:(gi + 1) * LANES], cb, sb, HEAD_DIM // 2, HEAD_DIM)
        if gi < 2:
            r = r * (HEAD_DIM ** -0.5)
        qkvb_ref[:, gi * LANES:(gi + 1) * LANES] = r.astype(BF16)
    qkvb_ref[:, 512:768] = b[:, 512:768].astype(BF16)

    gate_ref[...] = seg(SEG_G, 512)
    c_ref[...] = seg(SEG_C, 640)

    cd, sd = cd_ref[...], sd_ref[...]
    d = seg(SEG_D, 768)
    half_d = DIFF_DK // ROPE_FRAC // 2
    for gi in range(4):
        r = _rope_group(d[:, gi * LANES:(gi + 1) * LANES], cd, sd, half_d, DIFF_DK)
        if gi < 2:
            r = r * (DIFF_DK ** -0.5 * LOG2E)
        d_ref[:, gi * LANES:(gi + 1) * LANES] = r.astype(BF16)
    d_ref[:, 512:768] = d[:, 512:768].astype(BF16)
    vt = jnp.transpose(d[:, 512:768])
    for h in range(N_GH):
        vtd_ref[h * VT_ROWS:h * VT_ROWS + HEAD_DIM, :] = vt[h * HEAD_DIM:(h + 1) * HEAD_DIM, :].astype(BF16)
        vtd_ref[h * VT_ROWS + HEAD_DIM:(h + 1) * VT_ROWS, :] = ones_rows


def _in_call(x2, mod3, g, w_p, tabs, T):
    N, D = x2.shape
    nt = T // TM
    row = lambda i: (i, 0)
    tab_spec = pl.BlockSpec((TM, LANES), row)
    outs = [(1536, BF16), (256, BF16), (LANES, F32), (768, BF16), (512, F32), (640, F32), (768, BF16)]
    vt_rows = (VT_ROWS, N_GH * VT_ROWS)
    vt_shapes = [jax.ShapeDtypeStruct((N // T * r, T), BF16) for r in vt_rows]
    vt_specs = [pl.BlockSpec((r, TM), lambda i: (i // nt, i % nt)) for r in vt_rows]
    return pl.pallas_call(
        _in_kernel,
        out_shape=[jax.ShapeDtypeStruct((N, w), dt) for w, dt in outs] + vt_shapes,
        grid=(N // TM,),
        in_specs=[pl.BlockSpec((TM, D), row),
                  pl.BlockSpec((1, 6, D), lambda i: (i // nt, 0, 0)),
                  _resident((1, D), lambda i: (0, 0)),
                  _resident((D, W_COLS), lambda i: (0, 0))] + [tab_spec] * 6,
        out_specs=[pl.BlockSpec((TM, w), row) for w, _ in outs] + vt_specs,
        compiler_params=_cparams("arbitrary"),
        name="in_proj",
    )(x2, mod3, g, w_p, *tabs)


def _pair_head_norm(o, gain, center):
    low = _lane((1, LANES)) < HEAD_DIM
    inv = 1.0 / HEAD_DIM

    def seg_mean(v):
        m0 = jnp.sum(jnp.where(low, v, 0.0), axis=1, keepdims=True)
        m1 = jnp.sum(jnp.where(low, 0.0, v), axis=1, keepdims=True)
        return jnp.where(low, m0, m1) * inv

    if center:
        o = o - seg_mean(o)
    return o * lax.rsqrt(seg_mean(o * o) + EPS) * gain


class _Flash:
    def __init__(self, n, s_ref, p_ref, acc_ref, st_ref):
        self.n, self.s, self.p, self.acc, self.st = n, s_ref, p_ref, acc_ref, st_ref

    @staticmethod
    def scratch(n):
        return [pltpu.VMEM((n, TK, TQ), F32), pltpu.VMEM((n, TK, TQ), BF16),
                pltpu.VMEM((n, VT_ROWS, TQ), F32), pltpu.VMEM((3, n, TQ), F32)]

    def _row(self, k, i):
        return self.st.at[k, i:i + 1, :]

    def init(self):
        self.st[0] = jnp.full((self.n, TQ), NEG_INF, F32)
        self.acc[...] = jnp.zeros_like(self.acc)

    def scores(self, i, s):
        self.s[i] = s
        self._row(1, i)[...] = jnp.max(_fold8(s, jnp.maximum), axis=0, keepdims=True)

    def probs(self, i):
        m_old = self._row(0, i)[...]
        m_new = jnp.maximum(m_old, self._row(1, i)[...])
        for r in range(0, TK, FLASH_SLAB):
            self.p[i, r:r + FLASH_SLAB, :] = jnp.exp2(self.s[i, r:r + FLASH_SLAB, :] - m_new).astype(BF16)
        self._row(0, i)[...] = m_new
        self._row(2, i)[...] = jnp.exp2(m_old - m_new)

    def values(self, i, vt):
        self.acc[i] = self._row(2, i)[...] * self.acc[i] + _dot(vt, self.p[i])

    def result(self, i):
        return self.acc[i, 0:HEAD_DIM, :] / self.acc[i, HEAD_DIM:HEAD_DIM + 1, :]

    def pipeline(self, first, n_rest, score, value):
        score(first, True)

        def body(t, carry):
            for i in range(self.n):
                self.probs(i)
            score(t, False)
            value(jnp.where(t == 0, first, t - 1))
            return carry

        lax.fori_loop(0, n_rest, body, 0)
        for i in range(self.n):
            self.probs(i)
        value(jnp.where(n_rest == 0, first, n_rest - 1))


def _fold8(x, op=jnp.add, rows=8):
    parts = [x[r:r + rows] for r in range(0, x.shape[0], rows)]
    while len(parts) > 1:
        parts = [op(a, b) for a, b in zip(parts[0::2], parts[1::2])]
    return parts[0]


def _head_norm_t(o, gain):
    ms = jnp.mean(o * o, axis=0, keepdims=True)
    return o * lax.rsqrt(ms + EPS) * gain


def _dsa_kernel(q_ref, w_ref, kv_ref, vt_ref, g_ref, o_ref, bias_ref, key_ref, hi_ref, lo_ref, *flash_refs, topk):
    q0 = pl.program_id(1) * TQ
    nkt = (q0 + TQ + TK - 1) // TK
    qchunk = (q0 + _lane((1, TQ))) // CHUNK
    wts = jnp.transpose(w_ref[...]) * (IDX_HEADS ** -0.5 * IDX_DIM ** -0.5)

    def score_tile(kt, carry):
        k0 = pl.multiple_of(kt * TK, TK)
        kk = kv_ref[pl.ds(k0, TK), 0:LANES]
        tile = bias_ref.at[pl.ds(k0, TK), :]
        for h in range(IDX_HEADS):
            term = jnp.maximum(_dot_nt(kk, q_ref[:, (4 + h) * LANES:(5 + h) * LANES]), 0.0) * wts[h:h + 1, :]
            if h == 0:
                tile[...] = term
            elif h < IDX_HEADS - 1:
                tile[...] += term
        kchunk = (k0 + lax.broadcasted_iota(I32, (TK, 1), 0)) // CHUNK
        acc = jnp.where(kchunk <= qchunk, tile[...] + term, NEG_INF)
        bits = pltpu.bitcast(acc, I32)
        sign = bits >> 31
        key = ((bits & 0x7FFFFFFF) ^ sign) - sign
        key_ref[pl.ds(k0, TK), :] = key
        hi_ref[pl.ds(k0, TK), :] = (key >> 16).astype(I16)
        lo_ref[pl.ds(k0, TK), :] = (key ^ 0x8000).astype(I16)
        return carry

    lax.fori_loop(0, nkt, score_tile, 0)

    kf = float(topk)

    def count16(ref, cand):
        def body(kt, acc):
            k0 = pl.multiple_of(kt * TK, TK)
            hit = jnp.where(ref[pl.ds(k0, TK), :] >= cand, jnp.ones((TK, TQ), I16), jnp.zeros((TK, TQ), I16))
            return acc + _fold8(hit, rows=16)
        acc = lax.fori_loop(0, nkt, body, jnp.zeros((16, TQ), I16))
        return jnp.sum(acc.astype(F32), axis=0, keepdims=True)

    def search16(ref, cnt_all):
        c0 = count16(ref, jnp.zeros((1, TQ), I16))
        t = jnp.where(c0 >= kf, 0, -32768).astype(I32)
        cnt = jnp.where(c0 >= kf, c0, cnt_all)

        def step(it, carry):
            t, cnt = carry
            cand = t + (jnp.int32(1) << (14 - it))
            c = count16(ref, cand.astype(I16))
            ok = c >= kf
            return jnp.where(ok, cand, t), jnp.where(ok, c, cnt)

        return lax.fori_loop(0, 15, step, (t, cnt))

    def count(pred):
        def body(kt, acc):
            k0 = pl.multiple_of(kt * TK, TK)
            hit = jnp.where(pred(key_ref[pl.ds(k0, TK), :], k0), 1.0, 0.0)
            return acc + _fold8(hit)
        acc = lax.fori_loop(0, nkt, body, jnp.zeros((8, TQ), F32))
        return jnp.sum(acc, axis=0, keepdims=True)

    t_hi, cnt_hi = search16(hi_ref, (nkt * TK).astype(F32))
    t_hi16 = t_hi.astype(I16)

    def low_digit_tile(kt, carry):
        k0 = pl.multiple_of(kt * TK, TK)
        hi = hi_ref[pl.ds(k0, TK), :]
        in_bucket = jnp.where(hi == t_hi16, lo_ref[pl.ds(k0, TK), :], jnp.full((TK, TQ), -32768, I16))
        hi_ref[pl.ds(k0, TK), :] = jnp.where(hi > t_hi16, jnp.full((TK, TQ), 32767, I16), in_bucket)
        return carry

    lax.fori_loop(0, nkt, low_digit_tile, 0)
    t_lo, cnt = search16(hi_ref, cnt_hi)
    thr = t_hi * 65536 + (t_lo + 32768)
    floor_key = jnp.int32(NEG_KEY + 1)
    tied = jnp.max(jnp.where((cnt > kf) & (thr >= floor_key), 1.0, 0.0)) > 0.0

    @pl.when(jnp.logical_not(tied))
    def _():
        lim = jnp.maximum(thr, floor_key)

        def body(kt, carry):
            k0 = pl.multiple_of(kt * TK, TK)
            bias_ref[pl.ds(k0, TK), :] = jnp.where(key_ref[pl.ds(k0, TK), :] >= lim, 0.0, NEG_INF)
            return carry
        lax.fori_loop(0, nkt, body, 0)

    @pl.when(tied)
    def _():
        need = kf - count(lambda keys, k0: keys > thr)

        def idx(k0):
            return k0 + lax.broadcasted_iota(I32, (TK, 1), 0)

        def search_idx(it, lim):
            cand = lim + (jnp.int32(1) << (14 - it))
            c = count(lambda keys, k0: (keys == thr) & (idx(k0) < cand))
            return jnp.where(c <= need, cand, lim)

        lim = lax.fori_loop(0, 15, search_idx, jnp.zeros((1, TQ), I32))

        def body(kt, carry):
            k0 = pl.multiple_of(kt * TK, TK)
            keys = key_ref[pl.ds(k0, TK), :]
            sel = ((keys > thr) | ((keys == thr) & (idx(k0) < lim))) & (keys >= floor_key)
            bias_ref[pl.ds(k0, TK), :] = jnp.where(sel, 0.0, NEG_INF)
            return carry
        lax.fori_loop(0, nkt, body, 0)

    flash = _Flash(N_GH, *flash_refs)
    flash.init()

    def score(kt, is_first):
        k0 = pl.multiple_of(kt * TK, TK)
        kk = kv_ref[pl.ds(k0, TK), 0:LANES]
        bias = bias_ref[pl.ds(k0, TK), :]
        for h in range(N_GH):
            flash.scores(h, _dot_nt(kk, q_ref[:, h * LANES:(h + 1) * LANES]) + bias)

    def value(kt):
        k0 = pl.multiple_of(kt * TK, TK)
        for h in range(N_GH):
            flash.values(h, vt_ref[:, pl.ds(k0, TK)])

    flash.pipeline(nkt - 1, nkt - 1, score, value)
    y = jnp.concatenate([_head_norm_t(flash.result(h), g_ref[h * HEAD_DIM:(h + 1) * HEAD_DIM, :])
                         for h in range(N_GH)], axis=0)
    o_ref[...] = jnp.transpose(y).astype(BF16)


def _dsa_call(qa, wa, kva, vta, g_t, B, T):
    nq = T // TQ
    topk = min(TOPK_MAX, T // 4)
    return pl.pallas_call(
        functools.partial(_dsa_kernel, topk=topk),
        out_shape=jax.ShapeDtypeStruct((B * T, GROUP_W), BF16),
        grid=(B, nq),
        in_specs=[pl.BlockSpec((TQ, 1536), lambda b, i: (b * nq + i, 0)),
                  pl.BlockSpec((TQ, LANES), lambda b, i: (b * nq + i, 0)),
                  pl.BlockSpec((T, 256), lambda b, i: (b, 0)),
                  pl.BlockSpec((VT_ROWS, T), lambda b, i: (b, 0)),
                  pl.BlockSpec((GROUP_W, TQ), lambda b, i: (0, 0))],
        out_specs=pl.BlockSpec((TQ, GROUP_W), lambda b, i: (b * nq + i, 0)),
        scratch_shapes=[pltpu.VMEM((T, TQ), F32), pltpu.VMEM((T, TQ), I32), pltpu.VMEM((T, TQ), I16),
                        pltpu.VMEM((T, TQ), I16)] + _Flash.scratch(N_GH),
        compiler_params=_cparams("arbitrary", "arbitrary"),
        name="dsa",
    )(qa, wa, kva, vta, g_t)


def _diff_kernel(q_ref, k_ref, vt_ref, lam_ref, g_ref, o_ref, qm_ref, *flash_refs, lam_init):
    q0 = pl.program_id(1) * TQ
    nfull = (q0 + CHUNK) // TK
    qchunk = (q0 + _lane((1, TQ))) // CHUNK
    lv = lam_ref[...]
    lam = (jnp.exp(jnp.sum(lv[0:1] * lv[1:2], axis=1, keepdims=True))
           - jnp.exp(jnp.sum(lv[2:3] * lv[3:4], axis=1, keepdims=True)) + lam_init)
    n_maps = 2 * N_GH
    qt = jnp.transpose(q_ref[...].astype(F32))
    sub = lax.broadcasted_iota(I32, (LANES, 1), 0) // DIFF_DK
    for mi in range(n_maps):
        qg = qt[(mi // 4) * LANES:(mi // 4 + 1) * LANES, :]
        qm_ref[mi] = jnp.where(sub == mi % 4, qg, 0.0).astype(BF16)

    flash = _Flash(n_maps, *flash_refs)
    flash.init()

    def score(kt, masked):
        k0 = pl.multiple_of(kt * TK, TK)
        if masked:
            kchunk = (k0 + lax.broadcasted_iota(I32, (TK, 1), 0)) // CHUNK
            admissible = kchunk <= qchunk
        for mi in range(n_maps):
            g = mi // 4
            s = _dot(k_ref[pl.ds(k0, TK), g * LANES:(g + 1) * LANES], qm_ref[mi])
            flash.scores(mi, jnp.where(admissible, s, NEG_INF) if masked else s)

    def value(kt):
        k0 = pl.multiple_of(kt * TK, TK)
        for mi in range(n_maps):
            h = mi // 2
            flash.values(mi, vt_ref[h * VT_ROWS:(h + 1) * VT_ROWS, pl.ds(k0, TK)])

    flash.pipeline(nfull, nfull, score, value)
    outs = []
    for h in range(N_GH):
        o = flash.result(2 * h) - lam * flash.result(2 * h + 1)
        outs.append(_head_norm_t(o, g_ref[h * HEAD_DIM:(h + 1) * HEAD_DIM, :]) * (1.0 - lam_init))
    o_ref[...] = jnp.transpose(jnp.concatenate(outs, axis=0)).astype(BF16)


def _diff_call(dqkv, vtd, lam_vecs, g_t, lam_init, B, T):
    nq = T // TQ
    return pl.pallas_call(
        functools.partial(_diff_kernel, lam_init=lam_init),
        out_shape=jax.ShapeDtypeStruct((B * T, GROUP_W), BF16),
        grid=(B, nq),
        in_specs=[pl.BlockSpec((TQ, 256), lambda b, i: (b * nq + i, 0)),
                  pl.BlockSpec((T, 256), lambda b, i: (b, 1)),
                  pl.BlockSpec((N_GH * VT_ROWS, T), lambda b, i: (b, 0)),
                  pl.BlockSpec((4, LANES), lambda b, i: (0, 0)),
                  pl.BlockSpec((GROUP_W, TQ), lambda b, i: (0, 0))],
        out_specs=pl.BlockSpec((TQ, GROUP_W), lambda b, i: (b * nq + i, 0)),
        scratch_shapes=[pltpu.VMEM((2 * N_GH, LANES, TQ), BF16)] + _Flash.scratch(2 * N_GH),
        compiler_params=_cparams("arbitrary", "arbitrary"),
        name="diff",
    )(dqkv, dqkv, vtd, lam_vecs, g_t)


def _ret_tables():
    h = np.arange(N_GH, dtype=np.float32)
    log_g = jnp.log(1.0 - 2.0 ** (-5.0 - jnp.asarray(h)))
    t = np.arange(TB)
    same = (t[:, None] // CHUNK) == (t[None, :] // CHUNK)
    earlier = (t[None, :] // CHUNK) < (t[:, None] // CHUNK)
    dist = jnp.asarray(np.where(same, np.abs(t[:, None] - t[None, :]), t[:, None] - t[None, :]), F32)
    dmat = jnp.where(jnp.asarray(same | earlier)[None],
                     jnp.exp(dist[None] * log_g[:, None, None]), 0.0)
    tt = jnp.asarray(t, F32)
    xi = jnp.exp((tt + 1.0)[None, :] * log_g[:, None])
    zeta = jnp.exp((TB - 1.0 - tt)[None, :] * log_g[:, None])
    gblk = jnp.exp(TB * log_g)

    def lanes(tab):
        return jnp.repeat(tab.reshape(2, 2, TB), HEAD_DIM, axis=1).transpose(0, 2, 1)

    r = np.arange(LANES)
    blockdiag = (r[:, None] // HEAD_DIM) == (r[None, :] // HEAD_DIM)
    gb = jnp.where(jnp.asarray(blockdiag)[None],
                   jnp.repeat(gblk.reshape(2, 2), HEAD_DIM, axis=1)[:, :, None], 0.0)
    return dmat, lanes(xi), lanes(zeta), gb


def _ret_kernel(qkv_ref, gate_ref, dmat_ref, xi_ref, zeta_ref, gb_ref, g_ref, o_ref, r_ref):
    @pl.when(pl.program_id(1) == 0)
    def _():
        r_ref[...] = jnp.zeros_like(r_ref)

    lane = _lane((1, LANES))
    low = lane < HEAD_DIM
    r_idx = lax.broadcasted_iota(I32, (LANES, LANES), 0) // HEAD_DIM
    c_idx = lax.broadcasted_iota(I32, (LANES, LANES), 1) // HEAD_DIM
    for g in range(2):
        qg = qkv_ref[:, g * LANES:(g + 1) * LANES]
        kg = qkv_ref[:, 256 + g * LANES:256 + (g + 1) * LANES]
        vg = qkv_ref[:, 512 + g * LANES:512 + (g + 1) * LANES]
        state = r_ref[g]
        inter = _dot(qg, state.astype(BF16)) * xi_ref[g]
        parts = []
        for hh in range(2):
            qm = jnp.where(low, qg, jnp.zeros_like(qg)) if hh == 0 else jnp.where(low, jnp.zeros_like(qg), qg)
            s = _dot_nt(qm, kg) * dmat_ref[2 * g + hh]
            parts.append(_dot(s.astype(BF16), vg))
        o = jnp.where(low, parts[0], parts[1]) + inter
        kz = (kg.astype(F32) * zeta_ref[g]).astype(BF16)
        upd = jnp.where(r_idx == c_idx, _dot_tn(kz, vg), 0.0)
        r_ref[g] = state * gb_ref[g] + upd
        y = _pair_head_norm(o, g_ref[:, g * LANES:(g + 1) * LANES], True)
        o_ref[:, g * LANES:(g + 1) * LANES] = (_silu(gate_ref[:, g * LANES:(g + 1) * LANES]) * y).astype(BF16)


def _ret_call(qkvb, gates, tabs, g_b, B, T):
    nb = T // TB
    dmat, xi, zeta, gb = tabs
    const3 = lambda b, i: (0, 0, 0)
    return pl.pallas_call(
        _ret_kernel,
        out_shape=jax.ShapeDtypeStruct((B * T, GROUP_W), BF16),
        grid=(B, nb),
        in_specs=[pl.BlockSpec((TB, 768), lambda b, i: (b * nb + i, 0)),
                  pl.BlockSpec((TB, GROUP_W), lambda b, i: (b * nb + i, 0)),
                  pl.BlockSpec((N_GH, TB, TB), const3),
                  pl.BlockSpec((2, TB, LANES), const3),
                  pl.BlockSpec((2, TB, LANES), const3),
                  pl.BlockSpec((2, LANES, LANES), const3),
                  pl.BlockSpec((1, GROUP_W), lambda b, i: (0, 0))],
        out_specs=pl.BlockSpec((TB, GROUP_W), lambda b, i: (b * nb + i, 0)),
        scratch_shapes=[pltpu.VMEM((2, LANES, LANES), F32)],
        compiler_params=_cparams("arbitrary", "arbitrary"),
        name="ret",
    )(qkvb, gates, dmat, xi, zeta, gb, g_b)


def _gla_kernel(c_ref, gate_ref, wa2_ref, ba_ref, g_ref, o_ref, s_ref, b_scr, q_scr, k_scr, v_scr, acc_scr):
    @pl.when(pl.program_id(1) == 0)
    def _():
        s_ref[...] = jnp.zeros_like(s_ref)

    n_chunks = TB // CHUNK
    hmask = (lax.broadcasted_iota(I32, (LANES, GROUP_W), 0) // GLA_DK
             == lax.broadcasted_iota(I32, (LANES, GROUP_W), 1) // HEAD_DIM)
    hsum = jnp.where(hmask, 1.0, 0.0).astype(BF16)
    r_idx = lax.broadcasted_iota(I32, (TB, TB), 0)
    c_idx = lax.broadcasted_iota(I32, (TB, TB), 1)
    tri = jnp.where((r_idx >= c_idx) & (r_idx // CHUNK == c_idx // CHUNK), 1.0, 0.0).astype(BF16)

    z = _dot_split(c_ref[:, 512:640], wa2_ref[...]) + ba_ref[...]
    log_a = (jnp.minimum(z, 0.0) - jnp.log(1.0 + jnp.exp(-jnp.abs(z)))) * (1.0 / GLA_TAU)
    b = sum(_dot(tri, piece) for piece in _split_bf16(log_a, 3))
    q = c_ref[:, 0:128] * (GLA_DK ** -0.5)
    b_scr[...] = b * LOG2E
    q_scr[...] = q
    k_scr[...] = c_ref[:, 128:256]
    v_scr[...] = c_ref[:, 256:512]

    b_last = [b[(c + 1) * CHUNK - 1:(c + 1) * CHUNK, :] for c in range(n_chunks)]
    b_end = jnp.concatenate([jnp.broadcast_to(bl, (CHUNK, LANES)) for bl in b_last], axis=0)
    kd = (c_ref[:, 128:256] * jnp.exp(b_end - b)).astype(BF16)
    qe = (q * jnp.exp(b)).astype(BF16)
    vb = c_ref[:, 256:512].astype(BF16)
    state = s_ref[...]
    inters = []
    for c in range(n_chunks):
        rows = slice(c * CHUNK, (c + 1) * CHUNK)
        inters.append(_dot(qe[rows], state.astype(BF16)))
        scale = jnp.transpose(jnp.broadcast_to(jnp.exp(b_last[c]), (LANES, LANES)))
        state = jnp.where(hmask, state * jnp.concatenate([scale, scale], axis=1) + _dot_tn(kd[rows], vb[rows]), 0.0)
    s_ref[...] = state

    acc_scr[...] = jnp.concatenate(inters, axis=0)

    keys_per_step = 16

    def key_step(s4, carry):
        for c in range(n_chunks):
            rows = slice(c * CHUNK, (c + 1) * CHUNK)
            part = None
            for u in range(keys_per_step):
                r = c * CHUNK + s4 * keys_per_step + u
                gap = pltpu.bitcast(b_scr[rows, :] - b_scr[pl.ds(r, 1), :], I32)
                decay = jnp.exp2(pltpu.bitcast(gap | jnp.int32(INT_MIN), F32))
                w = (q_scr[rows, :] * decay * k_scr[pl.ds(r, 1), :]).astype(BF16)
                term = _dot(w, hsum) * v_scr[pl.ds(r, 1), :]
                part = term if part is None else part + term
            acc_scr[rows, :] += part
        return carry

    lax.fori_loop(0, CHUNK // keys_per_step, key_step, 0)
    o = acc_scr[...]
    for gi in range(2):
        y = _pair_head_norm(o[:, gi * LANES:(gi + 1) * LANES], g_ref[:, gi * LANES:(gi + 1) * LANES], True)
        o_ref[:, gi * LANES:(gi + 1) * LANES] = (_silu(gate_ref[:, gi * LANES:(gi + 1) * LANES]) * y).astype(BF16)


def _gla_call(cc, gates, wa2_p, ba, g_c, B, T):
    nb = T // TB
    return pl.pallas_call(
        _gla_kernel,
        out_shape=jax.ShapeDtypeStruct((B * T, GROUP_W), BF16),
        grid=(B, nb),
        in_specs=[pl.BlockSpec((TB, 640), lambda b, i: (b * nb + i, 0)),
                  pl.BlockSpec((TB, GROUP_W), lambda b, i: (b * nb + i, 1)),
                  pl.BlockSpec((LANES, LANES), lambda b, i: (0, 0)),
                  pl.BlockSpec((1, LANES), lambda b, i: (0, 0)),
                  pl.BlockSpec((1, GROUP_W), lambda b, i: (0, 0))],
        out_specs=pl.BlockSpec((TB, GROUP_W), lambda b, i: (b * nb + i, 0)),
        scratch_shapes=[pltpu.VMEM((LANES, GROUP_W), F32), pltpu.VMEM((TB, LANES), F32),
                        pltpu.VMEM((TB, LANES), F32), pltpu.VMEM((TB, LANES), F32),
                        pltpu.VMEM((TB, GROUP_W), F32), pltpu.VMEM((TB, GROUP_W), F32)],
        compiler_params=_cparams("arbitrary", "arbitrary"),
        name="gla",
    )(cc, gates, wa2_p, ba, g_c)


def _rms(y, g):
    return y * lax.rsqrt(jnp.mean(y * y, axis=-1, keepdims=True) + EPS) * g


def _out_mlp_kernel(x_ref, ya_ref, yb_ref, yc_ref, yd_ref, mod_ref, wo_ref, w1_ref, w2_ref,
                    gpost_ref, gpre_ref, gpost2_ref, o_ref):
    y = _dot(ya_ref[...], wo_ref[0:256, :])
    y = y + _dot(yb_ref[...], wo_ref[256:512, :])
    y = y + _dot(yc_ref[...], wo_ref[512:768, :])
    y = y + _dot(yd_ref[...], wo_ref[768:1024, :])
    x = x_ref[...] + mod_ref[0, 2:3, :] * _rms(y, gpost_ref[...])
    h = _rms(x, gpre_ref[...]) * (1.0 + mod_ref[0, 4:5, :]) + mod_ref[0, 3:4, :]
    hb = h.astype(BF16)
    acc = jnp.zeros(x.shape, F32)
    fc = 1024
    for f in range(D_FF // fc):
        u = jnp.maximum(_dot(hb, w1_ref[:, f * fc:(f + 1) * fc]), 0.0)
        acc = acc + _dot((u * u).astype(BF16), w2_ref[f * fc:(f + 1) * fc, :])
    o_ref[...] = x + mod_ref[0, 5:6, :] * _rms(acc, gpost2_ref[...])


def _out_mlp_call(x2, ys, mod3, w_out, w1, w2, g_post, g_pre2, g_post2, T):
    N, D = x2.shape
    nt = T // TM
    row = lambda i: (i, 0)
    const = lambda i: (0, 0)
    return pl.pallas_call(
        _out_mlp_kernel,
        out_shape=jax.ShapeDtypeStruct((N, D), F32),
        grid=(N // TM,),
        in_specs=[pl.BlockSpec((TM, D), row)] + [pl.BlockSpec((TM, GROUP_W), row)] * 4
                 + [pl.BlockSpec((1, 6, D), lambda i: (i // nt, 0, 0)),
                    _resident((D, D), const), _resident((D, D_FF), const), _resident((D_FF, D), const),
                    _resident((1, D), const), _resident((1, D), const), _resident((1, D), const)],
        out_specs=pl.BlockSpec((TM, D), row),
        compiler_params=_cparams("arbitrary"),
        name="out_mlp",
    )(x2, *ys, mod3, w_out, w1, w2, g_post, g_pre2, g_post2)


def _relayout_w_in(w_in):
    col = lambda k: w_in[:, _IN_OFF[k]:_IN_OFF[k + 1]]
    zeros = lambda n: jnp.zeros((w_in.shape[0], n), w_in.dtype)
    parts = [col(_A_Q), col(_A_QI), col(_A_K), col(_A_KI), col(_A_V), zeros(64),
             col(_A_WI), zeros(120),
             col(_B_Q), col(_B_K), col(_B_V),
             col(_B_G), col(_C_G),
             col(_C_Q), col(_C_K), col(_C_V), col(_C_A), zeros(112),
             col(_D_Q), col(_D_K), col(_D_V)]
    return jnp.concatenate(parts, axis=1).astype(BF16)


def _rope_tables(pos, rot, period, theta):
    half = rot // 2
    inv = theta ** (-jnp.arange(half, dtype=F32) / half)
    ang = pos.astype(F32)[..., None] * inv
    cos, sin = jnp.cos(ang), jnp.sin(ang)
    pad = ang.shape[:-1] + (period - rot,)
    ct = jnp.concatenate([cos, cos, jnp.ones(pad, F32)], axis=-1)
    st = jnp.concatenate([-sin, sin, jnp.zeros(pad, F32)], axis=-1)
    reps = LANES // period
    return (jnp.tile(ct, (1, 1, reps)).reshape(-1, LANES), jnp.tile(st, (1, 1, reps)).reshape(-1, LANES))


def _layer(x2, cond, tabs, ret_tabs, layer_idx, B, T, mod_w, mod_b, attn_pre_g, attn_post_g,
           mlp_pre_g, mlp_post_g, w_in, gla_wa2, gla_ba, lam_q1, lam_k1, lam_q2, lam_k2,
           head_norm_g, w_out, mlp_w1, mlp_w2):
    D = D_MODEL
    mod3 = _mod_call(cond, mod_w, mod_b).reshape(B, 6, D)
    qa, kva, wa, qkvb, gates, cc, dqkv, vta, vtd = _in_call(
        x2, mod3, attn_pre_g.reshape(1, D), _relayout_w_in(w_in), tabs, T)

    g_a, g_b, g_c, g_d = jnp.split(head_norm_g, 4)
    gain_t = lambda g: jnp.broadcast_to(g[:, None], (GROUP_W, TQ))
    y_a = _dsa_call(qa, wa, kva, vta, gain_t(g_a), B, T)
    y_b = _ret_call(qkvb, gates, ret_tabs, g_b.reshape(1, GROUP_W), B, T)
    wa2_p = jnp.zeros((LANES, LANES), F32).at[:GLA_RANK].set(gla_wa2)
    y_c = _gla_call(cc, gates, wa2_p, gla_ba.reshape(1, LANES), g_c.reshape(1, GROUP_W), B, T)
    lam_init = 0.8 - 0.6 * math.exp(-0.3 * layer_idx)
    lam_vecs = jnp.zeros((4, LANES), F32).at[:, :DIFF_DK].set(jnp.stack([lam_q1, lam_k1, lam_q2, lam_k2]))
    y_d = _diff_call(dqkv, vtd, lam_vecs, gain_t(g_d), lam_init, B, T)

    return _out_mlp_call(x2, (y_a, y_b, y_c, y_d), mod3, w_out.astype(BF16), mlp_w1.astype(BF16),
                         mlp_w2.astype(BF16), attn_post_g.reshape(1, D), mlp_pre_g.reshape(1, D),
                         mlp_post_g.reshape(1, D), T)


def kernel(x, c, positions, mod_w, mod_b, attn_pre_g, attn_post_g, mlp_pre_g, mlp_post_g, w_in,
           gla_wa2, gla_ba, lam_q1, lam_k1, lam_q2, lam_k2, head_norm_g, w_out, mlp_w1, mlp_w2):
    B, T, D = x.shape
    assert D == D_MODEL and T % TM == 0 and T % TB == 0
    tabs = (_rope_tables(positions, HEAD_DIM // ROPE_FRAC, HEAD_DIM, ROPE_THETA)
            + _rope_tables(positions, HEAD_DIM, HEAD_DIM, RET_THETA)
            + _rope_tables(positions, DIFF_DK // ROPE_FRAC, DIFF_DK, ROPE_THETA))
    ret_tabs = _ret_tables()
    x2 = x.reshape(B * T, D)
    for l in range(mod_w.shape[0]):
        x2 = _layer(x2, c, tabs, ret_tabs, l, B, T, mod_w[l], mod_b[l], attn_pre_g[l], attn_post_g[l],
                    mlp_pre_g[l], mlp_post_g[l], w_in[l], gla_wa2[l], gla_ba[l], lam_q1[l], lam_k1[l],
                    lam_q2[l], lam_k2[l], head_norm_g[l], w_out[l], mlp_w1[l], mlp_w2[l])
    return x2.reshape(B, T, D)
```

```python
import functools
import math

import numpy as np
import jax
import jax.numpy as jnp
from jax import lax
from jax.experimental import pallas as pl
from jax.experimental.pallas import tpu as pltpu

F32 = jnp.float32
BF16 = jnp.bfloat16
I32 = jnp.int32
I16 = jnp.int16

D_MODEL = 1024
CHUNK = 64
HEAD_DIM = 64
N_GH = 4
GROUP_W = N_GH * HEAD_DIM
IDX_HEADS = 8
IDX_DIM = 64
TOPK_MAX = 256
RET_THETA = 10000.0
GLA_DK = HEAD_DIM // 2
GLA_RANK = 16
GLA_TAU = 16.0
DIFF_DK = HEAD_DIM // 2
ROPE_THETA = 500000.0
ROPE_FRAC = 4
D_FF = 4 * D_MODEL
EPS = 1e-6
NEG_INF = -1e30

LANES = 128
VMEM_LIMIT = 56 * 1024 * 1024

_IN_SPLITS = (
    GROUP_W, HEAD_DIM, HEAD_DIM, IDX_HEADS * IDX_DIM, IDX_DIM, IDX_HEADS,
    GROUP_W, GROUP_W, GROUP_W, GROUP_W,
    N_GH * GLA_DK, N_GH * GLA_DK, GROUP_W, GLA_RANK, GROUP_W,
    2 * N_GH * DIFF_DK, 2 * N_GH * DIFF_DK, GROUP_W,
)
_IN_OFF = np.concatenate([[0], np.cumsum(_IN_SPLITS)]).tolist()
(_A_Q, _A_K, _A_V, _A_QI, _A_KI, _A_WI, _B_Q, _B_K, _B_V, _B_G,
 _C_Q, _C_K, _C_V, _C_A, _C_G, _D_Q, _D_K, _D_V) = range(18)

SEG_A = 0
SEG_W = 1024
SEG_B = 1152
SEG_G = 1920
SEG_C = 2432
SEG_D = 3072
W_COLS = 3840

NT_DIMS = (((1,), (1,)), ((), ()))
TN_DIMS = (((0,), (0,)), ((), ()))

TM = 512
TQ = 512
TK = 512
COUNT_ROWS = 512
FLASH_SLAB = 32
VT_ROWS = HEAD_DIM + 16
LOG2E = math.log2(math.e)
TB = 256


def _float_key(v):
    bits = int(np.array(v, np.float32).view(np.int32))
    return -(bits & 0x7FFFFFFF) if bits < 0 else bits


NEG_KEY = _float_key(NEG_INF)
INT_MIN = -(2 ** 31)


def _cparams(*sem):
    return pltpu.CompilerParams(dimension_semantics=sem, vmem_limit_bytes=VMEM_LIMIT)


def _resident(shape, index_map):
    return pl.BlockSpec(shape, index_map, pipeline_mode=pl.Buffered(1))


def _dot(a, b):
    return jnp.dot(a, b, preferred_element_type=F32)


def _dot_nt(a, b):
    return lax.dot_general(a, b, NT_DIMS, preferred_element_type=F32)


def _dot_tn(a, b):
    return lax.dot_general(a, b, TN_DIMS, preferred_element_type=F32)


def _dot_f32(a, b):
    return jnp.dot(a, b, preferred_element_type=F32, precision=lax.Precision.HIGHEST)


def _split_bf16(x, n):
    parts = []
    for _ in range(n):
        p = x.astype(BF16)
        parts.append(p)
        x = x - p.astype(F32)
    return parts


def _dot_split(a, b):
    (a0, a1), (b0, b1) = _split_bf16(a, 2), _split_bf16(b, 2)
    return _dot(a0, b0) + (_dot(a0, b1) + _dot(a1, b0))


def _lane(shape):
    return lax.broadcasted_iota(I32, shape, len(shape) - 1)


def _silu(x):
    return x / (1.0 + jnp.exp(-x))


def _mod_kernel(c_ref, w_ref, b_ref, o_ref):
    cond = _silu(c_ref[...])
    o_ref[...] = _dot_f32(cond, w_ref[...]) + b_ref[...]


def _mod_call(c, mod_w, mod_b):
    B, D = c.shape
    n = mod_w.shape[1] // D
    return pl.pallas_call(
        _mod_kernel,
        out_shape=jax.ShapeDtypeStruct((B, n * D), F32),
        grid=(n,),
        in_specs=[pl.BlockSpec((B, D), lambda j: (0, 0)),
                  pl.BlockSpec((D, D), lambda j: (0, j)),
                  pl.BlockSpec((1, D), lambda j: (0, j))],
        out_specs=pl.BlockSpec((B, D), lambda j: (0, j)),
        compiler_params=_cparams("arbitrary"),
        name="mod",
    )(c, mod_w, mod_b.reshape(1, -1))


def _rope_group(xg, cos, sin, half, period):
    lo = (_lane((1, LANES)) % period) < half
    swapped = jnp.where(lo, pltpu.roll(xg, LANES - half, 1), pltpu.roll(xg, half, 1))
    return xg * cos + swapped * sin


def _in_kernel(x_ref, mod_ref, g_ref, w_ref, ca_ref, sa_ref, cb_ref, sb_ref, cd_ref, sd_ref,
               qa_ref, kva_ref, wa_ref, qkvb_ref, gate_ref, c_ref, d_ref, vta_ref, vtd_ref):
    x = x_ref[...]
    h = x * lax.rsqrt(jnp.mean(x * x, axis=-1, keepdims=True) + EPS) * g_ref[...]
    h = h * (1.0 + mod_ref[0, 1:2, :]) + mod_ref[0, 0:1, :]
    hb = h.astype(BF16)

    def seg(off, width):
        return _dot(hb, w_ref[:, off:off + width])

    lane = _lane((1, LANES))
    low = lane < HEAD_DIM

    ca, sa = ca_ref[...], sa_ref[...]
    a = seg(SEG_A, 1024)
    half_a = HEAD_DIM // ROPE_FRAC // 2
    for gi in range(2):
        r = _rope_group(a[:, gi * LANES:(gi + 1) * LANES], ca, sa, half_a, HEAD_DIM) * (HEAD_DIM ** -0.5 * LOG2E)
        rs = pltpu.roll(r, HEAD_DIM, 1)
        qa_ref[:, (2 * gi) * LANES:(2 * gi + 1) * LANES] = jnp.where(low, r, 0.0).astype(BF16)
        qa_ref[:, (2 * gi + 1) * LANES:(2 * gi + 2) * LANES] = jnp.where(low, rs, 0.0).astype(BF16)
    for gi in range(4):
        r = _rope_group(a[:, (2 + gi) * LANES:(3 + gi) * LANES], ca, sa, half_a, HEAD_DIM)
        rs = pltpu.roll(r, HEAD_DIM, 1)
        qa_ref[:, (4 + 2 * gi) * LANES:(5 + 2 * gi) * LANES] = jnp.where(low, 0.0, rs).astype(BF16)
        qa_ref[:, (5 + 2 * gi) * LANES:(6 + 2 * gi) * LANES] = jnp.where(low, 0.0, r).astype(BF16)
    kva_ref[:, 0:LANES] = _rope_group(a[:, 6 * LANES:7 * LANES], ca, sa, half_a, HEAD_DIM).astype(BF16)
    kva_ref[:, LANES:2 * LANES] = a[:, 7 * LANES:8 * LANES].astype(BF16)
    ones_rows = jnp.ones((VT_ROWS - HEAD_DIM, TM), BF16)
    vta_ref[0:HEAD_DIM, :] = jnp.transpose(a[:, 7 * LANES:8 * LANES])[0:HEAD_DIM, :].astype(BF16)
    vta_ref[HEAD_DIM:VT_ROWS, :] = ones_rows

    wa_ref[...] = seg(SEG_W, LANES)

    cb, sb = cb_ref[...], sb_ref[...]
    b = seg(SEG_B, 768)
    for gi in range(4):
        r = _rope_group(b[:, gi * LANES:(gi + 1) * LANES], cb, sb, HEAD_DIM // 2, HEAD_DIM)
        if gi < 2:
            r = r * (HEAD_DIM ** -0.5)
        qkvb_ref[:, gi * LANES:(gi + 1) * LANES] = r.astype(BF16)
    qkvb_ref[:, 512:768] = b[:, 512:768].astype(BF16)

    gate_ref[...] = seg(SEG_G, 512)
    c_ref[...] = seg(SEG_C, 640)

    cd, sd = cd_ref[...], sd_ref[...]
    d = seg(SEG_D, 768)
    half_d = DIFF_DK // ROPE_FRAC // 2
    for gi in range(4):
        r = _rope_group(d[:, gi * LANES:(gi + 1) * LANES], cd, sd, half_d, DIFF_DK)
        if gi < 2:
            r = r * (DIFF_DK ** -0.5 * LOG2E)
        d_ref[:, gi * LANES:(gi + 1) * LANES] = r.astype(BF16)
    d_ref[:, 512:768] = d[:, 512:768].astype(BF16)
    vt = jnp.transpose(d[:, 512:768])
    for h in range(N_GH):
        vtd_ref[h * VT_ROWS:h * VT_ROWS + HEAD_DIM, :] = vt[h * HEAD_DIM:(h + 1) * HEAD_DIM, :].astype(BF16)
        vtd_ref[h * VT_ROWS + HEAD_DIM:(h + 1) * VT_ROWS, :] = ones_rows


def _in_call(x2, mod3, g, w_p, tabs, T):
    N, D = x2.shape
    nt = T // TM
    row = lambda i: (i, 0)
    tab_spec = pl.BlockSpec((TM, LANES), row)
    outs = [(1536, BF16), (256, BF16), (LANES, F32), (768, BF16), (512, F32), (640, F32), (768, BF16)]
    vt_rows = (VT_ROWS, N_GH * VT_ROWS)
    vt_shapes = [jax.ShapeDtypeStruct((N // T * r, T), BF16) for r in vt_rows]
    vt_specs = [pl.BlockSpec((r, TM), lambda i: (i // nt, i % nt)) for r in vt_rows]
    return pl.pallas_call(
        _in_kernel,
        out_shape=[jax.ShapeDtypeStruct((N, w), dt) for w, dt in outs] + vt_shapes,
        grid=(N // TM,),
        in_specs=[pl.BlockSpec((TM, D), row),
                  pl.BlockSpec((1, 6, D), lambda i: (i // nt, 0, 0)),
                  _resident((1, D), lambda i: (0, 0)),
                  _resident((D, W_COLS), lambda i: (0, 0))] + [tab_spec] * 6,
        out_specs=[pl.BlockSpec((TM, w), row) for w, _ in outs] + vt_specs,
        compiler_params=_cparams("arbitrary"),
        name="in_proj",
    )(x2, mod3, g, w_p, *tabs)


def _pair_head_norm(o, gain, center):
    low = _lane((1, LANES)) < HEAD_DIM
    inv = 1.0 / HEAD_DIM

    def seg_mean(v):
        m0 = jnp.sum(jnp.where(low, v, 0.0), axis=1, keepdims=True)
        m1 = jnp.sum(jnp.where(low, 0.0, v), axis=1, keepdims=True)
        return jnp.where(low, m0, m1) * inv

    if center:
        o = o - seg_mean(o)
    return o * lax.rsqrt(seg_mean(o * o) + EPS) * gain


class _Flash:
    def __init__(self, n, s_ref, p_ref, acc_ref, st_ref):
        self.n, self.s, self.p, self.acc, self.st = n, s_ref, p_ref, acc_ref, st_ref

    @staticmethod
    def scratch(n):
        return [pltpu.VMEM((n, TK, TQ), F32), pltpu.VMEM((n, TK, TQ), BF16),
                pltpu.VMEM((n, VT_ROWS, TQ), F32), pltpu.VMEM((3, n, TQ), F32)]

    def _row(self, k, i):
        return self.st.at[k, i:i + 1, :]

    def init(self):
        self.st[0] = jnp.full((self.n, TQ), NEG_INF, F32)
        self.acc[...] = jnp.zeros_like(self.acc)

    def scores(self, i, s):
        self.s[i] = s
        self._row(1, i)[...] = jnp.max(_fold8(s, jnp.maximum), axis=0, keepdims=True)

    def probs(self, i):
        m_old = self._row(0, i)[...]
        m_new = jnp.maximum(m_old, self._row(1, i)[...])
        for r in range(0, TK, FLASH_SLAB):
            self.p[i, r:r + FLASH_SLAB, :] = jnp.exp2(self.s[i, r:r + FLASH_SLAB, :] - m_new).astype(BF16)
        self._row(0, i)[...] = m_new
        self._row(2, i)[...] = jnp.exp2(m_old - m_new)

    def values(self, i, vt):
        self.acc[i] = self._row(2, i)[...] * self.acc[i] + _dot(vt, self.p[i])

    def result(self, i):
        return self.acc[i, 0:HEAD_DIM, :] / self.acc[i, HEAD_DIM:HEAD_DIM + 1, :]

    def pipeline(self, first, n_rest, score, value):
        score(first, True)

        def body(t, carry):
            for i in range(self.n):
                self.probs(i)
            score(t, False)
            value(jnp.where(t == 0, first, t - 1))
            return carry

        lax.fori_loop(0, n_rest, body, 0)
        for i in range(self.n):
            self.probs(i)
        value(jnp.where(n_rest == 0, first, n_rest - 1))


def _fold8(x, op=jnp.add, rows=8):
    parts = [x[r:r + rows] for r in range(0, x.shape[0], rows)]
    while len(parts) > 1:
        parts = [op(a, b) for a, b in zip(parts[0::2], parts[1::2])]
    return parts[0]


def _head_norm_t(o, gain):
    ms = jnp.mean(o * o, axis=0, keepdims=True)
    return o * lax.rsqrt(ms + EPS) * gain


def _dsa_kernel(q_ref, w_ref, kv_ref, vt_ref, g_ref, o_ref, bias_ref, key_ref, hi_ref, lo_ref, *flash_refs, topk):
    q0 = pl.program_id(1) * TQ
    nkt = (q0 + TQ + TK - 1) // TK
    qchunk = (q0 + _lane((1, TQ))) // CHUNK
    wts = jnp.transpose(w_ref[...]) * (IDX_HEADS ** -0.5 * IDX_DIM ** -0.5)

    def score_tile(kt, carry):
        k0 = pl.multiple_of(kt * TK, TK)
        kk = kv_ref[pl.ds(k0, TK), 0:LANES]
        tile = bias_ref.at[pl.ds(k0, TK), :]
        for h in range(IDX_HEADS):
            term = jnp.maximum(_dot_nt(kk, q_ref[:, (4 + h) * LANES:(5 + h) * LANES]), 0.0) * wts[h:h + 1, :]
            if h == 0:
                tile[...] = term
            elif h < IDX_HEADS - 1:
                tile[...] += term
        kchunk = (k0 + lax.broadcasted_iota(I32, (TK, 1), 0)) // CHUNK
        acc = jnp.where(kchunk <= qchunk, tile[...] + term, NEG_INF)
        bits = pltpu.bitcast(acc, I32)
        sign = bits >> 31
        key = ((bits & 0x7FFFFFFF) ^ sign) - sign
        key_ref[pl.ds(k0, TK), :] = key
        hi_ref[pl.ds(k0, TK), :] = (key >> 16).astype(I16)
        lo_ref[pl.ds(k0, TK), :] = (key ^ 0x8000).astype(I16)
        return carry

    lax.fori_loop(0, nkt, score_tile, 0)

    cr = min(COUNT_ROWS, key_ref.shape[0])
    n_count = (nkt * TK + cr - 1) // cr

    def pad_tile(kt, carry):
        k0 = pl.multiple_of(kt * TK, TK)
        hi_ref[pl.ds(k0, TK), :] = jnp.full((TK, TQ), -32768, I16)
        lo_ref[pl.ds(k0, TK), :] = jnp.full((TK, TQ), -32768, I16)
        return carry

    lax.fori_loop(nkt, n_count * (cr // TK), pad_tile, 0)
    kf = float(topk)

    def count16(ref, cand):
        def body(kc, acc):
            k0 = pl.multiple_of(kc * cr, cr)
            hit = jnp.where(ref[pl.ds(k0, cr), :] >= cand, jnp.ones((cr, TQ), I16), jnp.zeros((cr, TQ), I16))
            return acc + _fold8(hit, rows=16)
        acc = lax.fori_loop(0, n_count, body, jnp.zeros((16, TQ), I16))
        return jnp.sum(acc.astype(F32), axis=0, keepdims=True)

    def search16(ref, cnt_all):
        c0 = count16(ref, jnp.zeros((1, TQ), I16))
        t = jnp.where(c0 >= kf, 0, -32768).astype(I32)
        cnt = jnp.where(c0 >= kf, c0, cnt_all)

        def step(it, carry):
            t, cnt = carry
            cand = t + (jnp.int32(1) << (14 - it))
            c = count16(ref, cand.astype(I16))
            ok = c >= kf
            return jnp.where(ok, cand, t), jnp.where(ok, c, cnt)

        return lax.fori_loop(0, 15, step, (t, cnt))

    def count(pred):
        def body(kt, acc):
            k0 = pl.multiple_of(kt * TK, TK)
            hit = jnp.where(pred(key_ref[pl.ds(k0, TK), :], k0), 1.0, 0.0)
            return acc + _fold8(hit)
        acc = lax.fori_loop(0, nkt, body, jnp.zeros((8, TQ), F32))
        return jnp.sum(acc, axis=0, keepdims=True)

    t_hi, cnt_hi = search16(hi_ref, (nkt * TK).astype(F32))
    t_hi16 = t_hi.astype(I16)

    def low_digit_tile(kc, carry):
        k0 = pl.multiple_of(kc * cr, cr)
        hi = hi_ref[pl.ds(k0, cr), :]
        in_bucket = jnp.where(hi == t_hi16, lo_ref[pl.ds(k0, cr), :], jnp.full((cr, TQ), -32768, I16))
        hi_ref[pl.ds(k0, cr), :] = jnp.where(hi > t_hi16, jnp.full((cr, TQ), 32767, I16), in_bucket)
        return carry

    lax.fori_loop(0, n_count, low_digit_tile, 0)
    t_lo, cnt = search16(hi_ref, cnt_hi)
    thr = t_hi * 65536 + (t_lo + 32768)
    floor_key = jnp.int32(NEG_KEY + 1)
    tied = jnp.max(jnp.where((cnt > kf) & (thr >= floor_key), 1.0, 0.0)) > 0.0

    @pl.when(jnp.logical_not(tied))
    def _():
        lim = jnp.maximum(thr, floor_key)

        def body(kt, carry):
            k0 = pl.multiple_of(kt * TK, TK)
            bias_ref[pl.ds(k0, TK), :] = jnp.where(key_ref[pl.ds(k0, TK), :] >= lim, 0.0, NEG_INF)
            return carry
        lax.fori_loop(0, nkt, body, 0)

    @pl.when(tied)
    def _():
        need = kf - count(lambda keys, k0: keys > thr)

        def idx(k0):
            return k0 + lax.broadcasted_iota(I32, (TK, 1), 0)

        def search_idx(it, lim):
            cand = lim + (jnp.int32(1) << (14 - it))
            c = count(lambda keys, k0: (keys == thr) & (idx(k0) < cand))
            return jnp.where(c <= need, cand, lim)

        lim = lax.fori_loop(0, 15, search_idx, jnp.zeros((1, TQ), I32))

        def body(kt, carry):
            k0 = pl.multiple_of(kt * TK, TK)
            keys = key_ref[pl.ds(k0, TK), :]
            sel = ((keys > thr) | ((keys == thr) & (idx(k0) < lim))) & (keys >= floor_key)
            bias_ref[pl.ds(k0, TK), :] = jnp.where(sel, 0.0, NEG_INF)
            return carry
        lax.fori_loop(0, nkt, body, 0)

    flash = _Flash(N_GH, *flash_refs)
    flash.init()

    def score(kt, is_first):
        k0 = pl.multiple_of(kt * TK, TK)
        kk = kv_ref[pl.ds(k0, TK), 0:LANES]
        bias = bias_ref[pl.ds(k0, TK), :]
        for h in range(N_GH):
            flash.scores(h, _dot_nt(kk, q_ref[:, h * LANES:(h + 1) * LANES]) + bias)

    def value(kt):
        k0 = pl.multiple_of(kt * TK, TK)
        for h in range(N_GH):
            flash.values(h, vt_ref[:, pl.ds(k0, TK)])

    flash.pipeline(nkt - 1, nkt - 1, score, value)
    y = jnp.concatenate([_head_norm_t(flash.result(h), g_ref[h * HEAD_DIM:(h + 1) * HEAD_DIM, :])
                         for h in range(N_GH)], axis=0)
    o_ref[...] = jnp.transpose(y).astype(BF16)


def _dsa_call(qa, wa, kva, vta, g_t, B, T):
    nq = T // TQ
    topk = min(TOPK_MAX, T // 4)
    return pl.pallas_call(
        functools.partial(_dsa_kernel, topk=topk),
        out_shape=jax.ShapeDtypeStruct((B * T, GROUP_W), BF16),
        grid=(B, nq),
        in_specs=[pl.BlockSpec((TQ, 1536), lambda b, i: (b * nq + i, 0)),
                  pl.BlockSpec((TQ, LANES), lambda b, i: (b * nq + i, 0)),
                  pl.BlockSpec((T, 256), lambda b, i: (b, 0)),
                  pl.BlockSpec((VT_ROWS, T), lambda b, i: (b, 0)),
                  pl.BlockSpec((GROUP_W, TQ), lambda b, i: (0, 0))],
        out_specs=pl.BlockSpec((TQ, GROUP_W), lambda b, i: (b * nq + i, 0)),
        scratch_shapes=[pltpu.VMEM((T, TQ), F32), pltpu.VMEM((T, TQ), I32), pltpu.VMEM((T, TQ), I16),
                        pltpu.VMEM((T, TQ), I16)] + _Flash.scratch(N_GH),
        compiler_params=_cparams("arbitrary", "arbitrary"),
        name="dsa",
    )(qa, wa, kva, vta, g_t)


def _diff_kernel(q_ref, k_ref, vt_ref, lam_ref, g_ref, o_ref, qm_ref, *flash_refs, lam_init):
    q0 = pl.program_id(1) * TQ
    nfull = (q0 + CHUNK) // TK
    qchunk = (q0 + _lane((1, TQ))) // CHUNK
    lv = lam_ref[...]
    lam = (jnp.exp(jnp.sum(lv[0:1] * lv[1:2], axis=1, keepdims=True))
           - jnp.exp(jnp.sum(lv[2:3] * lv[3:4], axis=1, keepdims=True)) + lam_init)
    n_maps = 2 * N_GH
    qt = jnp.transpose(q_ref[...].astype(F32))
    sub = lax.broadcasted_iota(I32, (LANES, 1), 0) // DIFF_DK
    for mi in range(n_maps):
        qg = qt[(mi // 4) * LANES:(mi // 4 + 1) * LANES, :]
        qm_ref[mi] = jnp.where(sub == mi % 4, qg, 0.0).astype(BF16)

    flash = _Flash(n_maps, *flash_refs)
    flash.init()

    def score(kt, masked):
        k0 = pl.multiple_of(kt * TK, TK)
        if masked:
            kchunk = (k0 + lax.broadcasted_iota(I32, (TK, 1), 0)) // CHUNK
            admissible = kchunk <= qchunk
        for mi in range(n_maps):
            g = mi // 4
            s = _dot(k_ref[pl.ds(k0, TK), g * LANES:(g + 1) * LANES], qm_ref[mi])
            flash.scores(mi, jnp.where(admissible, s, NEG_INF) if masked else s)

    def value(kt):
        k0 = pl.multiple_of(kt * TK, TK)
        for mi in range(n_maps):
            h = mi // 2
            flash.values(mi, vt_ref[h * VT_ROWS:(h + 1) * VT_ROWS, pl.ds(k0, TK)])

    flash.pipeline(nfull, nfull, score, value)
    outs = []
    for h in range(N_GH):
        o = flash.result(2 * h) - lam * flash.result(2 * h + 1)
        outs.append(_head_norm_t(o, g_ref[h * HEAD_DIM:(h + 1) * HEAD_DIM, :]) * (1.0 - lam_init))
    o_ref[...] = jnp.transpose(jnp.concatenate(outs, axis=0)).astype(BF16)


def _diff_call(dqkv, vtd, lam_vecs, g_t, lam_init, B, T):
    nq = T // TQ
    return pl.pallas_call(
        functools.partial(_diff_kernel, lam_init=lam_init),
        out_shape=jax.ShapeDtypeStruct((B * T, GROUP_W), BF16),
        grid=(B, nq),
        in_specs=[pl.BlockSpec((TQ, 256), lambda b, i: (b * nq + i, 0)),
                  pl.BlockSpec((T, 256), lambda b, i: (b, 1)),
                  pl.BlockSpec((N_GH * VT_ROWS, T), lambda b, i: (b, 0)),
                  pl.BlockSpec((4, LANES), lambda b, i: (0, 0)),
                  pl.BlockSpec((GROUP_W, TQ), lambda b, i: (0, 0))],
        out_specs=pl.BlockSpec((TQ, GROUP_W), lambda b, i: (b * nq + i, 0)),
        scratch_shapes=[pltpu.VMEM((2 * N_GH, LANES, TQ), BF16)] + _Flash.scratch(2 * N_GH),
        compiler_params=_cparams("arbitrary", "arbitrary"),
        name="diff",
    )(dqkv, dqkv, vtd, lam_vecs, g_t)


def _ret_tables():
    h = np.arange(N_GH, dtype=np.float32)
    log_g = jnp.log(1.0 - 2.0 ** (-5.0 - jnp.asarray(h)))
    t = np.arange(TB)
    same = (t[:, None] // CHUNK) == (t[None, :] // CHUNK)
    earlier = (t[None, :] // CHUNK) < (t[:, None] // CHUNK)
    dist = jnp.asarray(np.where(same, np.abs(t[:, None] - t[None, :]), t[:, None] - t[None, :]), F32)
    dmat = jnp.where(jnp.asarray(same | earlier)[None],
                     jnp.exp(dist[None] * log_g[:, None, None]), 0.0)
    tt = jnp.asarray(t, F32)
    xi = jnp.exp((tt + 1.0)[None, :] * log_g[:, None])
    zeta = jnp.exp((TB - 1.0 - tt)[None, :] * log_g[:, None])
    gblk = jnp.exp(TB * log_g)

    def lanes(tab):
        return jnp.repeat(tab.reshape(2, 2, TB), HEAD_DIM, axis=1).transpose(0, 2, 1)

    r = np.arange(LANES)
    blockdiag = (r[:, None] // HEAD_DIM) == (r[None, :] // HEAD_DIM)
    gb = jnp.where(jnp.asarray(blockdiag)[None],
                   jnp.repeat(gblk.reshape(2, 2), HEAD_DIM, axis=1)[:, :, None], 0.0)
    return dmat, lanes(xi), lanes(zeta), gb


def _ret_kernel(qkv_ref, gate_ref, dmat_ref, xi_ref, zeta_ref, gb_ref, g_ref, o_ref, r_ref):
    @pl.when(pl.program_id(1) == 0)
    def _():
        r_ref[...] = jnp.zeros_like(r_ref)

    lane = _lane((1, LANES))
    low = lane < HEAD_DIM
    r_idx = lax.broadcasted_iota(I32, (LANES, LANES), 0) // HEAD_DIM
    c_idx = lax.broadcasted_iota(I32, (LANES, LANES), 1) // HEAD_DIM
    for g in range(2):
        qg = qkv_ref[:, g * LANES:(g + 1) * LANES]
        kg = qkv_ref[:, 256 + g * LANES:256 + (g + 1) * LANES]
        vg = qkv_ref[:, 512 + g * LANES:512 + (g + 1) * LANES]
        state = r_ref[g]
        inter = _dot(qg, state.astype(BF16)) * xi_ref[g]
        parts = []
        for hh in range(2):
            qm = jnp.where(low, qg, jnp.zeros_like(qg)) if hh == 0 else jnp.where(low, jnp.zeros_like(qg), qg)
            s = _dot_nt(qm, kg) * dmat_ref[2 * g + hh]
            parts.append(_dot(s.astype(BF16), vg))
        o = jnp.where(low, parts[0], parts[1]) + inter
        kz = (kg.astype(F32) * zeta_ref[g]).astype(BF16)
        upd = jnp.where(r_idx == c_idx, _dot_tn(kz, vg), 0.0)
        r_ref[g] = state * gb_ref[g] + upd
        y = _pair_head_norm(o, g_ref[:, g * LANES:(g + 1) * LANES], True)
        o_ref[:, g * LANES:(g + 1) * LANES] = (_silu(gate_ref[:, g * LANES:(g + 1) * LANES]) * y).astype(BF16)


def _ret_call(qkvb, gates, tabs, g_b, B, T):
    nb = T // TB
    dmat, xi, zeta, gb = tabs
    const3 = lambda b, i: (0, 0, 0)
    return pl.pallas_call(
        _ret_kernel,
        out_shape=jax.ShapeDtypeStruct((B * T, GROUP_W), BF16),
        grid=(B, nb),
        in_specs=[pl.BlockSpec((TB, 768), lambda b, i: (b * nb + i, 0)),
                  pl.BlockSpec((TB, GROUP_W), lambda b, i: (b * nb + i, 0)),
                  pl.BlockSpec((N_GH, TB, TB), const3),
                  pl.BlockSpec((2, TB, LANES), const3),
                  pl.BlockSpec((2, TB, LANES), const3),
                  pl.BlockSpec((2, LANES, LANES), const3),
                  pl.BlockSpec((1, GROUP_W), lambda b, i: (0, 0))],
        out_specs=pl.BlockSpec((TB, GROUP_W), lambda b, i: (b * nb + i, 0)),
        scratch_shapes=[pltpu.VMEM((2, LANES, LANES), F32)],
        compiler_params=_cparams("arbitrary", "arbitrary"),
        name="ret",
    )(qkvb, gates, dmat, xi, zeta, gb, g_b)


def _gla_kernel(c_ref, gate_ref, wa2_ref, ba_ref, g_ref, o_ref, s_ref, b_scr, q_scr, k_scr, v_scr, acc_scr):
    @pl.when(pl.program_id(1) == 0)
    def _():
        s_ref[...] = jnp.zeros_like(s_ref)

    n_chunks = TB // CHUNK
    hmask = (lax.broadcasted_iota(I32, (LANES, GROUP_W), 0) // GLA_DK
             == lax.broadcasted_iota(I32, (LANES, GROUP_W), 1) // HEAD_DIM)
    hsum = jnp.where(hmask, 1.0, 0.0).astype(BF16)
    r_idx = lax.broadcasted_iota(I32, (TB, TB), 0)
    c_idx = lax.broadcasted_iota(I32, (TB, TB), 1)
    tri = jnp.where((r_idx >= c_idx) & (r_idx // CHUNK == c_idx // CHUNK), 1.0, 0.0).astype(BF16)

    z = _dot_split(c_ref[:, 512:640], wa2_ref[...]) + ba_ref[...]
    log_a = (jnp.minimum(z, 0.0) - jnp.log(1.0 + jnp.exp(-jnp.abs(z)))) * (1.0 / GLA_TAU)
    b = sum(_dot(tri, piece) for piece in _split_bf16(log_a, 3))
    q = c_ref[:, 0:128] * (GLA_DK ** -0.5)
    b_scr[...] = b * LOG2E
    q_scr[...] = q
    k_scr[...] = c_ref[:, 128:256]
    v_scr[...] = c_ref[:, 256:512]

    b_last = [b[(c + 1) * CHUNK - 1:(c + 1) * CHUNK, :] for c in range(n_chunks)]
    b_end = jnp.concatenate([jnp.broadcast_to(bl, (CHUNK, LANES)) for bl in b_last], axis=0)
    kd = (c_ref[:, 128:256] * jnp.exp(b_end - b)).astype(BF16)
    qe = (q * jnp.exp(b)).astype(BF16)
    vb = c_ref[:, 256:512].astype(BF16)
    state = s_ref[...]
    inters = []
    for c in range(n_chunks):
        rows = slice(c * CHUNK, (c + 1) * CHUNK)
        inters.append(_dot(qe[rows], state.astype(BF16)))
        scale = jnp.transpose(jnp.broadcast_to(jnp.exp(b_last[c]), (LANES, LANES)))
        state = jnp.where(hmask, state * jnp.concatenate([scale, scale], axis=1) + _dot_tn(kd[rows], vb[rows]), 0.0)
    s_ref[...] = state

    acc_scr[...] = jnp.concatenate(inters, axis=0)

    keys_per_step = 32

    def key_step(s4, carry):
        for c in range(n_chunks):
            rows = slice(c * CHUNK, (c + 1) * CHUNK)
            part = None
            for u in range(keys_per_step):
                r = c * CHUNK + s4 * keys_per_step + u
                gap = pltpu.bitcast(b_scr[rows, :] - b_scr[pl.ds(r, 1), :], I32)
                decay = jnp.exp2(pltpu.bitcast(gap | jnp.int32(INT_MIN), F32))
                w = (q_scr[rows, :] * decay * k_scr[pl.ds(r, 1), :]).astype(BF16)
                term = _dot(w, hsum) * v_scr[pl.ds(r, 1), :]
                part = term if part is None else part + term
            acc_scr[rows, :] += part
        return carry

    lax.fori_loop(0, CHUNK // keys_per_step, key_step, 0)
    o = acc_scr[...]
    for gi in range(2):
        y = _pair_head_norm(o[:, gi * LANES:(gi + 1) * LANES], g_ref[:, gi * LANES:(gi + 1) * LANES], True)
        o_ref[:, gi * LANES:(gi + 1) * LANES] = (_silu(gate_ref[:, gi * LANES:(gi + 1) * LANES]) * y).astype(BF16)


def _gla_call(cc, gates, wa2_p, ba, g_c, B, T):
    nb = T // TB
    return pl.pallas_call(
        _gla_kernel,
        out_shape=jax.ShapeDtypeStruct((B * T, GROUP_W), BF16),
        grid=(B, nb),
        in_specs=[pl.BlockSpec((TB, 640), lambda b, i: (b * nb + i, 0)),
                  pl.BlockSpec((TB, GROUP_W), lambda b, i: (b * nb + i, 1)),
                  pl.BlockSpec((LANES, LANES), lambda b, i: (0, 0)),
                  pl.BlockSpec((1, LANES), lambda b, i: (0, 0)),
                  pl.BlockSpec((1, GROUP_W), lambda b, i: (0, 0))],
        out_specs=pl.BlockSpec((TB, GROUP_W), lambda b, i: (b * nb + i, 0)),
        scratch_shapes=[pltpu.VMEM((LANES, GROUP_W), F32), pltpu.VMEM((TB, LANES), F32),
                        pltpu.VMEM((TB, LANES), F32), pltpu.VMEM((TB, LANES), F32),
                        pltpu.VMEM((TB, GROUP_W), F32), pltpu.VMEM((TB, GROUP_W), F32)],
        compiler_params=_cparams("arbitrary", "arbitrary"),
        name="gla",
    )(cc, gates, wa2_p, ba, g_c)


def _rms(y, g):
    return y * lax.rsqrt(jnp.mean(y * y, axis=-1, keepdims=True) + EPS) * g


def _out_mlp_kernel(x_ref, ya_ref, yb_ref, yc_ref, yd_ref, mod_ref, wo_ref, w1_ref, w2_ref,
                    gpost_ref, gpre_ref, gpost2_ref, o_ref):
    y = _dot(ya_ref[...], wo_ref[0:256, :])
    y = y + _dot(yb_ref[...], wo_ref[256:512, :])
    y = y + _dot(yc_ref[...], wo_ref[512:768, :])
    y = y + _dot(yd_ref[...], wo_ref[768:1024, :])
    x = x_ref[...] + mod_ref[0, 2:3, :] * _rms(y, gpost_ref[...])
    h = _rms(x, gpre_ref[...]) * (1.0 + mod_ref[0, 4:5, :]) + mod_ref[0, 3:4, :]
    hb = h.astype(BF16)
    acc = jnp.zeros(x.shape, F32)
    fc = 1024
    for f in range(D_FF // fc):
        u = jnp.maximum(_dot(hb, w1_ref[:, f * fc:(f + 1) * fc]), 0.0)
        acc = acc + _dot((u * u).astype(BF16), w2_ref[f * fc:(f + 1) * fc, :])
    o_ref[...] = x + mod_ref[0, 5:6, :] * _rms(acc, gpost2_ref[...])


def _out_mlp_call(x2, ys, mod3, w_out, w1, w2, g_post, g_pre2, g_post2, T):
    N, D = x2.shape
    nt = T // TM
    row = lambda i: (i, 0)
    const = lambda i: (0, 0)
    return pl.pallas_call(
        _out_mlp_kernel,
        out_shape=jax.ShapeDtypeStruct((N, D), F32),
        grid=(N // TM,),
        in_specs=[pl.BlockSpec((TM, D), row)] + [pl.BlockSpec((TM, GROUP_W), row)] * 4
                 + [pl.BlockSpec((1, 6, D), lambda i: (i // nt, 0, 0)),
                    _resident((D, D), const), _resident((D, D_FF), const), _resident((D_FF, D), const),
                    _resident((1, D), const), _resident((1, D), const), _resident((1, D), const)],
        out_specs=pl.BlockSpec((TM, D), row),
        compiler_params=_cparams("arbitrary"),
        name="out_mlp",
    )(x2, *ys, mod3, w_out, w1, w2, g_post, g_pre2, g_post2)


def _relayout_w_in(w_in):
    col = lambda k: w_in[:, _IN_OFF[k]:_IN_OFF[k + 1]]
    zeros = lambda n: jnp.zeros((w_in.shape[0], n), w_in.dtype)
    parts = [col(_A_Q), col(_A_QI), col(_A_K), col(_A_KI), col(_A_V), zeros(64),
             col(_A_WI), zeros(120),
             col(_B_Q), col(_B_K), col(_B_V),
             col(_B_G), col(_C_G),
             col(_C_Q), col(_C_K), col(_C_V), col(_C_A), zeros(112),
             col(_D_Q), col(_D_K), col(_D_V)]
    return jnp.concatenate(parts, axis=1).astype(BF16)


def _rope_tables(pos, rot, period, theta):
    half = rot // 2
    inv = theta ** (-jnp.arange(half, dtype=F32) / half)
    ang = pos.astype(F32)[..., None] * inv
    cos, sin = jnp.cos(ang), jnp.sin(ang)
    pad = ang.shape[:-1] + (period - rot,)
    ct = jnp.concatenate([cos, cos, jnp.ones(pad, F32)], axis=-1)
    st = jnp.concatenate([-sin, sin, jnp.zeros(pad, F32)], axis=-1)
    reps = LANES // period
    return (jnp.tile(ct, (1, 1, reps)).reshape(-1, LANES), jnp.tile(st, (1, 1, reps)).reshape(-1, LANES))


def _layer(x2, cond, tabs, ret_tabs, layer_idx, B, T, mod_w, mod_b, attn_pre_g, attn_post_g,
           mlp_pre_g, mlp_post_g, w_in, gla_wa2, gla_ba, lam_q1, lam_k1, lam_q2, lam_k2,
           head_norm_g, w_out, mlp_w1, mlp_w2):
    D = D_MODEL
    mod3 = _mod_call(cond, mod_w, mod_b).reshape(B, 6, D)
    qa, kva, wa, qkvb, gates, cc, dqkv, vta, vtd = _in_call(
        x2, mod3, attn_pre_g.reshape(1, D), _relayout_w_in(w_in), tabs, T)

    g_a, g_b, g_c, g_d = jnp.split(head_norm_g, 4)
    gain_t = lambda g: jnp.broadcast_to(g[:, None], (GROUP_W, TQ))
    y_a = _dsa_call(qa, wa, kva, vta, gain_t(g_a), B, T)
    y_b = _ret_call(qkvb, gates, ret_tabs, g_b.reshape(1, GROUP_W), B, T)
    wa2_p = jnp.zeros((LANES, LANES), F32).at[:GLA_RANK].set(gla_wa2)
    y_c = _gla_call(cc, gates, wa2_p, gla_ba.reshape(1, LANES), g_c.reshape(1, GROUP_W), B, T)
    lam_init = 0.8 - 0.6 * math.exp(-0.3 * layer_idx)
    lam_vecs = jnp.zeros((4, LANES), F32).at[:, :DIFF_DK].set(jnp.stack([lam_q1, lam_k1, lam_q2, lam_k2]))
    y_d = _diff_call(dqkv, vtd, lam_vecs, gain_t(g_d), lam_init, B, T)

    return _out_mlp_call(x2, (y_a, y_b, y_c, y_d), mod3, w_out.astype(BF16), mlp_w1.astype(BF16),
                         mlp_w2.astype(BF16), attn_post_g.reshape(1, D), mlp_pre_g.reshape(1, D),
                         mlp_post_g.reshape(1, D), T)


def kernel(x, c, positions, mod_w, mod_b, attn_pre_g, attn_post_g, mlp_pre_g, mlp_post_g, w_in,
           gla_wa2, gla_ba, lam_q1, lam_k1, lam_q2, lam_k2, head_norm_g, w_out, mlp_w1, mlp_w2):
    B, T, D = x.shape
    assert D == D_MODEL and T % TM == 0 and T % TB == 0
    tabs = (_rope_tables(positions, HEAD_DIM // ROPE_FRAC, HEAD_DIM, ROPE_THETA)
            + _rope_tables(positions, HEAD_DIM, HEAD_DIM, RET_THETA)
            + _rope_tables(positions, DIFF_DK // ROPE_FRAC, DIFF_DK, ROPE_THETA))
    ret_tabs = _ret_tables()
    x2 = x.reshape(B * T, D)
    for l in range(mod_w.shape[0]):
        x2 = _layer(x2, c, tabs, ret_tabs, l, B, T, mod_w[l], mod_b[l], attn_pre_g[l], attn_post_g[l],
                    mlp_pre_g[l], mlp_post_g[l], w_in[l], gla_wa2[l], gla_ba[l], lam_q1[l], lam_k1[l],
                    lam_q2[l], lam_k2[l], head_norm_g[l], w_out[l], mlp_w1[l], mlp_w2[l])
    return x2.reshape(B, T, D)
```

```python
import functools
import math

import numpy as np
import jax
import jax.numpy as jnp
from jax import lax
from jax.experimental import pallas as pl
from jax.experimental.pallas import tpu as pltpu

F32 = jnp.float32
BF16 = jnp.bfloat16
I32 = jnp.int32
I16 = jnp.int16

D_MODEL = 1024
CHUNK = 64
HEAD_DIM = 64
N_GH = 4
GROUP_W = N_GH * HEAD_DIM
IDX_HEADS = 8
IDX_DIM = 64
TOPK_MAX = 256
RET_THETA = 10000.0
GLA_DK = HEAD_DIM // 2
GLA_RANK = 16
GLA_TAU = 16.0
DIFF_DK = HEAD_DIM // 2
ROPE_THETA = 500000.0
ROPE_FRAC = 4
D_FF = 4 * D_MODEL
EPS = 1e-6
NEG_INF = -1e30

LANES = 128
VMEM_LIMIT = 56 * 1024 * 1024

_IN_SPLITS = (
    GROUP_W, HEAD_DIM, HEAD_DIM, IDX_HEADS * IDX_DIM, IDX_DIM, IDX_HEADS,
    GROUP_W, GROUP_W, GROUP_W, GROUP_W,
    N_GH * GLA_DK, N_GH * GLA_DK, GROUP_W, GLA_RANK, GROUP_W,
    2 * N_GH * DIFF_DK, 2 * N_GH * DIFF_DK, GROUP_W,
)
_IN_OFF = np.concatenate([[0], np.cumsum(_IN_SPLITS)]).tolist()
(_A_Q, _A_K, _A_V, _A_QI, _A_KI, _A_WI, _B_Q, _B_K, _B_V, _B_G,
 _C_Q, _C_K, _C_V, _C_A, _C_G, _D_Q, _D_K, _D_V) = range(18)

SEG_A = 0
SEG_W = 1024
SEG_B = 1152
SEG_G = 1920
SEG_C = 2432
SEG_D = 3072
W_COLS = 3840

NT_DIMS = (((1,), (1,)), ((), ()))
TN_DIMS = (((0,), (0,)), ((), ()))

TM = 512
TQ_DSA = 256
TQ_DIFF = 512
TK = 512
COUNT_ROWS = 512
FLASH_SLAB = 32
VT_ROWS = HEAD_DIM + 16
LOG2E = math.log2(math.e)
TB = 256


def _float_key(v):
    bits = int(np.array(v, np.float32).view(np.int32))
    return -(bits & 0x7FFFFFFF) if bits < 0 else bits


NEG_KEY = _float_key(NEG_INF)
INT_MIN = -(2 ** 31)


def _cparams(*sem):
    return pltpu.CompilerParams(dimension_semantics=sem, vmem_limit_bytes=VMEM_LIMIT)


def _resident(shape, index_map):
    return pl.BlockSpec(shape, index_map, pipeline_mode=pl.Buffered(1))


def _dot(a, b):
    return jnp.dot(a, b, preferred_element_type=F32)


def _dot_nt(a, b):
    return lax.dot_general(a, b, NT_DIMS, preferred_element_type=F32)


def _dot_tn(a, b):
    return lax.dot_general(a, b, TN_DIMS, preferred_element_type=F32)


def _dot_f32(a, b):
    return jnp.dot(a, b, preferred_element_type=F32, precision=lax.Precision.HIGHEST)


def _split_bf16(x, n):
    parts = []
    for _ in range(n):
        p = x.astype(BF16)
        parts.append(p)
        x = x - p.astype(F32)
    return parts


def _dot_split(a, b):
    (a0, a1), (b0, b1) = _split_bf16(a, 2), _split_bf16(b, 2)
    return _dot(a0, b0) + (_dot(a0, b1) + _dot(a1, b0))


def _lane(shape):
    return lax.broadcasted_iota(I32, shape, len(shape) - 1)


def _silu(x):
    return x / (1.0 + jnp.exp(-x))


def _mod_kernel(c_ref, w_ref, b_ref, o_ref):
    cond = _silu(c_ref[...])
    o_ref[...] = _dot_f32(cond, w_ref[...]) + b_ref[...]


def _mod_call(c, mod_w, mod_b):
    B, D = c.shape
    n = mod_w.shape[1] // D
    return pl.pallas_call(
        _mod_kernel,
        out_shape=jax.ShapeDtypeStruct((B, n * D), F32),
        grid=(n,),
        in_specs=[pl.BlockSpec((B, D), lambda j: (0, 0)),
                  pl.BlockSpec((D, D), lambda j: (0, j)),
                  pl.BlockSpec((1, D), lambda j: (0, j))],
        out_specs=pl.BlockSpec((B, D), lambda j: (0, j)),
        compiler_params=_cparams("arbitrary"),
        name="mod",
    )(c, mod_w, mod_b.reshape(1, -1))


def _rope_group(xg, cos, sin, half, period):
    lo = (_lane((1, LANES)) % period) < half
    swapped = jnp.where(lo, pltpu.roll(xg, LANES - half, 1), pltpu.roll(xg, half, 1))
    return xg * cos + swapped * sin


def _in_kernel(x_ref, mod_ref, g_ref, w_ref, ca_ref, sa_ref, cb_ref, sb_ref, cd_ref, sd_ref,
               qa_ref, kva_ref, wa_ref, qkvb_ref, gate_ref, c_ref, d_ref, vta_ref, vtd_ref):
    x = x_ref[...]
    h = x * lax.rsqrt(jnp.mean(x * x, axis=-1, keepdims=True) + EPS) * g_ref[...]
    h = h * (1.0 + mod_ref[0, 1:2, :]) + mod_ref[0, 0:1, :]
    hb = h.astype(BF16)

    def seg(off, width):
        return _dot(hb, w_ref[:, off:off + width])

    lane = _lane((1, LANES))
    low = lane < HEAD_DIM

    ca, sa = ca_ref[...], sa_ref[...]
    a = seg(SEG_A, 1024)
    half_a = HEAD_DIM // ROPE_FRAC // 2
    for gi in range(2):
        r = _rope_group(a[:, gi * LANES:(gi + 1) * LANES], ca, sa, half_a, HEAD_DIM) * (HEAD_DIM ** -0.5 * LOG2E)
        rs = pltpu.roll(r, HEAD_DIM, 1)
        qa_ref[:, (2 * gi) * LANES:(2 * gi + 1) * LANES] = jnp.where(low, r, 0.0).astype(BF16)
        qa_ref[:, (2 * gi + 1) * LANES:(2 * gi + 2) * LANES] = jnp.where(low, rs, 0.0).astype(BF16)
    for gi in range(4):
        r = _rope_group(a[:, (2 + gi) * LANES:(3 + gi) * LANES], ca, sa, half_a, HEAD_DIM)
        rs = pltpu.roll(r, HEAD_DIM, 1)
        qa_ref[:, (4 + 2 * gi) * LANES:(5 + 2 * gi) * LANES] = jnp.where(low, 0.0, rs).astype(BF16)
        qa_ref[:, (5 + 2 * gi) * LANES:(6 + 2 * gi) * LANES] = jnp.where(low, 0.0, r).astype(BF16)
    kva_ref[:, 0:LANES] = _rope_group(a[:, 6 * LANES:7 * LANES], ca, sa, half_a, HEAD_DIM).astype(BF16)
    kva_ref[:, LANES:2 * LANES] = a[:, 7 * LANES:8 * LANES].astype(BF16)
    ones_rows = jnp.ones((VT_ROWS - HEAD_DIM, TM), BF16)
    vta_ref[0:HEAD_DIM, :] = jnp.transpose(a[:, 7 * LANES:8 * LANES])[0:HEAD_DIM, :].astype(BF16)
    vta_ref[HEAD_DIM:VT_ROWS, :] = ones_rows

    wa_ref[...] = seg(SEG_W, LANES)

    cb, sb = cb_ref[...], sb_ref[...]
    b = seg(SEG_B, 768)
    for gi in range(4):
        r = _rope_group(b[:, gi * LANES:(gi + 1) * LANES], cb, sb, HEAD_DIM // 2, HEAD_DIM)
        if gi < 2:
            r = r * (HEAD_DIM ** -0.5)
        qkvb_ref[:, gi * LANES:(gi + 1) * LANES] = r.astype(BF16)
    qkvb_ref[:, 512:768] = b[:, 512:768].astype(BF16)

    gate_ref[...] = seg(SEG_G, 512)
    c_ref[...] = seg(SEG_C, 640)

    cd, sd = cd_ref[...], sd_ref[...]
    d = seg(SEG_D, 768)
    half_d = DIFF_DK // ROPE_FRAC // 2
    for gi in range(4):
        r = _rope_group(d[:, gi * LANES:(gi + 1) * LANES], cd, sd, half_d, DIFF_DK)
        if gi < 2:
            r = r * (DIFF_DK ** -0.5 * LOG2E)
        d_ref[:, gi * LANES:(gi + 1) * LANES] = r.astype(BF16)
    d_ref[:, 512:768] = d[:, 512:768].astype(BF16)
    vt = jnp.transpose(d[:, 512:768])
    for h in range(N_GH):
        vtd_ref[h * VT_ROWS:h * VT_ROWS + HEAD_DIM, :] = vt[h * HEAD_DIM:(h + 1) * HEAD_DIM, :].astype(BF16)
        vtd_ref[h * VT_ROWS + HEAD_DIM:(h + 1) * VT_ROWS, :] = ones_rows


def _in_call(x2, mod3, g, w_p, tabs, T):
    N, D = x2.shape
    nt = T // TM
    row = lambda i: (i, 0)
    tab_spec = pl.BlockSpec((TM, LANES), row)
    outs = [(1536, BF16), (256, BF16), (LANES, F32), (768, BF16), (512, F32), (640, F32), (768, BF16)]
    vt_rows = (VT_ROWS, N_GH * VT_ROWS)
    vt_shapes = [jax.ShapeDtypeStruct((N // T * r, T), BF16) for r in vt_rows]
    vt_specs = [pl.BlockSpec((r, TM), lambda i: (i // nt, i % nt)) for r in vt_rows]
    return pl.pallas_call(
        _in_kernel,
        out_shape=[jax.ShapeDtypeStruct((N, w), dt) for w, dt in outs] + vt_shapes,
        grid=(N // TM,),
        in_specs=[pl.BlockSpec((TM, D), row),
                  pl.BlockSpec((1, 6, D), lambda i: (i // nt, 0, 0)),
                  _resident((1, D), lambda i: (0, 0)),
                  _resident((D, W_COLS), lambda i: (0, 0))] + [tab_spec] * 6,
        out_specs=[pl.BlockSpec((TM, w), row) for w, _ in outs] + vt_specs,
        compiler_params=_cparams("arbitrary"),
        name="in_proj",
    )(x2, mod3, g, w_p, *tabs)


def _pair_head_norm(o, gain, center):
    low = _lane((1, LANES)) < HEAD_DIM
    inv = 1.0 / HEAD_DIM

    def seg_mean(v):
        m0 = jnp.sum(jnp.where(low, v, 0.0), axis=1, keepdims=True)
        m1 = jnp.sum(jnp.where(low, 0.0, v), axis=1, keepdims=True)
        return jnp.where(low, m0, m1) * inv

    if center:
        o = o - seg_mean(o)
    return o * lax.rsqrt(seg_mean(o * o) + EPS) * gain


class _Flash:
    def __init__(self, n, s_ref, p_ref, acc_ref, st_ref):
        self.n, self.s, self.p, self.acc, self.st = n, s_ref, p_ref, acc_ref, st_ref
        self.tq = s_ref.shape[-1]

    @staticmethod
    def scratch(n, tq):
        return [pltpu.VMEM((n, TK, tq), F32), pltpu.VMEM((n, TK, tq), BF16),
                pltpu.VMEM((n, VT_ROWS, tq), F32), pltpu.VMEM((3, n, tq), F32)]

    def _row(self, k, i):
        return self.st.at[k, i:i + 1, :]

    def init(self):
        self.st[0] = jnp.full((self.n, self.tq), NEG_INF, F32)
        self.acc[...] = jnp.zeros_like(self.acc)

    def scores(self, i, s):
        self.s[i] = s
        self._row(1, i)[...] = jnp.max(_fold8(s, jnp.maximum), axis=0, keepdims=True)

    def probs(self, i):
        m_old = self._row(0, i)[...]
        m_new = jnp.maximum(m_old, self._row(1, i)[...])
        for r in range(0, TK, FLASH_SLAB):
            self.p[i, r:r + FLASH_SLAB, :] = jnp.exp2(self.s[i, r:r + FLASH_SLAB, :] - m_new).astype(BF16)
        self._row(0, i)[...] = m_new
        self._row(2, i)[...] = jnp.exp2(m_old - m_new)

    def values(self, i, vt):
        self.acc[i] = self._row(2, i)[...] * self.acc[i] + _dot(vt, self.p[i])

    def result(self, i):
        return self.acc[i, 0:HEAD_DIM, :] / self.acc[i, HEAD_DIM:HEAD_DIM + 1, :]

    def pipeline(self, first, n_rest, score, value):
        score(first, True)

        def body(t, carry):
            for i in range(self.n):
                self.probs(i)
            score(t, False)
            value(jnp.where(t == 0, first, t - 1))
            return carry

        lax.fori_loop(0, n_rest, body, 0)
        for i in range(self.n):
            self.probs(i)
        value(jnp.where(n_rest == 0, first, n_rest - 1))


def _fold8(x, op=jnp.add, rows=8):
    parts = [x[r:r + rows] for r in range(0, x.shape[0], rows)]
    while len(parts) > 1:
        parts = [op(a, b) for a, b in zip(parts[0::2], parts[1::2])]
    return parts[0]


def _head_norm_t(o, gain):
    ms = jnp.mean(o * o, axis=0, keepdims=True)
    return o * lax.rsqrt(ms + EPS) * gain


def _dsa_kernel(q_ref, w_ref, kv_ref, vt_ref, g_ref, o_ref, bias_ref, key_ref, hi_ref, lo_ref, *flash_refs, topk):
    TQ = TQ_DSA
    q0 = pl.program_id(1) * TQ
    nkt = (q0 + TQ + TK - 1) // TK
    qchunk = (q0 + _lane((1, TQ))) // CHUNK
    wts = jnp.transpose(w_ref[...]) * (IDX_HEADS ** -0.5 * IDX_DIM ** -0.5)

    def score_tile(kt, carry):
        k0 = pl.multiple_of(kt * TK, TK)
        kk = kv_ref[pl.ds(k0, TK), 0:LANES]
        tile = bias_ref.at[pl.ds(k0, TK), :]
        for h in range(IDX_HEADS):
            term = jnp.maximum(_dot_nt(kk, q_ref[:, (4 + h) * LANES:(5 + h) * LANES]), 0.0) * wts[h:h + 1, :]
            if h == 0:
                tile[...] = term
            elif h < IDX_HEADS - 1:
                tile[...] += term
        kchunk = (k0 + lax.broadcasted_iota(I32, (TK, 1), 0)) // CHUNK
        acc = jnp.where(kchunk <= qchunk, tile[...] + term, NEG_INF)
        bits = pltpu.bitcast(acc, I32)
        sign = bits >> 31
        key = ((bits & 0x7FFFFFFF) ^ sign) - sign
        key_ref[pl.ds(k0, TK), :] = key
        hi_ref[pl.ds(k0, TK), :] = (key >> 16).astype(I16)
        lo_ref[pl.ds(k0, TK), :] = (key ^ 0x8000).astype(I16)
        return carry

    lax.fori_loop(0, nkt, score_tile, 0)

    cr = min(COUNT_ROWS, key_ref.shape[0])
    n_count = (nkt * TK + cr - 1) // cr

    def pad_tile(kt, carry):
        k0 = pl.multiple_of(kt * TK, TK)
        hi_ref[pl.ds(k0, TK), :] = jnp.full((TK, TQ), -32768, I16)
        lo_ref[pl.ds(k0, TK), :] = jnp.full((TK, TQ), -32768, I16)
        return carry

    lax.fori_loop(nkt, n_count * (cr // TK), pad_tile, 0)
    kf = float(topk)

    def count16(ref, cand):
        def body(kc, acc):
            k0 = pl.multiple_of(kc * cr, cr)
            hit = jnp.where(ref[pl.ds(k0, cr), :] >= cand, jnp.ones((cr, TQ), I16), jnp.zeros((cr, TQ), I16))
            return acc + _fold8(hit, rows=16)
        acc = lax.fori_loop(0, n_count, body, jnp.zeros((16, TQ), I16))
        return jnp.sum(acc.astype(F32), axis=0, keepdims=True)

    def search16(ref, cnt_all):
        c0 = count16(ref, jnp.zeros((1, TQ), I16))
        t = jnp.where(c0 >= kf, 0, -32768).astype(I32)
        cnt = jnp.where(c0 >= kf, c0, cnt_all)

        def step(it, carry):
            t, cnt = carry
            cand = t + (jnp.int32(1) << (14 - it))
            c = count16(ref, cand.astype(I16))
            ok = c >= kf
            return jnp.where(ok, cand, t), jnp.where(ok, c, cnt)

        return lax.fori_loop(0, 15, step, (t, cnt))

    def count(pred):
        def body(kt, acc):
            k0 = pl.multiple_of(kt * TK, TK)
            hit = jnp.where(pred(key_ref[pl.ds(k0, TK), :], k0), 1.0, 0.0)
            return acc + _fold8(hit)
        acc = lax.fori_loop(0, nkt, body, jnp.zeros((8, TQ), F32))
        return jnp.sum(acc, axis=0, keepdims=True)

    t_hi, cnt_hi = search16(hi_ref, (nkt * TK).astype(F32))
    t_hi16 = t_hi.astype(I16)

    def low_digit_tile(kc, carry):
        k0 = pl.multiple_of(kc * cr, cr)
        hi = hi_ref[pl.ds(k0, cr), :]
        in_bucket = jnp.where(hi == t_hi16, lo_ref[pl.ds(k0, cr), :], jnp.full((cr, TQ), -32768, I16))
        hi_ref[pl.ds(k0, cr), :] = jnp.where(hi > t_hi16, jnp.full((cr, TQ), 32767, I16), in_bucket)
        return carry

    lax.fori_loop(0, n_count, low_digit_tile, 0)
    t_lo, cnt = search16(hi_ref, cnt_hi)
    thr = t_hi * 65536 + (t_lo + 32768)
    floor_key = jnp.int32(NEG_KEY + 1)
    tied = jnp.max(jnp.where((cnt > kf) & (thr >= floor_key), 1.0, 0.0)) > 0.0

    @pl.when(jnp.logical_not(tied))
    def _():
        lim = jnp.maximum(thr, floor_key)

        def body(kt, carry):
            k0 = pl.multiple_of(kt * TK, TK)
            bias_ref[pl.ds(k0, TK), :] = jnp.where(key_ref[pl.ds(k0, TK), :] >= lim, 0.0, NEG_INF)
            return carry
        lax.fori_loop(0, nkt, body, 0)

    @pl.when(tied)
    def _():
        need = kf - count(lambda keys, k0: keys > thr)

        def idx(k0):
            return k0 + lax.broadcasted_iota(I32, (TK, 1), 0)

        def search_idx(it, lim):
            cand = lim + (jnp.int32(1) << (14 - it))
            c = count(lambda keys, k0: (keys == thr) & (idx(k0) < cand))
            return jnp.where(c <= need, cand, lim)

        lim = lax.fori_loop(0, 15, search_idx, jnp.zeros((1, TQ), I32))

        def body(kt, carry):
            k0 = pl.multiple_of(kt * TK, TK)
            keys = key_ref[pl.ds(k0, TK), :]
            sel = ((keys > thr) | ((keys == thr) & (idx(k0) < lim))) & (keys >= floor_key)
            bias_ref[pl.ds(k0, TK), :] = jnp.where(sel, 0.0, NEG_INF)
            return carry
        lax.fori_loop(0, nkt, body, 0)

    flash = _Flash(N_GH, *flash_refs)
    flash.init()

    def score(kt, is_first):
        k0 = pl.multiple_of(kt * TK, TK)
        kk = kv_ref[pl.ds(k0, TK), 0:LANES]
        bias = bias_ref[pl.ds(k0, TK), :]
        for h in range(N_GH):
            flash.scores(h, _dot_nt(kk, q_ref[:, h * LANES:(h + 1) * LANES]) + bias)

    def value(kt):
        k0 = pl.multiple_of(kt * TK, TK)
        for h in range(N_GH):
            flash.values(h, vt_ref[:, pl.ds(k0, TK)])

    flash.pipeline(nkt - 1, nkt - 1, score, value)
    y = jnp.concatenate([_head_norm_t(flash.result(h), g_ref[h * HEAD_DIM:(h + 1) * HEAD_DIM, :])
                         for h in range(N_GH)], axis=0)
    o_ref[...] = jnp.transpose(y).astype(BF16)


def _dsa_call(qa, wa, kva, vta, g_t, B, T):
    TQ = TQ_DSA
    nq = T // TQ
    topk = min(TOPK_MAX, T // 4)
    return pl.pallas_call(
        functools.partial(_dsa_kernel, topk=topk),
        out_shape=jax.ShapeDtypeStruct((B * T, GROUP_W), BF16),
        grid=(B, nq),
        in_specs=[pl.BlockSpec((TQ, 1536), lambda b, i: (b * nq + i, 0)),
                  pl.BlockSpec((TQ, LANES), lambda b, i: (b * nq + i, 0)),
                  pl.BlockSpec((T, 256), lambda b, i: (b, 0)),
                  pl.BlockSpec((VT_ROWS, T), lambda b, i: (b, 0)),
                  pl.BlockSpec((GROUP_W, TQ), lambda b, i: (0, 0))],
        out_specs=pl.BlockSpec((TQ, GROUP_W), lambda b, i: (b * nq + i, 0)),
        scratch_shapes=[pltpu.VMEM((T, TQ), F32), pltpu.VMEM((T, TQ), I32), pltpu.VMEM((T, TQ), I16),
                        pltpu.VMEM((T, TQ), I16)] + _Flash.scratch(N_GH, TQ),
        compiler_params=_cparams("arbitrary", "arbitrary"),
        name="dsa",
    )(qa, wa, kva, vta, g_t)


def _diff_kernel(q_ref, k_ref, vt_ref, lam_ref, g_ref, o_ref, qm_ref, *flash_refs, lam_init):
    TQ = TQ_DIFF
    q0 = pl.program_id(1) * TQ
    nfull = (q0 + CHUNK) // TK
    qchunk = (q0 + _lane((1, TQ))) // CHUNK
    lv = lam_ref[...]
    lam = (jnp.exp(jnp.sum(lv[0:1] * lv[1:2], axis=1, keepdims=True))
           - jnp.exp(jnp.sum(lv[2:3] * lv[3:4], axis=1, keepdims=True)) + lam_init)
    n_maps = 2 * N_GH
    qt = jnp.transpose(q_ref[...].astype(F32))
    sub = lax.broadcasted_iota(I32, (LANES, 1), 0) // DIFF_DK
    for mi in range(n_maps):
        qg = qt[(mi // 4) * LANES:(mi // 4 + 1) * LANES, :]
        qm_ref[mi] = jnp.where(sub == mi % 4, qg, 0.0).astype(BF16)

    flash = _Flash(n_maps, *flash_refs)
    flash.init()

    def score(kt, masked):
        k0 = pl.multiple_of(kt * TK, TK)
        if masked:
            kchunk = (k0 + lax.broadcasted_iota(I32, (TK, 1), 0)) // CHUNK
            admissible = kchunk <= qchunk
        for mi in range(n_maps):
            g = mi // 4
            s = _dot(k_ref[pl.ds(k0, TK), g * LANES:(g + 1) * LANES], qm_ref[mi])
            flash.scores(mi, jnp.where(admissible, s, NEG_INF) if masked else s)

    def value(kt):
        k0 = pl.multiple_of(kt * TK, TK)
        for mi in range(n_maps):
            h = mi // 2
            flash.values(mi, vt_ref[h * VT_ROWS:(h + 1) * VT_ROWS, pl.ds(k0, TK)])

    flash.pipeline(nfull, nfull, score, value)
    outs = []
    for h in range(N_GH):
        o = flash.result(2 * h) - lam * flash.result(2 * h + 1)
        outs.append(_head_norm_t(o, g_ref[h * HEAD_DIM:(h + 1) * HEAD_DIM, :]) * (1.0 - lam_init))
    o_ref[...] = jnp.transpose(jnp.concatenate(outs, axis=0)).astype(BF16)


def _diff_call(dqkv, vtd, lam_vecs, g_t, lam_init, B, T):
    TQ = TQ_DIFF
    nq = T // TQ
    return pl.pallas_call(
        functools.partial(_diff_kernel, lam_init=lam_init),
        out_shape=jax.ShapeDtypeStruct((B * T, GROUP_W), BF16),
        grid=(B, nq),
        in_specs=[pl.BlockSpec((TQ, 256), lambda b, i: (b * nq + i, 0)),
                  pl.BlockSpec((T, 256), lambda b, i: (b, 1)),
                  pl.BlockSpec((N_GH * VT_ROWS, T), lambda b, i: (b, 0)),
                  pl.BlockSpec((4, LANES), lambda b, i: (0, 0)),
                  pl.BlockSpec((GROUP_W, TQ), lambda b, i: (0, 0))],
        out_specs=pl.BlockSpec((TQ, GROUP_W), lambda b, i: (b * nq + i, 0)),
        scratch_shapes=[pltpu.VMEM((2 * N_GH, LANES, TQ), BF16)] + _Flash.scratch(2 * N_GH, TQ),
        compiler_params=_cparams("arbitrary", "arbitrary"),
        name="diff",
    )(dqkv, dqkv, vtd, lam_vecs, g_t)


def _ret_tables():
    h = np.arange(N_GH, dtype=np.float32)
    log_g = jnp.log(1.0 - 2.0 ** (-5.0 - jnp.asarray(h)))
    t = np.arange(TB)
    same = (t[:, None] // CHUNK) == (t[None, :] // CHUNK)
    earlier = (t[None, :] // CHUNK) < (t[:, None] // CHUNK)
    dist = jnp.asarray(np.where(same, np.abs(t[:, None] - t[None, :]), t[:, None] - t[None, :]), F32)
    dmat = jnp.where(jnp.asarray(same | earlier)[None],
                     jnp.exp(dist[None] * log_g[:, None, None]), 0.0)
    tt = jnp.asarray(t, F32)
    xi = jnp.exp((tt + 1.0)[None, :] * log_g[:, None])
    zeta = jnp.exp((TB - 1.0 - tt)[None, :] * log_g[:, None])
    gblk = jnp.exp(TB * log_g)

    def lanes(tab):
        return jnp.repeat(tab.reshape(2, 2, TB), HEAD_DIM, axis=1).transpose(0, 2, 1)

    r = np.arange(LANES)
    blockdiag = (r[:, None] // HEAD_DIM) == (r[None, :] // HEAD_DIM)
    gb = jnp.where(jnp.asarray(blockdiag)[None],
                   jnp.repeat(gblk.reshape(2, 2), HEAD_DIM, axis=1)[:, :, None], 0.0)
    return dmat, lanes(xi), lanes(zeta), gb


def _ret_kernel(qkv_ref, gate_ref, dmat_ref, xi_ref, zeta_ref, gb_ref, g_ref, o_ref, r_ref):
    @pl.when(pl.program_id(1) == 0)
    def _():
        r_ref[...] = jnp.zeros_like(r_ref)

    lane = _lane((1, LANES))
    low = lane < HEAD_DIM
    r_idx = lax.broadcasted_iota(I32, (LANES, LANES), 0) // HEAD_DIM
    c_idx = lax.broadcasted_iota(I32, (LANES, LANES), 1) // HEAD_DIM
    for g in range(2):
        qg = qkv_ref[:, g * LANES:(g + 1) * LANES]
        kg = qkv_ref[:, 256 + g * LANES:256 + (g + 1) * LANES]
        vg = qkv_ref[:, 512 + g * LANES:512 + (g + 1) * LANES]
        state = r_ref[g]
        inter = _dot(qg, state.astype(BF16)) * xi_ref[g]
        parts = []
        for hh in range(2):
            qm = jnp.where(low, qg, jnp.zeros_like(qg)) if hh == 0 else jnp.where(low, jnp.zeros_like(qg), qg)
            s = _dot_nt(qm, kg) * dmat_ref[2 * g + hh]
            parts.append(_dot(s.astype(BF16), vg))
        o = jnp.where(low, parts[0], parts[1]) + inter
        kz = (kg.astype(F32) * zeta_ref[g]).astype(BF16)
        upd = jnp.where(r_idx == c_idx, _dot_tn(kz, vg), 0.0)
        r_ref[g] = state * gb_ref[g] + upd
        y = _pair_head_norm(o, g_ref[:, g * LANES:(g + 1) * LANES], True)
        o_ref[:, g * LANES:(g + 1) * LANES] = (_silu(gate_ref[:, g * LANES:(g + 1) * LANES]) * y).astype(BF16)


def _ret_call(qkvb, gates, tabs, g_b, B, T):
    nb = T // TB
    dmat, xi, zeta, gb = tabs
    const3 = lambda b, i: (0, 0, 0)
    return pl.pallas_call(
        _ret_kernel,
        out_shape=jax.ShapeDtypeStruct((B * T, GROUP_W), BF16),
        grid=(B, nb),
        in_specs=[pl.BlockSpec((TB, 768), lambda b, i: (b * nb + i, 0)),
                  pl.BlockSpec((TB, GROUP_W), lambda b, i: (b * nb + i, 0)),
                  pl.BlockSpec((N_GH, TB, TB), const3),
                  pl.BlockSpec((2, TB, LANES), const3),
                  pl.BlockSpec((2, TB, LANES), const3),
                  pl.BlockSpec((2, LANES, LANES), const3),
                  pl.BlockSpec((1, GROUP_W), lambda b, i: (0, 0))],
        out_specs=pl.BlockSpec((TB, GROUP_W), lambda b, i: (b * nb + i, 0)),
        scratch_shapes=[pltpu.VMEM((2, LANES, LANES), F32)],
        compiler_params=_cparams("arbitrary", "arbitrary"),
        name="ret",
    )(qkvb, gates, dmat, xi, zeta, gb, g_b)


def _gla_kernel(c_ref, gate_ref, wa2_ref, ba_ref, g_ref, o_ref, s_ref, b_scr, q_scr, k_scr, v_scr, acc_scr):
    @pl.when(pl.program_id(1) == 0)
    def _():
        s_ref[...] = jnp.zeros_like(s_ref)

    n_chunks = TB // CHUNK
    hmask = (lax.broadcasted_iota(I32, (LANES, GROUP_W), 0) // GLA_DK
             == lax.broadcasted_iota(I32, (LANES, GROUP_W), 1) // HEAD_DIM)
    hsum = jnp.where(hmask, 1.0, 0.0).astype(BF16)
    r_idx = lax.broadcasted_iota(I32, (TB, TB), 0)
    c_idx = lax.broadcasted_iota(I32, (TB, TB), 1)
    tri = jnp.where((r_idx >= c_idx) & (r_idx // CHUNK == c_idx // CHUNK), 1.0, 0.0).astype(BF16)

    z = _dot_split(c_ref[:, 512:640], wa2_ref[...]) + ba_ref[...]
    log_a = (jnp.minimum(z, 0.0) - jnp.log(1.0 + jnp.exp(-jnp.abs(z)))) * (1.0 / GLA_TAU)
    b = sum(_dot(tri, piece) for piece in _split_bf16(log_a, 3))
    q = c_ref[:, 0:128] * (GLA_DK ** -0.5)
    b_scr[...] = b * LOG2E
    q_scr[...] = q
    k_scr[...] = c_ref[:, 128:256]
    v_scr[...] = c_ref[:, 256:512]

    b_last = [b[(c + 1) * CHUNK - 1:(c + 1) * CHUNK, :] for c in range(n_chunks)]
    b_end = jnp.concatenate([jnp.broadcast_to(bl, (CHUNK, LANES)) for bl in b_last], axis=0)
    kd = (c_ref[:, 128:256] * jnp.exp(b_end - b)).astype(BF16)
    qe = (q * jnp.exp(b)).astype(BF16)
    vb = c_ref[:, 256:512].astype(BF16)
    state = s_ref[...]
    inters = []
    for c in range(n_chunks):
        rows = slice(c * CHUNK, (c + 1) * CHUNK)
        inters.append(_dot(qe[rows], state.astype(BF16)))
        scale = jnp.transpose(jnp.broadcast_to(jnp.exp(b_last[c]), (LANES, LANES)))
        state = jnp.where(hmask, state * jnp.concatenate([scale, scale], axis=1) + _dot_tn(kd[rows], vb[rows]), 0.0)
    s_ref[...] = state

    acc_scr[...] = jnp.concatenate(inters, axis=0)

    keys_per_step = 32

    def key_step(s4, carry):
        for c in range(n_chunks):
            rows = slice(c * CHUNK, (c + 1) * CHUNK)
            part = None
            for u in range(keys_per_step):
                r = c * CHUNK + s4 * keys_per_step + u
                gap = pltpu.bitcast(b_scr[rows, :] - b_scr[pl.ds(r, 1), :], I32)
                decay = jnp.exp2(pltpu.bitcast(gap | jnp.int32(INT_MIN), F32))
                w = (q_scr[rows, :] * decay * k_scr[pl.ds(r, 1), :]).astype(BF16)
                term = _dot(w, hsum) * v_scr[pl.ds(r, 1), :]
                part = term if part is None else part + term
            acc_scr[rows, :] += part
        return carry

    lax.fori_loop(0, CHUNK // keys_per_step, key_step, 0)
    o = acc_scr[...]
    for gi in range(2):
        y = _pair_head_norm(o[:, gi * LANES:(gi + 1) * LANES], g_ref[:, gi * LANES:(gi + 1) * LANES], True)
        o_ref[:, gi * LANES:(gi + 1) * LANES] = (_silu(gate_ref[:, gi * LANES:(gi + 1) * LANES]) * y).astype(BF16)


def _gla_call(cc, gates, wa2_p, ba, g_c, B, T):
    nb = T // TB
    return pl.pallas_call(
        _gla_kernel,
        out_shape=jax.ShapeDtypeStruct((B * T, GROUP_W), BF16),
        grid=(B, nb),
        in_specs=[pl.BlockSpec((TB, 640), lambda b, i: (b * nb + i, 0)),
                  pl.BlockSpec((TB, GROUP_W), lambda b, i: (b * nb + i, 1)),
                  pl.BlockSpec((LANES, LANES), lambda b, i: (0, 0)),
                  pl.BlockSpec((1, LANES), lambda b, i: (0, 0)),
                  pl.BlockSpec((1, GROUP_W), lambda b, i: (0, 0))],
        out_specs=pl.BlockSpec((TB, GROUP_W), lambda b, i: (b * nb + i, 0)),
        scratch_shapes=[pltpu.VMEM((LANES, GROUP_W), F32), pltpu.VMEM((TB, LANES), F32),
                        pltpu.VMEM((TB, LANES), F32), pltpu.VMEM((TB, LANES), F32),
                        pltpu.VMEM((TB, GROUP_W), F32), pltpu.VMEM((TB, GROUP_W), F32)],
        compiler_params=_cparams("arbitrary", "arbitrary"),
        name="gla",
    )(cc, gates, wa2_p, ba, g_c)


def _rms(y, g):
    return y * lax.rsqrt(jnp.mean(y * y, axis=-1, keepdims=True) + EPS) * g


def _out_mlp_kernel(x_ref, ya_ref, yb_ref, yc_ref, yd_ref, mod_ref, wo_ref, w1_ref, w2_ref,
                    gpost_ref, gpre_ref, gpost2_ref, o_ref):
    y = _dot(ya_ref[...], wo_ref[0:256, :])
    y = y + _dot(yb_ref[...], wo_ref[256:512, :])
    y = y + _dot(yc_ref[...], wo_ref[512:768, :])
    y = y + _dot(yd_ref[...], wo_ref[768:1024, :])
    x = x_ref[...] + mod_ref[0, 2:3, :] * _rms(y, gpost_ref[...])
    h = _rms(x, gpre_ref[...]) * (1.0 + mod_ref[0, 4:5, :]) + mod_ref[0, 3:4, :]
    hb = h.astype(BF16)
    acc = jnp.zeros(x.shape, F32)
    fc = 1024
    for f in range(D_FF // fc):
        u = jnp.maximum(_dot(hb, w1_ref[:, f * fc:(f + 1) * fc]), 0.0)
        acc = acc + _dot((u * u).astype(BF16), w2_ref[f * fc:(f + 1) * fc, :])
    o_ref[...] = x + mod_ref[0, 5:6, :] * _rms(acc, gpost2_ref[...])


def _out_mlp_call(x2, ys, mod3, w_out, w1, w2, g_post, g_pre2, g_post2, T):
    N, D = x2.shape
    nt = T // TM
    row = lambda i: (i, 0)
    const = lambda i: (0, 0)
    return pl.pallas_call(
        _out_mlp_kernel,
        out_shape=jax.ShapeDtypeStruct((N, D), F32),
        grid=(N // TM,),
        in_specs=[pl.BlockSpec((TM, D), row)] + [pl.BlockSpec((TM, GROUP_W), row)] * 4
                 + [pl.BlockSpec((1, 6, D), lambda i: (i // nt, 0, 0)),
                    _resident((D, D), const), _resident((D, D_FF), const), _resident((D_FF, D), const),
                    _resident((1, D), const), _resident((1, D), const), _resident((1, D), const)],
        out_specs=pl.BlockSpec((TM, D), row),
        compiler_params=_cparams("arbitrary"),
        name="out_mlp",
    )(x2, *ys, mod3, w_out, w1, w2, g_post, g_pre2, g_post2)


def _relayout_w_in(w_in):
    col = lambda k: w_in[:, _IN_OFF[k]:_IN_OFF[k + 1]]
    zeros = lambda n: jnp.zeros((w_in.shape[0], n), w_in.dtype)
    parts = [col(_A_Q), col(_A_QI), col(_A_K), col(_A_KI), col(_A_V), zeros(64),
             col(_A_WI), zeros(120),
             col(_B_Q), col(_B_K), col(_B_V),
             col(_B_G), col(_C_G),
             col(_C_Q), col(_C_K), col(_C_V), col(_C_A), zeros(112),
             col(_D_Q), col(_D_K), col(_D_V)]
    return jnp.concatenate(parts, axis=1).astype(BF16)


def _rope_tables(pos, rot, period, theta):
    half = rot // 2
    inv = theta ** (-jnp.arange(half, dtype=F32) / half)
    ang = pos.astype(F32)[..., None] * inv
    cos, sin = jnp.cos(ang), jnp.sin(ang)
    pad = ang.shape[:-1] + (period - rot,)
    ct = jnp.concatenate([cos, cos, jnp.ones(pad, F32)], axis=-1)
    st = jnp.concatenate([-sin, sin, jnp.zeros(pad, F32)], axis=-1)
    reps = LANES // period
    return (jnp.tile(ct, (1, 1, reps)).reshape(-1, LANES), jnp.tile(st, (1, 1, reps)).reshape(-1, LANES))


def _layer(x2, cond, tabs, ret_tabs, layer_idx, B, T, mod_w, mod_b, attn_pre_g, attn_post_g,
           mlp_pre_g, mlp_post_g, w_in, gla_wa2, gla_ba, lam_q1, lam_k1, lam_q2, lam_k2,
           head_norm_g, w_out, mlp_w1, mlp_w2):
    D = D_MODEL
    mod3 = _mod_call(cond, mod_w, mod_b).reshape(B, 6, D)
    qa, kva, wa, qkvb, gates, cc, dqkv, vta, vtd = _in_call(
        x2, mod3, attn_pre_g.reshape(1, D), _relayout_w_in(w_in), tabs, T)

    g_a, g_b, g_c, g_d = jnp.split(head_norm_g, 4)
    gain_t = lambda g, tq: jnp.broadcast_to(g[:, None], (GROUP_W, tq))
    y_a = _dsa_call(qa, wa, kva, vta, gain_t(g_a, TQ_DSA), B, T)
    y_b = _ret_call(qkvb, gates, ret_tabs, g_b.reshape(1, GROUP_W), B, T)
    wa2_p = jnp.zeros((LANES, LANES), F32).at[:GLA_RANK].set(gla_wa2)
    y_c = _gla_call(cc, gates, wa2_p, gla_ba.reshape(1, LANES), g_c.reshape(1, GROUP_W), B, T)
    lam_init = 0.8 - 0.6 * math.exp(-0.3 * layer_idx)
    lam_vecs = jnp.zeros((4, LANES), F32).at[:, :DIFF_DK].set(jnp.stack([lam_q1, lam_k1, lam_q2, lam_k2]))
    y_d = _diff_call(dqkv, vtd, lam_vecs, gain_t(g_d, TQ_DIFF), lam_init, B, T)

    return _out_mlp_call(x2, (y_a, y_b, y_c, y_d), mod3, w_out.astype(BF16), mlp_w1.astype(BF16),
                         mlp_w2.astype(BF16), attn_post_g.reshape(1, D), mlp_pre_g.reshape(1, D),
                         mlp_post_g.reshape(1, D), T)


def kernel(x, c, positions, mod_w, mod_b, attn_pre_g, attn_post_g, mlp_pre_g, mlp_post_g, w_in,
           gla_wa2, gla_ba, lam_q1, lam_k1, lam_q2, lam_k2, head_norm_g, w_out, mlp_w1, mlp_w2):
    B, T, D = x.shape
    assert D == D_MODEL and T % TM == 0 and T % TB == 0
    tabs = (_rope_tables(positions, HEAD_DIM // ROPE_FRAC, HEAD_DIM, ROPE_THETA)
            + _rope_tables(positions, HEAD_DIM, HEAD_DIM, RET_THETA)
            + _rope_tables(positions, DIFF_DK // ROPE_FRAC, DIFF_DK, ROPE_THETA))
    ret_tabs = _ret_tables()
    x2 = x.reshape(B * T, D)
    for l in range(mod_w.shape[0]):
        x2 = _layer(x2, c, tabs, ret_tabs, l, B, T, mod_w[l], mod_b[l], attn_pre_g[l], attn_post_g[l],
                    mlp_pre_g[l], mlp_post_g[l], w_in[l], gla_wa2[l], gla_ba[l], lam_q1[l], lam_k1[l],
                    lam_q2[l], lam_k2[l], head_norm_g[l], w_out[l], mlp_w1[l], mlp_w2[l])
    return x2.reshape(B, T, D)
```

```python
import functools
import math

import numpy as np
import jax
import jax.numpy as jnp
from jax import lax
from jax.experimental import pallas as pl
from jax.experimental.pallas import tpu as pltpu

F32 = jnp.float32
BF16 = jnp.bfloat16
I32 = jnp.int32
I16 = jnp.int16

D_MODEL = 1024
CHUNK = 64
HEAD_DIM = 64
N_GH = 4
GROUP_W = N_GH * HEAD_DIM
IDX_HEADS = 8
IDX_DIM = 64
TOPK_MAX = 256
RET_THETA = 10000.0
GLA_DK = HEAD_DIM // 2
GLA_RANK = 16
GLA_TAU = 16.0
DIFF_DK = HEAD_DIM // 2
ROPE_THETA = 500000.0
ROPE_FRAC = 4
D_FF = 4 * D_MODEL
EPS = 1e-6
NEG_INF = -1e30

LANES = 128
VMEM_LIMIT = 56 * 1024 * 1024

_IN_SPLITS = (
    GROUP_W, HEAD_DIM, HEAD_DIM, IDX_HEADS * IDX_DIM, IDX_DIM, IDX_HEADS,
    GROUP_W, GROUP_W, GROUP_W, GROUP_W,
    N_GH * GLA_DK, N_GH * GLA_DK, GROUP_W, GLA_RANK, GROUP_W,
    2 * N_GH * DIFF_DK, 2 * N_GH * DIFF_DK, GROUP_W,
)
_IN_OFF = np.concatenate([[0], np.cumsum(_IN_SPLITS)]).tolist()
(_A_Q, _A_K, _A_V, _A_QI, _A_KI, _A_WI, _B_Q, _B_K, _B_V, _B_G,
 _C_Q, _C_K, _C_V, _C_A, _C_G, _D_Q, _D_K, _D_V) = range(18)

SEG_A = 0
SEG_W = 1024
SEG_B = 1152
SEG_G = 1920
SEG_C = 2432
SEG_D = 3072
W_COLS = 3840

NT_DIMS = (((1,), (1,)), ((), ()))
TN_DIMS = (((0,), (0,)), ((), ()))

TM = 512
TQ_DSA = 256
TQ_DIFF = 512
TK = 512
COUNT_ROWS = 512
FLASH_SLAB = 32
VT_ROWS = HEAD_DIM + 16
LOG2E = math.log2(math.e)
TB = 256


def _float_key(v):
    bits = int(np.array(v, np.float32).view(np.int32))
    return -(bits & 0x7FFFFFFF) if bits < 0 else bits


NEG_KEY = _float_key(NEG_INF)
INT_MIN = -(2 ** 31)


def _cparams(*sem):
    return pltpu.CompilerParams(dimension_semantics=sem, vmem_limit_bytes=VMEM_LIMIT)


def _resident(shape, index_map):
    return pl.BlockSpec(shape, index_map, pipeline_mode=pl.Buffered(1))


def _dot(a, b):
    return jnp.dot(a, b, preferred_element_type=F32)


def _dot_nt(a, b):
    return lax.dot_general(a, b, NT_DIMS, preferred_element_type=F32)


def _dot_tn(a, b):
    return lax.dot_general(a, b, TN_DIMS, preferred_element_type=F32)


def _dot_f32(a, b):
    return jnp.dot(a, b, preferred_element_type=F32, precision=lax.Precision.HIGHEST)


def _split_bf16(x, n):
    parts = []
    for _ in range(n):
        p = x.astype(BF16)
        parts.append(p)
        x = x - p.astype(F32)
    return parts


def _dot_split(a, b):
    (a0, a1), (b0, b1) = _split_bf16(a, 2), _split_bf16(b, 2)
    return _dot(a0, b0) + (_dot(a0, b1) + _dot(a1, b0))


def _lane(shape):
    return lax.broadcasted_iota(I32, shape, len(shape) - 1)


def _silu(x):
    return x / (1.0 + jnp.exp(-x))


def _mod_kernel(c_ref, w_ref, b_ref, o_ref):
    cond = _silu(c_ref[...])
    o_ref[...] = _dot_f32(cond, w_ref[...]) + b_ref[...]


def _mod_call(c, mod_w, mod_b):
    B, D = c.shape
    n = mod_w.shape[1] // D
    return pl.pallas_call(
        _mod_kernel,
        out_shape=jax.ShapeDtypeStruct((B, n * D), F32),
        grid=(n,),
        in_specs=[pl.BlockSpec((B, D), lambda j: (0, 0)),
                  pl.BlockSpec((D, D), lambda j: (0, j)),
                  pl.BlockSpec((1, D), lambda j: (0, j))],
        out_specs=pl.BlockSpec((B, D), lambda j: (0, j)),
        compiler_params=_cparams("arbitrary"),
        name="mod",
    )(c, mod_w, mod_b.reshape(1, -1))


def _rope_group(xg, cos, sin, half, period):
    lo = (_lane((1, LANES)) % period) < half
    swapped = jnp.where(lo, pltpu.roll(xg, LANES - half, 1), pltpu.roll(xg, half, 1))
    return xg * cos + swapped * sin


def _in_kernel(x_ref, mod_ref, g_ref, w_ref, ca_ref, sa_ref, cb_ref, sb_ref, cd_ref, sd_ref,
               qa_ref, kva_ref, wa_ref, qkvb_ref, gate_ref, c_ref, d_ref, vta_ref, vtd_ref):
    x = x_ref[...]
    h = x * lax.rsqrt(jnp.mean(x * x, axis=-1, keepdims=True) + EPS) * g_ref[...]
    h = h * (1.0 + mod_ref[0, 1:2, :]) + mod_ref[0, 0:1, :]
    hb = h.astype(BF16)

    def seg(off, width):
        return _dot(hb, w_ref[:, off:off + width])

    lane = _lane((1, LANES))
    low = lane < HEAD_DIM

    ca, sa = ca_ref[...], sa_ref[...]
    a = seg(SEG_A, 1024)
    half_a = HEAD_DIM // ROPE_FRAC // 2
    for gi in range(2):
        r = _rope_group(a[:, gi * LANES:(gi + 1) * LANES], ca, sa, half_a, HEAD_DIM) * (HEAD_DIM ** -0.5 * LOG2E)
        rs = pltpu.roll(r, HEAD_DIM, 1)
        qa_ref[:, (2 * gi) * LANES:(2 * gi + 1) * LANES] = jnp.where(low, r, 0.0).astype(BF16)
        qa_ref[:, (2 * gi + 1) * LANES:(2 * gi + 2) * LANES] = jnp.where(low, rs, 0.0).astype(BF16)
    for gi in range(4):
        r = _rope_group(a[:, (2 + gi) * LANES:(3 + gi) * LANES], ca, sa, half_a, HEAD_DIM)
        rs = pltpu.roll(r, HEAD_DIM, 1)
        qa_ref[:, (4 + 2 * gi) * LANES:(5 + 2 * gi) * LANES] = jnp.where(low, 0.0, rs).astype(BF16)
        qa_ref[:, (5 + 2 * gi) * LANES:(6 + 2 * gi) * LANES] = jnp.where(low, 0.0, r).astype(BF16)
    kva_ref[:, 0:LANES] = _rope_group(a[:, 6 * LANES:7 * LANES], ca, sa, half_a, HEAD_DIM).astype(BF16)
    kva_ref[:, LANES:2 * LANES] = a[:, 7 * LANES:8 * LANES].astype(BF16)
    ones_rows = jnp.ones((VT_ROWS - HEAD_DIM, TM), BF16)
    vta_ref[0:HEAD_DIM, :] = jnp.transpose(a[:, 7 * LANES:8 * LANES])[0:HEAD_DIM, :].astype(BF16)
    vta_ref[HEAD_DIM:VT_ROWS, :] = ones_rows

    wa_ref[...] = seg(SEG_W, LANES)

    cb, sb = cb_ref[...], sb_ref[...]
    b = seg(SEG_B, 768)
    for gi in range(4):
        r = _rope_group(b[:, gi * LANES:(gi + 1) * LANES], cb, sb, HEAD_DIM // 2, HEAD_DIM)
        if gi < 2:
            r = r * (HEAD_DIM ** -0.5)
        qkvb_ref[:, gi * LANES:(gi + 1) * LANES] = r.astype(BF16)
    qkvb_ref[:, 512:768] = b[:, 512:768].astype(BF16)

    gate_ref[...] = seg(SEG_G, 512)
    c_ref[...] = seg(SEG_C, 640)

    cd, sd = cd_ref[...], sd_ref[...]
    d = seg(SEG_D, 768)
    half_d = DIFF_DK // ROPE_FRAC // 2
    for gi in range(4):
        r = _rope_group(d[:, gi * LANES:(gi + 1) * LANES], cd, sd, half_d, DIFF_DK)
        if gi < 2:
            r = r * (DIFF_DK ** -0.5 * LOG2E)
        d_ref[:, gi * LANES:(gi + 1) * LANES] = r.astype(BF16)
    d_ref[:, 512:768] = d[:, 512:768].astype(BF16)
    vt = jnp.transpose(d[:, 512:768])
    for h in range(N_GH):
        vtd_ref[h * VT_ROWS:h * VT_ROWS + HEAD_DIM, :] = vt[h * HEAD_DIM:(h + 1) * HEAD_DIM, :].astype(BF16)
        vtd_ref[h * VT_ROWS + HEAD_DIM:(h + 1) * VT_ROWS, :] = ones_rows


def _in_call(x2, mod3, g, w_p, tabs, T):
    N, D = x2.shape
    nt = T // TM
    row = lambda i: (i, 0)
    tab_spec = pl.BlockSpec((TM, LANES), row)
    outs = [(1536, BF16), (256, BF16), (LANES, F32), (768, BF16), (512, F32), (640, F32), (768, BF16)]
    vt_rows = (VT_ROWS, N_GH * VT_ROWS)
    vt_shapes = [jax.ShapeDtypeStruct((N // T * r, T), BF16) for r in vt_rows]
    vt_specs = [pl.BlockSpec((r, TM), lambda i: (i // nt, i % nt)) for r in vt_rows]
    return pl.pallas_call(
        _in_kernel,
        out_shape=[jax.ShapeDtypeStruct((N, w), dt) for w, dt in outs] + vt_shapes,
        grid=(N // TM,),
        in_specs=[pl.BlockSpec((TM, D), row),
                  pl.BlockSpec((1, 6, D), lambda i: (i // nt, 0, 0)),
                  _resident((1, D), lambda i: (0, 0)),
                  _resident((D, W_COLS), lambda i: (0, 0))] + [tab_spec] * 6,
        out_specs=[pl.BlockSpec((TM, w), row) for w, _ in outs] + vt_specs,
        compiler_params=_cparams("arbitrary"),
        name="in_proj",
    )(x2, mod3, g, w_p, *tabs)


def _pair_head_norm(o, gain, center):
    low = _lane((1, LANES)) < HEAD_DIM
    inv = 1.0 / HEAD_DIM

    def seg_mean(v):
        m0 = jnp.sum(jnp.where(low, v, 0.0), axis=1, keepdims=True)
        m1 = jnp.sum(jnp.where(low, 0.0, v), axis=1, keepdims=True)
        return jnp.where(low, m0, m1) * inv

    if center:
        o = o - seg_mean(o)
    return o * lax.rsqrt(seg_mean(o * o) + EPS) * gain


class _Flash:
    def __init__(self, n, s_ref, p_ref, acc_ref, st_ref):
        self.n, self.s, self.p, self.acc, self.st = n, s_ref, p_ref, acc_ref, st_ref
        self.tq = s_ref.shape[-1]

    @staticmethod
    def scratch(n, tq):
        return [pltpu.VMEM((n, TK, tq), F32), pltpu.VMEM((n, TK, tq), BF16),
                pltpu.VMEM((n, VT_ROWS, tq), F32), pltpu.VMEM((3, n, tq), F32)]

    def _row(self, k, i):
        return self.st.at[k, i:i + 1, :]

    def init(self):
        self.st[0] = jnp.full((self.n, self.tq), NEG_INF, F32)
        self.acc[...] = jnp.zeros_like(self.acc)

    def scores(self, i, s):
        self.s[i] = s
        self._row(1, i)[...] = jnp.max(_fold8(s, jnp.maximum), axis=0, keepdims=True)

    def probs(self, i):
        m_old = self._row(0, i)[...]
        m_new = jnp.maximum(m_old, self._row(1, i)[...])
        for r in range(0, TK, FLASH_SLAB):
            self.p[i, r:r + FLASH_SLAB, :] = jnp.exp2(self.s[i, r:r + FLASH_SLAB, :] - m_new).astype(BF16)
        self._row(0, i)[...] = m_new
        self._row(2, i)[...] = jnp.exp2(m_old - m_new)

    def values(self, i, vt):
        self.acc[i] = self._row(2, i)[...] * self.acc[i] + _dot(vt, self.p[i])

    def result(self, i):
        return self.acc[i, 0:HEAD_DIM, :] / self.acc[i, HEAD_DIM:HEAD_DIM + 1, :]

    def pipeline(self, first, n_rest, score, value):
        score(first, True)

        def body(t, carry):
            for i in range(self.n):
                self.probs(i)
            score(t, False)
            value(jnp.where(t == 0, first, t - 1))
            return carry

        lax.fori_loop(0, n_rest, body, 0)
        for i in range(self.n):
            self.probs(i)
        value(jnp.where(n_rest == 0, first, n_rest - 1))


def _fold8(x, op=jnp.add, rows=8):
    parts = [x[r:r + rows] for r in range(0, x.shape[0], rows)]
    while len(parts) > 1:
        parts = [op(a, b) for a, b in zip(parts[0::2], parts[1::2])]
    return parts[0]


def _head_norm_t(o, gain):
    ms = jnp.mean(o * o, axis=0, keepdims=True)
    return o * lax.rsqrt(ms + EPS) * gain


def _dsa_kernel(q_ref, w_ref, kv_ref, vt_ref, g_ref, ridx_ref, o_ref, bias_ref, key_ref, hi_ref, lo_ref, u_ref,
                *flash_refs, topk):
    TQ = TQ_DSA
    q0 = pl.program_id(1) * TQ
    nkt = (q0 + TQ + TK - 1) // TK
    qchunk = (q0 + _lane((1, TQ))) // CHUNK
    wts = jnp.transpose(w_ref[...]) * (IDX_HEADS ** -0.5 * IDX_DIM ** -0.5)

    def score_tile(kt, carry):
        k0 = pl.multiple_of(kt * TK, TK)
        kk = kv_ref[pl.ds(k0, TK), 0:LANES]
        tile = bias_ref.at[pl.ds(k0, TK), :]
        for h in range(IDX_HEADS):
            term = jnp.maximum(_dot_nt(kk, q_ref[:, (4 + h) * LANES:(5 + h) * LANES]), 0.0) * wts[h:h + 1, :]
            if h == 0:
                tile[...] = term
            elif h < IDX_HEADS - 1:
                tile[...] += term
        kchunk = (k0 + lax.broadcasted_iota(I32, (TK, 1), 0)) // CHUNK
        acc = jnp.where(kchunk <= qchunk, tile[...] + term, NEG_INF)
        bits = pltpu.bitcast(acc, I32)
        sign = bits >> 31
        key = ((bits & 0x7FFFFFFF) ^ sign) - sign
        key_ref[pl.ds(k0, TK), :] = key
        hi_ref[pl.ds(k0, TK), :] = (key >> 16).astype(I16)
        lo_ref[pl.ds(k0, TK), :] = (key ^ 0x8000).astype(I16)
        return carry

    lax.fori_loop(0, nkt, score_tile, 0)

    cr = min(COUNT_ROWS, key_ref.shape[0])
    n_count = (nkt * TK + cr - 1) // cr

    def pad_tile(kt, carry):
        k0 = pl.multiple_of(kt * TK, TK)
        hi_ref[pl.ds(k0, TK), :] = jnp.full((TK, TQ), -32768, I16)
        lo_ref[pl.ds(k0, TK), :] = jnp.full((TK, TQ), -32768, I16)
        return carry

    lax.fori_loop(nkt, n_count * (cr // TK), pad_tile, 0)
    kf = float(topk)

    def count16(ref, cand):
        def body(kc, acc):
            k0 = pl.multiple_of(kc * cr, cr)
            hit = jnp.where(ref[pl.ds(k0, cr), :] >= cand, jnp.ones((cr, TQ), I16), jnp.zeros((cr, TQ), I16))
            return acc + _fold8(hit, rows=16)
        acc = lax.fori_loop(0, n_count, body, jnp.zeros((16, TQ), I16))
        return jnp.sum(acc.astype(F32), axis=0, keepdims=True)

    def search16(ref, cnt_all, signed=True, bits=15):
        if signed:
            c0 = count16(ref, jnp.zeros((1, TQ), I16))
            t = jnp.where(c0 >= kf, 0, -32768).astype(I32)
            cnt = jnp.where(c0 >= kf, c0, cnt_all)
        else:
            t, cnt = jnp.zeros((1, TQ), I32), cnt_all

        def step(it, carry):
            t, cnt = carry
            cand = t + (jnp.int32(1) << (bits - 1 - it))
            c = count16(ref, cand.astype(I16))
            ok = c >= kf
            return jnp.where(ok, cand, t), jnp.where(ok, c, cnt)

        return lax.fori_loop(0, bits, step, (t, cnt))

    t_hi, cnt_hi = search16(hi_ref, (nkt * TK).astype(F32))
    t_hi16 = t_hi.astype(I16)

    top = jnp.full((cr, TQ), 32767, I16)
    bottom = jnp.full((cr, TQ), -32768, I16)

    def low_digit_tile(kc, carry):
        k0 = pl.multiple_of(kc * cr, cr)
        hi = hi_ref[pl.ds(k0, cr), :]
        in_bucket = jnp.where(hi == t_hi16, lo_ref[pl.ds(k0, cr), :], bottom)
        u_ref[pl.ds(k0, cr), :] = jnp.where(hi > t_hi16, top, in_bucket)
        return carry

    lax.fori_loop(0, n_count, low_digit_tile, 0)
    t_lo, cnt_lo = search16(u_ref, cnt_hi)
    t_lo16 = t_lo.astype(I16)
    thr = t_hi * 65536 + (t_lo + 32768)

    def index_digit_tile(kc, carry):
        k0 = pl.multiple_of(kc * cr, cr)
        hi, lo = hi_ref[pl.ds(k0, cr), :], lo_ref[pl.ds(k0, cr), :]
        same_hi = jnp.where(lo == t_lo16, ridx_ref[pl.ds(k0, cr), :], jnp.where(lo > t_lo16, top, bottom))
        u_ref[pl.ds(k0, cr), :] = jnp.where(hi == t_hi16, same_hi, jnp.where(hi > t_hi16, top, bottom))
        return carry

    lax.fori_loop(0, n_count, index_digit_tile, 0)
    n_keys = key_ref.shape[0]
    t_idx, _ = search16(u_ref, cnt_lo, signed=False, bits=(n_keys - 1).bit_length())

    floor_key = jnp.int32(NEG_KEY + 1)
    short = thr < floor_key
    lim_tie = jnp.where(short, floor_key, thr)
    lim_late = jnp.where(short, floor_key, thr + 1)
    last_tie = jnp.where(short, -1, (n_keys - 1) - t_idx)

    def select_tile(kt, carry):
        k0 = pl.multiple_of(kt * TK, TK)
        idx = k0 + lax.broadcasted_iota(I32, (TK, 1), 0)
        lim = jnp.where(idx <= last_tie, lim_tie, lim_late)
        bias_ref[pl.ds(k0, TK), :] = jnp.where(key_ref[pl.ds(k0, TK), :] >= lim, 0.0, NEG_INF)
        return carry

    lax.fori_loop(0, nkt, select_tile, 0)

    flash = _Flash(N_GH, *flash_refs)
    flash.init()

    def score(kt, is_first):
        k0 = pl.multiple_of(kt * TK, TK)
        kk = kv_ref[pl.ds(k0, TK), 0:LANES]
        bias = bias_ref[pl.ds(k0, TK), :]
        for h in range(N_GH):
            flash.scores(h, _dot_nt(kk, q_ref[:, h * LANES:(h + 1) * LANES]) + bias)

    def value(kt):
        k0 = pl.multiple_of(kt * TK, TK)
        for h in range(N_GH):
            flash.values(h, vt_ref[:, pl.ds(k0, TK)])

    flash.pipeline(nkt - 1, nkt - 1, score, value)
    y = jnp.concatenate([_head_norm_t(flash.result(h), g_ref[h * HEAD_DIM:(h + 1) * HEAD_DIM, :])
                         for h in range(N_GH)], axis=0)
    o_ref[...] = jnp.transpose(y).astype(BF16)


def _dsa_call(qa, wa, kva, vta, g_t, B, T):
    TQ = TQ_DSA
    nq = T // TQ
    ridx = jnp.broadcast_to((T - 1 - jnp.arange(T, dtype=I32)).astype(I16)[:, None], (T, TQ))
    topk = min(TOPK_MAX, T // 4)
    return pl.pallas_call(
        functools.partial(_dsa_kernel, topk=topk),
        out_shape=jax.ShapeDtypeStruct((B * T, GROUP_W), BF16),
        grid=(B, nq),
        in_specs=[pl.BlockSpec((TQ, 1536), lambda b, i: (b * nq + i, 0)),
                  pl.BlockSpec((TQ, LANES), lambda b, i: (b * nq + i, 0)),
                  pl.BlockSpec((T, 256), lambda b, i: (b, 0)),
                  pl.BlockSpec((VT_ROWS, T), lambda b, i: (b, 0)),
                  pl.BlockSpec((GROUP_W, TQ), lambda b, i: (0, 0)),
                  _resident((T, TQ), lambda b, i: (0, 0))],
        out_specs=pl.BlockSpec((TQ, GROUP_W), lambda b, i: (b * nq + i, 0)),
        scratch_shapes=[pltpu.VMEM((T, TQ), F32), pltpu.VMEM((T, TQ), I32), pltpu.VMEM((T, TQ), I16),
                        pltpu.VMEM((T, TQ), I16), pltpu.VMEM((T, TQ), I16)] + _Flash.scratch(N_GH, TQ),
        compiler_params=_cparams("arbitrary", "arbitrary"),
        name="dsa",
    )(qa, wa, kva, vta, g_t, ridx)


def _diff_kernel(q_ref, k_ref, vt_ref, lam_ref, g_ref, o_ref, qm_ref, *flash_refs, lam_init):
    TQ = TQ_DIFF
    q0 = pl.program_id(1) * TQ
    nfull = (q0 + CHUNK) // TK
    qchunk = (q0 + _lane((1, TQ))) // CHUNK
    lv = lam_ref[...]
    lam = (jnp.exp(jnp.sum(lv[0:1] * lv[1:2], axis=1, keepdims=True))
           - jnp.exp(jnp.sum(lv[2:3] * lv[3:4], axis=1, keepdims=True)) + lam_init)
    n_maps = 2 * N_GH
    qt = jnp.transpose(q_ref[...].astype(F32))
    sub = lax.broadcasted_iota(I32, (LANES, 1), 0) // DIFF_DK
    for mi in range(n_maps):
        qg = qt[(mi // 4) * LANES:(mi // 4 + 1) * LANES, :]
        qm_ref[mi] = jnp.where(sub == mi % 4, qg, 0.0).astype(BF16)

    flash = _Flash(n_maps, *flash_refs)
    flash.init()

    def score(kt, masked):
        k0 = pl.multiple_of(kt * TK, TK)
        if masked:
            kchunk = (k0 + lax.broadcasted_iota(I32, (TK, 1), 0)) // CHUNK
            admissible = kchunk <= qchunk
        for mi in range(n_maps):
            g = mi // 4
            s = _dot(k_ref[pl.ds(k0, TK), g * LANES:(g + 1) * LANES], qm_ref[mi])
            flash.scores(mi, jnp.where(admissible, s, NEG_INF) if masked else s)

    def value(kt):
        k0 = pl.multiple_of(kt * TK, TK)
        for mi in range(n_maps):
            h = mi // 2
            flash.values(mi, vt_ref[h * VT_ROWS:(h + 1) * VT_ROWS, pl.ds(k0, TK)])

    flash.pipeline(nfull, nfull, score, value)
    outs = []
    for h in range(N_GH):
        o = flash.result(2 * h) - lam * flash.result(2 * h + 1)
        outs.append(_head_norm_t(o, g_ref[h * HEAD_DIM:(h + 1) * HEAD_DIM, :]) * (1.0 - lam_init))
    o_ref[...] = jnp.transpose(jnp.concatenate(outs, axis=0)).astype(BF16)


def _diff_call(dqkv, vtd, lam_vecs, g_t, lam_init, B, T):
    TQ = TQ_DIFF
    nq = T // TQ
    return pl.pallas_call(
        functools.partial(_diff_kernel, lam_init=lam_init),
        out_shape=jax.ShapeDtypeStruct((B * T, GROUP_W), BF16),
        grid=(B, nq),
        in_specs=[pl.BlockSpec((TQ, 256), lambda b, i: (b * nq + i, 0)),
                  pl.BlockSpec((T, 256), lambda b, i: (b, 1)),
                  pl.BlockSpec((N_GH * VT_ROWS, T), lambda b, i: (b, 0)),
                  pl.BlockSpec((4, LANES), lambda b, i: (0, 0)),
                  pl.BlockSpec((GROUP_W, TQ), lambda b, i: (0, 0))],
        out_specs=pl.BlockSpec((TQ, GROUP_W), lambda b, i: (b * nq + i, 0)),
        scratch_shapes=[pltpu.VMEM((2 * N_GH, LANES, TQ), BF16)] + _Flash.scratch(2 * N_GH, TQ),
        compiler_params=_cparams("arbitrary", "arbitrary"),
        name="diff",
    )(dqkv, dqkv, vtd, lam_vecs, g_t)


def _ret_tables():
    h = np.arange(N_GH, dtype=np.float32)
    log_g = jnp.log(1.0 - 2.0 ** (-5.0 - jnp.asarray(h)))
    t = np.arange(TB)
    same = (t[:, None] // CHUNK) == (t[None, :] // CHUNK)
    earlier = (t[None, :] // CHUNK) < (t[:, None] // CHUNK)
    dist = jnp.asarray(np.where(same, np.abs(t[:, None] - t[None, :]), t[:, None] - t[None, :]), F32)
    dmat = jnp.where(jnp.asarray(same | earlier)[None],
                     jnp.exp(dist[None] * log_g[:, None, None]), 0.0)
    tt = jnp.asarray(t, F32)
    xi = jnp.exp((tt + 1.0)[None, :] * log_g[:, None])
    zeta = jnp.exp((TB - 1.0 - tt)[None, :] * log_g[:, None])
    gblk = jnp.exp(TB * log_g)

    def lanes(tab):
        return jnp.repeat(tab.reshape(2, 2, TB), HEAD_DIM, axis=1).transpose(0, 2, 1)

    r = np.arange(LANES)
    blockdiag = (r[:, None] // HEAD_DIM) == (r[None, :] // HEAD_DIM)
    gb = jnp.where(jnp.asarray(blockdiag)[None],
                   jnp.repeat(gblk.reshape(2, 2), HEAD_DIM, axis=1)[:, :, None], 0.0)
    return dmat, lanes(xi), lanes(zeta), gb


def _ret_kernel(qkv_ref, gate_ref, dmat_ref, xi_ref, zeta_ref, gb_ref, g_ref, o_ref, r_ref):
    @pl.when(pl.program_id(1) == 0)
    def _():
        r_ref[...] = jnp.zeros_like(r_ref)

    lane = _lane((1, LANES))
    low = lane < HEAD_DIM
    r_idx = lax.broadcasted_iota(I32, (LANES, LANES), 0) // HEAD_DIM
    c_idx = lax.broadcasted_iota(I32, (LANES, LANES), 1) // HEAD_DIM
    for g in range(2):
        qg = qkv_ref[:, g * LANES:(g + 1) * LANES]
        kg = qkv_ref[:, 256 + g * LANES:256 + (g + 1) * LANES]
        vg = qkv_ref[:, 512 + g * LANES:512 + (g + 1) * LANES]
        state = r_ref[g]
        inter = _dot(qg, state.astype(BF16)) * xi_ref[g]
        parts = []
        for hh in range(2):
            qm = jnp.where(low, qg, jnp.zeros_like(qg)) if hh == 0 else jnp.where(low, jnp.zeros_like(qg), qg)
            s = _dot_nt(qm, kg) * dmat_ref[2 * g + hh]
            parts.append(_dot(s.astype(BF16), vg))
        o = jnp.where(low, parts[0], parts[1]) + inter
        kz = (kg.astype(F32) * zeta_ref[g]).astype(BF16)
        upd = jnp.where(r_idx == c_idx, _dot_tn(kz, vg), 0.0)
        r_ref[g] = state * gb_ref[g] + upd
        y = _pair_head_norm(o, g_ref[:, g * LANES:(g + 1) * LANES], True)
        o_ref[:, g * LANES:(g + 1) * LANES] = (_silu(gate_ref[:, g * LANES:(g + 1) * LANES]) * y).astype(BF16)


def _ret_call(qkvb, gates, tabs, g_b, B, T):
    nb = T // TB
    dmat, xi, zeta, gb = tabs
    const3 = lambda b, i: (0, 0, 0)
    return pl.pallas_call(
        _ret_kernel,
        out_shape=jax.ShapeDtypeStruct((B * T, GROUP_W), BF16),
        grid=(B, nb),
        in_specs=[pl.BlockSpec((TB, 768), lambda b, i: (b * nb + i, 0)),
                  pl.BlockSpec((TB, GROUP_W), lambda b, i: (b * nb + i, 0)),
                  pl.BlockSpec((N_GH, TB, TB), const3),
                  pl.BlockSpec((2, TB, LANES), const3),
                  pl.BlockSpec((2, TB, LANES), const3),
                  pl.BlockSpec((2, LANES, LANES), const3),
                  pl.BlockSpec((1, GROUP_W), lambda b, i: (0, 0))],
        out_specs=pl.BlockSpec((TB, GROUP_W), lambda b, i: (b * nb + i, 0)),
        scratch_shapes=[pltpu.VMEM((2, LANES, LANES), F32)],
        compiler_params=_cparams("arbitrary", "arbitrary"),
        name="ret",
    )(qkvb, gates, dmat, xi, zeta, gb, g_b)


def _gla_kernel(c_ref, gate_ref, wa2_ref, ba_ref, g_ref, o_ref, s_ref, b_scr, q_scr, k_scr, v_scr, acc_scr):
    @pl.when(pl.program_id(1) == 0)
    def _():
        s_ref[...] = jnp.zeros_like(s_ref)

    n_chunks = TB // CHUNK
    hmask = (lax.broadcasted_iota(I32, (LANES, GROUP_W), 0) // GLA_DK
             == lax.broadcasted_iota(I32, (LANES, GROUP_W), 1) // HEAD_DIM)
    hsum = jnp.where(hmask, 1.0, 0.0).astype(BF16)
    r_idx = lax.broadcasted_iota(I32, (TB, TB), 0)
    c_idx = lax.broadcasted_iota(I32, (TB, TB), 1)
    tri = jnp.where((r_idx >= c_idx) & (r_idx // CHUNK == c_idx // CHUNK), 1.0, 0.0).astype(BF16)

    z = _dot_split(c_ref[:, 512:640], wa2_ref[...]) + ba_ref[...]
    log_a = (jnp.minimum(z, 0.0) - jnp.log(1.0 + jnp.exp(-jnp.abs(z)))) * (1.0 / GLA_TAU)
    b = sum(_dot(tri, piece) for piece in _split_bf16(log_a, 3))
    q = c_ref[:, 0:128] * (GLA_DK ** -0.5)
    b_scr[...] = b * LOG2E
    q_scr[...] = q
    k_scr[...] = c_ref[:, 128:256]
    v_scr[...] = c_ref[:, 256:512]

    b_last = [b[(c + 1) * CHUNK - 1:(c + 1) * CHUNK, :] for c in range(n_chunks)]
    b_end = jnp.concatenate([jnp.broadcast_to(bl, (CHUNK, LANES)) for bl in b_last], axis=0)
    kd = (c_ref[:, 128:256] * jnp.exp(b_end - b)).astype(BF16)
    qe = (q * jnp.exp(b)).astype(BF16)
    vb = c_ref[:, 256:512].astype(BF16)
    state = s_ref[...]
    inters = []
    for c in range(n_chunks):
        rows = slice(c * CHUNK, (c + 1) * CHUNK)
        inters.append(_dot(qe[rows], state.astype(BF16)))
        scale = jnp.transpose(jnp.broadcast_to(jnp.exp(b_last[c]), (LANES, LANES)))
        state = jnp.where(hmask, state * jnp.concatenate([scale, scale], axis=1) + _dot_tn(kd[rows], vb[rows]), 0.0)
    s_ref[...] = state

    acc_scr[...] = jnp.concatenate(inters, axis=0)

    keys_per_step = 32

    def key_step(s4, carry):
        for c in range(n_chunks):
            rows = slice(c * CHUNK, (c + 1) * CHUNK)
            part = None
            for u in range(keys_per_step):
                r = c * CHUNK + s4 * keys_per_step + u
                gap = pltpu.bitcast(b_scr[rows, :] - b_scr[pl.ds(r, 1), :], I32)
                decay = jnp.exp2(pltpu.bitcast(gap | jnp.int32(INT_MIN), F32))
                w = (q_scr[rows, :] * decay * k_scr[pl.ds(r, 1), :]).astype(BF16)
                term = _dot(w, hsum) * v_scr[pl.ds(r, 1), :]
                part = term if part is None else part + term
            acc_scr[rows, :] += part
        return carry

    lax.fori_loop(0, CHUNK // keys_per_step, key_step, 0)
    o = acc_scr[...]
    for gi in range(2):
        y = _pair_head_norm(o[:, gi * LANES:(gi + 1) * LANES], g_ref[:, gi * LANES:(gi + 1) * LANES], True)
        o_ref[:, gi * LANES:(gi + 1) * LANES] = (_silu(gate_ref[:, gi * LANES:(gi + 1) * LANES]) * y).astype(BF16)


def _gla_call(cc, gates, wa2_p, ba, g_c, B, T):
    nb = T // TB
    return pl.pallas_call(
        _gla_kernel,
        out_shape=jax.ShapeDtypeStruct((B * T, GROUP_W), BF16),
        grid=(B, nb),
        in_specs=[pl.BlockSpec((TB, 640), lambda b, i: (b * nb + i, 0)),
                  pl.BlockSpec((TB, GROUP_W), lambda b, i: (b * nb + i, 1)),
                  pl.BlockSpec((LANES, LANES), lambda b, i: (0, 0)),
                  pl.BlockSpec((1, LANES), lambda b, i: (0, 0)),
                  pl.BlockSpec((1, GROUP_W), lambda b, i: (0, 0))],
        out_specs=pl.BlockSpec((TB, GROUP_W), lambda b, i: (b * nb + i, 0)),
        scratch_shapes=[pltpu.VMEM((LANES, GROUP_W), F32), pltpu.VMEM((TB, LANES), F32),
                        pltpu.VMEM((TB, LANES), F32), pltpu.VMEM((TB, LANES), F32),
                        pltpu.VMEM((TB, GROUP_W), F32), pltpu.VMEM((TB, GROUP_W), F32)],
        compiler_params=_cparams("arbitrary", "arbitrary"),
        name="gla",
    )(cc, gates, wa2_p, ba, g_c)


def _rms(y, g):
    return y * lax.rsqrt(jnp.mean(y * y, axis=-1, keepdims=True) + EPS) * g


def _out_mlp_kernel(x_ref, ya_ref, yb_ref, yc_ref, yd_ref, mod_ref, wo_ref, w1_ref, w2_ref,
                    gpost_ref, gpre_ref, gpost2_ref, o_ref):
    y = _dot(ya_ref[...], wo_ref[0:256, :])
    y = y + _dot(yb_ref[...], wo_ref[256:512, :])
    y = y + _dot(yc_ref[...], wo_ref[512:768, :])
    y = y + _dot(yd_ref[...], wo_ref[768:1024, :])
    x = x_ref[...] + mod_ref[0, 2:3, :] * _rms(y, gpost_ref[...])
    h = _rms(x, gpre_ref[...]) * (1.0 + mod_ref[0, 4:5, :]) + mod_ref[0, 3:4, :]
    hb = h.astype(BF16)
    acc = jnp.zeros(x.shape, F32)
    fc = 1024
    for f in range(D_FF // fc):
        u = jnp.maximum(_dot(hb, w1_ref[:, f * fc:(f + 1) * fc]), 0.0)
        acc = acc + _dot((u * u).astype(BF16), w2_ref[f * fc:(f + 1) * fc, :])
    o_ref[...] = x + mod_ref[0, 5:6, :] * _rms(acc, gpost2_ref[...])


def _out_mlp_call(x2, ys, mod3, w_out, w1, w2, g_post, g_pre2, g_post2, T):
    N, D = x2.shape
    nt = T // TM
    row = lambda i: (i, 0)
    const = lambda i: (0, 0)
    return pl.pallas_call(
        _out_mlp_kernel,
        out_shape=jax.ShapeDtypeStruct((N, D), F32),
        grid=(N // TM,),
        in_specs=[pl.BlockSpec((TM, D), row)] + [pl.BlockSpec((TM, GROUP_W), row)] * 4
                 + [pl.BlockSpec((1, 6, D), lambda i: (i // nt, 0, 0)),
                    _resident((D, D), const), _resident((D, D_FF), const), _resident((D_FF, D), const),
                    _resident((1, D), const), _resident((1, D), const), _resident((1, D), const)],
        out_specs=pl.BlockSpec((TM, D), row),
        compiler_params=_cparams("arbitrary"),
        name="out_mlp",
    )(x2, *ys, mod3, w_out, w1, w2, g_post, g_pre2, g_post2)


def _relayout_w_in(w_in):
    col = lambda k: w_in[:, _IN_OFF[k]:_IN_OFF[k + 1]]
    zeros = lambda n: jnp.zeros((w_in.shape[0], n), w_in.dtype)
    parts = [col(_A_Q), col(_A_QI), col(_A_K), col(_A_KI), col(_A_V), zeros(64),
             col(_A_WI), zeros(120),
             col(_B_Q), col(_B_K), col(_B_V),
             col(_B_G), col(_C_G),
             col(_C_Q), col(_C_K), col(_C_V), col(_C_A), zeros(112),
             col(_D_Q), col(_D_K), col(_D_V)]
    return jnp.concatenate(parts, axis=1).astype(BF16)


def _rope_tables(pos, rot, period, theta):
    half = rot // 2
    inv = theta ** (-jnp.arange(half, dtype=F32) / half)
    ang = pos.astype(F32)[..., None] * inv
    cos, sin = jnp.cos(ang), jnp.sin(ang)
    pad = ang.shape[:-1] + (period - rot,)
    ct = jnp.concatenate([cos, cos, jnp.ones(pad, F32)], axis=-1)
    st = jnp.concatenate([-sin, sin, jnp.zeros(pad, F32)], axis=-1)
    reps = LANES // period
    return (jnp.tile(ct, (1, 1, reps)).reshape(-1, LANES), jnp.tile(st, (1, 1, reps)).reshape(-1, LANES))


def _layer(x2, cond, tabs, ret_tabs, layer_idx, B, T, mod_w, mod_b, attn_pre_g, attn_post_g,
           mlp_pre_g, mlp_post_g, w_in, gla_wa2, gla_ba, lam_q1, lam_k1, lam_q2, lam_k2,
           head_norm_g, w_out, mlp_w1, mlp_w2):
    D = D_MODEL
    mod3 = _mod_call(cond, mod_w, mod_b).reshape(B, 6, D)
    qa, kva, wa, qkvb, gates, cc, dqkv, vta, vtd = _in_call(
        x2, mod3, attn_pre_g.reshape(1, D), _relayout_w_in(w_in), tabs, T)

    g_a, g_b, g_c, g_d = jnp.split(head_norm_g, 4)
    gain_t = lambda g, tq: jnp.broadcast_to(g[:, None], (GROUP_W, tq))
    y_a = _dsa_call(qa, wa, kva, vta, gain_t(g_a, TQ_DSA), B, T)
    y_b = _ret_call(qkvb, gates, ret_tabs, g_b.reshape(1, GROUP_W), B, T)
    wa2_p = jnp.zeros((LANES, LANES), F32).at[:GLA_RANK].set(gla_wa2)
    y_c = _gla_call(cc, gates, wa2_p, gla_ba.reshape(1, LANES), g_c.reshape(1, GROUP_W), B, T)
    lam_init = 0.8 - 0.6 * math.exp(-0.3 * layer_idx)
    lam_vecs = jnp.zeros((4, LANES), F32).at[:, :DIFF_DK].set(jnp.stack([lam_q1, lam_k1, lam_q2, lam_k2]))
    y_d = _diff_call(dqkv, vtd, lam_vecs, gain_t(g_d, TQ_DIFF), lam_init, B, T)

    return _out_mlp_call(x2, (y_a, y_b, y_c, y_d), mod3, w_out.astype(BF16), mlp_w1.astype(BF16),
                         mlp_w2.astype(BF16), attn_post_g.reshape(1, D), mlp_pre_g.reshape(1, D),
                         mlp_post_g.reshape(1, D), T)


def kernel(x, c, positions, mod_w, mod_b, attn_pre_g, attn_post_g, mlp_pre_g, mlp_post_g, w_in,
           gla_wa2, gla_ba, lam_q1, lam_k1, lam_q2, lam_k2, head_norm_g, w_out, mlp_w1, mlp_w2):
    B, T, D = x.shape
    assert D == D_MODEL and T % TM == 0 and T % TB == 0
    tabs = (_rope_tables(positions, HEAD_DIM // ROPE_FRAC, HEAD_DIM, ROPE_THETA)
            + _rope_tables(positions, HEAD_DIM, HEAD_DIM, RET_THETA)
            + _rope_tables(positions, DIFF_DK // ROPE_FRAC, DIFF_DK, ROPE_THETA))
    ret_tabs = _ret_tables()
    x2 = x.reshape(B * T, D)
    for l in range(mod_w.shape[0]):
        x2 = _layer(x2, c, tabs, ret_tabs, l, B, T, mod_w[l], mod_b[l], attn_pre_g[l], attn_post_g[l],
                    mlp_pre_g[l], mlp_post_g[l], w_in[l], gla_wa2[l], gla_ba[l], lam_q1[l], lam_k1[l],
                    lam_q2[l], lam_k2[l], head_norm_g[l], w_out[l], mlp_w1[l], mlp_w2[l])
    return x2.reshape(B, T, D)
```

```python
import functools
import math

import numpy as np
import jax
import jax.numpy as jnp
from jax import lax
from jax.experimental import pallas as pl
from jax.experimental.pallas import tpu as pltpu

F32 = jnp.float32
BF16 = jnp.bfloat16
I32 = jnp.int32
I16 = jnp.int16

D_MODEL = 1024
CHUNK = 64
HEAD_DIM = 64
N_GH = 4
GROUP_W = N_GH * HEAD_DIM
IDX_HEADS = 8
IDX_DIM = 64
TOPK_MAX = 256
RET_THETA = 10000.0
GLA_DK = HEAD_DIM // 2
GLA_RANK = 16
GLA_TAU = 16.0
DIFF_DK = HEAD_DIM // 2
ROPE_THETA = 500000.0
ROPE_FRAC = 4
D_FF = 4 * D_MODEL
EPS = 1e-6
NEG_INF = -1e30

LANES = 128
VMEM_LIMIT = 56 * 1024 * 1024

_IN_SPLITS = (
    GROUP_W, HEAD_DIM, HEAD_DIM, IDX_HEADS * IDX_DIM, IDX_DIM, IDX_HEADS,
    GROUP_W, GROUP_W, GROUP_W, GROUP_W,
    N_GH * GLA_DK, N_GH * GLA_DK, GROUP_W, GLA_RANK, GROUP_W,
    2 * N_GH * DIFF_DK, 2 * N_GH * DIFF_DK, GROUP_W,
)
_IN_OFF = np.concatenate([[0], np.cumsum(_IN_SPLITS)]).tolist()
(_A_Q, _A_K, _A_V, _A_QI, _A_KI, _A_WI, _B_Q, _B_K, _B_V, _B_G,
 _C_Q, _C_K, _C_V, _C_A, _C_G, _D_Q, _D_K, _D_V) = range(18)

SEG_A = 0
SEG_W = 1024
SEG_B = 1152
SEG_G = 1920
SEG_C = 2432
SEG_D = 3072
W_COLS = 3840

NT_DIMS = (((1,), (1,)), ((), ()))
TN_DIMS = (((0,), (0,)), ((), ()))

TM = 512
TQ_DSA = 256
TQ_DIFF = 512
TK = 512
COUNT_ROWS = 512
FLASH_SLAB = 32
VT_ROWS = HEAD_DIM + 16
LOG2E = math.log2(math.e)
TB = 256


def _float_key(v):
    bits = int(np.array(v, np.float32).view(np.int32))
    return -(bits & 0x7FFFFFFF) if bits < 0 else bits


NEG_KEY = _float_key(NEG_INF)
INT_MIN = -(2 ** 31)


def _cparams(*sem):
    return pltpu.CompilerParams(dimension_semantics=sem, vmem_limit_bytes=VMEM_LIMIT)


def _resident(shape, index_map):
    return pl.BlockSpec(shape, index_map, pipeline_mode=pl.Buffered(1))


def _dot(a, b):
    return jnp.dot(a, b, preferred_element_type=F32)


def _dot_nt(a, b):
    return lax.dot_general(a, b, NT_DIMS, preferred_element_type=F32)


def _dot_tn(a, b):
    return lax.dot_general(a, b, TN_DIMS, preferred_element_type=F32)


def _dot_f32(a, b):
    return jnp.dot(a, b, preferred_element_type=F32, precision=lax.Precision.HIGHEST)


def _split_bf16(x, n):
    parts = []
    for _ in range(n):
        p = x.astype(BF16)
        parts.append(p)
        x = x - p.astype(F32)
    return parts


def _dot_split(a, b):
    (a0, a1), (b0, b1) = _split_bf16(a, 2), _split_bf16(b, 2)
    return _dot(a0, b0) + (_dot(a0, b1) + _dot(a1, b0))


def _lane(shape):
    return lax.broadcasted_iota(I32, shape, len(shape) - 1)


def _silu(x):
    return x / (1.0 + jnp.exp(-x))


def _mod_kernel(c_ref, w_ref, b_ref, o_ref):
    cond = _silu(c_ref[...])
    o_ref[...] = _dot_f32(cond, w_ref[...]) + b_ref[...]


def _mod_call(c, mod_w, mod_b):
    B, D = c.shape
    n = mod_w.shape[1] // D
    return pl.pallas_call(
        _mod_kernel,
        out_shape=jax.ShapeDtypeStruct((B, n * D), F32),
        grid=(n,),
        in_specs=[pl.BlockSpec((B, D), lambda j: (0, 0)),
                  pl.BlockSpec((D, D), lambda j: (0, j)),
                  pl.BlockSpec((1, D), lambda j: (0, j))],
        out_specs=pl.BlockSpec((B, D), lambda j: (0, j)),
        compiler_params=_cparams("arbitrary"),
        name="mod",
    )(c, mod_w, mod_b.reshape(1, -1))


def _rope_group(xg, cos, sin, half, period):
    lo = (_lane((1, LANES)) % period) < half
    swapped = jnp.where(lo, pltpu.roll(xg, LANES - half, 1), pltpu.roll(xg, half, 1))
    return xg * cos + swapped * sin


def _in_kernel(x_ref, mod_ref, g_ref, w_ref, ca_ref, sa_ref, cb_ref, sb_ref, cd_ref, sd_ref,
               qa_ref, kva_ref, wa_ref, qkvb_ref, gate_ref, c_ref, d_ref, vta_ref, vtd_ref):
    x = x_ref[...]
    h = x * lax.rsqrt(jnp.mean(x * x, axis=-1, keepdims=True) + EPS) * g_ref[...]
    h = h * (1.0 + mod_ref[0, 1:2, :]) + mod_ref[0, 0:1, :]
    hb = h.astype(BF16)

    def seg(off, width):
        return _dot(hb, w_ref[:, off:off + width])

    lane = _lane((1, LANES))
    low = lane < HEAD_DIM

    ca, sa = ca_ref[...], sa_ref[...]
    a = seg(SEG_A, 1024)
    half_a = HEAD_DIM // ROPE_FRAC // 2
    for gi in range(2):
        r = _rope_group(a[:, gi * LANES:(gi + 1) * LANES], ca, sa, half_a, HEAD_DIM) * (HEAD_DIM ** -0.5 * LOG2E)
        rs = pltpu.roll(r, HEAD_DIM, 1)
        qa_ref[:, (2 * gi) * LANES:(2 * gi + 1) * LANES] = jnp.where(low, r, 0.0).astype(BF16)
        qa_ref[:, (2 * gi + 1) * LANES:(2 * gi + 2) * LANES] = jnp.where(low, rs, 0.0).astype(BF16)
    for gi in range(4):
        r = _rope_group(a[:, (2 + gi) * LANES:(3 + gi) * LANES], ca, sa, half_a, HEAD_DIM)
        rs = pltpu.roll(r, HEAD_DIM, 1)
        qa_ref[:, (4 + 2 * gi) * LANES:(5 + 2 * gi) * LANES] = jnp.where(low, 0.0, rs).astype(BF16)
        qa_ref[:, (5 + 2 * gi) * LANES:(6 + 2 * gi) * LANES] = jnp.where(low, 0.0, r).astype(BF16)
    kva_ref[:, 0:LANES] = _rope_group(a[:, 6 * LANES:7 * LANES], ca, sa, half_a, HEAD_DIM).astype(BF16)
    kva_ref[:, LANES:2 * LANES] = a[:, 7 * LANES:8 * LANES].astype(BF16)
    ones_rows = jnp.ones((VT_ROWS - HEAD_DIM, TM), BF16)
    vta_ref[0:HEAD_DIM, :] = jnp.transpose(a[:, 7 * LANES:8 * LANES])[0:HEAD_DIM, :].astype(BF16)
    vta_ref[HEAD_DIM:VT_ROWS, :] = ones_rows

    wa_ref[...] = seg(SEG_W, LANES)

    cb, sb = cb_ref[...], sb_ref[...]
    b = seg(SEG_B, 768)
    for gi in range(4):
        r = _rope_group(b[:, gi * LANES:(gi + 1) * LANES], cb, sb, HEAD_DIM // 2, HEAD_DIM)
        if gi < 2:
            r = r * (HEAD_DIM ** -0.5)
        qkvb_ref[:, gi * LANES:(gi + 1) * LANES] = r.astype(BF16)
    qkvb_ref[:, 512:768] = b[:, 512:768].astype(BF16)

    gate_ref[...] = seg(SEG_G, 512)
    c_ref[...] = seg(SEG_C, 640)

    cd, sd = cd_ref[...], sd_ref[...]
    d = seg(SEG_D, 768)
    half_d = DIFF_DK // ROPE_FRAC // 2
    for gi in range(4):
        r = _rope_group(d[:, gi * LANES:(gi + 1) * LANES], cd, sd, half_d, DIFF_DK)
        if gi < 2:
            r = r * (DIFF_DK ** -0.5 * LOG2E)
        d_ref[:, gi * LANES:(gi + 1) * LANES] = r.astype(BF16)
    d_ref[:, 512:768] = d[:, 512:768].astype(BF16)
    vt = jnp.transpose(d[:, 512:768])
    for h in range(N_GH):
        vtd_ref[h * VT_ROWS:h * VT_ROWS + HEAD_DIM, :] = vt[h * HEAD_DIM:(h + 1) * HEAD_DIM, :].astype(BF16)
        vtd_ref[h * VT_ROWS + HEAD_DIM:(h + 1) * VT_ROWS, :] = ones_rows


def _in_call(x2, mod3, g, w_p, tabs, T):
    N, D = x2.shape
    nt = T // TM
    row = lambda i: (i, 0)
    tab_spec = pl.BlockSpec((TM, LANES), row)
    outs = [(1536, BF16), (256, BF16), (LANES, F32), (768, BF16), (512, F32), (640, F32), (768, BF16)]
    vt_rows = (VT_ROWS, N_GH * VT_ROWS)
    vt_shapes = [jax.ShapeDtypeStruct((N // T * r, T), BF16) for r in vt_rows]
    vt_specs = [pl.BlockSpec((r, TM), lambda i: (i // nt, i % nt)) for r in vt_rows]
    return pl.pallas_call(
        _in_kernel,
        out_shape=[jax.ShapeDtypeStruct((N, w), dt) for w, dt in outs] + vt_shapes,
        grid=(N // TM,),
        in_specs=[pl.BlockSpec((TM, D), row),
                  pl.BlockSpec((1, 6, D), lambda i: (i // nt, 0, 0)),
                  _resident((1, D), lambda i: (0, 0)),
                  _resident((D, W_COLS), lambda i: (0, 0))] + [tab_spec] * 6,
        out_specs=[pl.BlockSpec((TM, w), row) for w, _ in outs] + vt_specs,
        compiler_params=_cparams("arbitrary"),
        name="in_proj",
    )(x2, mod3, g, w_p, *tabs)


def _pair_head_norm(o, gain, center):
    low = _lane((1, LANES)) < HEAD_DIM
    inv = 1.0 / HEAD_DIM

    def seg_mean(v):
        m0 = jnp.sum(jnp.where(low, v, 0.0), axis=1, keepdims=True)
        m1 = jnp.sum(jnp.where(low, 0.0, v), axis=1, keepdims=True)
        return jnp.where(low, m0, m1) * inv

    if center:
        o = o - seg_mean(o)
    return o * lax.rsqrt(seg_mean(o * o) + EPS) * gain


class _Flash:
    def __init__(self, n, s_ref, p_ref, acc_ref, st_ref):
        self.n, self.s, self.p, self.acc, self.st = n, s_ref, p_ref, acc_ref, st_ref
        self.tq = s_ref.shape[-1]

    @staticmethod
    def scratch(n, tq):
        return [pltpu.VMEM((n, TK, tq), F32), pltpu.VMEM((n, TK, tq), BF16),
                pltpu.VMEM((n, VT_ROWS, tq), F32), pltpu.VMEM((3, n, tq), F32)]

    def _row(self, k, i):
        return self.st.at[k, i:i + 1, :]

    def init(self):
        self.st[0] = jnp.full((self.n, self.tq), NEG_INF, F32)
        self.acc[...] = jnp.zeros_like(self.acc)

    def scores(self, i, s):
        self.s[i] = s
        self._row(1, i)[...] = jnp.max(_fold8(s, jnp.maximum), axis=0, keepdims=True)

    def probs(self, i):
        m_old = self._row(0, i)[...]
        m_new = jnp.maximum(m_old, self._row(1, i)[...])
        for r in range(0, TK, FLASH_SLAB):
            self.p[i, r:r + FLASH_SLAB, :] = jnp.exp2(self.s[i, r:r + FLASH_SLAB, :] - m_new).astype(BF16)
        self._row(0, i)[...] = m_new
        self._row(2, i)[...] = jnp.exp2(m_old - m_new)

    def values(self, i, vt):
        self.acc[i] = self._row(2, i)[...] * self.acc[i] + _dot(vt, self.p[i])

    def result(self, i):
        return self.acc[i, 0:HEAD_DIM, :] / self.acc[i, HEAD_DIM:HEAD_DIM + 1, :]

    def pipeline(self, first, n_rest, score, value):
        score(first, True)

        def body(t, carry):
            for i in range(self.n):
                self.probs(i)
            score(t, False)
            value(jnp.where(t == 0, first, t - 1))
            return carry

        lax.fori_loop(0, n_rest, body, 0)
        for i in range(self.n):
            self.probs(i)
        value(jnp.where(n_rest == 0, first, n_rest - 1))


def _fold8(x, op=jnp.add, rows=8):
    parts = [x[r:r + rows] for r in range(0, x.shape[0], rows)]
    while len(parts) > 1:
        parts = [op(a, b) for a, b in zip(parts[0::2], parts[1::2])]
    return parts[0]


def _head_norm_t(o, gain):
    ms = jnp.mean(o * o, axis=0, keepdims=True)
    return o * lax.rsqrt(ms + EPS) * gain


def _dsa_kernel(q_ref, w_ref, kv_ref, vt_ref, g_ref, ridx_ref, o_ref, bias_ref, key_ref, hi_ref, lo_ref, u_ref,
                *flash_refs, topk):
    TQ = TQ_DSA
    q0 = pl.program_id(1) * TQ
    nkt = (q0 + TQ + TK - 1) // TK
    qchunk = (q0 + _lane((1, TQ))) // CHUNK
    wts = jnp.transpose(w_ref[...]) * (IDX_HEADS ** -0.5 * IDX_DIM ** -0.5)

    def score_tile(kt, carry):
        k0 = pl.multiple_of(kt * TK, TK)
        kk = kv_ref[pl.ds(k0, TK), 0:LANES]
        tile = bias_ref.at[pl.ds(k0, TK), :]
        for h in range(IDX_HEADS):
            term = jnp.maximum(_dot_nt(kk, q_ref[:, (4 + h) * LANES:(5 + h) * LANES]), 0.0) * wts[h:h + 1, :]
            if h == 0:
                tile[...] = term
            elif h < IDX_HEADS - 1:
                tile[...] += term
        kchunk = (k0 + lax.broadcasted_iota(I32, (TK, 1), 0)) // CHUNK
        acc = jnp.where(kchunk <= qchunk, tile[...] + term, NEG_INF)
        bits = pltpu.bitcast(acc, I32)
        sign = bits >> 31
        key = ((bits & 0x7FFFFFFF) ^ sign) - sign
        key_ref[pl.ds(k0, TK), :] = key
        hi_ref[pl.ds(k0, TK), :] = (key >> 16).astype(I16)
        lo_ref[pl.ds(k0, TK), :] = (key ^ 0x8000).astype(I16)
        return carry

    lax.fori_loop(0, nkt, score_tile, 0)

    cr = min(COUNT_ROWS, key_ref.shape[0])
    n_count = (nkt * TK + cr - 1) // cr

    def pad_tile(kt, carry):
        k0 = pl.multiple_of(kt * TK, TK)
        hi_ref[pl.ds(k0, TK), :] = jnp.full((TK, TQ), -32768, I16)
        lo_ref[pl.ds(k0, TK), :] = jnp.full((TK, TQ), -32768, I16)
        return carry

    lax.fori_loop(nkt, n_count * (cr // TK), pad_tile, 0)
    kf = float(topk)

    def count16(ref, cand):
        def body(kc, acc):
            k0 = pl.multiple_of(kc * cr, cr)
            hit = jnp.where(ref[pl.ds(k0, cr), :] >= cand, jnp.ones((cr, TQ), I16), jnp.zeros((cr, TQ), I16))
            return acc + _fold8(hit, rows=16)
        acc = lax.fori_loop(0, n_count, body, jnp.zeros((16, TQ), I16))
        return jnp.sum(acc.astype(F32), axis=0, keepdims=True)

    def search16(ref, cnt_all, signed=True, bits=15):
        if signed:
            c0 = count16(ref, jnp.zeros((1, TQ), I16))
            t = jnp.where(c0 >= kf, 0, -32768).astype(I32)
            cnt = jnp.where(c0 >= kf, c0, cnt_all)
        else:
            t, cnt = jnp.zeros((1, TQ), I32), cnt_all

        def step(it, carry):
            t, cnt = carry
            cand = t + (jnp.int32(1) << (bits - 1 - it))
            c = count16(ref, cand.astype(I16))
            ok = c >= kf
            return jnp.where(ok, cand, t), jnp.where(ok, c, cnt)

        return lax.fori_loop(0, bits, step, (t, cnt))

    t_hi, cnt_hi = search16(hi_ref, (nkt * TK).astype(F32))
    t_hi16 = t_hi.astype(I16)

    top = jnp.full((cr, TQ), 32767, I16)
    bottom = jnp.full((cr, TQ), -32768, I16)

    def low_digit_tile(kc, carry):
        k0 = pl.multiple_of(kc * cr, cr)
        hi = hi_ref[pl.ds(k0, cr), :]
        in_bucket = jnp.where(hi == t_hi16, lo_ref[pl.ds(k0, cr), :], bottom)
        u_ref[pl.ds(k0, cr), :] = jnp.where(hi > t_hi16, top, in_bucket)
        return carry

    lax.fori_loop(0, n_count, low_digit_tile, 0)
    t_lo, cnt_lo = search16(u_ref, cnt_hi)
    t_lo16 = t_lo.astype(I16)
    thr = t_hi * 65536 + (t_lo + 32768)

    def index_digit_tile(kc, carry):
        k0 = pl.multiple_of(kc * cr, cr)
        hi, lo = hi_ref[pl.ds(k0, cr), :], lo_ref[pl.ds(k0, cr), :]
        same_hi = jnp.where(lo == t_lo16, ridx_ref[pl.ds(k0, cr), :], jnp.where(lo > t_lo16, top, bottom))
        u_ref[pl.ds(k0, cr), :] = jnp.where(hi == t_hi16, same_hi, jnp.where(hi > t_hi16, top, bottom))
        return carry

    n_keys = key_ref.shape[0]
    floor_key = jnp.int32(NEG_KEY + 1)
    short = thr < floor_key

    def index_search():
        lax.fori_loop(0, n_count, index_digit_tile, 0)
        return search16(u_ref, cnt_lo, signed=False, bits=(n_keys - 1).bit_length())[0]

    any_tie = jnp.max(jnp.where(short, 0.0, cnt_lo - kf)) > 0.0
    t_idx = lax.cond(any_tie, index_search, lambda: jnp.zeros((1, TQ), I32))
    lim_tie = jnp.where(short, floor_key, thr)
    lim_late = jnp.where(short, floor_key, thr + 1)
    last_tie = jnp.where(short, -1, (n_keys - 1) - t_idx)

    def select_tile(kt, carry):
        k0 = pl.multiple_of(kt * TK, TK)
        idx = k0 + lax.broadcasted_iota(I32, (TK, 1), 0)
        lim = jnp.where(idx <= last_tie, lim_tie, lim_late)
        bias_ref[pl.ds(k0, TK), :] = jnp.where(key_ref[pl.ds(k0, TK), :] >= lim, 0.0, NEG_INF)
        return carry

    lax.fori_loop(0, nkt, select_tile, 0)

    flash = _Flash(N_GH, *flash_refs)
    flash.init()

    def score(kt, is_first):
        k0 = pl.multiple_of(kt * TK, TK)
        kk = kv_ref[pl.ds(k0, TK), 0:LANES]
        bias = bias_ref[pl.ds(k0, TK), :]
        for h in range(N_GH):
            flash.scores(h, _dot_nt(kk, q_ref[:, h * LANES:(h + 1) * LANES]) + bias)

    def value(kt):
        k0 = pl.multiple_of(kt * TK, TK)
        for h in range(N_GH):
            flash.values(h, vt_ref[:, pl.ds(k0, TK)])

    flash.pipeline(nkt - 1, nkt - 1, score, value)
    y = jnp.concatenate([_head_norm_t(flash.result(h), g_ref[h * HEAD_DIM:(h + 1) * HEAD_DIM, :])
                         for h in range(N_GH)], axis=0)
    o_ref[...] = jnp.transpose(y).astype(BF16)


def _dsa_call(qa, wa, kva, vta, g_t, B, T):
    TQ = TQ_DSA
    nq = T // TQ
    ridx = jnp.broadcast_to((T - 1 - jnp.arange(T, dtype=I32)).astype(I16)[:, None], (T, TQ))
    topk = min(TOPK_MAX, T // 4)
    return pl.pallas_call(
        functools.partial(_dsa_kernel, topk=topk),
        out_shape=jax.ShapeDtypeStruct((B * T, GROUP_W), BF16),
        grid=(B, nq),
        in_specs=[pl.BlockSpec((TQ, 1536), lambda b, i: (b * nq + i, 0)),
                  pl.BlockSpec((TQ, LANES), lambda b, i: (b * nq + i, 0)),
                  pl.BlockSpec((T, 256), lambda b, i: (b, 0)),
                  pl.BlockSpec((VT_ROWS, T), lambda b, i: (b, 0)),
                  pl.BlockSpec((GROUP_W, TQ), lambda b, i: (0, 0)),
                  _resident((T, TQ), lambda b, i: (0, 0))],
        out_specs=pl.BlockSpec((TQ, GROUP_W), lambda b, i: (b * nq + i, 0)),
        scratch_shapes=[pltpu.VMEM((T, TQ), F32), pltpu.VMEM((T, TQ), I32), pltpu.VMEM((T, TQ), I16),
                        pltpu.VMEM((T, TQ), I16), pltpu.VMEM((T, TQ), I16)] + _Flash.scratch(N_GH, TQ),
        compiler_params=_cparams("arbitrary", "arbitrary"),
        name="dsa",
    )(qa, wa, kva, vta, g_t, ridx)


def _diff_kernel(q_ref, k_ref, vt_ref, lam_ref, g_ref, o_ref, qm_ref, *flash_refs, lam_init):
    TQ = TQ_DIFF
    q0 = pl.program_id(1) * TQ
    nfull = (q0 + CHUNK) // TK
    qchunk = (q0 + _lane((1, TQ))) // CHUNK
    lv = lam_ref[...]
    lam = (jnp.exp(jnp.sum(lv[0:1] * lv[1:2], axis=1, keepdims=True))
           - jnp.exp(jnp.sum(lv[2:3] * lv[3:4], axis=1, keepdims=True)) + lam_init)
    n_maps = 2 * N_GH
    qt = jnp.transpose(q_ref[...].astype(F32))
    sub = lax.broadcasted_iota(I32, (LANES, 1), 0) // DIFF_DK
    for mi in range(n_maps):
        qg = qt[(mi // 4) * LANES:(mi // 4 + 1) * LANES, :]
        qm_ref[mi] = jnp.where(sub == mi % 4, qg, 0.0).astype(BF16)

    flash = _Flash(n_maps, *flash_refs)
    flash.init()

    def score(kt, masked):
        k0 = pl.multiple_of(kt * TK, TK)
        if masked:
            kchunk = (k0 + lax.broadcasted_iota(I32, (TK, 1), 0)) // CHUNK
            admissible = kchunk <= qchunk
        for mi in range(n_maps):
            g = mi // 4
            s = _dot(k_ref[pl.ds(k0, TK), g * LANES:(g + 1) * LANES], qm_ref[mi])
            flash.scores(mi, jnp.where(admissible, s, NEG_INF) if masked else s)

    def value(kt):
        k0 = pl.multiple_of(kt * TK, TK)
        for mi in range(n_maps):
            h = mi // 2
            flash.values(mi, vt_ref[h * VT_ROWS:(h + 1) * VT_ROWS, pl.ds(k0, TK)])

    flash.pipeline(nfull, nfull, score, value)
    outs = []
    for h in range(N_GH):
        o = flash.result(2 * h) - lam * flash.result(2 * h + 1)
        outs.append(_head_norm_t(o, g_ref[h * HEAD_DIM:(h + 1) * HEAD_DIM, :]) * (1.0 - lam_init))
    o_ref[...] = jnp.transpose(jnp.concatenate(outs, axis=0)).astype(BF16)


def _diff_call(dqkv, vtd, lam_vecs, g_t, lam_init, B, T):
    TQ = TQ_DIFF
    nq = T // TQ
    return pl.pallas_call(
        functools.partial(_diff_kernel, lam_init=lam_init),
        out_shape=jax.ShapeDtypeStruct((B * T, GROUP_W), BF16),
        grid=(B, nq),
        in_specs=[pl.BlockSpec((TQ, 256), lambda b, i: (b * nq + i, 0)),
                  pl.BlockSpec((T, 256), lambda b, i: (b, 1)),
                  pl.BlockSpec((N_GH * VT_ROWS, T), lambda b, i: (b, 0)),
                  pl.BlockSpec((4, LANES), lambda b, i: (0, 0)),
                  pl.BlockSpec((GROUP_W, TQ), lambda b, i: (0, 0))],
        out_specs=pl.BlockSpec((TQ, GROUP_W), lambda b, i: (b * nq + i, 0)),
        scratch_shapes=[pltpu.VMEM((2 * N_GH, LANES, TQ), BF16)] + _Flash.scratch(2 * N_GH, TQ),
        compiler_params=_cparams("arbitrary", "arbitrary"),
        name="diff",
    )(dqkv, dqkv, vtd, lam_vecs, g_t)


def _ret_tables():
    h = np.arange(N_GH, dtype=np.float32)
    log_g = jnp.log(1.0 - 2.0 ** (-5.0 - jnp.asarray(h)))
    t = np.arange(TB)
    same = (t[:, None] // CHUNK) == (t[None, :] // CHUNK)
    earlier = (t[None, :] // CHUNK) < (t[:, None] // CHUNK)
    dist = jnp.asarray(np.where(same, np.abs(t[:, None] - t[None, :]), t[:, None] - t[None, :]), F32)
    dmat = jnp.where(jnp.asarray(same | earlier)[None],
                     jnp.exp(dist[None] * log_g[:, None, None]), 0.0)
    tt = jnp.asarray(t, F32)
    xi = jnp.exp((tt + 1.0)[None, :] * log_g[:, None])
    zeta = jnp.exp((TB - 1.0 - tt)[None, :] * log_g[:, None])
    gblk = jnp.exp(TB * log_g)

    def lanes(tab):
        return jnp.repeat(tab.reshape(2, 2, TB), HEAD_DIM, axis=1).transpose(0, 2, 1)

    r = np.arange(LANES)
    blockdiag = (r[:, None] // HEAD_DIM) == (r[None, :] // HEAD_DIM)
    gb = jnp.where(jnp.asarray(blockdiag)[None],
                   jnp.repeat(gblk.reshape(2, 2), HEAD_DIM, axis=1)[:, :, None], 0.0)
    return dmat, lanes(xi), lanes(zeta), gb


def _ret_kernel(qkv_ref, gate_ref, dmat_ref, xi_ref, zeta_ref, gb_ref, g_ref, o_ref, r_ref):
    @pl.when(pl.program_id(1) == 0)
    def _():
        r_ref[...] = jnp.zeros_like(r_ref)

    lane = _lane((1, LANES))
    low = lane < HEAD_DIM
    r_idx = lax.broadcasted_iota(I32, (LANES, LANES), 0) // HEAD_DIM
    c_idx = lax.broadcasted_iota(I32, (LANES, LANES), 1) // HEAD_DIM
    for g in range(2):
        qg = qkv_ref[:, g * LANES:(g + 1) * LANES]
        kg = qkv_ref[:, 256 + g * LANES:256 + (g + 1) * LANES]
        vg = qkv_ref[:, 512 + g * LANES:512 + (g + 1) * LANES]
        state = r_ref[g]
        inter = _dot(qg, state.astype(BF16)) * xi_ref[g]
        parts = []
        for hh in range(2):
            qm = jnp.where(low, qg, jnp.zeros_like(qg)) if hh == 0 else jnp.where(low, jnp.zeros_like(qg), qg)
            s = _dot_nt(qm, kg) * dmat_ref[2 * g + hh]
            parts.append(_dot(s.astype(BF16), vg))
        o = jnp.where(low, parts[0], parts[1]) + inter
        kz = (kg.astype(F32) * zeta_ref[g]).astype(BF16)
        upd = jnp.where(r_idx == c_idx, _dot_tn(kz, vg), 0.0)
        r_ref[g] = state * gb_ref[g] + upd
        y = _pair_head_norm(o, g_ref[:, g * LANES:(g + 1) * LANES], True)
        o_ref[:, g * LANES:(g + 1) * LANES] = (_silu(gate_ref[:, g * LANES:(g + 1) * LANES]) * y).astype(BF16)


def _ret_call(qkvb, gates, tabs, g_b, B, T):
    nb = T // TB
    dmat, xi, zeta, gb = tabs
    const3 = lambda b, i: (0, 0, 0)
    return pl.pallas_call(
        _ret_kernel,
        out_shape=jax.ShapeDtypeStruct((B * T, GROUP_W), BF16),
        grid=(B, nb),
        in_specs=[pl.BlockSpec((TB, 768), lambda b, i: (b * nb + i, 0)),
                  pl.BlockSpec((TB, GROUP_W), lambda b, i: (b * nb + i, 0)),
                  pl.BlockSpec((N_GH, TB, TB), const3),
                  pl.BlockSpec((2, TB, LANES), const3),
                  pl.BlockSpec((2, TB, LANES), const3),
                  pl.BlockSpec((2, LANES, LANES), const3),
                  pl.BlockSpec((1, GROUP_W), lambda b, i: (0, 0))],
        out_specs=pl.BlockSpec((TB, GROUP_W), lambda b, i: (b * nb + i, 0)),
        scratch_shapes=[pltpu.VMEM((2, LANES, LANES), F32)],
        compiler_params=_cparams("arbitrary", "arbitrary"),
        name="ret",
    )(qkvb, gates, dmat, xi, zeta, gb, g_b)


def _gla_kernel(c_ref, gate_ref, wa2_ref, ba_ref, g_ref, o_ref, s_ref, b_scr, q_scr, k_scr, v_scr, acc_scr):
    @pl.when(pl.program_id(1) == 0)
    def _():
        s_ref[...] = jnp.zeros_like(s_ref)

    n_chunks = TB // CHUNK
    hmask = (lax.broadcasted_iota(I32, (LANES, GROUP_W), 0) // GLA_DK
             == lax.broadcasted_iota(I32, (LANES, GROUP_W), 1) // HEAD_DIM)
    hsum = jnp.where(hmask, 1.0, 0.0).astype(BF16)
    r_idx = lax.broadcasted_iota(I32, (TB, TB), 0)
    c_idx = lax.broadcasted_iota(I32, (TB, TB), 1)
    tri = jnp.where((r_idx >= c_idx) & (r_idx // CHUNK == c_idx // CHUNK), 1.0, 0.0).astype(BF16)

    z = _dot_split(c_ref[:, 512:640], wa2_ref[...]) + ba_ref[...]
    log_a = (jnp.minimum(z, 0.0) - jnp.log(1.0 + jnp.exp(-jnp.abs(z)))) * (1.0 / GLA_TAU)
    b = sum(_dot(tri, piece) for piece in _split_bf16(log_a, 3))
    q = c_ref[:, 0:128] * (GLA_DK ** -0.5)
    b_scr[...] = b * LOG2E
    q_scr[...] = q
    k_scr[...] = c_ref[:, 128:256]
    v_scr[...] = c_ref[:, 256:512]

    b_last = [b[(c + 1) * CHUNK - 1:(c + 1) * CHUNK, :] for c in range(n_chunks)]
    b_end = jnp.concatenate([jnp.broadcast_to(bl, (CHUNK, LANES)) for bl in b_last], axis=0)
    kd = (c_ref[:, 128:256] * jnp.exp(b_end - b)).astype(BF16)
    qe = (q * jnp.exp(b)).astype(BF16)
    vb = c_ref[:, 256:512].astype(BF16)
    state = s_ref[...]
    inters = []
    for c in range(n_chunks):
        rows = slice(c * CHUNK, (c + 1) * CHUNK)
        inters.append(_dot(qe[rows], state.astype(BF16)))
        scale = jnp.transpose(jnp.broadcast_to(jnp.exp(b_last[c]), (LANES, LANES)))
        state = jnp.where(hmask, state * jnp.concatenate([scale, scale], axis=1) + _dot_tn(kd[rows], vb[rows]), 0.0)
    s_ref[...] = state

    acc_scr[...] = jnp.concatenate(inters, axis=0)

    keys_per_step = 32

    def key_step(s4, carry):
        for c in range(n_chunks):
            rows = slice(c * CHUNK, (c + 1) * CHUNK)
            part = None
            for u in range(keys_per_step):
                r = c * CHUNK + s4 * keys_per_step + u
                gap = pltpu.bitcast(b_scr[rows, :] - b_scr[pl.ds(r, 1), :], I32)
                decay = jnp.exp2(pltpu.bitcast(gap | jnp.int32(INT_MIN), F32))
                w = (q_scr[rows, :] * decay * k_scr[pl.ds(r, 1), :]).astype(BF16)
                term = _dot(w, hsum) * v_scr[pl.ds(r, 1), :]
                part = term if part is None else part + term
            acc_scr[rows, :] += part
        return carry

    lax.fori_loop(0, CHUNK // keys_per_step, key_step, 0)
    o = acc_scr[...]
    for gi in range(2):
        y = _pair_head_norm(o[:, gi * LANES:(gi + 1) * LANES], g_ref[:, gi * LANES:(gi + 1) * LANES], True)
        o_ref[:, gi * LANES:(gi + 1) * LANES] = (_silu(gate_ref[:, gi * LANES:(gi + 1) * LANES]) * y).astype(BF16)


def _gla_call(cc, gates, wa2_p, ba, g_c, B, T):
    nb = T // TB
    return pl.pallas_call(
        _gla_kernel,
        out_shape=jax.ShapeDtypeStruct((B * T, GROUP_W), BF16),
        grid=(B, nb),
        in_specs=[pl.BlockSpec((TB, 640), lambda b, i: (b * nb + i, 0)),
                  pl.BlockSpec((TB, GROUP_W), lambda b, i: (b * nb + i, 1)),
                  pl.BlockSpec((LANES, LANES), lambda b, i: (0, 0)),
                  pl.BlockSpec((1, LANES), lambda b, i: (0, 0)),
                  pl.BlockSpec((1, GROUP_W), lambda b, i: (0, 0))],
        out_specs=pl.BlockSpec((TB, GROUP_W), lambda b, i: (b * nb + i, 0)),
        scratch_shapes=[pltpu.VMEM((LANES, GROUP_W), F32), pltpu.VMEM((TB, LANES), F32),
                        pltpu.VMEM((TB, LANES), F32), pltpu.VMEM((TB, LANES), F32),
                        pltpu.VMEM((TB, GROUP_W), F32), pltpu.VMEM((TB, GROUP_W), F32)],
        compiler_params=_cparams("arbitrary", "arbitrary"),
        name="gla",
    )(cc, gates, wa2_p, ba, g_c)


def _rms(y, g):
    return y * lax.rsqrt(jnp.mean(y * y, axis=-1, keepdims=True) + EPS) * g


def _out_mlp_kernel(x_ref, ya_ref, yb_ref, yc_ref, yd_ref, mod_ref, wo_ref, w1_ref, w2_ref,
                    gpost_ref, gpre_ref, gpost2_ref, o_ref):
    y = _dot(ya_ref[...], wo_ref[0:256, :])
    y = y + _dot(yb_ref[...], wo_ref[256:512, :])
    y = y + _dot(yc_ref[...], wo_ref[512:768, :])
    y = y + _dot(yd_ref[...], wo_ref[768:1024, :])
    x = x_ref[...] + mod_ref[0, 2:3, :] * _rms(y, gpost_ref[...])
    h = _rms(x, gpre_ref[...]) * (1.0 + mod_ref[0, 4:5, :]) + mod_ref[0, 3:4, :]
    hb = h.astype(BF16)
    acc = jnp.zeros(x.shape, F32)
    fc = 1024
    for f in range(D_FF // fc):
        u = jnp.maximum(_dot(hb, w1_ref[:, f * fc:(f + 1) * fc]), 0.0)
        acc = acc + _dot((u * u).astype(BF16), w2_ref[f * fc:(f + 1) * fc, :])
    o_ref[...] = x + mod_ref[0, 5:6, :] * _rms(acc, gpost2_ref[...])


def _out_mlp_call(x2, ys, mod3, w_out, w1, w2, g_post, g_pre2, g_post2, T):
    N, D = x2.shape
    nt = T // TM
    row = lambda i: (i, 0)
    const = lambda i: (0, 0)
    return pl.pallas_call(
        _out_mlp_kernel,
        out_shape=jax.ShapeDtypeStruct((N, D), F32),
        grid=(N // TM,),
        in_specs=[pl.BlockSpec((TM, D), row)] + [pl.BlockSpec((TM, GROUP_W), row)] * 4
                 + [pl.BlockSpec((1, 6, D), lambda i: (i // nt, 0, 0)),
                    _resident((D, D), const), _resident((D, D_FF), const), _resident((D_FF, D), const),
                    _resident((1, D), const), _resident((1, D), const), _resident((1, D), const)],
        out_specs=pl.BlockSpec((TM, D), row),
        compiler_params=_cparams("arbitrary"),
        name="out_mlp",
    )(x2, *ys, mod3, w_out, w1, w2, g_post, g_pre2, g_post2)


def _relayout_w_in(w_in):
    col = lambda k: w_in[:, _IN_OFF[k]:_IN_OFF[k + 1]]
    zeros = lambda n: jnp.zeros((w_in.shape[0], n), w_in.dtype)
    parts = [col(_A_Q), col(_A_QI), col(_A_K), col(_A_KI), col(_A_V), zeros(64),
             col(_A_WI), zeros(120),
             col(_B_Q), col(_B_K), col(_B_V),
             col(_B_G), col(_C_G),
             col(_C_Q), col(_C_K), col(_C_V), col(_C_A), zeros(112),
             col(_D_Q), col(_D_K), col(_D_V)]
    return jnp.concatenate(parts, axis=1).astype(BF16)


def _rope_tables(pos, rot, period, theta):
    half = rot // 2
    inv = theta ** (-jnp.arange(half, dtype=F32) / half)
    ang = pos.astype(F32)[..., None] * inv
    cos, sin = jnp.cos(ang), jnp.sin(ang)
    pad = ang.shape[:-1] + (period - rot,)
    ct = jnp.concatenate([cos, cos, jnp.ones(pad, F32)], axis=-1)
    st = jnp.concatenate([-sin, sin, jnp.zeros(pad, F32)], axis=-1)
    reps = LANES // period
    return (jnp.tile(ct, (1, 1, reps)).reshape(-1, LANES), jnp.tile(st, (1, 1, reps)).reshape(-1, LANES))


def _layer(x2, cond, tabs, ret_tabs, layer_idx, B, T, mod_w, mod_b, attn_pre_g, attn_post_g,
           mlp_pre_g, mlp_post_g, w_in, gla_wa2, gla_ba, lam_q1, lam_k1, lam_q2, lam_k2,
           head_norm_g, w_out, mlp_w1, mlp_w2):
    D = D_MODEL
    mod3 = _mod_call(cond, mod_w, mod_b).reshape(B, 6, D)
    qa, kva, wa, qkvb, gates, cc, dqkv, vta, vtd = _in_call(
        x2, mod3, attn_pre_g.reshape(1, D), _relayout_w_in(w_in), tabs, T)

    g_a, g_b, g_c, g_d = jnp.split(head_norm_g, 4)
    gain_t = lambda g, tq: jnp.broadcast_to(g[:, None], (GROUP_W, tq))
    y_a = _dsa_call(qa, wa, kva, vta, gain_t(g_a, TQ_DSA), B, T)
    y_b = _ret_call(qkvb, gates, ret_tabs, g_b.reshape(1, GROUP_W), B, T)
    wa2_p = jnp.zeros((LANES, LANES), F32).at[:GLA_RANK].set(gla_wa2)
    y_c = _gla_call(cc, gates, wa2_p, gla_ba.reshape(1, LANES), g_c.reshape(1, GROUP_W), B, T)
    lam_init = 0.8 - 0.6 * math.exp(-0.3 * layer_idx)
    lam_vecs = jnp.zeros((4, LANES), F32).at[:, :DIFF_DK].set(jnp.stack([lam_q1, lam_k1, lam_q2, lam_k2]))
    y_d = _diff_call(dqkv, vtd, lam_vecs, gain_t(g_d, TQ_DIFF), lam_init, B, T)

    return _out_mlp_call(x2, (y_a, y_b, y_c, y_d), mod3, w_out.astype(BF16), mlp_w1.astype(BF16),
                         mlp_w2.astype(BF16), attn_post_g.reshape(1, D), mlp_pre_g.reshape(1, D),
                         mlp_post_g.reshape(1, D), T)


def kernel(x, c, positions, mod_w, mod_b, attn_pre_g, attn_post_g, mlp_pre_g, mlp_post_g, w_in,
           gla_wa2, gla_ba, lam_q1, lam_k1, lam_q2, lam_k2, head_norm_g, w_out, mlp_w1, mlp_w2):
    B, T, D = x.shape
    assert D == D_MODEL and T % TM == 0 and T % TB == 0
    tabs = (_rope_tables(positions, HEAD_DIM // ROPE_FRAC, HEAD_DIM, ROPE_THETA)
            + _rope_tables(positions, HEAD_DIM, HEAD_DIM, RET_THETA)
            + _rope_tables(positions, DIFF_DK // ROPE_FRAC, DIFF_DK, ROPE_THETA))
    ret_tabs = _ret_tables()
    x2 = x.reshape(B * T, D)
    for l in range(mod_w.shape[0]):
        x2 = _layer(x2, c, tabs, ret_tabs, l, B, T, mod_w[l], mod_b[l], attn_pre_g[l], attn_post_g[l],
                    mlp_pre_g[l], mlp_post_g[l], w_in[l], gla_wa2[l], gla_ba[l], lam_q1[l], lam_k1[l],
                    lam_q2[l], lam_k2[l], head_norm_g[l], w_out[l], mlp_w1[l], mlp_w2[l])
    return x2.reshape(B, T, D)
```

```python
import functools
import math

import numpy as np
import jax
import jax.numpy as jnp
from jax import lax
from jax.experimental import pallas as pl
from jax.experimental.pallas import tpu as pltpu

F32 = jnp.float32
BF16 = jnp.bfloat16
I32 = jnp.int32
I16 = jnp.int16

D_MODEL = 1024
CHUNK = 64
HEAD_DIM = 64
N_GH = 4
GROUP_W = N_GH * HEAD_DIM
IDX_HEADS = 8
IDX_DIM = 64
TOPK_MAX = 256
RET_THETA = 10000.0
GLA_DK = HEAD_DIM // 2
GLA_RANK = 16
GLA_TAU = 16.0
DIFF_DK = HEAD_DIM // 2
ROPE_THETA = 500000.0
ROPE_FRAC = 4
D_FF = 4 * D_MODEL
EPS = 1e-6
NEG_INF = -1e30

LANES = 128
VMEM_LIMIT = 56 * 1024 * 1024

_IN_SPLITS = (
    GROUP_W, HEAD_DIM, HEAD_DIM, IDX_HEADS * IDX_DIM, IDX_DIM, IDX_HEADS,
    GROUP_W, GROUP_W, GROUP_W, GROUP_W,
    N_GH * GLA_DK, N_GH * GLA_DK, GROUP_W, GLA_RANK, GROUP_W,
    2 * N_GH * DIFF_DK, 2 * N_GH * DIFF_DK, GROUP_W,
)
_IN_OFF = np.concatenate([[0], np.cumsum(_IN_SPLITS)]).tolist()
(_A_Q, _A_K, _A_V, _A_QI, _A_KI, _A_WI, _B_Q, _B_K, _B_V, _B_G,
 _C_Q, _C_K, _C_V, _C_A, _C_G, _D_Q, _D_K, _D_V) = range(18)

SEG_A = 0
SEG_W = 1024
SEG_B = 1152
SEG_G = 1920
SEG_C = 2432
SEG_D = 3072
W_COLS = 3840

NT_DIMS = (((1,), (1,)), ((), ()))
TN_DIMS = (((0,), (0,)), ((), ()))

TM = 512
TQ_DSA = 256
TQ_DIFF = 512
TK = 512
COUNT_ROWS = 512
FLASH_SLAB = 32
VT_ROWS = HEAD_DIM + 16
LOG2E = math.log2(math.e)
TB = 256


def _float_key(v):
    bits = int(np.array(v, np.float32).view(np.int32))
    return -(bits & 0x7FFFFFFF) if bits < 0 else bits


NEG_KEY = _float_key(NEG_INF)
INT_MIN = -(2 ** 31)


def _cparams(*sem):
    return pltpu.CompilerParams(dimension_semantics=sem, vmem_limit_bytes=VMEM_LIMIT)


def _resident(shape, index_map):
    return pl.BlockSpec(shape, index_map, pipeline_mode=pl.Buffered(1))


def _dot(a, b):
    return jnp.dot(a, b, preferred_element_type=F32)


def _dot_nt(a, b):
    return lax.dot_general(a, b, NT_DIMS, preferred_element_type=F32)


def _dot_tn(a, b):
    return lax.dot_general(a, b, TN_DIMS, preferred_element_type=F32)


def _dot_f32(a, b):
    return jnp.dot(a, b, preferred_element_type=F32, precision=lax.Precision.HIGHEST)


def _split_bf16(x, n):
    parts = []
    for _ in range(n):
        p = x.astype(BF16)
        parts.append(p)
        x = x - p.astype(F32)
    return parts


def _dot_split(a, b):
    (a0, a1), (b0, b1) = _split_bf16(a, 2), _split_bf16(b, 2)
    return _dot(a0, b0) + (_dot(a0, b1) + _dot(a1, b0))


def _lane(shape):
    return lax.broadcasted_iota(I32, shape, len(shape) - 1)


def _silu(x):
    return x / (1.0 + jnp.exp(-x))


def _mod_kernel(c_ref, w_ref, b_ref, o_ref):
    cond = _silu(c_ref[...])
    o_ref[...] = _dot_f32(cond, w_ref[...]) + b_ref[...]


def _mod_call(c, mod_w, mod_b):
    B, D = c.shape
    n = mod_w.shape[1] // D
    return pl.pallas_call(
        _mod_kernel,
        out_shape=jax.ShapeDtypeStruct((B, n * D), F32),
        grid=(n,),
        in_specs=[pl.BlockSpec((B, D), lambda j: (0, 0)),
                  pl.BlockSpec((D, D), lambda j: (0, j)),
                  pl.BlockSpec((1, D), lambda j: (0, j))],
        out_specs=pl.BlockSpec((B, D), lambda j: (0, j)),
        compiler_params=_cparams("arbitrary"),
        name="mod",
    )(c, mod_w, mod_b.reshape(1, -1))


def _rope_group(xg, cos, sin, half, period):
    lo = (_lane((1, LANES)) % period) < half
    swapped = jnp.where(lo, pltpu.roll(xg, LANES - half, 1), pltpu.roll(xg, half, 1))
    return xg * cos + swapped * sin


def _in_kernel(x_ref, mod_ref, g_ref, w_ref, ca_ref, sa_ref, cb_ref, sb_ref, cd_ref, sd_ref,
               qa_ref, kva_ref, wa_ref, qkvb_ref, gate_ref, c_ref, d_ref, vta_ref, vtd_ref):
    x = x_ref[...]
    h = x * lax.rsqrt(jnp.mean(x * x, axis=-1, keepdims=True) + EPS) * g_ref[...]
    h = h * (1.0 + mod_ref[0, 1:2, :]) + mod_ref[0, 0:1, :]
    hb = h.astype(BF16)

    def seg(off, width):
        return _dot(hb, w_ref[:, off:off + width])

    lane = _lane((1, LANES))
    low = lane < HEAD_DIM

    ca, sa = ca_ref[...], sa_ref[...]
    a = seg(SEG_A, 1024)
    half_a = HEAD_DIM // ROPE_FRAC // 2
    for gi in range(2):
        r = _rope_group(a[:, gi * LANES:(gi + 1) * LANES], ca, sa, half_a, HEAD_DIM) * (HEAD_DIM ** -0.5 * LOG2E)
        rs = pltpu.roll(r, HEAD_DIM, 1)
        qa_ref[:, (2 * gi) * LANES:(2 * gi + 1) * LANES] = jnp.where(low, r, 0.0).astype(BF16)
        qa_ref[:, (2 * gi + 1) * LANES:(2 * gi + 2) * LANES] = jnp.where(low, rs, 0.0).astype(BF16)
    for gi in range(4):
        r = _rope_group(a[:, (2 + gi) * LANES:(3 + gi) * LANES], ca, sa, half_a, HEAD_DIM)
        rs = pltpu.roll(r, HEAD_DIM, 1)
        qa_ref[:, (4 + 2 * gi) * LANES:(5 + 2 * gi) * LANES] = jnp.where(low, 0.0, rs).astype(BF16)
        qa_ref[:, (5 + 2 * gi) * LANES:(6 + 2 * gi) * LANES] = jnp.where(low, 0.0, r).astype(BF16)
    kva_ref[:, 0:LANES] = _rope_group(a[:, 6 * LANES:7 * LANES], ca, sa, half_a, HEAD_DIM).astype(BF16)
    kva_ref[:, LANES:2 * LANES] = a[:, 7 * LANES:8 * LANES].astype(BF16)
    ones_rows = jnp.ones((VT_ROWS - HEAD_DIM, TM), BF16)
    vta_ref[0:HEAD_DIM, :] = jnp.transpose(a[:, 7 * LANES:8 * LANES])[0:HEAD_DIM, :].astype(BF16)
    vta_ref[HEAD_DIM:VT_ROWS, :] = ones_rows

    wa_ref[...] = seg(SEG_W, LANES)

    cb, sb = cb_ref[...], sb_ref[...]
    b = seg(SEG_B, 768)
    for gi in range(4):
        r = _rope_group(b[:, gi * LANES:(gi + 1) * LANES], cb, sb, HEAD_DIM // 2, HEAD_DIM)
        if gi < 2:
            r = r * (HEAD_DIM ** -0.5)
        qkvb_ref[:, gi * LANES:(gi + 1) * LANES] = r.astype(BF16)
    qkvb_ref[:, 512:768] = b[:, 512:768].astype(BF16)

    gate_ref[...] = seg(SEG_G, 512)
    c_ref[...] = seg(SEG_C, 640)

    cd, sd = cd_ref[...], sd_ref[...]
    d = seg(SEG_D, 768)
    half_d = DIFF_DK // ROPE_FRAC // 2
    for gi in range(4):
        r = _rope_group(d[:, gi * LANES:(gi + 1) * LANES], cd, sd, half_d, DIFF_DK)
        if gi < 2:
            r = r * (DIFF_DK ** -0.5 * LOG2E)
        d_ref[:, gi * LANES:(gi + 1) * LANES] = r.astype(BF16)
    d_ref[:, 512:768] = d[:, 512:768].astype(BF16)
    vt = jnp.transpose(d[:, 512:768])
    for h in range(N_GH):
        vtd_ref[h * VT_ROWS:h * VT_ROWS + HEAD_DIM, :] = vt[h * HEAD_DIM:(h + 1) * HEAD_DIM, :].astype(BF16)
        vtd_ref[h * VT_ROWS + HEAD_DIM:(h + 1) * VT_ROWS, :] = ones_rows


def _in_call(x2, mod3, g, w_p, tabs, T):
    N, D = x2.shape
    nt = T // TM
    row = lambda i: (i, 0)
    tab_spec = pl.BlockSpec((TM, LANES), row)
    outs = [(1536, BF16), (256, BF16), (LANES, F32), (768, BF16), (512, F32), (640, F32), (768, BF16)]
    vt_rows = (VT_ROWS, N_GH * VT_ROWS)
    vt_shapes = [jax.ShapeDtypeStruct((N // T * r, T), BF16) for r in vt_rows]
    vt_specs = [pl.BlockSpec((r, TM), lambda i: (i // nt, i % nt)) for r in vt_rows]
    return pl.pallas_call(
        _in_kernel,
        out_shape=[jax.ShapeDtypeStruct((N, w), dt) for w, dt in outs] + vt_shapes,
        grid=(N // TM,),
        in_specs=[pl.BlockSpec((TM, D), row),
                  pl.BlockSpec((1, 6, D), lambda i: (i // nt, 0, 0)),
                  _resident((1, D), lambda i: (0, 0)),
                  _resident((D, W_COLS), lambda i: (0, 0))] + [tab_spec] * 6,
        out_specs=[pl.BlockSpec((TM, w), row) for w, _ in outs] + vt_specs,
        compiler_params=_cparams("arbitrary"),
        name="in_proj",
    )(x2, mod3, g, w_p, *tabs)


def _pair_head_norm(o, gain, center):
    low = _lane((1, LANES)) < HEAD_DIM
    inv = 1.0 / HEAD_DIM

    def seg_mean(v):
        m0 = jnp.sum(jnp.where(low, v, 0.0), axis=1, keepdims=True)
        m1 = jnp.sum(jnp.where(low, 0.0, v), axis=1, keepdims=True)
        return jnp.where(low, m0, m1) * inv

    if center:
        o = o - seg_mean(o)
    return o * lax.rsqrt(seg_mean(o * o) + EPS) * gain


class _Flash:
    def __init__(self, n, s_ref, p_ref, acc_ref, st_ref):
        self.n, self.s, self.p, self.acc, self.st = n, s_ref, p_ref, acc_ref, st_ref
        self.tq = s_ref.shape[-1]

    @staticmethod
    def scratch(n, tq):
        return [pltpu.VMEM((n, TK, tq), F32), pltpu.VMEM((n, TK, tq), BF16),
                pltpu.VMEM((n, VT_ROWS, tq), F32), pltpu.VMEM((3, n, tq), F32)]

    def _row(self, k, i):
        return self.st.at[k, i:i + 1, :]

    def init(self):
        self.st[0] = jnp.full((self.n, self.tq), NEG_INF, F32)
        self.acc[...] = jnp.zeros_like(self.acc)

    def scores(self, i, s):
        self.s[i] = s
        self._row(1, i)[...] = jnp.max(_fold8(s, jnp.maximum), axis=0, keepdims=True)

    def probs(self, i):
        m_old = self._row(0, i)[...]
        m_new = jnp.maximum(m_old, self._row(1, i)[...])
        for r in range(0, TK, FLASH_SLAB):
            self.p[i, r:r + FLASH_SLAB, :] = jnp.exp2(self.s[i, r:r + FLASH_SLAB, :] - m_new).astype(BF16)
        self._row(0, i)[...] = m_new
        self._row(2, i)[...] = jnp.exp2(m_old - m_new)

    def values(self, i, vt):
        self.acc[i] = self._row(2, i)[...] * self.acc[i] + _dot(vt, self.p[i])

    def result(self, i):
        return self.acc[i, 0:HEAD_DIM, :] / self.acc[i, HEAD_DIM:HEAD_DIM + 1, :]

    def pipeline(self, first, n_rest, score, value):
        score(first, True)

        def body(t, carry):
            for i in range(self.n):
                self.probs(i)
            score(t, False)
            value(jnp.where(t == 0, first, t - 1))
            return carry

        lax.fori_loop(0, n_rest, body, 0)
        for i in range(self.n):
            self.probs(i)
        value(jnp.where(n_rest == 0, first, n_rest - 1))


def _fold8(x, op=jnp.add, rows=8):
    parts = [x[r:r + rows] for r in range(0, x.shape[0], rows)]
    while len(parts) > 1:
        parts = [op(a, b) for a, b in zip(parts[0::2], parts[1::2])]
    return parts[0]


def _head_norm_t(o, gain):
    ms = jnp.mean(o * o, axis=0, keepdims=True)
    return o * lax.rsqrt(ms + EPS) * gain


def _dsa_kernel(q_ref, w_ref, kv_ref, vt_ref, g_ref, ridx_ref, o_ref, bias_ref, key_ref, hi_ref, lo_ref, u_ref,
                *flash_refs, topk):
    TQ = TQ_DSA
    q0 = pl.program_id(1) * TQ
    nkt = (q0 + TQ + TK - 1) // TK
    qchunk = (q0 + _lane((1, TQ))) // CHUNK
    wts = jnp.transpose(w_ref[...]) * (IDX_HEADS ** -0.5 * IDX_DIM ** -0.5)

    def score_tile(kt, carry):
        k0 = pl.multiple_of(kt * TK, TK)
        kk = kv_ref[pl.ds(k0, TK), 0:LANES]
        tile = bias_ref.at[pl.ds(k0, TK), :]
        for h in range(IDX_HEADS):
            term = jnp.maximum(_dot_nt(kk, q_ref[:, (4 + h) * LANES:(5 + h) * LANES]), 0.0) * wts[h:h + 1, :]
            if h == 0:
                tile[...] = term
            elif h < IDX_HEADS - 1:
                tile[...] += term
        kchunk = (k0 + lax.broadcasted_iota(I32, (TK, 1), 0)) // CHUNK
        acc = jnp.where(kchunk <= qchunk, tile[...] + term, NEG_INF)
        bits = pltpu.bitcast(acc, I32)
        sign = bits >> 31
        key = ((bits & 0x7FFFFFFF) ^ sign) - sign
        key_ref[pl.ds(k0, TK), :] = key
        hi_ref[pl.ds(k0, TK), :] = (key >> 16).astype(I16)
        lo_ref[pl.ds(k0, TK), :] = (key ^ 0x8000).astype(I16)
        return carry

    lax.fori_loop(0, nkt, score_tile, 0)

    cr = min(COUNT_ROWS, key_ref.shape[0])
    n_count = (nkt * TK + cr - 1) // cr

    def pad_tile(kt, carry):
        k0 = pl.multiple_of(kt * TK, TK)
        hi_ref[pl.ds(k0, TK), :] = jnp.full((TK, TQ), -32768, I16)
        lo_ref[pl.ds(k0, TK), :] = jnp.full((TK, TQ), -32768, I16)
        return carry

    lax.fori_loop(nkt, n_count * (cr // TK), pad_tile, 0)
    kf = float(topk)

    def count16(ref, cand):
        def body(kc, acc):
            k0 = pl.multiple_of(kc * cr, cr)
            hit = jnp.where(ref[pl.ds(k0, cr), :] >= cand, jnp.ones((cr, TQ), I16), jnp.zeros((cr, TQ), I16))
            return acc + _fold8(hit, rows=16)
        acc = lax.fori_loop(0, n_count, body, jnp.zeros((16, TQ), I16))
        return jnp.sum(acc.astype(F32), axis=0, keepdims=True)

    def search16(ref, cnt_all, signed=True, bits=15):
        if signed:
            c0 = count16(ref, jnp.zeros((1, TQ), I16))
            t = jnp.where(c0 >= kf, 0, -32768).astype(I32)
            cnt = jnp.where(c0 >= kf, c0, cnt_all)
        else:
            t, cnt = jnp.zeros((1, TQ), I32), cnt_all

        def step(it, carry):
            t, cnt = carry
            cand = t + (jnp.int32(1) << (bits - 1 - it))
            c = count16(ref, cand.astype(I16))
            ok = c >= kf
            return jnp.where(ok, cand, t), jnp.where(ok, c, cnt)

        return lax.fori_loop(0, bits, step, (t, cnt))

    t_hi, cnt_hi = search16(hi_ref, (nkt * TK).astype(F32))
    t_hi16 = t_hi.astype(I16)

    top = jnp.full((cr, TQ), 32767, I16)
    bottom = jnp.full((cr, TQ), -32768, I16)

    def low_digit_tile(kc, carry):
        k0 = pl.multiple_of(kc * cr, cr)
        hi = hi_ref[pl.ds(k0, cr), :]
        in_bucket = jnp.where(hi == t_hi16, lo_ref[pl.ds(k0, cr), :], bottom)
        u_ref[pl.ds(k0, cr), :] = jnp.where(hi > t_hi16, top, in_bucket)
        return carry

    lax.fori_loop(0, n_count, low_digit_tile, 0)
    t_lo, cnt_lo = search16(u_ref, cnt_hi)
    t_lo16 = t_lo.astype(I16)
    thr = t_hi * 65536 + (t_lo + 32768)

    def index_digit_tile(kc, carry):
        k0 = pl.multiple_of(kc * cr, cr)
        hi, lo = hi_ref[pl.ds(k0, cr), :], lo_ref[pl.ds(k0, cr), :]
        same_hi = jnp.where(lo == t_lo16, ridx_ref[pl.ds(k0, cr), :], jnp.where(lo > t_lo16, top, bottom))
        u_ref[pl.ds(k0, cr), :] = jnp.where(hi == t_hi16, same_hi, jnp.where(hi > t_hi16, top, bottom))
        return carry

    n_keys = key_ref.shape[0]
    floor_key = jnp.int32(NEG_KEY + 1)
    short = thr < floor_key

    def index_search():
        lax.fori_loop(0, n_count, index_digit_tile, 0)
        return search16(u_ref, cnt_lo, signed=False, bits=(n_keys - 1).bit_length())[0]

    any_tie = jnp.max(jnp.where(short, 0.0, cnt_lo - kf)) > 0.0
    t_idx = lax.cond(any_tie, index_search, lambda: jnp.zeros((1, TQ), I32))
    lim_tie = jnp.where(short, floor_key, thr)
    lim_late = jnp.where(short, floor_key, thr + 1)
    last_tie = jnp.where(short, -1, (n_keys - 1) - t_idx)

    def select_tile(kt, carry):
        k0 = pl.multiple_of(kt * TK, TK)
        idx = k0 + lax.broadcasted_iota(I32, (TK, 1), 0)
        lim = jnp.where(idx <= last_tie, lim_tie, lim_late)
        bias_ref[pl.ds(k0, TK), :] = jnp.where(key_ref[pl.ds(k0, TK), :] >= lim, 0.0, NEG_INF)
        return carry

    lax.fori_loop(0, nkt, select_tile, 0)

    flash = _Flash(N_GH, *flash_refs)
    flash.init()

    def score(kt, is_first):
        k0 = pl.multiple_of(kt * TK, TK)
        kk = kv_ref[pl.ds(k0, TK), 0:LANES]
        bias = bias_ref[pl.ds(k0, TK), :]
        for h in range(N_GH):
            flash.scores(h, _dot_nt(kk, q_ref[:, h * LANES:(h + 1) * LANES]) + bias)

    def value(kt):
        k0 = pl.multiple_of(kt * TK, TK)
        for h in range(N_GH):
            flash.values(h, vt_ref[:, pl.ds(k0, TK)])

    flash.pipeline(nkt - 1, nkt - 1, score, value)
    y = jnp.concatenate([_head_norm_t(flash.result(h), g_ref[h * HEAD_DIM:(h + 1) * HEAD_DIM, :])
                         for h in range(N_GH)], axis=0)
    o_ref[...] = jnp.transpose(y).astype(BF16)


def _dsa_call(qa, wa, kva, vta, g_t, B, T):
    TQ = TQ_DSA
    nq = T // TQ
    ridx = jnp.broadcast_to((T - 1 - jnp.arange(T, dtype=I32)).astype(I16)[:, None], (T, TQ))
    topk = min(TOPK_MAX, T // 4)
    return pl.pallas_call(
        functools.partial(_dsa_kernel, topk=topk),
        out_shape=jax.ShapeDtypeStruct((B * T, GROUP_W), BF16),
        grid=(B, nq),
        in_specs=[pl.BlockSpec((TQ, 1536), lambda b, i: (b * nq + i, 0)),
                  pl.BlockSpec((TQ, LANES), lambda b, i: (b * nq + i, 0)),
                  pl.BlockSpec((T, 256), lambda b, i: (b, 0)),
                  pl.BlockSpec((VT_ROWS, T), lambda b, i: (b, 0)),
                  pl.BlockSpec((GROUP_W, TQ), lambda b, i: (0, 0)),
                  _resident((T, TQ), lambda b, i: (0, 0))],
        out_specs=pl.BlockSpec((TQ, GROUP_W), lambda b, i: (b * nq + i, 0)),
        scratch_shapes=[pltpu.VMEM((T, TQ), F32), pltpu.VMEM((T, TQ), I32), pltpu.VMEM((T, TQ), I16),
                        pltpu.VMEM((T, TQ), I16), pltpu.VMEM((T, TQ), I16)] + _Flash.scratch(N_GH, TQ),
        compiler_params=_cparams("arbitrary", "arbitrary"),
        name="dsa",
    )(qa, wa, kva, vta, g_t, ridx)


def _diff_kernel(q_ref, k_ref, vt_ref, lam_ref, g_ref, o_ref, qm_ref, *flash_refs, lam_init):
    TQ = TQ_DIFF
    q0 = pl.program_id(1) * TQ
    nfull = (q0 + CHUNK) // TK
    qchunk = (q0 + _lane((1, TQ))) // CHUNK
    lv = lam_ref[...]
    lam = (jnp.exp(jnp.sum(lv[0:1] * lv[1:2], axis=1, keepdims=True))
           - jnp.exp(jnp.sum(lv[2:3] * lv[3:4], axis=1, keepdims=True)) + lam_init)
    n_maps = 2 * N_GH
    qt = jnp.transpose(q_ref[...].astype(F32))
    sub = lax.broadcasted_iota(I32, (LANES, 1), 0) // DIFF_DK
    for mi in range(n_maps):
        qg = qt[(mi // 4) * LANES:(mi // 4 + 1) * LANES, :]
        qm_ref[mi] = jnp.where(sub == mi % 4, qg, 0.0).astype(BF16)

    flash = _Flash(n_maps, *flash_refs)
    flash.init()

    def score(kt, masked):
        k0 = pl.multiple_of(kt * TK, TK)
        if masked:
            kchunk = (k0 + lax.broadcasted_iota(I32, (TK, 1), 0)) // CHUNK
            admissible = kchunk <= qchunk
        for mi in range(n_maps):
            g = mi // 4
            s = _dot(k_ref[pl.ds(k0, TK), g * LANES:(g + 1) * LANES], qm_ref[mi])
            flash.scores(mi, jnp.where(admissible, s, NEG_INF) if masked else s)

    def value(kt):
        k0 = pl.multiple_of(kt * TK, TK)
        for mi in range(n_maps):
            h = mi // 2
            flash.values(mi, vt_ref[h * VT_ROWS:(h + 1) * VT_ROWS, pl.ds(k0, TK)])

    flash.pipeline(nfull, nfull, score, value)
    outs = []
    for h in range(N_GH):
        o = flash.result(2 * h) - lam * flash.result(2 * h + 1)
        outs.append(_head_norm_t(o, g_ref[h * HEAD_DIM:(h + 1) * HEAD_DIM, :]) * (1.0 - lam_init))
    o_ref[...] = jnp.transpose(jnp.concatenate(outs, axis=0)).astype(BF16)


def _diff_call(dqkv, vtd, lam_vecs, g_t, lam_init, B, T):
    TQ = TQ_DIFF
    nq = T // TQ
    return pl.pallas_call(
        functools.partial(_diff_kernel, lam_init=lam_init),
        out_shape=jax.ShapeDtypeStruct((B * T, GROUP_W), BF16),
        grid=(B, nq),
        in_specs=[pl.BlockSpec((TQ, 256), lambda b, i: (b * nq + i, 0)),
                  pl.BlockSpec((T, 256), lambda b, i: (b, 1)),
                  pl.BlockSpec((N_GH * VT_ROWS, T), lambda b, i: (b, 0)),
                  pl.BlockSpec((4, LANES), lambda b, i: (0, 0)),
                  pl.BlockSpec((GROUP_W, TQ), lambda b, i: (0, 0))],
        out_specs=pl.BlockSpec((TQ, GROUP_W), lambda b, i: (b * nq + i, 0)),
        scratch_shapes=[pltpu.VMEM((2 * N_GH, LANES, TQ), BF16)] + _Flash.scratch(2 * N_GH, TQ),
        compiler_params=_cparams("arbitrary", "arbitrary"),
        name="diff",
    )(dqkv, dqkv, vtd, lam_vecs, g_t)


def _ret_tables():
    h = np.arange(N_GH, dtype=np.float32)
    log_g = jnp.log(1.0 - 2.0 ** (-5.0 - jnp.asarray(h)))
    t = np.arange(TB)
    same = (t[:, None] // CHUNK) == (t[None, :] // CHUNK)
    earlier = (t[None, :] // CHUNK) < (t[:, None] // CHUNK)
    dist = jnp.asarray(np.where(same, np.abs(t[:, None] - t[None, :]), t[:, None] - t[None, :]), F32)
    dmat = jnp.where(jnp.asarray(same | earlier)[None],
                     jnp.exp(dist[None] * log_g[:, None, None]), 0.0)
    tt = jnp.asarray(t, F32)
    xi = jnp.exp((tt + 1.0)[None, :] * log_g[:, None])
    zeta = jnp.exp((TB - 1.0 - tt)[None, :] * log_g[:, None])
    gblk = jnp.exp(TB * log_g)

    def lanes(tab):
        return jnp.repeat(tab.reshape(2, 2, TB), HEAD_DIM, axis=1).transpose(0, 2, 1)

    r = np.arange(LANES)
    blockdiag = (r[:, None] // HEAD_DIM) == (r[None, :] // HEAD_DIM)
    gb = jnp.where(jnp.asarray(blockdiag)[None],
                   jnp.repeat(gblk.reshape(2, 2), HEAD_DIM, axis=1)[:, :, None], 0.0)
    return dmat, lanes(xi), lanes(zeta), gb


def _ret_kernel(qkv_ref, gate_ref, dmat_ref, xi_ref, zeta_ref, gb_ref, g_ref, o_ref, r_ref):
    @pl.when(pl.program_id(1) == 0)
    def _():
        r_ref[...] = jnp.zeros_like(r_ref)

    lane = _lane((1, LANES))
    low = lane < HEAD_DIM
    r_idx = lax.broadcasted_iota(I32, (LANES, LANES), 0) // HEAD_DIM
    c_idx = lax.broadcasted_iota(I32, (LANES, LANES), 1) // HEAD_DIM
    for g in range(2):
        qg = qkv_ref[:, g * LANES:(g + 1) * LANES]
        kg = qkv_ref[:, 256 + g * LANES:256 + (g + 1) * LANES]
        vg = qkv_ref[:, 512 + g * LANES:512 + (g + 1) * LANES]
        state = r_ref[g]
        inter = _dot(qg, state.astype(BF16)) * xi_ref[g]
        parts = []
        for hh in range(2):
            qm = jnp.where(low, qg, jnp.zeros_like(qg)) if hh == 0 else jnp.where(low, jnp.zeros_like(qg), qg)
            s = _dot_nt(qm, kg) * dmat_ref[2 * g + hh]
            parts.append(_dot(s.astype(BF16), vg))
        o = jnp.where(low, parts[0], parts[1]) + inter
        kz = (kg.astype(F32) * zeta_ref[g]).astype(BF16)
        upd = jnp.where(r_idx == c_idx, _dot_tn(kz, vg), 0.0)
        r_ref[g] = state * gb_ref[g] + upd
        y = _pair_head_norm(o, g_ref[:, g * LANES:(g + 1) * LANES], True)
        o_ref[:, g * LANES:(g + 1) * LANES] = (_silu(gate_ref[:, g * LANES:(g + 1) * LANES]) * y).astype(BF16)


def _ret_specs(nb):
    const3 = lambda b, i: (0, 0, 0)
    in_specs = [pl.BlockSpec((TB, 768), lambda b, i: (b * nb + i, 0)),
                pl.BlockSpec((TB, GROUP_W), lambda b, i: (b * nb + i, 0)),
                pl.BlockSpec((N_GH, TB, TB), const3),
                pl.BlockSpec((2, TB, LANES), const3),
                pl.BlockSpec((2, TB, LANES), const3),
                pl.BlockSpec((2, LANES, LANES), const3),
                pl.BlockSpec((1, GROUP_W), lambda b, i: (0, 0))]
    return in_specs, [pltpu.VMEM((2, LANES, LANES), F32)]


def _gla_kernel(c_ref, gate_ref, wa2_ref, ba_ref, g_ref, o_ref, s_ref, b_scr, q_scr, k_scr, v_scr, acc_scr):
    @pl.when(pl.program_id(1) == 0)
    def _():
        s_ref[...] = jnp.zeros_like(s_ref)

    n_chunks = TB // CHUNK
    hmask = (lax.broadcasted_iota(I32, (LANES, GROUP_W), 0) // GLA_DK
             == lax.broadcasted_iota(I32, (LANES, GROUP_W), 1) // HEAD_DIM)
    hsum = jnp.where(hmask, 1.0, 0.0).astype(BF16)
    r_idx = lax.broadcasted_iota(I32, (TB, TB), 0)
    c_idx = lax.broadcasted_iota(I32, (TB, TB), 1)
    tri = jnp.where((r_idx >= c_idx) & (r_idx // CHUNK == c_idx // CHUNK), 1.0, 0.0).astype(BF16)

    z = _dot_split(c_ref[:, 512:640], wa2_ref[...]) + ba_ref[...]
    log_a = (jnp.minimum(z, 0.0) - jnp.log(1.0 + jnp.exp(-jnp.abs(z)))) * (1.0 / GLA_TAU)
    b = sum(_dot(tri, piece) for piece in _split_bf16(log_a, 3))
    q = c_ref[:, 0:128] * (GLA_DK ** -0.5)
    b_scr[...] = b * LOG2E
    q_scr[...] = q
    k_scr[...] = c_ref[:, 128:256]
    v_scr[...] = c_ref[:, 256:512]

    b_last = [b[(c + 1) * CHUNK - 1:(c + 1) * CHUNK, :] for c in range(n_chunks)]
    b_end = jnp.concatenate([jnp.broadcast_to(bl, (CHUNK, LANES)) for bl in b_last], axis=0)
    kd = (c_ref[:, 128:256] * jnp.exp(b_end - b)).astype(BF16)
    qe = (q * jnp.exp(b)).astype(BF16)
    vb = c_ref[:, 256:512].astype(BF16)
    state = s_ref[...]
    inters = []
    for c in range(n_chunks):
        rows = slice(c * CHUNK, (c + 1) * CHUNK)
        inters.append(_dot(qe[rows], state.astype(BF16)))
        scale = jnp.transpose(jnp.broadcast_to(jnp.exp(b_last[c]), (LANES, LANES)))
        state = jnp.where(hmask, state * jnp.concatenate([scale, scale], axis=1) + _dot_tn(kd[rows], vb[rows]), 0.0)
    s_ref[...] = state

    acc_scr[...] = jnp.concatenate(inters, axis=0)

    keys_per_step = 64

    def key_step(s4, carry):
        for c in range(n_chunks):
            rows = slice(c * CHUNK, (c + 1) * CHUNK)
            part = None
            for u in range(keys_per_step):
                r = c * CHUNK + s4 * keys_per_step + u
                gap = pltpu.bitcast(b_scr[rows, :] - b_scr[pl.ds(r, 1), :], I32)
                decay = jnp.exp2(pltpu.bitcast(gap | jnp.int32(INT_MIN), F32))
                w = (q_scr[rows, :] * decay * k_scr[pl.ds(r, 1), :]).astype(BF16)
                term = _dot(w, hsum) * v_scr[pl.ds(r, 1), :]
                part = term if part is None else part + term
            acc_scr[rows, :] += part
        return carry

    lax.fori_loop(0, CHUNK // keys_per_step, key_step, 0)
    o = acc_scr[...]
    for gi in range(2):
        y = _pair_head_norm(o[:, gi * LANES:(gi + 1) * LANES], g_ref[:, gi * LANES:(gi + 1) * LANES], True)
        o_ref[:, gi * LANES:(gi + 1) * LANES] = (_silu(gate_ref[:, gi * LANES:(gi + 1) * LANES]) * y).astype(BF16)


def _gla_specs(nb):
    in_specs = [pl.BlockSpec((TB, 640), lambda b, i: (b * nb + i, 0)),
                pl.BlockSpec((TB, GROUP_W), lambda b, i: (b * nb + i, 1)),
                pl.BlockSpec((LANES, LANES), lambda b, i: (0, 0)),
                pl.BlockSpec((1, LANES), lambda b, i: (0, 0)),
                pl.BlockSpec((1, GROUP_W), lambda b, i: (0, 0))]
    scratch = [pltpu.VMEM((LANES, GROUP_W), F32), pltpu.VMEM((TB, LANES), F32),
               pltpu.VMEM((TB, LANES), F32), pltpu.VMEM((TB, LANES), F32),
               pltpu.VMEM((TB, GROUP_W), F32), pltpu.VMEM((TB, GROUP_W), F32)]
    return in_specs, scratch


def _ret_gla_kernel(*refs):
    n_ret, n_gla = 7, 5
    o_b, o_c = refs[n_ret + n_gla:n_ret + n_gla + 2]
    scratch = refs[n_ret + n_gla + 2:]
    _ret_kernel(*refs[:n_ret], o_b, scratch[0])
    _gla_kernel(*refs[n_ret:n_ret + n_gla], o_c, *scratch[1:])


def _ret_gla_call(qkvb, gates, ret_tabs, g_b, cc, wa2_p, ba, g_c, B, T):
    nb = T // TB
    ret_in, ret_scratch = _ret_specs(nb)
    gla_in, gla_scratch = _gla_specs(nb)
    out = jax.ShapeDtypeStruct((B * T, GROUP_W), BF16)
    out_spec = pl.BlockSpec((TB, GROUP_W), lambda b, i: (b * nb + i, 0))
    return pl.pallas_call(
        _ret_gla_kernel,
        out_shape=[out, out],
        grid=(B, nb),
        in_specs=ret_in + gla_in,
        out_specs=[out_spec, out_spec],
        scratch_shapes=ret_scratch + gla_scratch,
        compiler_params=_cparams("arbitrary", "arbitrary"),
        name="ret_gla",
    )(qkvb, gates, *ret_tabs, g_b, cc, gates, wa2_p, ba, g_c)


def _rms(y, g):
    return y * lax.rsqrt(jnp.mean(y * y, axis=-1, keepdims=True) + EPS) * g


def _out_mlp_kernel(x_ref, ya_ref, yb_ref, yc_ref, yd_ref, mod_ref, wo_ref, w1_ref, w2_ref,
                    gpost_ref, gpre_ref, gpost2_ref, o_ref):
    y = _dot(ya_ref[...], wo_ref[0:256, :])
    y = y + _dot(yb_ref[...], wo_ref[256:512, :])
    y = y + _dot(yc_ref[...], wo_ref[512:768, :])
    y = y + _dot(yd_ref[...], wo_ref[768:1024, :])
    x = x_ref[...] + mod_ref[0, 2:3, :] * _rms(y, gpost_ref[...])
    h = _rms(x, gpre_ref[...]) * (1.0 + mod_ref[0, 4:5, :]) + mod_ref[0, 3:4, :]
    hb = h.astype(BF16)
    acc = jnp.zeros(x.shape, F32)
    fc = 1024
    for f in range(D_FF // fc):
        u = jnp.maximum(_dot(hb, w1_ref[:, f * fc:(f + 1) * fc]), 0.0)
        acc = acc + _dot((u * u).astype(BF16), w2_ref[f * fc:(f + 1) * fc, :])
    o_ref[...] = x + mod_ref[0, 5:6, :] * _rms(acc, gpost2_ref[...])


def _out_mlp_call(x2, ys, mod3, w_out, w1, w2, g_post, g_pre2, g_post2, T):
    N, D = x2.shape
    nt = T // TM
    row = lambda i: (i, 0)
    const = lambda i: (0, 0)
    return pl.pallas_call(
        _out_mlp_kernel,
        out_shape=jax.ShapeDtypeStruct((N, D), F32),
        grid=(N // TM,),
        in_specs=[pl.BlockSpec((TM, D), row)] + [pl.BlockSpec((TM, GROUP_W), row)] * 4
                 + [pl.BlockSpec((1, 6, D), lambda i: (i // nt, 0, 0)),
                    _resident((D, D), const), _resident((D, D_FF), const), _resident((D_FF, D), const),
                    _resident((1, D), const), _resident((1, D), const), _resident((1, D), const)],
        out_specs=pl.BlockSpec((TM, D), row),
        compiler_params=_cparams("arbitrary"),
        name="out_mlp",
    )(x2, *ys, mod3, w_out, w1, w2, g_post, g_pre2, g_post2)


def _relayout_w_in(w_in):
    col = lambda k: w_in[:, _IN_OFF[k]:_IN_OFF[k + 1]]
    zeros = lambda n: jnp.zeros((w_in.shape[0], n), w_in.dtype)
    parts = [col(_A_Q), col(_A_QI), col(_A_K), col(_A_KI), col(_A_V), zeros(64),
             col(_A_WI), zeros(120),
             col(_B_Q), col(_B_K), col(_B_V),
             col(_B_G), col(_C_G),
             col(_C_Q), col(_C_K), col(_C_V), col(_C_A), zeros(112),
             col(_D_Q), col(_D_K), col(_D_V)]
    return jnp.concatenate(parts, axis=1).astype(BF16)


def _rope_tables(pos, rot, period, theta):
    half = rot // 2
    inv = theta ** (-jnp.arange(half, dtype=F32) / half)
    ang = pos.astype(F32)[..., None] * inv
    cos, sin = jnp.cos(ang), jnp.sin(ang)
    pad = ang.shape[:-1] + (period - rot,)
    ct = jnp.concatenate([cos, cos, jnp.ones(pad, F32)], axis=-1)
    st = jnp.concatenate([-sin, sin, jnp.zeros(pad, F32)], axis=-1)
    reps = LANES // period
    return (jnp.tile(ct, (1, 1, reps)).reshape(-1, LANES), jnp.tile(st, (1, 1, reps)).reshape(-1, LANES))


def _layer(x2, cond, tabs, ret_tabs, layer_idx, B, T, mod_w, mod_b, attn_pre_g, attn_post_g,
           mlp_pre_g, mlp_post_g, w_in, gla_wa2, gla_ba, lam_q1, lam_k1, lam_q2, lam_k2,
           head_norm_g, w_out, mlp_w1, mlp_w2):
    D = D_MODEL
    mod3 = _mod_call(cond, mod_w, mod_b).reshape(B, 6, D)
    qa, kva, wa, qkvb, gates, cc, dqkv, vta, vtd = _in_call(
        x2, mod3, attn_pre_g.reshape(1, D), _relayout_w_in(w_in), tabs, T)

    g_a, g_b, g_c, g_d = jnp.split(head_norm_g, 4)
    gain_t = lambda g, tq: jnp.broadcast_to(g[:, None], (GROUP_W, tq))
    y_a = _dsa_call(qa, wa, kva, vta, gain_t(g_a, TQ_DSA), B, T)
    wa2_p = jnp.zeros((LANES, LANES), F32).at[:GLA_RANK].set(gla_wa2)
    y_b, y_c = _ret_gla_call(qkvb, gates, ret_tabs, g_b.reshape(1, GROUP_W), cc, wa2_p,
                             gla_ba.reshape(1, LANES), g_c.reshape(1, GROUP_W), B, T)
    lam_init = 0.8 - 0.6 * math.exp(-0.3 * layer_idx)
    lam_vecs = jnp.zeros((4, LANES), F32).at[:, :DIFF_DK].set(jnp.stack([lam_q1, lam_k1, lam_q2, lam_k2]))
    y_d = _diff_call(dqkv, vtd, lam_vecs, gain_t(g_d, TQ_DIFF), lam_init, B, T)

    return _out_mlp_call(x2, (y_a, y_b, y_c, y_d), mod3, w_out.astype(BF16), mlp_w1.astype(BF16),
                         mlp_w2.astype(BF16), attn_post_g.reshape(1, D), mlp_pre_g.reshape(1, D),
                         mlp_post_g.reshape(1, D), T)


def kernel(x, c, positions, mod_w, mod_b, attn_pre_g, attn_post_g, mlp_pre_g, mlp_post_g, w_in,
           gla_wa2, gla_ba, lam_q1, lam_k1, lam_q2, lam_k2, head_norm_g, w_out, mlp_w1, mlp_w2):
    B, T, D = x.shape
    assert D == D_MODEL and T % TM == 0 and T % TB == 0
    tabs = (_rope_tables(positions, HEAD_DIM // ROPE_FRAC, HEAD_DIM, ROPE_THETA)
            + _rope_tables(positions, HEAD_DIM, HEAD_DIM, RET_THETA)
            + _rope_tables(positions, DIFF_DK // ROPE_FRAC, DIFF_DK, ROPE_THETA))
    ret_tabs = _ret_tables()
    x2 = x.reshape(B * T, D)
    for l in range(mod_w.shape[0]):
        x2 = _layer(x2, c, tabs, ret_tabs, l, B, T, mod_w[l], mod_b[l], attn_pre_g[l], attn_post_g[l],
                    mlp_pre_g[l], mlp_post_g[l], w_in[l], gla_wa2[l], gla_ba[l], lam_q1[l], lam_k1[l],
                    lam_q2[l], lam_k2[l], head_norm_g[l], w_out[l], mlp_w1[l], mlp_w2[l])
    return x2.reshape(B, T, D)
```

```python
import functools
import math

import numpy as np
import jax
import jax.numpy as jnp
from jax import lax
from jax.experimental import pallas as pl
from jax.experimental.pallas import tpu as pltpu

F32 = jnp.float32
BF16 = jnp.bfloat16
I32 = jnp.int32
I16 = jnp.int16

D_MODEL = 1024
CHUNK = 64
HEAD_DIM = 64
N_GH = 4
GROUP_W = N_GH * HEAD_DIM
IDX_HEADS = 8
IDX_DIM = 64
TOPK_MAX = 256
RET_THETA = 10000.0
GLA_DK = HEAD_DIM // 2
GLA_RANK = 16
GLA_TAU = 16.0
DIFF_DK = HEAD_DIM // 2
ROPE_THETA = 500000.0
ROPE_FRAC = 4
D_FF = 4 * D_MODEL
EPS = 1e-6
NEG_INF = -1e30

LANES = 128
VMEM_LIMIT = 56 * 1024 * 1024

_IN_SPLITS = (
    GROUP_W, HEAD_DIM, HEAD_DIM, IDX_HEADS * IDX_DIM, IDX_DIM, IDX_HEADS,
    GROUP_W, GROUP_W, GROUP_W, GROUP_W,
    N_GH * GLA_DK, N_GH * GLA_DK, GROUP_W, GLA_RANK, GROUP_W,
    2 * N_GH * DIFF_DK, 2 * N_GH * DIFF_DK, GROUP_W,
)
_IN_OFF = np.concatenate([[0], np.cumsum(_IN_SPLITS)]).tolist()
(_A_Q, _A_K, _A_V, _A_QI, _A_KI, _A_WI, _B_Q, _B_K, _B_V, _B_G,
 _C_Q, _C_K, _C_V, _C_A, _C_G, _D_Q, _D_K, _D_V) = range(18)

SEG_A = 0
SEG_W = 1024
SEG_B = 1152
SEG_G = 1920
SEG_C = 2432
SEG_D = 3072
W_COLS = 3840

NT_DIMS = (((1,), (1,)), ((), ()))
TN_DIMS = (((0,), (0,)), ((), ()))

TM = 512
TQ_DSA = 256
TQ_DIFF = 512
TK = 512
COUNT_ROWS = 512
FLASH_SLAB = 32
VT_ROWS = HEAD_DIM + 16
LOG2E = math.log2(math.e)
TB = 256


def _float_key(v):
    bits = int(np.array(v, np.float32).view(np.int32))
    return -(bits & 0x7FFFFFFF) if bits < 0 else bits


NEG_KEY = _float_key(NEG_INF)
INT_MIN = -(2 ** 31)


def _cparams(*sem):
    return pltpu.CompilerParams(dimension_semantics=sem, vmem_limit_bytes=VMEM_LIMIT)


def _resident(shape, index_map):
    return pl.BlockSpec(shape, index_map, pipeline_mode=pl.Buffered(1))


def _dot(a, b):
    return jnp.dot(a, b, preferred_element_type=F32)


def _dot_nt(a, b):
    return lax.dot_general(a, b, NT_DIMS, preferred_element_type=F32)


def _dot_tn(a, b):
    return lax.dot_general(a, b, TN_DIMS, preferred_element_type=F32)


def _dot_f32(a, b):
    return jnp.dot(a, b, preferred_element_type=F32, precision=lax.Precision.HIGHEST)


def _split_bf16(x, n):
    parts = []
    for _ in range(n):
        p = x.astype(BF16)
        parts.append(p)
        x = x - p.astype(F32)
    return parts


def _dot_split(a, b):
    (a0, a1), (b0, b1) = _split_bf16(a, 2), _split_bf16(b, 2)
    return _dot(a0, b0) + (_dot(a0, b1) + _dot(a1, b0))


def _lane(shape):
    return lax.broadcasted_iota(I32, shape, len(shape) - 1)


def _silu(x):
    return x / (1.0 + jnp.exp(-x))


def _mod_kernel(c_ref, w_ref, b_ref, o_ref):
    cond = _silu(c_ref[...])
    o_ref[...] = _dot_f32(cond, w_ref[...]) + b_ref[...]


def _mod_call(c, mod_w, mod_b):
    B, D = c.shape
    n = mod_w.shape[1] // D
    return pl.pallas_call(
        _mod_kernel,
        out_shape=jax.ShapeDtypeStruct((B, n * D), F32),
        grid=(n,),
        in_specs=[pl.BlockSpec((B, D), lambda j: (0, 0)),
                  pl.BlockSpec((D, D), lambda j: (0, j)),
                  pl.BlockSpec((1, D), lambda j: (0, j))],
        out_specs=pl.BlockSpec((B, D), lambda j: (0, j)),
        compiler_params=_cparams("arbitrary"),
        name="mod",
    )(c, mod_w, mod_b.reshape(1, -1))


def _rope_group(xg, cos, sin, half, period):
    lo = (_lane((1, LANES)) % period) < half
    swapped = jnp.where(lo, pltpu.roll(xg, LANES - half, 1), pltpu.roll(xg, half, 1))
    return xg * cos + swapped * sin


def _in_kernel(x_ref, mod_ref, g_ref, w_ref, ca_ref, sa_ref, cb_ref, sb_ref, cd_ref, sd_ref,
               qa_ref, kva_ref, wa_ref, qkvb_ref, gate_ref, c_ref, d_ref, vta_ref, vtd_ref):
    x = x_ref[...]
    h = x * lax.rsqrt(jnp.mean(x * x, axis=-1, keepdims=True) + EPS) * g_ref[...]
    h = h * (1.0 + mod_ref[0, 1:2, :]) + mod_ref[0, 0:1, :]
    hb = h.astype(BF16)

    def seg(off, width):
        return _dot(hb, w_ref[:, off:off + width])

    lane = _lane((1, LANES))
    low = lane < HEAD_DIM

    def lanes(t_ref):
        t = t_ref[...]
        return jnp.concatenate([t] * (LANES // t.shape[1]), axis=1)

    ca, sa = lanes(ca_ref), lanes(sa_ref)
    a = seg(SEG_A, 1024)
    half_a = HEAD_DIM // ROPE_FRAC // 2
    for gi in range(2):
        r = _rope_group(a[:, gi * LANES:(gi + 1) * LANES], ca, sa, half_a, HEAD_DIM) * (HEAD_DIM ** -0.5 * LOG2E)
        rs = pltpu.roll(r, HEAD_DIM, 1)
        qa_ref[:, (2 * gi) * LANES:(2 * gi + 1) * LANES] = jnp.where(low, r, 0.0).astype(BF16)
        qa_ref[:, (2 * gi + 1) * LANES:(2 * gi + 2) * LANES] = jnp.where(low, rs, 0.0).astype(BF16)
    for gi in range(4):
        r = _rope_group(a[:, (2 + gi) * LANES:(3 + gi) * LANES], ca, sa, half_a, HEAD_DIM)
        rs = pltpu.roll(r, HEAD_DIM, 1)
        qa_ref[:, (4 + 2 * gi) * LANES:(5 + 2 * gi) * LANES] = jnp.where(low, 0.0, rs).astype(BF16)
        qa_ref[:, (5 + 2 * gi) * LANES:(6 + 2 * gi) * LANES] = jnp.where(low, 0.0, r).astype(BF16)
    kva_ref[:, 0:LANES] = _rope_group(a[:, 6 * LANES:7 * LANES], ca, sa, half_a, HEAD_DIM).astype(BF16)
    kva_ref[:, LANES:2 * LANES] = a[:, 7 * LANES:8 * LANES].astype(BF16)
    ones_rows = jnp.ones((VT_ROWS - HEAD_DIM, TM), BF16)
    vta_ref[0:HEAD_DIM, :] = jnp.transpose(a[:, 7 * LANES:8 * LANES])[0:HEAD_DIM, :].astype(BF16)
    vta_ref[HEAD_DIM:VT_ROWS, :] = ones_rows

    wa_ref[...] = seg(SEG_W, LANES)

    cb, sb = lanes(cb_ref), lanes(sb_ref)
    b = seg(SEG_B, 768)
    for gi in range(4):
        r = _rope_group(b[:, gi * LANES:(gi + 1) * LANES], cb, sb, HEAD_DIM // 2, HEAD_DIM)
        if gi < 2:
            r = r * (HEAD_DIM ** -0.5)
        qkvb_ref[:, gi * LANES:(gi + 1) * LANES] = r.astype(BF16)
    qkvb_ref[:, 512:768] = b[:, 512:768].astype(BF16)

    gate_ref[...] = seg(SEG_G, 512)
    c_ref[...] = seg(SEG_C, 640)

    cd, sd = lanes(cd_ref), lanes(sd_ref)
    d = seg(SEG_D, 768)
    half_d = DIFF_DK // ROPE_FRAC // 2
    for gi in range(4):
        r = _rope_group(d[:, gi * LANES:(gi + 1) * LANES], cd, sd, half_d, DIFF_DK)
        if gi < 2:
            r = r * (DIFF_DK ** -0.5 * LOG2E)
        d_ref[:, gi * LANES:(gi + 1) * LANES] = r.astype(BF16)
    d_ref[:, 512:768] = d[:, 512:768].astype(BF16)
    vt = jnp.transpose(d[:, 512:768])
    for h in range(N_GH):
        vtd_ref[h * VT_ROWS:h * VT_ROWS + HEAD_DIM, :] = vt[h * HEAD_DIM:(h + 1) * HEAD_DIM, :].astype(BF16)
        vtd_ref[h * VT_ROWS + HEAD_DIM:(h + 1) * VT_ROWS, :] = ones_rows


def _in_call(x2, mod3, g, w_p, tabs, T):
    N, D = x2.shape
    nt = T // TM
    row = lambda i: (i, 0)
    tab_specs = [pl.BlockSpec((TM, t.shape[1]), row) for t in tabs]
    outs = [(1536, BF16), (256, BF16), (LANES, F32), (768, BF16), (512, F32), (640, F32), (768, BF16)]
    vt_rows = (VT_ROWS, N_GH * VT_ROWS)
    vt_shapes = [jax.ShapeDtypeStruct((N // T * r, T), BF16) for r in vt_rows]
    vt_specs = [pl.BlockSpec((r, TM), lambda i: (i // nt, i % nt)) for r in vt_rows]
    return pl.pallas_call(
        _in_kernel,
        out_shape=[jax.ShapeDtypeStruct((N, w), dt) for w, dt in outs] + vt_shapes,
        grid=(N // TM,),
        in_specs=[pl.BlockSpec((TM, D), row),
                  pl.BlockSpec((1, 6, D), lambda i: (i // nt, 0, 0)),
                  _resident((1, D), lambda i: (0, 0)),
                  _resident((D, W_COLS), lambda i: (0, 0))] + tab_specs,
        out_specs=[pl.BlockSpec((TM, w), row) for w, _ in outs] + vt_specs,
        compiler_params=_cparams("arbitrary"),
        name="in_proj",
    )(x2, mod3, g, w_p, *tabs)


def _pair_head_norm(o, gain, center):
    low = _lane((1, LANES)) < HEAD_DIM
    inv = 1.0 / HEAD_DIM

    def seg_mean(v):
        m0 = jnp.sum(jnp.where(low, v, 0.0), axis=1, keepdims=True)
        m1 = jnp.sum(jnp.where(low, 0.0, v), axis=1, keepdims=True)
        return jnp.where(low, m0, m1) * inv

    if center:
        o = o - seg_mean(o)
    return o * lax.rsqrt(seg_mean(o * o) + EPS) * gain


class _Flash:
    def __init__(self, n, s_ref, p_ref, acc_ref, st_ref):
        self.n, self.s, self.p, self.acc, self.st = n, s_ref, p_ref, acc_ref, st_ref
        self.tq = s_ref.shape[-1]

    @staticmethod
    def scratch(n, tq):
        return [pltpu.VMEM((n, TK, tq), F32), pltpu.VMEM((n, TK, tq), BF16),
                pltpu.VMEM((n, VT_ROWS, tq), F32), pltpu.VMEM((3, n, tq), F32)]

    def _row(self, k, i):
        return self.st.at[k, i:i + 1, :]

    def init(self):
        self.st[0] = jnp.full((self.n, self.tq), NEG_INF, F32)
        self.acc[...] = jnp.zeros_like(self.acc)

    def scores(self, i, s):
        self.s[i] = s
        self._row(1, i)[...] = jnp.max(_fold8(s, jnp.maximum), axis=0, keepdims=True)

    def probs(self, i):
        m_old = self._row(0, i)[...]
        m_new = jnp.maximum(m_old, self._row(1, i)[...])
        for r in range(0, TK, FLASH_SLAB):
            self.p[i, r:r + FLASH_SLAB, :] = jnp.exp2(self.s[i, r:r + FLASH_SLAB, :] - m_new).astype(BF16)
        self._row(0, i)[...] = m_new
        self._row(2, i)[...] = jnp.exp2(m_old - m_new)

    def values(self, i, vt):
        self.acc[i] = self._row(2, i)[...] * self.acc[i] + _dot(vt, self.p[i])

    def result(self, i):
        return self.acc[i, 0:HEAD_DIM, :] / self.acc[i, HEAD_DIM:HEAD_DIM + 1, :]

    def pipeline(self, first, n_rest, score, value):
        score(first, True)

        def body(t, carry):
            for i in range(self.n):
                self.probs(i)
            score(t, False)
            value(jnp.where(t == 0, first, t - 1))
            return carry

        lax.fori_loop(0, n_rest, body, 0)
        for i in range(self.n):
            self.probs(i)
        value(jnp.where(n_rest == 0, first, n_rest - 1))


def _fold8(x, op=jnp.add, rows=8):
    parts = [x[r:r + rows] for r in range(0, x.shape[0], rows)]
    while len(parts) > 1:
        parts = [op(a, b) for a, b in zip(parts[0::2], parts[1::2])]
    return parts[0]


def _head_norm_t(o, gain):
    ms = jnp.mean(o * o, axis=0, keepdims=True)
    return o * lax.rsqrt(ms + EPS) * gain


def _dsa_kernel(q_ref, w_ref, kv_ref, vt_ref, g_ref, ridx_ref, o_ref, bias_ref, key_ref, hi_ref, lo_ref, u_ref,
                *flash_refs, topk):
    TQ = TQ_DSA
    q0 = pl.program_id(1) * TQ
    nkt = (q0 + TQ + TK - 1) // TK
    qchunk = (q0 + _lane((1, TQ))) // CHUNK
    wts = jnp.transpose(w_ref[...]) * (IDX_HEADS ** -0.5 * IDX_DIM ** -0.5)

    def score_tile(kt, carry):
        k0 = pl.multiple_of(kt * TK, TK)
        kk = kv_ref[pl.ds(k0, TK), 0:LANES]
        tile = bias_ref.at[pl.ds(k0, TK), :]
        for h in range(IDX_HEADS):
            term = jnp.maximum(_dot_nt(kk, q_ref[:, (4 + h) * LANES:(5 + h) * LANES]), 0.0) * wts[h:h + 1, :]
            if h == 0:
                tile[...] = term
            elif h < IDX_HEADS - 1:
                tile[...] += term
        kchunk = (k0 + lax.broadcasted_iota(I32, (TK, 1), 0)) // CHUNK
        acc = jnp.where(kchunk <= qchunk, tile[...] + term, NEG_INF)
        bits = pltpu.bitcast(acc, I32)
        sign = bits >> 31
        key = ((bits & 0x7FFFFFFF) ^ sign) - sign
        key_ref[pl.ds(k0, TK), :] = key
        hi_ref[pl.ds(k0, TK), :] = (key >> 16).astype(I16)
        lo_ref[pl.ds(k0, TK), :] = (key ^ 0x8000).astype(I16)
        return carry

    lax.fori_loop(0, nkt, score_tile, 0)

    cr = min(COUNT_ROWS, key_ref.shape[0])
    n_count = (nkt * TK + cr - 1) // cr

    def pad_tile(kt, carry):
        k0 = pl.multiple_of(kt * TK, TK)
        hi_ref[pl.ds(k0, TK), :] = jnp.full((TK, TQ), -32768, I16)
        lo_ref[pl.ds(k0, TK), :] = jnp.full((TK, TQ), -32768, I16)
        return carry

    lax.fori_loop(nkt, n_count * (cr // TK), pad_tile, 0)
    kf = float(topk)

    def count16(ref, cand):
        def body(kc, acc):
            k0 = pl.multiple_of(kc * cr, cr)
            hit = jnp.where(ref[pl.ds(k0, cr), :] >= cand, jnp.ones((cr, TQ), I16), jnp.zeros((cr, TQ), I16))
            return acc + _fold8(hit, rows=16)
        acc = lax.fori_loop(0, n_count, body, jnp.zeros((16, TQ), I16))
        return jnp.sum(acc.astype(F32), axis=0, keepdims=True)

    def search16(ref, cnt_all, signed=True, bits=15):
        if signed:
            c0 = count16(ref, jnp.zeros((1, TQ), I16))
            t = jnp.where(c0 >= kf, 0, -32768).astype(I32)
            cnt = jnp.where(c0 >= kf, c0, cnt_all)
        else:
            t, cnt = jnp.zeros((1, TQ), I32), cnt_all

        def step(it, carry):
            t, cnt = carry
            cand = t + (jnp.int32(1) << (bits - 1 - it))
            c = count16(ref, cand.astype(I16))
            ok = c >= kf
            return jnp.where(ok, cand, t), jnp.where(ok, c, cnt)

        return lax.fori_loop(0, bits, step, (t, cnt))

    t_hi, cnt_hi = search16(hi_ref, (nkt * TK).astype(F32))
    t_hi16 = t_hi.astype(I16)

    top = jnp.full((cr, TQ), 32767, I16)
    bottom = jnp.full((cr, TQ), -32768, I16)

    def low_digit_tile(kc, carry):
        k0 = pl.multiple_of(kc * cr, cr)
        hi = hi_ref[pl.ds(k0, cr), :]
        in_bucket = jnp.where(hi == t_hi16, lo_ref[pl.ds(k0, cr), :], bottom)
        u_ref[pl.ds(k0, cr), :] = jnp.where(hi > t_hi16, top, in_bucket)
        return carry

    lax.fori_loop(0, n_count, low_digit_tile, 0)
    t_lo, cnt_lo = search16(u_ref, cnt_hi)
    t_lo16 = t_lo.astype(I16)
    thr = t_hi * 65536 + (t_lo + 32768)

    def index_digit_tile(kc, carry):
        k0 = pl.multiple_of(kc * cr, cr)
        hi, lo = hi_ref[pl.ds(k0, cr), :], lo_ref[pl.ds(k0, cr), :]
        same_hi = jnp.where(lo == t_lo16, ridx_ref[pl.ds(k0, cr), :], jnp.where(lo > t_lo16, top, bottom))
        u_ref[pl.ds(k0, cr), :] = jnp.where(hi == t_hi16, same_hi, jnp.where(hi > t_hi16, top, bottom))
        return carry

    n_keys = key_ref.shape[0]
    floor_key = jnp.int32(NEG_KEY + 1)
    short = thr < floor_key

    def index_search():
        lax.fori_loop(0, n_count, index_digit_tile, 0)
        return search16(u_ref, cnt_lo, signed=False, bits=(n_keys - 1).bit_length())[0]

    any_tie = jnp.max(jnp.where(short, 0.0, cnt_lo - kf)) > 0.0
    t_idx = lax.cond(any_tie, index_search, lambda: jnp.zeros((1, TQ), I32))
    lim_tie = jnp.where(short, floor_key, thr)
    lim_late = jnp.where(short, floor_key, thr + 1)
    last_tie = jnp.where(short, -1, (n_keys - 1) - t_idx)

    def select_tile(kt, carry):
        k0 = pl.multiple_of(kt * TK, TK)
        idx = k0 + lax.broadcasted_iota(I32, (TK, 1), 0)
        lim = jnp.where(idx <= last_tie, lim_tie, lim_late)
        bias_ref[pl.ds(k0, TK), :] = jnp.where(key_ref[pl.ds(k0, TK), :] >= lim, 0.0, NEG_INF)
        return carry

    lax.fori_loop(0, nkt, select_tile, 0)

    flash = _Flash(N_GH, *flash_refs)
    flash.init()

    def score(kt, is_first):
        k0 = pl.multiple_of(kt * TK, TK)
        kk = kv_ref[pl.ds(k0, TK), 0:LANES]
        bias = bias_ref[pl.ds(k0, TK), :]
        for h in range(N_GH):
            flash.scores(h, _dot_nt(kk, q_ref[:, h * LANES:(h + 1) * LANES]) + bias)

    def value(kt):
        k0 = pl.multiple_of(kt * TK, TK)
        for h in range(N_GH):
            flash.values(h, vt_ref[:, pl.ds(k0, TK)])

    flash.pipeline(nkt - 1, nkt - 1, score, value)
    y = jnp.concatenate([_head_norm_t(flash.result(h), g_ref[h * HEAD_DIM:(h + 1) * HEAD_DIM, :])
                         for h in range(N_GH)], axis=0)
    o_ref[...] = jnp.transpose(y).astype(BF16)


def _dsa_call(qa, wa, kva, vta, g_t, B, T):
    TQ = TQ_DSA
    nq = T // TQ
    ridx = jnp.broadcast_to((T - 1 - jnp.arange(T, dtype=I32)).astype(I16)[:, None], (T, TQ))
    topk = min(TOPK_MAX, T // 4)
    return pl.pallas_call(
        functools.partial(_dsa_kernel, topk=topk),
        out_shape=jax.ShapeDtypeStruct((B * T, GROUP_W), BF16),
        grid=(B, nq),
        in_specs=[pl.BlockSpec((TQ, 1536), lambda b, i: (b * nq + i, 0)),
                  pl.BlockSpec((TQ, LANES), lambda b, i: (b * nq + i, 0)),
                  pl.BlockSpec((T, 256), lambda b, i: (b, 0)),
                  pl.BlockSpec((VT_ROWS, T), lambda b, i: (b, 0)),
                  pl.BlockSpec((GROUP_W, TQ), lambda b, i: (0, 0)),
                  _resident((T, TQ), lambda b, i: (0, 0))],
        out_specs=pl.BlockSpec((TQ, GROUP_W), lambda b, i: (b * nq + i, 0)),
        scratch_shapes=[pltpu.VMEM((T, TQ), F32), pltpu.VMEM((T, TQ), I32), pltpu.VMEM((T, TQ), I16),
                        pltpu.VMEM((T, TQ), I16), pltpu.VMEM((T, TQ), I16)] + _Flash.scratch(N_GH, TQ),
        compiler_params=_cparams("arbitrary", "arbitrary"),
        name="dsa",
    )(qa, wa, kva, vta, g_t, ridx)


def _diff_kernel(q_ref, k_ref, vt_ref, lam_ref, g_ref, o_ref, qm_ref, *flash_refs, lam_init):
    TQ = TQ_DIFF
    q0 = pl.program_id(1) * TQ
    nfull = (q0 + CHUNK) // TK
    qchunk = (q0 + _lane((1, TQ))) // CHUNK
    lv = lam_ref[...]
    lam = (jnp.exp(jnp.sum(lv[0:1] * lv[1:2], axis=1, keepdims=True))
           - jnp.exp(jnp.sum(lv[2:3] * lv[3:4], axis=1, keepdims=True)) + lam_init)
    n_maps = 2 * N_GH
    qt = jnp.transpose(q_ref[...].astype(F32))
    sub = lax.broadcasted_iota(I32, (LANES, 1), 0) // DIFF_DK
    for mi in range(n_maps):
        qg = qt[(mi // 4) * LANES:(mi // 4 + 1) * LANES, :]
        qm_ref[mi] = jnp.where(sub == mi % 4, qg, 0.0).astype(BF16)

    flash = _Flash(n_maps, *flash_refs)
    flash.init()

    def score(kt, masked):
        k0 = pl.multiple_of(kt * TK, TK)
        if masked:
            kchunk = (k0 + lax.broadcasted_iota(I32, (TK, 1), 0)) // CHUNK
            admissible = kchunk <= qchunk
        for mi in range(n_maps):
            g = mi // 4
            s = _dot(k_ref[pl.ds(k0, TK), g * LANES:(g + 1) * LANES], qm_ref[mi])
            flash.scores(mi, jnp.where(admissible, s, NEG_INF) if masked else s)

    def value(kt):
        k0 = pl.multiple_of(kt * TK, TK)
        for mi in range(n_maps):
            h = mi // 2
            flash.values(mi, vt_ref[h * VT_ROWS:(h + 1) * VT_ROWS, pl.ds(k0, TK)])

    flash.pipeline(nfull, nfull, score, value)
    outs = []
    for h in range(N_GH):
        o = flash.result(2 * h) - lam * flash.result(2 * h + 1)
        outs.append(_head_norm_t(o, g_ref[h * HEAD_DIM:(h + 1) * HEAD_DIM, :]) * (1.0 - lam_init))
    o_ref[...] = jnp.transpose(jnp.concatenate(outs, axis=0)).astype(BF16)


def _diff_call(dqkv, vtd, lam_vecs, g_t, lam_init, B, T):
    TQ = TQ_DIFF
    nq = T // TQ
    return pl.pallas_call(
        functools.partial(_diff_kernel, lam_init=lam_init),
        out_shape=jax.ShapeDtypeStruct((B * T, GROUP_W), BF16),
        grid=(B, nq),
        in_specs=[pl.BlockSpec((TQ, 256), lambda b, i: (b * nq + i, 0)),
                  pl.BlockSpec((T, 256), lambda b, i: (b, 1)),
                  pl.BlockSpec((N_GH * VT_ROWS, T), lambda b, i: (b, 0)),
                  pl.BlockSpec((4, LANES), lambda b, i: (0, 0)),
                  pl.BlockSpec((GROUP_W, TQ), lambda b, i: (0, 0))],
        out_specs=pl.BlockSpec((TQ, GROUP_W), lambda b, i: (b * nq + i, 0)),
        scratch_shapes=[pltpu.VMEM((2 * N_GH, LANES, TQ), BF16)] + _Flash.scratch(2 * N_GH, TQ),
        compiler_params=_cparams("arbitrary", "arbitrary"),
        name="diff",
    )(dqkv, dqkv, vtd, lam_vecs, g_t)


def _ret_tables():
    h = np.arange(N_GH, dtype=np.float32)
    log_g = jnp.log(1.0 - 2.0 ** (-5.0 - jnp.asarray(h)))
    t = np.arange(TB)
    same = (t[:, None] // CHUNK) == (t[None, :] // CHUNK)
    earlier = (t[None, :] // CHUNK) < (t[:, None] // CHUNK)
    dist = jnp.asarray(np.where(same, np.abs(t[:, None] - t[None, :]), t[:, None] - t[None, :]), F32)
    dmat = jnp.where(jnp.asarray(same | earlier)[None],
                     jnp.exp(dist[None] * log_g[:, None, None]), 0.0)
    tt = jnp.asarray(t, F32)
    xi = jnp.exp((tt + 1.0)[None, :] * log_g[:, None])
    zeta = jnp.exp((TB - 1.0 - tt)[None, :] * log_g[:, None])
    gblk = jnp.exp(TB * log_g)

    def lanes(tab):
        return jnp.repeat(tab.reshape(2, 2, TB), HEAD_DIM, axis=1).transpose(0, 2, 1)

    r = np.arange(LANES)
    blockdiag = (r[:, None] // HEAD_DIM) == (r[None, :] // HEAD_DIM)
    gb = jnp.where(jnp.asarray(blockdiag)[None],
                   jnp.repeat(gblk.reshape(2, 2), HEAD_DIM, axis=1)[:, :, None], 0.0)
    return dmat, lanes(xi), lanes(zeta), gb


def _ret_kernel(qkv_ref, gate_ref, dmat_ref, xi_ref, zeta_ref, gb_ref, g_ref, o_ref, r_ref):
    @pl.when(pl.program_id(1) == 0)
    def _():
        r_ref[...] = jnp.zeros_like(r_ref)

    lane = _lane((1, LANES))
    low = lane < HEAD_DIM
    r_idx = lax.broadcasted_iota(I32, (LANES, LANES), 0) // HEAD_DIM
    c_idx = lax.broadcasted_iota(I32, (LANES, LANES), 1) // HEAD_DIM
    for g in range(2):
        qg = qkv_ref[:, g * LANES:(g + 1) * LANES]
        kg = qkv_ref[:, 256 + g * LANES:256 + (g + 1) * LANES]
        vg = qkv_ref[:, 512 + g * LANES:512 + (g + 1) * LANES]
        state = r_ref[g]
        inter = _dot(qg, state.astype(BF16)) * xi_ref[g]
        parts = []
        for hh in range(2):
            qm = jnp.where(low, qg, jnp.zeros_like(qg)) if hh == 0 else jnp.where(low, jnp.zeros_like(qg), qg)
            s = _dot_nt(qm, kg) * dmat_ref[2 * g + hh]
            parts.append(_dot(s.astype(BF16), vg))
        o = jnp.where(low, parts[0], parts[1]) + inter
        kz = (kg.astype(F32) * zeta_ref[g]).astype(BF16)
        upd = jnp.where(r_idx == c_idx, _dot_tn(kz, vg), 0.0)
        r_ref[g] = state * gb_ref[g] + upd
        y = _pair_head_norm(o, g_ref[:, g * LANES:(g + 1) * LANES], True)
        o_ref[:, g * LANES:(g + 1) * LANES] = (_silu(gate_ref[:, g * LANES:(g + 1) * LANES]) * y).astype(BF16)


def _ret_specs(nb):
    const3 = lambda b, i: (0, 0, 0)
    in_specs = [pl.BlockSpec((TB, 768), lambda b, i: (b * nb + i, 0)),
                pl.BlockSpec((TB, GROUP_W), lambda b, i: (b * nb + i, 0)),
                pl.BlockSpec((N_GH, TB, TB), const3),
                pl.BlockSpec((2, TB, LANES), const3),
                pl.BlockSpec((2, TB, LANES), const3),
                pl.BlockSpec((2, LANES, LANES), const3),
                pl.BlockSpec((1, GROUP_W), lambda b, i: (0, 0))]
    return in_specs, [pltpu.VMEM((2, LANES, LANES), F32)]


def _gla_kernel(c_ref, gate_ref, wa2_ref, ba_ref, g_ref, o_ref, s_ref, b_scr, q_scr, k_scr, v_scr, acc_scr):
    @pl.when(pl.program_id(1) == 0)
    def _():
        s_ref[...] = jnp.zeros_like(s_ref)

    n_chunks = TB // CHUNK
    hmask = (lax.broadcasted_iota(I32, (LANES, GROUP_W), 0) // GLA_DK
             == lax.broadcasted_iota(I32, (LANES, GROUP_W), 1) // HEAD_DIM)
    hsum = jnp.where(hmask, 1.0, 0.0).astype(BF16)
    r_idx = lax.broadcasted_iota(I32, (TB, TB), 0)
    c_idx = lax.broadcasted_iota(I32, (TB, TB), 1)
    tri = jnp.where((r_idx >= c_idx) & (r_idx // CHUNK == c_idx // CHUNK), 1.0, 0.0).astype(BF16)

    z = _dot_split(c_ref[:, 512:640], wa2_ref[...]) + ba_ref[...]
    log_a = (jnp.minimum(z, 0.0) - jnp.log(1.0 + jnp.exp(-jnp.abs(z)))) * (1.0 / GLA_TAU)
    b = sum(_dot(tri, piece) for piece in _split_bf16(log_a, 3))
    q = c_ref[:, 0:128] * (GLA_DK ** -0.5)
    b_scr[...] = b * LOG2E
    q_scr[...] = q
    k_scr[...] = c_ref[:, 128:256]
    v_scr[...] = c_ref[:, 256:512]

    b_last = [b[(c + 1) * CHUNK - 1:(c + 1) * CHUNK, :] for c in range(n_chunks)]
    b_end = jnp.concatenate([jnp.broadcast_to(bl, (CHUNK, LANES)) for bl in b_last], axis=0)
    kd = (c_ref[:, 128:256] * jnp.exp(b_end - b)).astype(BF16)
    qe = (q * jnp.exp(b)).astype(BF16)
    vb = c_ref[:, 256:512].astype(BF16)
    state = s_ref[...]
    inters = []
    for c in range(n_chunks):
        rows = slice(c * CHUNK, (c + 1) * CHUNK)
        inters.append(_dot(qe[rows], state.astype(BF16)))
        scale = jnp.transpose(jnp.broadcast_to(jnp.exp(b_last[c]), (LANES, LANES)))
        state = jnp.where(hmask, state * jnp.concatenate([scale, scale], axis=1) + _dot_tn(kd[rows], vb[rows]), 0.0)
    s_ref[...] = state

    acc_scr[...] = jnp.concatenate(inters, axis=0)

    keys_per_step = 64

    def key_step(s4, carry):
        for c in range(n_chunks):
            rows = slice(c * CHUNK, (c + 1) * CHUNK)
            part = None
            for u in range(keys_per_step):
                r = c * CHUNK + s4 * keys_per_step + u
                gap = pltpu.bitcast(b_scr[rows, :] - b_scr[pl.ds(r, 1), :], I32)
                decay = jnp.exp2(pltpu.bitcast(gap | jnp.int32(INT_MIN), F32))
                w = (q_scr[rows, :] * decay * k_scr[pl.ds(r, 1), :]).astype(BF16)
                term = _dot(w, hsum) * v_scr[pl.ds(r, 1), :]
                part = term if part is None else part + term
            acc_scr[rows, :] += part
        return carry

    lax.fori_loop(0, CHUNK // keys_per_step, key_step, 0)
    o = acc_scr[...]
    for gi in range(2):
        y = _pair_head_norm(o[:, gi * LANES:(gi + 1) * LANES], g_ref[:, gi * LANES:(gi + 1) * LANES], True)
        o_ref[:, gi * LANES:(gi + 1) * LANES] = (_silu(gate_ref[:, gi * LANES:(gi + 1) * LANES]) * y).astype(BF16)


def _gla_specs(nb):
    in_specs = [pl.BlockSpec((TB, 640), lambda b, i: (b * nb + i, 0)),
                pl.BlockSpec((TB, GROUP_W), lambda b, i: (b * nb + i, 1)),
                pl.BlockSpec((LANES, LANES), lambda b, i: (0, 0)),
                pl.BlockSpec((1, LANES), lambda b, i: (0, 0)),
                pl.BlockSpec((1, GROUP_W), lambda b, i: (0, 0))]
    scratch = [pltpu.VMEM((LANES, GROUP_W), F32), pltpu.VMEM((TB, LANES), F32),
               pltpu.VMEM((TB, LANES), F32), pltpu.VMEM((TB, LANES), F32),
               pltpu.VMEM((TB, GROUP_W), F32), pltpu.VMEM((TB, GROUP_W), F32)]
    return in_specs, scratch


def _ret_gla_kernel(*refs):
    n_ret, n_gla = 7, 5
    o_b, o_c = refs[n_ret + n_gla:n_ret + n_gla + 2]
    scratch = refs[n_ret + n_gla + 2:]
    _ret_kernel(*refs[:n_ret], o_b, scratch[0])
    _gla_kernel(*refs[n_ret:n_ret + n_gla], o_c, *scratch[1:])


def _ret_gla_call(qkvb, gates, ret_tabs, g_b, cc, wa2_p, ba, g_c, B, T):
    nb = T // TB
    ret_in, ret_scratch = _ret_specs(nb)
    gla_in, gla_scratch = _gla_specs(nb)
    out = jax.ShapeDtypeStruct((B * T, GROUP_W), BF16)
    out_spec = pl.BlockSpec((TB, GROUP_W), lambda b, i: (b * nb + i, 0))
    return pl.pallas_call(
        _ret_gla_kernel,
        out_shape=[out, out],
        grid=(B, nb),
        in_specs=ret_in + gla_in,
        out_specs=[out_spec, out_spec],
        scratch_shapes=ret_scratch + gla_scratch,
        compiler_params=_cparams("arbitrary", "arbitrary"),
        name="ret_gla",
    )(qkvb, gates, *ret_tabs, g_b, cc, gates, wa2_p, ba, g_c)


def _rms(y, g):
    return y * lax.rsqrt(jnp.mean(y * y, axis=-1, keepdims=True) + EPS) * g


def _out_mlp_kernel(x_ref, ya_ref, yb_ref, yc_ref, yd_ref, mod_ref, wo_ref, w1_ref, w2_ref,
                    gpost_ref, gpre_ref, gpost2_ref, o_ref):
    y = _dot(ya_ref[...], wo_ref[0:256, :])
    y = y + _dot(yb_ref[...], wo_ref[256:512, :])
    y = y + _dot(yc_ref[...], wo_ref[512:768, :])
    y = y + _dot(yd_ref[...], wo_ref[768:1024, :])
    x = x_ref[...] + mod_ref[0, 2:3, :] * _rms(y, gpost_ref[...])
    h = _rms(x, gpre_ref[...]) * (1.0 + mod_ref[0, 4:5, :]) + mod_ref[0, 3:4, :]
    hb = h.astype(BF16)
    acc = jnp.zeros(x.shape, F32)
    fc = 1024
    for f in range(D_FF // fc):
        u = jnp.maximum(_dot(hb, w1_ref[:, f * fc:(f + 1) * fc]), 0.0)
        acc = acc + _dot((u * u).astype(BF16), w2_ref[f * fc:(f + 1) * fc, :])
    o_ref[...] = x + mod_ref[0, 5:6, :] * _rms(acc, gpost2_ref[...])


def _out_mlp_call(x2, ys, mod3, w_out, w1, w2, g_post, g_pre2, g_post2, T):
    N, D = x2.shape
    nt = T // TM
    row = lambda i: (i, 0)
    const = lambda i: (0, 0)
    return pl.pallas_call(
        _out_mlp_kernel,
        out_shape=jax.ShapeDtypeStruct((N, D), F32),
        grid=(N // TM,),
        in_specs=[pl.BlockSpec((TM, D), row)] + [pl.BlockSpec((TM, GROUP_W), row)] * 4
                 + [pl.BlockSpec((1, 6, D), lambda i: (i // nt, 0, 0)),
                    _resident((D, D), const), _resident((D, D_FF), const), _resident((D_FF, D), const),
                    _resident((1, D), const), _resident((1, D), const), _resident((1, D), const)],
        out_specs=pl.BlockSpec((TM, D), row),
        compiler_params=_cparams("arbitrary"),
        name="out_mlp",
    )(x2, *ys, mod3, w_out, w1, w2, g_post, g_pre2, g_post2)


def _relayout_w_in(w_in):
    col = lambda k: w_in[:, _IN_OFF[k]:_IN_OFF[k + 1]]
    zeros = lambda n: jnp.zeros((w_in.shape[0], n), w_in.dtype)
    parts = [col(_A_Q), col(_A_QI), col(_A_K), col(_A_KI), col(_A_V), zeros(64),
             col(_A_WI), zeros(120),
             col(_B_Q), col(_B_K), col(_B_V),
             col(_B_G), col(_C_G),
             col(_C_Q), col(_C_K), col(_C_V), col(_C_A), zeros(112),
             col(_D_Q), col(_D_K), col(_D_V)]
    return jnp.concatenate(parts, axis=1).astype(BF16)


def _rope_tables(pos, rot, period, theta):
    half = rot // 2
    inv = theta ** (-jnp.arange(half, dtype=F32) / half)
    ang = pos.astype(F32)[..., None] * inv
    cos, sin = jnp.cos(ang), jnp.sin(ang)
    pad = ang.shape[:-1] + (period - rot,)
    ct = jnp.concatenate([cos, cos, jnp.ones(pad, F32)], axis=-1)
    st = jnp.concatenate([-sin, sin, jnp.zeros(pad, F32)], axis=-1)
    return ct.reshape(-1, period), st.reshape(-1, period)


def _layer(x2, cond, tabs, ret_tabs, layer_idx, B, T, mod_w, mod_b, attn_pre_g, attn_post_g,
           mlp_pre_g, mlp_post_g, w_in, gla_wa2, gla_ba, lam_q1, lam_k1, lam_q2, lam_k2,
           head_norm_g, w_out, mlp_w1, mlp_w2):
    D = D_MODEL
    mod3 = _mod_call(cond, mod_w, mod_b).reshape(B, 6, D)
    qa, kva, wa, qkvb, gates, cc, dqkv, vta, vtd = _in_call(
        x2, mod3, attn_pre_g.reshape(1, D), _relayout_w_in(w_in), tabs, T)

    g_a, g_b, g_c, g_d = jnp.split(head_norm_g, 4)
    gain_t = lambda g, tq: jnp.broadcast_to(g[:, None], (GROUP_W, tq))
    y_a = _dsa_call(qa, wa, kva, vta, gain_t(g_a, TQ_DSA), B, T)
    wa2_p = jnp.zeros((LANES, LANES), F32).at[:GLA_RANK].set(gla_wa2)
    y_b, y_c = _ret_gla_call(qkvb, gates, ret_tabs, g_b.reshape(1, GROUP_W), cc, wa2_p,
                             gla_ba.reshape(1, LANES), g_c.reshape(1, GROUP_W), B, T)
    lam_init = 0.8 - 0.6 * math.exp(-0.3 * layer_idx)
    lam_vecs = jnp.zeros((4, LANES), F32).at[:, :DIFF_DK].set(jnp.stack([lam_q1, lam_k1, lam_q2, lam_k2]))
    y_d = _diff_call(dqkv, vtd, lam_vecs, gain_t(g_d, TQ_DIFF), lam_init, B, T)

    return _out_mlp_call(x2, (y_a, y_b, y_c, y_d), mod3, w_out.astype(BF16), mlp_w1.astype(BF16),
                         mlp_w2.astype(BF16), attn_post_g.reshape(1, D), mlp_pre_g.reshape(1, D),
                         mlp_post_g.reshape(1, D), T)


def kernel(x, c, positions, mod_w, mod_b, attn_pre_g, attn_post_g, mlp_pre_g, mlp_post_g, w_in,
           gla_wa2, gla_ba, lam_q1, lam_k1, lam_q2, lam_k2, head_norm_g, w_out, mlp_w1, mlp_w2):
    B, T, D = x.shape
    assert D == D_MODEL and T % TM == 0 and T % TB == 0
    tabs = (_rope_tables(positions, HEAD_DIM // ROPE_FRAC, HEAD_DIM, ROPE_THETA)
            + _rope_tables(positions, HEAD_DIM, HEAD_DIM, RET_THETA)
            + _rope_tables(positions, DIFF_DK // ROPE_FRAC, DIFF_DK, ROPE_THETA))
    ret_tabs = _ret_tables()
    x2 = x.reshape(B * T, D)
    for l in range(mod_w.shape[0]):
        x2 = _layer(x2, c, tabs, ret_tabs, l, B, T, mod_w[l], mod_b[l], attn_pre_g[l], attn_post_g[l],
                    mlp_pre_g[l], mlp_post_g[l], w_in[l], gla_wa2[l], gla_ba[l], lam_q1[l], lam_k1[l],
                    lam_q2[l], lam_k2[l], head_norm_g[l], w_out[l], mlp_w1[l], mlp_w2[l])
    return x2.reshape(B, T, D)
```

```python
import functools
import math

import numpy as np
import jax
import jax.numpy as jnp
from jax import lax
from jax.experimental import pallas as pl
from jax.experimental.pallas import tpu as pltpu

F32 = jnp.float32
BF16 = jnp.bfloat16
I32 = jnp.int32
I16 = jnp.int16

D_MODEL = 1024
CHUNK = 64
HEAD_DIM = 64
N_GH = 4
GROUP_W = N_GH * HEAD_DIM
IDX_HEADS = 8
IDX_DIM = 64
TOPK_MAX = 256
RET_THETA = 10000.0
GLA_DK = HEAD_DIM // 2
GLA_RANK = 16
GLA_TAU = 16.0
DIFF_DK = HEAD_DIM // 2
ROPE_THETA = 500000.0
ROPE_FRAC = 4
D_FF = 4 * D_MODEL
EPS = 1e-6
NEG_INF = -1e30

LANES = 128
VMEM_LIMIT = 56 * 1024 * 1024

_IN_SPLITS = (
    GROUP_W, HEAD_DIM, HEAD_DIM, IDX_HEADS * IDX_DIM, IDX_DIM, IDX_HEADS,
    GROUP_W, GROUP_W, GROUP_W, GROUP_W,
    N_GH * GLA_DK, N_GH * GLA_DK, GROUP_W, GLA_RANK, GROUP_W,
    2 * N_GH * DIFF_DK, 2 * N_GH * DIFF_DK, GROUP_W,
)
_IN_OFF = np.concatenate([[0], np.cumsum(_IN_SPLITS)]).tolist()
(_A_Q, _A_K, _A_V, _A_QI, _A_KI, _A_WI, _B_Q, _B_K, _B_V, _B_G,
 _C_Q, _C_K, _C_V, _C_A, _C_G, _D_Q, _D_K, _D_V) = range(18)

SEG_A = 0
SEG_W = 1024
SEG_B = 1152
SEG_G = 1920
SEG_C = 2432
SEG_D = 3072
W_COLS = 3840
QA_W = 12 * LANES
KVA_W = 2 * LANES
QKV_W = 3 * GROUP_W
GATES_W = 2 * GROUP_W
C_W = 5 * LANES
I16_MIN, I16_MAX = -32768, 32767

NT_DIMS = (((1,), (1,)), ((), ()))
TN_DIMS = (((0,), (0,)), ((), ()))

TM = 512
TQ_DSA = 256
TQ_DIFF = 512
TK = 512
COUNT_ROWS = 512
FLASH_SLAB = 32
VT_ROWS = HEAD_DIM + 16
LOG2E = math.log2(math.e)
TB = 256


def _float_key(v):
    bits = int(np.array(v, np.float32).view(np.int32))
    return -(bits & 0x7FFFFFFF) if bits < 0 else bits


NEG_KEY = _float_key(NEG_INF)
INT_MIN = -(2 ** 31)


def _cparams(*sem):
    return pltpu.CompilerParams(dimension_semantics=sem, vmem_limit_bytes=VMEM_LIMIT)


def _resident(shape, index_map):
    return pl.BlockSpec(shape, index_map, pipeline_mode=pl.Buffered(1))


def _dot(a, b):
    return jnp.dot(a, b, preferred_element_type=F32)


def _dot_nt(a, b):
    return lax.dot_general(a, b, NT_DIMS, preferred_element_type=F32)


def _dot_tn(a, b):
    return lax.dot_general(a, b, TN_DIMS, preferred_element_type=F32)


def _dot_f32(a, b):
    return jnp.dot(a, b, preferred_element_type=F32, precision=lax.Precision.HIGHEST)


def _split_bf16(x, n):
    parts = []
    for _ in range(n):
        p = x.astype(BF16)
        parts.append(p)
        x = x - p.astype(F32)
    return parts


def _dot_split(a, b):
    (a0, a1), (b0, b1) = _split_bf16(a, 2), _split_bf16(b, 2)
    return _dot(a0, b0) + (_dot(a0, b1) + _dot(a1, b0))


def _lane(shape):
    return lax.broadcasted_iota(I32, shape, len(shape) - 1)


def _silu(x):
    return x / (1.0 + jnp.exp(-x))


def _mod_kernel(c_ref, w_ref, b_ref, o_ref):
    cond = _silu(c_ref[...])
    o_ref[...] = _dot_f32(cond, w_ref[...]) + b_ref[...]


def _mod_call(c, mod_w, mod_b):
    B, D = c.shape
    n = mod_w.shape[1] // D
    return pl.pallas_call(
        _mod_kernel,
        out_shape=jax.ShapeDtypeStruct((B, n * D), F32),
        grid=(n,),
        in_specs=[pl.BlockSpec((B, D), lambda j: (0, 0)),
                  pl.BlockSpec((D, D), lambda j: (0, j)),
                  pl.BlockSpec((1, D), lambda j: (0, j))],
        out_specs=pl.BlockSpec((B, D), lambda j: (0, j)),
        compiler_params=_cparams("arbitrary"),
        name="mod",
    )(c, mod_w, mod_b.reshape(1, -1))


def _rope_group(xg, cos, sin, half, period):
    lo = (_lane((1, LANES)) % period) < half
    swapped = jnp.where(lo, pltpu.roll(xg, LANES - half, 1), pltpu.roll(xg, half, 1))
    return xg * cos + swapped * sin


def _in_kernel(x_ref, mod_ref, g_ref, w_ref, ca_ref, sa_ref, cb_ref, sb_ref, cd_ref, sd_ref,
               qa_ref, kva_ref, wa_ref, qkvb_ref, gate_ref, c_ref, d_ref, vta_ref, vtd_ref):
    x = x_ref[...]
    h = x * lax.rsqrt(jnp.mean(x * x, axis=-1, keepdims=True) + EPS) * g_ref[...]
    h = h * (1.0 + mod_ref[0, 1:2, :]) + mod_ref[0, 0:1, :]
    hb = h.astype(BF16)

    def seg(off, width):
        return _dot(hb, w_ref[:, off:off + width])

    lane = _lane((1, LANES))
    low = lane < HEAD_DIM

    def lanes(t_ref):
        t = t_ref[...]
        return jnp.concatenate([t] * (LANES // t.shape[1]), axis=1)

    ca, sa = lanes(ca_ref), lanes(sa_ref)
    a = seg(SEG_A, 1024)
    half_a = HEAD_DIM // ROPE_FRAC // 2
    for gi in range(2):
        r = _rope_group(a[:, gi * LANES:(gi + 1) * LANES], ca, sa, half_a, HEAD_DIM) * (HEAD_DIM ** -0.5 * LOG2E)
        rs = pltpu.roll(r, HEAD_DIM, 1)
        qa_ref[:, (2 * gi) * LANES:(2 * gi + 1) * LANES] = jnp.where(low, r, 0.0).astype(BF16)
        qa_ref[:, (2 * gi + 1) * LANES:(2 * gi + 2) * LANES] = jnp.where(low, rs, 0.0).astype(BF16)
    for gi in range(4):
        r = _rope_group(a[:, (2 + gi) * LANES:(3 + gi) * LANES], ca, sa, half_a, HEAD_DIM)
        rs = pltpu.roll(r, HEAD_DIM, 1)
        qa_ref[:, (4 + 2 * gi) * LANES:(5 + 2 * gi) * LANES] = jnp.where(low, 0.0, rs).astype(BF16)
        qa_ref[:, (5 + 2 * gi) * LANES:(6 + 2 * gi) * LANES] = jnp.where(low, 0.0, r).astype(BF16)
    kva_ref[:, 0:LANES] = _rope_group(a[:, 6 * LANES:7 * LANES], ca, sa, half_a, HEAD_DIM).astype(BF16)
    kva_ref[:, LANES:2 * LANES] = a[:, 7 * LANES:8 * LANES].astype(BF16)
    ones_rows = jnp.ones((VT_ROWS - HEAD_DIM, TM), BF16)
    vta_ref[0:HEAD_DIM, :] = jnp.transpose(a[:, 7 * LANES:8 * LANES])[0:HEAD_DIM, :].astype(BF16)
    vta_ref[HEAD_DIM:VT_ROWS, :] = ones_rows

    wa_ref[...] = seg(SEG_W, LANES)

    cb, sb = lanes(cb_ref), lanes(sb_ref)
    b = seg(SEG_B, QKV_W)
    for gi in range(4):
        r = _rope_group(b[:, gi * LANES:(gi + 1) * LANES], cb, sb, HEAD_DIM // 2, HEAD_DIM)
        if gi < 2:
            r = r * (HEAD_DIM ** -0.5)
        qkvb_ref[:, gi * LANES:(gi + 1) * LANES] = r.astype(BF16)
    qkvb_ref[:, 512:768] = b[:, 512:768].astype(BF16)

    gate_ref[...] = seg(SEG_G, GATES_W)
    c_ref[...] = seg(SEG_C, C_W)

    cd, sd = lanes(cd_ref), lanes(sd_ref)
    d = seg(SEG_D, QKV_W)
    half_d = DIFF_DK // ROPE_FRAC // 2
    for gi in range(4):
        r = _rope_group(d[:, gi * LANES:(gi + 1) * LANES], cd, sd, half_d, DIFF_DK)
        if gi < 2:
            r = r * (DIFF_DK ** -0.5 * LOG2E)
        d_ref[:, gi * LANES:(gi + 1) * LANES] = r.astype(BF16)
    d_ref[:, 512:768] = d[:, 512:768].astype(BF16)
    vt = jnp.transpose(d[:, 512:768])
    for h in range(N_GH):
        vtd_ref[h * VT_ROWS:h * VT_ROWS + HEAD_DIM, :] = vt[h * HEAD_DIM:(h + 1) * HEAD_DIM, :].astype(BF16)
        vtd_ref[h * VT_ROWS + HEAD_DIM:(h + 1) * VT_ROWS, :] = ones_rows


def _in_call(x2, mod3, g, w_p, tabs, T):
    N, D = x2.shape
    nt = T // TM
    row = lambda i: (i, 0)
    tab_specs = [pl.BlockSpec((TM, t.shape[1]), row) for t in tabs]
    outs = [(QA_W, BF16), (KVA_W, BF16), (LANES, F32), (QKV_W, BF16), (GATES_W, F32), (C_W, F32), (QKV_W, BF16)]
    vt_rows = (VT_ROWS, N_GH * VT_ROWS)
    vt_shapes = [jax.ShapeDtypeStruct((N // T * r, T), BF16) for r in vt_rows]
    vt_specs = [pl.BlockSpec((r, TM), lambda i: (i // nt, i % nt)) for r in vt_rows]
    return pl.pallas_call(
        _in_kernel,
        out_shape=[jax.ShapeDtypeStruct((N, w), dt) for w, dt in outs] + vt_shapes,
        grid=(N // TM,),
        in_specs=[pl.BlockSpec((TM, D), row),
                  pl.BlockSpec((1, 6, D), lambda i: (i // nt, 0, 0)),
                  _resident((1, D), lambda i: (0, 0)),
                  _resident((D, W_COLS), lambda i: (0, 0))] + tab_specs,
        out_specs=[pl.BlockSpec((TM, w), row) for w, _ in outs] + vt_specs,
        compiler_params=_cparams("arbitrary"),
        name="in_proj",
    )(x2, mod3, g, w_p, *tabs)


def _pair_head_norm(o, gain, center):
    low = _lane((1, LANES)) < HEAD_DIM
    inv = 1.0 / HEAD_DIM

    def seg_mean(v):
        m0 = jnp.sum(jnp.where(low, v, 0.0), axis=1, keepdims=True)
        m1 = jnp.sum(jnp.where(low, 0.0, v), axis=1, keepdims=True)
        return jnp.where(low, m0, m1) * inv

    if center:
        o = o - seg_mean(o)
    return o * lax.rsqrt(seg_mean(o * o) + EPS) * gain


class _Flash:
    def __init__(self, n, s_ref, p_ref, acc_ref, st_ref):
        self.n, self.s, self.p, self.acc, self.st = n, s_ref, p_ref, acc_ref, st_ref
        self.tq = s_ref.shape[-1]

    @staticmethod
    def scratch(n, tq):
        return [pltpu.VMEM((n, TK, tq), F32), pltpu.VMEM((n, TK, tq), BF16),
                pltpu.VMEM((n, VT_ROWS, tq), F32), pltpu.VMEM((3, n, tq), F32)]

    def _row(self, k, i):
        return self.st.at[k, i:i + 1, :]

    def init(self):
        self.st[0] = jnp.full((self.n, self.tq), NEG_INF, F32)
        self.acc[...] = jnp.zeros_like(self.acc)

    def scores(self, i, s):
        self.s[i] = s
        self._row(1, i)[...] = jnp.max(_fold8(s, jnp.maximum), axis=0, keepdims=True)

    def probs(self, i):
        m_old = self._row(0, i)[...]
        m_new = jnp.maximum(m_old, self._row(1, i)[...])
        for r in range(0, TK, FLASH_SLAB):
            self.p[i, r:r + FLASH_SLAB, :] = jnp.exp2(self.s[i, r:r + FLASH_SLAB, :] - m_new).astype(BF16)
        self._row(0, i)[...] = m_new
        self._row(2, i)[...] = jnp.exp2(m_old - m_new)

    def values(self, i, vt):
        self.acc[i] = self._row(2, i)[...] * self.acc[i] + _dot(vt, self.p[i])

    def result(self, i):
        return self.acc[i, 0:HEAD_DIM, :] / self.acc[i, HEAD_DIM:HEAD_DIM + 1, :]

    def pipeline(self, first, n_rest, score, value):
        score(first, True)

        def body(t, carry):
            for i in range(self.n):
                self.probs(i)
            score(t, False)
            value(jnp.where(t == 0, first, t - 1))
            return carry

        lax.fori_loop(0, n_rest, body, 0)
        for i in range(self.n):
            self.probs(i)
        value(jnp.where(n_rest == 0, first, n_rest - 1))


def _fold8(x, op=jnp.add, rows=8):
    parts = [x[r:r + rows] for r in range(0, x.shape[0], rows)]
    while len(parts) > 1:
        parts = [op(a, b) for a, b in zip(parts[0::2], parts[1::2])]
    return parts[0]


def _head_norm_t(o, gain):
    ms = jnp.mean(o * o, axis=0, keepdims=True)
    return o * lax.rsqrt(ms + EPS) * gain


def _dsa_kernel(q_ref, w_ref, kv_ref, vt_ref, g_ref, ridx_ref, o_ref, bias_ref, key_ref, hi_ref, lo_ref, u_ref,
                *flash_refs, topk):
    TQ = TQ_DSA
    q0 = pl.program_id(1) * TQ
    nkt = (q0 + TQ + TK - 1) // TK
    qchunk = (q0 + _lane((1, TQ))) // CHUNK
    wts = jnp.transpose(w_ref[...]) * (IDX_HEADS ** -0.5 * IDX_DIM ** -0.5)

    def score_tile(kt, carry):
        k0 = pl.multiple_of(kt * TK, TK)
        kk = kv_ref[pl.ds(k0, TK), 0:LANES]
        tile = bias_ref.at[pl.ds(k0, TK), :]
        for h in range(IDX_HEADS):
            term = jnp.maximum(_dot_nt(kk, q_ref[:, (4 + h) * LANES:(5 + h) * LANES]), 0.0) * wts[h:h + 1, :]
            if h == 0:
                tile[...] = term
            elif h < IDX_HEADS - 1:
                tile[...] += term
        kchunk = (k0 + lax.broadcasted_iota(I32, (TK, 1), 0)) // CHUNK
        acc = jnp.where(kchunk <= qchunk, tile[...] + term, NEG_INF)
        bits = pltpu.bitcast(acc, I32)
        sign = bits >> 31
        key = ((bits & 0x7FFFFFFF) ^ sign) - sign
        key_ref[pl.ds(k0, TK), :] = key
        hi_ref[pl.ds(k0, TK), :] = (key >> 16).astype(I16)
        lo_ref[pl.ds(k0, TK), :] = (key ^ 0x8000).astype(I16)
        return carry

    lax.fori_loop(0, nkt, score_tile, 0)

    cr = min(COUNT_ROWS, key_ref.shape[0])
    n_count = (nkt * TK + cr - 1) // cr

    def pad_tile(kt, carry):
        k0 = pl.multiple_of(kt * TK, TK)
        hi_ref[pl.ds(k0, TK), :] = jnp.full((TK, TQ), I16_MIN, I16)
        lo_ref[pl.ds(k0, TK), :] = jnp.full((TK, TQ), I16_MIN, I16)
        return carry

    lax.fori_loop(nkt, n_count * (cr // TK), pad_tile, 0)
    kf = float(topk)

    def count16(ref, cand):
        def body(kc, acc):
            k0 = pl.multiple_of(kc * cr, cr)
            hit = jnp.where(ref[pl.ds(k0, cr), :] >= cand, jnp.ones((cr, TQ), I16), jnp.zeros((cr, TQ), I16))
            return acc + _fold8(hit, rows=16)
        acc = lax.fori_loop(0, n_count, body, jnp.zeros((16, TQ), I16))
        return jnp.sum(acc.astype(F32), axis=0, keepdims=True)

    def search16(ref, cnt_all, signed=True, bits=15):
        if signed:
            c0 = count16(ref, jnp.zeros((1, TQ), I16))
            t = jnp.where(c0 >= kf, 0, I16_MIN).astype(I32)
            cnt = jnp.where(c0 >= kf, c0, cnt_all)
        else:
            t, cnt = jnp.zeros((1, TQ), I32), cnt_all

        def step(it, carry):
            t, cnt = carry
            cand = t + (jnp.int32(1) << (bits - 1 - it))
            c = count16(ref, cand.astype(I16))
            ok = c >= kf
            return jnp.where(ok, cand, t), jnp.where(ok, c, cnt)

        return lax.fori_loop(0, bits, step, (t, cnt))

    t_hi, cnt_hi = search16(hi_ref, (nkt * TK).astype(F32))
    t_hi16 = t_hi.astype(I16)

    top = jnp.full((cr, TQ), I16_MAX, I16)
    bottom = jnp.full((cr, TQ), I16_MIN, I16)

    def low_digit_tile(kc, carry):
        k0 = pl.multiple_of(kc * cr, cr)
        hi = hi_ref[pl.ds(k0, cr), :]
        in_bucket = jnp.where(hi == t_hi16, lo_ref[pl.ds(k0, cr), :], bottom)
        u_ref[pl.ds(k0, cr), :] = jnp.where(hi > t_hi16, top, in_bucket)
        return carry

    lax.fori_loop(0, n_count, low_digit_tile, 0)
    t_lo, cnt_lo = search16(u_ref, cnt_hi)
    t_lo16 = t_lo.astype(I16)
    thr = t_hi * 65536 + (t_lo + 32768)

    def index_digit_tile(kc, carry):
        k0 = pl.multiple_of(kc * cr, cr)
        hi, lo = hi_ref[pl.ds(k0, cr), :], lo_ref[pl.ds(k0, cr), :]
        same_hi = jnp.where(lo == t_lo16, ridx_ref[pl.ds(k0, cr), :], jnp.where(lo > t_lo16, top, bottom))
        u_ref[pl.ds(k0, cr), :] = jnp.where(hi == t_hi16, same_hi, jnp.where(hi > t_hi16, top, bottom))
        return carry

    n_keys = key_ref.shape[0]
    floor_key = jnp.int32(NEG_KEY + 1)
    short = thr < floor_key

    def index_search():
        lax.fori_loop(0, n_count, index_digit_tile, 0)
        return search16(u_ref, cnt_lo, signed=False, bits=(n_keys - 1).bit_length())[0]

    any_tie = jnp.max(jnp.where(short, 0.0, cnt_lo - kf)) > 0.0
    t_idx = lax.cond(any_tie, index_search, lambda: jnp.zeros((1, TQ), I32))
    lim_tie = jnp.where(short, floor_key, thr)
    lim_late = jnp.where(short, floor_key, thr + 1)
    last_tie = jnp.where(short, -1, (n_keys - 1) - t_idx)

    def select_tile(kt, carry):
        k0 = pl.multiple_of(kt * TK, TK)
        idx = k0 + lax.broadcasted_iota(I32, (TK, 1), 0)
        lim = jnp.where(idx <= last_tie, lim_tie, lim_late)
        bias_ref[pl.ds(k0, TK), :] = jnp.where(key_ref[pl.ds(k0, TK), :] >= lim, 0.0, NEG_INF)
        return carry

    lax.fori_loop(0, nkt, select_tile, 0)

    flash = _Flash(N_GH, *flash_refs)
    flash.init()

    def score(kt, is_first):
        k0 = pl.multiple_of(kt * TK, TK)
        kk = kv_ref[pl.ds(k0, TK), 0:LANES]
        bias = bias_ref[pl.ds(k0, TK), :]
        for h in range(N_GH):
            flash.scores(h, _dot_nt(kk, q_ref[:, h * LANES:(h + 1) * LANES]) + bias)

    def value(kt):
        k0 = pl.multiple_of(kt * TK, TK)
        for h in range(N_GH):
            flash.values(h, vt_ref[:, pl.ds(k0, TK)])

    flash.pipeline(nkt - 1, nkt - 1, score, value)
    y = jnp.concatenate([_head_norm_t(flash.result(h), g_ref[h * HEAD_DIM:(h + 1) * HEAD_DIM, :])
                         for h in range(N_GH)], axis=0)
    o_ref[...] = jnp.transpose(y).astype(BF16)


def _dsa_call(qa, wa, kva, vta, g_t, B, T):
    TQ = TQ_DSA
    nq = T // TQ
    ridx = jnp.broadcast_to((T - 1 - jnp.arange(T, dtype=I32)).astype(I16)[:, None], (T, TQ))
    topk = min(TOPK_MAX, T // 4)
    return pl.pallas_call(
        functools.partial(_dsa_kernel, topk=topk),
        out_shape=jax.ShapeDtypeStruct((B * T, GROUP_W), BF16),
        grid=(B, nq),
        in_specs=[pl.BlockSpec((TQ, QA_W), lambda b, i: (b * nq + i, 0)),
                  pl.BlockSpec((TQ, LANES), lambda b, i: (b * nq + i, 0)),
                  pl.BlockSpec((T, KVA_W), lambda b, i: (b, 0)),
                  pl.BlockSpec((VT_ROWS, T), lambda b, i: (b, 0)),
                  pl.BlockSpec((GROUP_W, TQ), lambda b, i: (0, 0)),
                  _resident((T, TQ), lambda b, i: (0, 0))],
        out_specs=pl.BlockSpec((TQ, GROUP_W), lambda b, i: (b * nq + i, 0)),
        scratch_shapes=[pltpu.VMEM((T, TQ), F32), pltpu.VMEM((T, TQ), I32), pltpu.VMEM((T, TQ), I16),
                        pltpu.VMEM((T, TQ), I16), pltpu.VMEM((T, TQ), I16)] + _Flash.scratch(N_GH, TQ),
        compiler_params=_cparams("arbitrary", "arbitrary"),
        name="dsa",
    )(qa, wa, kva, vta, g_t, ridx)


def _diff_kernel(q_ref, k_ref, vt_ref, lam_ref, g_ref, o_ref, qm_ref, *flash_refs, lam_init):
    TQ = TQ_DIFF
    q0 = pl.program_id(1) * TQ
    nfull = (q0 + CHUNK) // TK
    qchunk = (q0 + _lane((1, TQ))) // CHUNK
    lv = lam_ref[...]
    lam = (jnp.exp(jnp.sum(lv[0:1] * lv[1:2], axis=1, keepdims=True))
           - jnp.exp(jnp.sum(lv[2:3] * lv[3:4], axis=1, keepdims=True)) + lam_init)
    n_maps = 2 * N_GH
    qt = jnp.transpose(q_ref[...].astype(F32))
    sub = lax.broadcasted_iota(I32, (LANES, 1), 0) // DIFF_DK
    for mi in range(n_maps):
        qg = qt[(mi // 4) * LANES:(mi // 4 + 1) * LANES, :]
        qm_ref[mi] = jnp.where(sub == mi % 4, qg, 0.0).astype(BF16)

    flash = _Flash(n_maps, *flash_refs)
    flash.init()

    def score(kt, masked):
        k0 = pl.multiple_of(kt * TK, TK)
        if masked:
            kchunk = (k0 + lax.broadcasted_iota(I32, (TK, 1), 0)) // CHUNK
            admissible = kchunk <= qchunk
        for mi in range(n_maps):
            g = mi // 4
            s = _dot(k_ref[pl.ds(k0, TK), g * LANES:(g + 1) * LANES], qm_ref[mi])
            flash.scores(mi, jnp.where(admissible, s, NEG_INF) if masked else s)

    def value(kt):
        k0 = pl.multiple_of(kt * TK, TK)
        for mi in range(n_maps):
            h = mi // 2
            flash.values(mi, vt_ref[h * VT_ROWS:(h + 1) * VT_ROWS, pl.ds(k0, TK)])

    flash.pipeline(nfull, nfull, score, value)
    outs = []
    for h in range(N_GH):
        o = flash.result(2 * h) - lam * flash.result(2 * h + 1)
        outs.append(_head_norm_t(o, g_ref[h * HEAD_DIM:(h + 1) * HEAD_DIM, :]) * (1.0 - lam_init))
    o_ref[...] = jnp.transpose(jnp.concatenate(outs, axis=0)).astype(BF16)


def _diff_call(dqkv, vtd, lam_vecs, g_t, lam_init, B, T):
    TQ = TQ_DIFF
    nq = T // TQ
    return pl.pallas_call(
        functools.partial(_diff_kernel, lam_init=lam_init),
        out_shape=jax.ShapeDtypeStruct((B * T, GROUP_W), BF16),
        grid=(B, nq),
        in_specs=[pl.BlockSpec((TQ, 256), lambda b, i: (b * nq + i, 0)),
                  pl.BlockSpec((T, 256), lambda b, i: (b, 1)),
                  pl.BlockSpec((N_GH * VT_ROWS, T), lambda b, i: (b, 0)),
                  pl.BlockSpec((4, LANES), lambda b, i: (0, 0)),
                  pl.BlockSpec((GROUP_W, TQ), lambda b, i: (0, 0))],
        out_specs=pl.BlockSpec((TQ, GROUP_W), lambda b, i: (b * nq + i, 0)),
        scratch_shapes=[pltpu.VMEM((2 * N_GH, LANES, TQ), BF16)] + _Flash.scratch(2 * N_GH, TQ),
        compiler_params=_cparams("arbitrary", "arbitrary"),
        name="diff",
    )(dqkv, dqkv, vtd, lam_vecs, g_t)


def _ret_tables():
    h = np.arange(N_GH, dtype=np.float32)
    log_g = jnp.log(1.0 - 2.0 ** (-5.0 - jnp.asarray(h)))
    t = np.arange(TB)
    same = (t[:, None] // CHUNK) == (t[None, :] // CHUNK)
    earlier = (t[None, :] // CHUNK) < (t[:, None] // CHUNK)
    dist = jnp.asarray(np.where(same, np.abs(t[:, None] - t[None, :]), t[:, None] - t[None, :]), F32)
    dmat = jnp.where(jnp.asarray(same | earlier)[None],
                     jnp.exp(dist[None] * log_g[:, None, None]), 0.0)
    tt = jnp.asarray(t, F32)
    xi = jnp.exp((tt + 1.0)[None, :] * log_g[:, None])
    zeta = jnp.exp((TB - 1.0 - tt)[None, :] * log_g[:, None])
    gblk = jnp.exp(TB * log_g)

    def lanes(tab):
        return jnp.repeat(tab.reshape(2, 2, TB), HEAD_DIM, axis=1).transpose(0, 2, 1)

    r = np.arange(LANES)
    blockdiag = (r[:, None] // HEAD_DIM) == (r[None, :] // HEAD_DIM)
    gb = jnp.where(jnp.asarray(blockdiag)[None],
                   jnp.repeat(gblk.reshape(2, 2), HEAD_DIM, axis=1)[:, :, None], 0.0)
    return dmat, lanes(xi), lanes(zeta), gb


def _ret_kernel(qkv_ref, gate_ref, dmat_ref, xi_ref, zeta_ref, gb_ref, g_ref, o_ref, r_ref):
    @pl.when(pl.program_id(1) == 0)
    def _():
        r_ref[...] = jnp.zeros_like(r_ref)

    lane = _lane((1, LANES))
    low = lane < HEAD_DIM
    r_idx = lax.broadcasted_iota(I32, (LANES, LANES), 0) // HEAD_DIM
    c_idx = lax.broadcasted_iota(I32, (LANES, LANES), 1) // HEAD_DIM
    for g in range(2):
        qg = qkv_ref[:, g * LANES:(g + 1) * LANES]
        kg = qkv_ref[:, 256 + g * LANES:256 + (g + 1) * LANES]
        vg = qkv_ref[:, 512 + g * LANES:512 + (g + 1) * LANES]
        state = r_ref[g]
        inter = _dot(qg, state.astype(BF16)) * xi_ref[g]
        parts = []
        for hh in range(2):
            qm = jnp.where(low, qg, jnp.zeros_like(qg)) if hh == 0 else jnp.where(low, jnp.zeros_like(qg), qg)
            s = _dot_nt(qm, kg) * dmat_ref[2 * g + hh]
            parts.append(_dot(s.astype(BF16), vg))
        o = jnp.where(low, parts[0], parts[1]) + inter
        kz = (kg.astype(F32) * zeta_ref[g]).astype(BF16)
        upd = jnp.where(r_idx == c_idx, _dot_tn(kz, vg), 0.0)
        r_ref[g] = state * gb_ref[g] + upd
        y = _pair_head_norm(o, g_ref[:, g * LANES:(g + 1) * LANES], True)
        o_ref[:, g * LANES:(g + 1) * LANES] = (_silu(gate_ref[:, g * LANES:(g + 1) * LANES]) * y).astype(BF16)


def _ret_specs(nb):
    const3 = lambda b, i: (0, 0, 0)
    in_specs = [pl.BlockSpec((TB, QKV_W), lambda b, i: (b * nb + i, 0)),
                pl.BlockSpec((TB, GROUP_W), lambda b, i: (b * nb + i, 0)),
                pl.BlockSpec((N_GH, TB, TB), const3),
                pl.BlockSpec((2, TB, LANES), const3),
                pl.BlockSpec((2, TB, LANES), const3),
                pl.BlockSpec((2, LANES, LANES), const3),
                pl.BlockSpec((1, GROUP_W), lambda b, i: (0, 0))]
    return in_specs, [pltpu.VMEM((2, LANES, LANES), F32)]


def _gla_kernel(c_ref, gate_ref, wa2_ref, ba_ref, g_ref, o_ref, s_ref, b_scr, q_scr, k_scr, v_scr, acc_scr):
    @pl.when(pl.program_id(1) == 0)
    def _():
        s_ref[...] = jnp.zeros_like(s_ref)

    n_chunks = TB // CHUNK
    hmask = (lax.broadcasted_iota(I32, (LANES, GROUP_W), 0) // GLA_DK
             == lax.broadcasted_iota(I32, (LANES, GROUP_W), 1) // HEAD_DIM)
    hsum = jnp.where(hmask, 1.0, 0.0).astype(BF16)
    r_idx = lax.broadcasted_iota(I32, (TB, TB), 0)
    c_idx = lax.broadcasted_iota(I32, (TB, TB), 1)
    tri = jnp.where((r_idx >= c_idx) & (r_idx // CHUNK == c_idx // CHUNK), 1.0, 0.0).astype(BF16)

    z = _dot_split(c_ref[:, 512:640], wa2_ref[...]) + ba_ref[...]
    log_a = (jnp.minimum(z, 0.0) - jnp.log(1.0 + jnp.exp(-jnp.abs(z)))) * (1.0 / GLA_TAU)
    b = sum(_dot(tri, piece) for piece in _split_bf16(log_a, 3))
    q = c_ref[:, 0:128] * (GLA_DK ** -0.5)
    b_scr[...] = b * LOG2E
    q_scr[...] = q
    k_scr[...] = c_ref[:, 128:256]
    v_scr[...] = c_ref[:, 256:512]

    b_last = [b[(c + 1) * CHUNK - 1:(c + 1) * CHUNK, :] for c in range(n_chunks)]
    b_end = jnp.concatenate([jnp.broadcast_to(bl, (CHUNK, LANES)) for bl in b_last], axis=0)
    kd = (c_ref[:, 128:256] * jnp.exp(b_end - b)).astype(BF16)
    qe = (q * jnp.exp(b)).astype(BF16)
    vb = c_ref[:, 256:512].astype(BF16)
    state = s_ref[...]
    inters = []
    for c in range(n_chunks):
        rows = slice(c * CHUNK, (c + 1) * CHUNK)
        inters.append(_dot(qe[rows], state.astype(BF16)))
        scale = jnp.transpose(jnp.broadcast_to(jnp.exp(b_last[c]), (LANES, LANES)))
        state = jnp.where(hmask, state * jnp.concatenate([scale, scale], axis=1) + _dot_tn(kd[rows], vb[rows]), 0.0)
    s_ref[...] = state

    acc_scr[...] = jnp.concatenate(inters, axis=0)

    keys_per_step = 64

    def key_step(s4, carry):
        for c in range(n_chunks):
            rows = slice(c * CHUNK, (c + 1) * CHUNK)
            part = None
            for u in range(keys_per_step):
                r = c * CHUNK + s4 * keys_per_step + u
                gap = pltpu.bitcast(b_scr[rows, :] - b_scr[pl.ds(r, 1), :], I32)
                decay = jnp.exp2(pltpu.bitcast(gap | jnp.int32(INT_MIN), F32))
                w = (q_scr[rows, :] * decay * k_scr[pl.ds(r, 1), :]).astype(BF16)
                term = _dot(w, hsum) * v_scr[pl.ds(r, 1), :]
                part = term if part is None else part + term
            acc_scr[rows, :] += part
        return carry

    lax.fori_loop(0, CHUNK // keys_per_step, key_step, 0)
    o = acc_scr[...]
    for gi in range(2):
        y = _pair_head_norm(o[:, gi * LANES:(gi + 1) * LANES], g_ref[:, gi * LANES:(gi + 1) * LANES], True)
        o_ref[:, gi * LANES:(gi + 1) * LANES] = (_silu(gate_ref[:, gi * LANES:(gi + 1) * LANES]) * y).astype(BF16)


def _gla_specs(nb):
    in_specs = [pl.BlockSpec((TB, C_W), lambda b, i: (b * nb + i, 0)),
                pl.BlockSpec((TB, GROUP_W), lambda b, i: (b * nb + i, 1)),
                pl.BlockSpec((LANES, LANES), lambda b, i: (0, 0)),
                pl.BlockSpec((1, LANES), lambda b, i: (0, 0)),
                pl.BlockSpec((1, GROUP_W), lambda b, i: (0, 0))]
    scratch = [pltpu.VMEM((LANES, GROUP_W), F32), pltpu.VMEM((TB, LANES), F32),
               pltpu.VMEM((TB, LANES), F32), pltpu.VMEM((TB, LANES), F32),
               pltpu.VMEM((TB, GROUP_W), F32), pltpu.VMEM((TB, GROUP_W), F32)]
    return in_specs, scratch


def _ret_gla_kernel(*refs):
    n_ret, n_gla = 7, 5
    o_b, o_c = refs[n_ret + n_gla:n_ret + n_gla + 2]
    scratch = refs[n_ret + n_gla + 2:]
    _ret_kernel(*refs[:n_ret], o_b, scratch[0])
    _gla_kernel(*refs[n_ret:n_ret + n_gla], o_c, *scratch[1:])


def _ret_gla_call(qkvb, gates, ret_tabs, g_b, cc, wa2_p, ba, g_c, B, T):
    nb = T // TB
    ret_in, ret_scratch = _ret_specs(nb)
    gla_in, gla_scratch = _gla_specs(nb)
    out = jax.ShapeDtypeStruct((B * T, GROUP_W), BF16)
    out_spec = pl.BlockSpec((TB, GROUP_W), lambda b, i: (b * nb + i, 0))
    return pl.pallas_call(
        _ret_gla_kernel,
        out_shape=[out, out],
        grid=(B, nb),
        in_specs=ret_in + gla_in,
        out_specs=[out_spec, out_spec],
        scratch_shapes=ret_scratch + gla_scratch,
        compiler_params=_cparams("arbitrary", "arbitrary"),
        name="ret_gla",
    )(qkvb, gates, *ret_tabs, g_b, cc, gates, wa2_p, ba, g_c)


def _rms(y, g):
    return y * lax.rsqrt(jnp.mean(y * y, axis=-1, keepdims=True) + EPS) * g


def _out_mlp_kernel(x_ref, ya_ref, yb_ref, yc_ref, yd_ref, mod_ref, wo_ref, w1_ref, w2_ref,
                    gpost_ref, gpre_ref, gpost2_ref, o_ref):
    y = _dot(ya_ref[...], wo_ref[0:256, :])
    y = y + _dot(yb_ref[...], wo_ref[256:512, :])
    y = y + _dot(yc_ref[...], wo_ref[512:768, :])
    y = y + _dot(yd_ref[...], wo_ref[768:1024, :])
    x = x_ref[...] + mod_ref[0, 2:3, :] * _rms(y, gpost_ref[...])
    h = _rms(x, gpre_ref[...]) * (1.0 + mod_ref[0, 4:5, :]) + mod_ref[0, 3:4, :]
    hb = h.astype(BF16)
    acc = jnp.zeros(x.shape, F32)
    fc = 1024
    for f in range(D_FF // fc):
        u = jnp.maximum(_dot(hb, w1_ref[:, f * fc:(f + 1) * fc]), 0.0)
        acc = acc + _dot((u * u).astype(BF16), w2_ref[f * fc:(f + 1) * fc, :])
    o_ref[...] = x + mod_ref[0, 5:6, :] * _rms(acc, gpost2_ref[...])


def _out_mlp_call(x2, ys, mod3, w_out, w1, w2, g_post, g_pre2, g_post2, T):
    N, D = x2.shape
    nt = T // TM
    row = lambda i: (i, 0)
    const = lambda i: (0, 0)
    return pl.pallas_call(
        _out_mlp_kernel,
        out_shape=jax.ShapeDtypeStruct((N, D), F32),
        grid=(N // TM,),
        in_specs=[pl.BlockSpec((TM, D), row)] + [pl.BlockSpec((TM, GROUP_W), row)] * 4
                 + [pl.BlockSpec((1, 6, D), lambda i: (i // nt, 0, 0)),
                    _resident((D, D), const), _resident((D, D_FF), const), _resident((D_FF, D), const),
                    _resident((1, D), const), _resident((1, D), const), _resident((1, D), const)],
        out_specs=pl.BlockSpec((TM, D), row),
        compiler_params=_cparams("arbitrary"),
        name="out_mlp",
    )(x2, *ys, mod3, w_out, w1, w2, g_post, g_pre2, g_post2)


def _relayout_w_in(w_in):
    col = lambda k: w_in[:, _IN_OFF[k]:_IN_OFF[k + 1]]
    zeros = lambda n: jnp.zeros((w_in.shape[0], n), w_in.dtype)
    parts = [col(_A_Q), col(_A_QI), col(_A_K), col(_A_KI), col(_A_V), zeros(64),
             col(_A_WI), zeros(120),
             col(_B_Q), col(_B_K), col(_B_V),
             col(_B_G), col(_C_G),
             col(_C_Q), col(_C_K), col(_C_V), col(_C_A), zeros(112),
             col(_D_Q), col(_D_K), col(_D_V)]
    return jnp.concatenate(parts, axis=1).astype(BF16)


def _rope_tables(pos, rot, period, theta):
    half = rot // 2
    inv = theta ** (-jnp.arange(half, dtype=F32) / half)
    ang = pos.astype(F32)[..., None] * inv
    cos, sin = jnp.cos(ang), jnp.sin(ang)
    pad = ang.shape[:-1] + (period - rot,)
    ct = jnp.concatenate([cos, cos, jnp.ones(pad, F32)], axis=-1)
    st = jnp.concatenate([-sin, sin, jnp.zeros(pad, F32)], axis=-1)
    return ct.reshape(-1, period), st.reshape(-1, period)


def _layer(x2, cond, tabs, ret_tabs, layer_idx, B, T, mod_w, mod_b, attn_pre_g, attn_post_g,
           mlp_pre_g, mlp_post_g, w_in, gla_wa2, gla_ba, lam_q1, lam_k1, lam_q2, lam_k2,
           head_norm_g, w_out, mlp_w1, mlp_w2):
    D = D_MODEL
    mod3 = _mod_call(cond, mod_w, mod_b).reshape(B, 6, D)
    qa, kva, wa, qkvb, gates, cc, dqkv, vta, vtd = _in_call(
        x2, mod3, attn_pre_g.reshape(1, D), _relayout_w_in(w_in), tabs, T)

    g_a, g_b, g_c, g_d = jnp.split(head_norm_g, 4)
    gain_t = lambda g, tq: jnp.broadcast_to(g[:, None], (GROUP_W, tq))
    y_a = _dsa_call(qa, wa, kva, vta, gain_t(g_a, TQ_DSA), B, T)
    wa2_p = jnp.zeros((LANES, LANES), F32).at[:GLA_RANK].set(gla_wa2)
    y_b, y_c = _ret_gla_call(qkvb, gates, ret_tabs, g_b.reshape(1, GROUP_W), cc, wa2_p,
                             gla_ba.reshape(1, LANES), g_c.reshape(1, GROUP_W), B, T)
    lam_init = 0.8 - 0.6 * math.exp(-0.3 * layer_idx)
    lam_vecs = jnp.zeros((4, LANES), F32).at[:, :DIFF_DK].set(jnp.stack([lam_q1, lam_k1, lam_q2, lam_k2]))
    y_d = _diff_call(dqkv, vtd, lam_vecs, gain_t(g_d, TQ_DIFF), lam_init, B, T)

    return _out_mlp_call(x2, (y_a, y_b, y_c, y_d), mod3, w_out.astype(BF16), mlp_w1.astype(BF16),
                         mlp_w2.astype(BF16), attn_post_g.reshape(1, D), mlp_pre_g.reshape(1, D),
                         mlp_post_g.reshape(1, D), T)


def kernel(x, c, positions, mod_w, mod_b, attn_pre_g, attn_post_g, mlp_pre_g, mlp_post_g, w_in,
           gla_wa2, gla_ba, lam_q1, lam_k1, lam_q2, lam_k2, head_norm_g, w_out, mlp_w1, mlp_w2):
    B, T, D = x.shape
    assert D == D_MODEL and T % TM == 0 and T % TB == 0
    tabs = (_rope_tables(positions, HEAD_DIM // ROPE_FRAC, HEAD_DIM, ROPE_THETA)
            + _rope_tables(positions, HEAD_DIM, HEAD_DIM, RET_THETA)
            + _rope_tables(positions, DIFF_DK // ROPE_FRAC, DIFF_DK, ROPE_THETA))
    ret_tabs = _ret_tables()
    x2 = x.reshape(B * T, D)
    for l in range(mod_w.shape[0]):
        x2 = _layer(x2, c, tabs, ret_tabs, l, B, T, mod_w[l], mod_b[l], attn_pre_g[l], attn_post_g[l],
                    mlp_pre_g[l], mlp_post_g[l], w_in[l], gla_wa2[l], gla_ba[l], lam_q1[l], lam_k1[l],
                    lam_q2[l], lam_k2[l], head_norm_g[l], w_out[l], mlp_w1[l], mlp_w2[l])
    return x2.reshape(B, T, D)
```

```python
import functools
import math

import numpy as np
import jax
import jax.numpy as jnp
from jax import lax
from jax.experimental import pallas as pl
from jax.experimental.pallas import tpu as pltpu

F32 = jnp.float32
BF16 = jnp.bfloat16
I32 = jnp.int32
I16 = jnp.int16

D_MODEL = 1024
CHUNK = 64
HEAD_DIM = 64
N_GH = 4
GROUP_W = N_GH * HEAD_DIM
IDX_HEADS = 8
IDX_DIM = 64
TOPK_MAX = 256
RET_THETA = 10000.0
GLA_DK = HEAD_DIM // 2
GLA_RANK = 16
GLA_TAU = 16.0
DIFF_DK = HEAD_DIM // 2
ROPE_THETA = 500000.0
ROPE_FRAC = 4
D_FF = 4 * D_MODEL
EPS = 1e-6
NEG_INF = -1e30

LANES = 128
VMEM_LIMIT = 56 * 1024 * 1024

_IN_SPLITS = (
    GROUP_W, HEAD_DIM, HEAD_DIM, IDX_HEADS * IDX_DIM, IDX_DIM, IDX_HEADS,
    GROUP_W, GROUP_W, GROUP_W, GROUP_W,
    N_GH * GLA_DK, N_GH * GLA_DK, GROUP_W, GLA_RANK, GROUP_W,
    2 * N_GH * DIFF_DK, 2 * N_GH * DIFF_DK, GROUP_W,
)
_IN_OFF = np.concatenate([[0], np.cumsum(_IN_SPLITS)]).tolist()
(_A_Q, _A_K, _A_V, _A_QI, _A_KI, _A_WI, _B_Q, _B_K, _B_V, _B_G,
 _C_Q, _C_K, _C_V, _C_A, _C_G, _D_Q, _D_K, _D_V) = range(18)

SEG_A = 0
SEG_W = 1024
SEG_B = 1152
SEG_G = 1920
SEG_C = 2432
SEG_D = 3072
W_COLS = 3840
QA_W = 12 * LANES
KVA_W = 2 * LANES
QKV_W = 3 * GROUP_W
GATES_W = 2 * GROUP_W
C_W = 5 * LANES
I16_MIN, I16_MAX = -32768, 32767

NT_DIMS = (((1,), (1,)), ((), ()))
TN_DIMS = (((0,), (0,)), ((), ()))

TM = 512
TQ_DSA = 256
TQ_DIFF = 512
TK = 512
COUNT_ROWS = 512
FLASH_SLAB = 32
VT_ROWS = HEAD_DIM + 16
LOG2E = math.log2(math.e)
TB = 256


def _float_key(v):
    bits = int(np.array(v, np.float32).view(np.int32))
    return -(bits & 0x7FFFFFFF) if bits < 0 else bits


NEG_KEY = _float_key(NEG_INF)
INT_MIN = -(2 ** 31)


def _cparams(*sem):
    return pltpu.CompilerParams(dimension_semantics=sem, vmem_limit_bytes=VMEM_LIMIT)


def _resident(shape, index_map):
    return pl.BlockSpec(shape, index_map, pipeline_mode=pl.Buffered(1))


def _dot(a, b):
    return jnp.dot(a, b, preferred_element_type=F32)


def _dot_nt(a, b):
    return lax.dot_general(a, b, NT_DIMS, preferred_element_type=F32)


def _dot_tn(a, b):
    return lax.dot_general(a, b, TN_DIMS, preferred_element_type=F32)


def _dot_f32(a, b):
    return jnp.dot(a, b, preferred_element_type=F32, precision=lax.Precision.HIGHEST)


def _split_bf16(x, n):
    parts = []
    for _ in range(n):
        p = x.astype(BF16)
        parts.append(p)
        x = x - p.astype(F32)
    return parts


def _dot_split(a, b):
    (a0, a1), (b0, b1) = _split_bf16(a, 2), _split_bf16(b, 2)
    return _dot(a0, b0) + (_dot(a0, b1) + _dot(a1, b0))


def _lane(shape):
    return lax.broadcasted_iota(I32, shape, len(shape) - 1)


def _silu(x):
    return x / (1.0 + jnp.exp(-x))


def _mod_kernel(c_ref, w_ref, b_ref, o_ref):
    cond = _silu(c_ref[...])
    o_ref[...] = _dot_f32(cond, w_ref[...]) + b_ref[...]


def _mod_call(c, mod_w, mod_b):
    B, D = c.shape
    n = mod_w.shape[1] // D
    return pl.pallas_call(
        _mod_kernel,
        out_shape=jax.ShapeDtypeStruct((B, n * D), F32),
        grid=(n,),
        in_specs=[pl.BlockSpec((B, D), lambda j: (0, 0)),
                  pl.BlockSpec((D, D), lambda j: (0, j)),
                  pl.BlockSpec((1, D), lambda j: (0, j))],
        out_specs=pl.BlockSpec((B, D), lambda j: (0, j)),
        compiler_params=_cparams("arbitrary"),
        name="mod",
    )(c, mod_w, mod_b.reshape(1, -1))


def _rope_group(xg, cos, sin, half, period):
    lo = (_lane((1, LANES)) % period) < half
    swapped = jnp.where(lo, pltpu.roll(xg, LANES - half, 1), pltpu.roll(xg, half, 1))
    return xg * cos + swapped * sin


def _in_kernel(x_ref, mod_ref, g_ref, w_ref, ca_ref, sa_ref, cb_ref, sb_ref, cd_ref, sd_ref,
               qa_ref, kva_ref, wa_ref, qkvb_ref, gate_ref, c_ref, d_ref, vta_ref, vtd_ref):
    x = x_ref[...]
    h = x * lax.rsqrt(jnp.mean(x * x, axis=-1, keepdims=True) + EPS) * g_ref[...]
    h = h * (1.0 + mod_ref[0, 1:2, :]) + mod_ref[0, 0:1, :]
    hb = h.astype(BF16)

    def seg(off, width):
        return _dot(hb, w_ref[:, off:off + width])

    lane = _lane((1, LANES))
    low = lane < HEAD_DIM

    def lanes(t_ref):
        t = t_ref[...]
        return jnp.concatenate([t] * (LANES // t.shape[1]), axis=1)

    ca, sa = lanes(ca_ref), lanes(sa_ref)
    a = seg(SEG_A, 1024)
    half_a = HEAD_DIM // ROPE_FRAC // 2
    for gi in range(2):
        r = _rope_group(a[:, gi * LANES:(gi + 1) * LANES], ca, sa, half_a, HEAD_DIM) * (HEAD_DIM ** -0.5 * LOG2E)
        rs = pltpu.roll(r, HEAD_DIM, 1)
        qa_ref[:, (2 * gi) * LANES:(2 * gi + 1) * LANES] = jnp.where(low, r, 0.0).astype(BF16)
        qa_ref[:, (2 * gi + 1) * LANES:(2 * gi + 2) * LANES] = jnp.where(low, rs, 0.0).astype(BF16)
    for gi in range(4):
        r = _rope_group(a[:, (2 + gi) * LANES:(3 + gi) * LANES], ca, sa, half_a, HEAD_DIM)
        rs = pltpu.roll(r, HEAD_DIM, 1)
        qa_ref[:, (4 + 2 * gi) * LANES:(5 + 2 * gi) * LANES] = jnp.where(low, 0.0, rs).astype(BF16)
        qa_ref[:, (5 + 2 * gi) * LANES:(6 + 2 * gi) * LANES] = jnp.where(low, 0.0, r).astype(BF16)
    kva_ref[:, 0:LANES] = _rope_group(a[:, 6 * LANES:7 * LANES], ca, sa, half_a, HEAD_DIM).astype(BF16)
    kva_ref[:, LANES:2 * LANES] = a[:, 7 * LANES:8 * LANES].astype(BF16)
    ones_rows = jnp.ones((VT_ROWS - HEAD_DIM, TM), BF16)
    vta_ref[0:HEAD_DIM, :] = jnp.transpose(a[:, 7 * LANES:8 * LANES])[0:HEAD_DIM, :].astype(BF16)
    vta_ref[HEAD_DIM:VT_ROWS, :] = ones_rows

    wa_ref[...] = seg(SEG_W, LANES)

    cb, sb = lanes(cb_ref), lanes(sb_ref)
    b = seg(SEG_B, QKV_W)
    for gi in range(4):
        r = _rope_group(b[:, gi * LANES:(gi + 1) * LANES], cb, sb, HEAD_DIM // 2, HEAD_DIM)
        if gi < 2:
            r = r * (HEAD_DIM ** -0.5)
        qkvb_ref[:, gi * LANES:(gi + 1) * LANES] = r.astype(BF16)
    qkvb_ref[:, 512:768] = b[:, 512:768].astype(BF16)

    gate_ref[...] = seg(SEG_G, GATES_W)
    c_ref[...] = seg(SEG_C, C_W)

    cd, sd = lanes(cd_ref), lanes(sd_ref)
    d = seg(SEG_D, QKV_W)
    half_d = DIFF_DK // ROPE_FRAC // 2
    for gi in range(4):
        r = _rope_group(d[:, gi * LANES:(gi + 1) * LANES], cd, sd, half_d, DIFF_DK)
        if gi < 2:
            r = r * (DIFF_DK ** -0.5 * LOG2E)
        d_ref[:, gi * LANES:(gi + 1) * LANES] = r.astype(BF16)
    d_ref[:, 512:768] = d[:, 512:768].astype(BF16)
    vt = jnp.transpose(d[:, 512:768])
    for h in range(N_GH):
        vtd_ref[h * VT_ROWS:h * VT_ROWS + HEAD_DIM, :] = vt[h * HEAD_DIM:(h + 1) * HEAD_DIM, :].astype(BF16)
        vtd_ref[h * VT_ROWS + HEAD_DIM:(h + 1) * VT_ROWS, :] = ones_rows


def _in_call(x2, mod3, g, w_p, tabs, T):
    N, D = x2.shape
    nt = T // TM
    row = lambda i: (i, 0)
    tab_specs = [pl.BlockSpec((TM, t.shape[1]), row) for t in tabs]
    outs = [(QA_W, BF16), (KVA_W, BF16), (LANES, F32), (QKV_W, BF16), (GATES_W, F32), (C_W, F32), (QKV_W, BF16)]
    vt_rows = (VT_ROWS, N_GH * VT_ROWS)
    vt_shapes = [jax.ShapeDtypeStruct((N // T * r, T), BF16) for r in vt_rows]
    vt_specs = [pl.BlockSpec((r, TM), lambda i: (i // nt, i % nt)) for r in vt_rows]
    return pl.pallas_call(
        _in_kernel,
        out_shape=[jax.ShapeDtypeStruct((N, w), dt) for w, dt in outs] + vt_shapes,
        grid=(N // TM,),
        in_specs=[pl.BlockSpec((TM, D), row),
                  pl.BlockSpec((1, 6, D), lambda i: (i // nt, 0, 0)),
                  _resident((1, D), lambda i: (0, 0)),
                  _resident((D, W_COLS), lambda i: (0, 0))] + tab_specs,
        out_specs=[pl.BlockSpec((TM, w), row) for w, _ in outs] + vt_specs,
        compiler_params=_cparams("arbitrary"),
        name="in_proj",
    )(x2, mod3, g, w_p, *tabs)


def _pair_head_norm(o, gain, center):
    low = _lane((1, LANES)) < HEAD_DIM
    inv = 1.0 / HEAD_DIM

    def seg_mean(v):
        m0 = jnp.sum(jnp.where(low, v, 0.0), axis=1, keepdims=True)
        m1 = jnp.sum(jnp.where(low, 0.0, v), axis=1, keepdims=True)
        return jnp.where(low, m0, m1) * inv

    if center:
        o = o - seg_mean(o)
    return o * lax.rsqrt(seg_mean(o * o) + EPS) * gain


class _Flash:
    def __init__(self, n, s_ref, p_ref, acc_ref, st_ref):
        self.n, self.s, self.p, self.acc, self.st = n, s_ref, p_ref, acc_ref, st_ref
        self.tq = s_ref.shape[-1]

    @staticmethod
    def scratch(n, tq):
        return [pltpu.VMEM((n, TK, tq), F32), pltpu.VMEM((n, TK, tq), BF16),
                pltpu.VMEM((n, VT_ROWS, tq), F32), pltpu.VMEM((3, n, tq), F32)]

    def _row(self, k, i):
        return self.st.at[k, i:i + 1, :]

    def init(self):
        self.st[0] = jnp.full((self.n, self.tq), NEG_INF, F32)
        self.acc[...] = jnp.zeros_like(self.acc)

    def scores(self, i, s):
        self.s[i] = s
        self._row(1, i)[...] = jnp.max(_fold8(s, jnp.maximum), axis=0, keepdims=True)

    def probs(self, i):
        m_old = self._row(0, i)[...]
        m_new = jnp.maximum(m_old, self._row(1, i)[...])
        for r in range(0, TK, FLASH_SLAB):
            self.p[i, r:r + FLASH_SLAB, :] = jnp.exp2(self.s[i, r:r + FLASH_SLAB, :] - m_new).astype(BF16)
        self._row(0, i)[...] = m_new
        self._row(2, i)[...] = jnp.exp2(m_old - m_new)

    def values(self, i, vt):
        self.acc[i] = self._row(2, i)[...] * self.acc[i] + _dot(vt, self.p[i])

    def result(self, i):
        return self.acc[i, 0:HEAD_DIM, :] / self.acc[i, HEAD_DIM:HEAD_DIM + 1, :]

    def pipeline(self, first, n_rest, score, value):
        score(first, True)

        def body(t, carry):
            for i in range(self.n):
                self.probs(i)
            score(t, False)
            value(jnp.where(t == 0, first, t - 1))
            return carry

        lax.fori_loop(0, n_rest, body, 0)
        for i in range(self.n):
            self.probs(i)
        value(jnp.where(n_rest == 0, first, n_rest - 1))


def _fold8(x, op=jnp.add, rows=8):
    parts = [x[r:r + rows] for r in range(0, x.shape[0], rows)]
    while len(parts) > 1:
        parts = [op(a, b) for a, b in zip(parts[0::2], parts[1::2])]
    return parts[0]


def _head_norm_t(o, gain):
    ms = jnp.mean(o * o, axis=0, keepdims=True)
    return o * lax.rsqrt(ms + EPS) * gain


def _dsa_kernel(q_ref, w_ref, kv_ref, vt_ref, g_ref, ridx_ref, o_ref, bias_ref, key_ref, hi_ref, lo_ref, u_ref,
                *flash_refs, topk):
    TQ = TQ_DSA
    q0 = pl.program_id(1) * TQ
    nkt = (q0 + TQ + TK - 1) // TK
    qchunk = (q0 + _lane((1, TQ))) // CHUNK
    wts = jnp.transpose(w_ref[...]) * (IDX_HEADS ** -0.5 * IDX_DIM ** -0.5)

    def score_tile(kt, carry, masked=False):
        k0 = pl.multiple_of(kt * TK, TK)
        kk = kv_ref[pl.ds(k0, TK), 0:LANES]
        tile = bias_ref.at[pl.ds(k0, TK), :]

        def head_term(h):
            return jnp.maximum(_dot_nt(kk, q_ref[:, (4 + h) * LANES:(5 + h) * LANES]), 0.0) * wts[h:h + 1, :]

        for h in range(0, IDX_HEADS, 2):
            term = head_term(h) + head_term(h + 1)
            if h == 0:
                tile[...] = term
            elif h < IDX_HEADS - 2:
                tile[...] += term
        acc = tile[...] + term
        if masked:
            kchunk = (k0 + lax.broadcasted_iota(I32, (TK, 1), 0)) // CHUNK
            acc = jnp.where(kchunk <= qchunk, acc, NEG_INF)
        bits = pltpu.bitcast(acc, I32)
        sign = bits >> 31
        key = ((bits & 0x7FFFFFFF) ^ sign) - sign
        key_ref[pl.ds(k0, TK), :] = key
        hi_ref[pl.ds(k0, TK), :] = (key >> 16).astype(I16)
        lo_ref[pl.ds(k0, TK), :] = (key ^ 0x8000).astype(I16)
        return carry

    lax.fori_loop(0, nkt - 1, score_tile, 0)
    score_tile(nkt - 1, 0, masked=True)

    cr = min(COUNT_ROWS, key_ref.shape[0])
    n_count = (nkt * TK + cr - 1) // cr

    def pad_tile(kt, carry):
        k0 = pl.multiple_of(kt * TK, TK)
        hi_ref[pl.ds(k0, TK), :] = jnp.full((TK, TQ), I16_MIN, I16)
        lo_ref[pl.ds(k0, TK), :] = jnp.full((TK, TQ), I16_MIN, I16)
        return carry

    lax.fori_loop(nkt, n_count * (cr // TK), pad_tile, 0)
    kf = float(topk)

    def count16(ref, cand):
        def body(kc, acc):
            k0 = pl.multiple_of(kc * cr, cr)
            hit = jnp.where(ref[pl.ds(k0, cr), :] >= cand, jnp.ones((cr, TQ), I16), jnp.zeros((cr, TQ), I16))
            return acc + _fold8(hit, rows=16)
        acc = lax.fori_loop(0, n_count, body, jnp.zeros((16, TQ), I16))
        return jnp.sum(acc.astype(F32), axis=0, keepdims=True)

    def search16(ref, cnt_all, signed=True, bits=15):
        if signed:
            c0 = count16(ref, jnp.zeros((1, TQ), I16))
            t = jnp.where(c0 >= kf, 0, I16_MIN).astype(I32)
            cnt = jnp.where(c0 >= kf, c0, cnt_all)
        else:
            t, cnt = jnp.zeros((1, TQ), I32), cnt_all

        def step(it, carry):
            t, cnt = carry
            cand = t + (jnp.int32(1) << (bits - 1 - it))
            c = count16(ref, cand.astype(I16))
            ok = c >= kf
            return jnp.where(ok, cand, t), jnp.where(ok, c, cnt)

        return lax.fori_loop(0, bits, step, (t, cnt))

    t_hi, cnt_hi = search16(hi_ref, (nkt * TK).astype(F32))
    t_hi16 = t_hi.astype(I16)

    top = jnp.full((cr, TQ), I16_MAX, I16)
    bottom = jnp.full((cr, TQ), I16_MIN, I16)

    def low_digit_tile(kc, carry):
        k0 = pl.multiple_of(kc * cr, cr)
        hi = hi_ref[pl.ds(k0, cr), :]
        in_bucket = jnp.where(hi == t_hi16, lo_ref[pl.ds(k0, cr), :], bottom)
        u_ref[pl.ds(k0, cr), :] = jnp.where(hi > t_hi16, top, in_bucket)
        return carry

    lax.fori_loop(0, n_count, low_digit_tile, 0)
    t_lo, cnt_lo = search16(u_ref, cnt_hi)
    t_lo16 = t_lo.astype(I16)
    thr = t_hi * 65536 + (t_lo + 32768)

    def index_digit_tile(kc, carry):
        k0 = pl.multiple_of(kc * cr, cr)
        hi, lo = hi_ref[pl.ds(k0, cr), :], lo_ref[pl.ds(k0, cr), :]
        same_hi = jnp.where(lo == t_lo16, ridx_ref[pl.ds(k0, cr), :], jnp.where(lo > t_lo16, top, bottom))
        u_ref[pl.ds(k0, cr), :] = jnp.where(hi == t_hi16, same_hi, jnp.where(hi > t_hi16, top, bottom))
        return carry

    n_keys = key_ref.shape[0]
    floor_key = jnp.int32(NEG_KEY + 1)
    short = thr < floor_key

    def index_search():
        lax.fori_loop(0, n_count, index_digit_tile, 0)
        return search16(u_ref, cnt_lo, signed=False, bits=(n_keys - 1).bit_length())[0]

    any_tie = jnp.max(jnp.where(short, 0.0, cnt_lo - kf)) > 0.0
    t_idx = lax.cond(any_tie, index_search, lambda: jnp.zeros((1, TQ), I32))
    lim_tie = jnp.where(short, floor_key, thr)
    lim_late = jnp.where(short, floor_key, thr + 1)
    last_tie = jnp.where(short, -1, (n_keys - 1) - t_idx)

    def select_tile(kt, carry):
        k0 = pl.multiple_of(kt * TK, TK)
        idx = k0 + lax.broadcasted_iota(I32, (TK, 1), 0)
        lim = jnp.where(idx <= last_tie, lim_tie, lim_late)
        bias_ref[pl.ds(k0, TK), :] = jnp.where(key_ref[pl.ds(k0, TK), :] >= lim, 0.0, NEG_INF)
        return carry

    lax.fori_loop(0, nkt, select_tile, 0)

    flash = _Flash(N_GH, *flash_refs)
    flash.init()

    def score(kt, is_first):
        k0 = pl.multiple_of(kt * TK, TK)
        kk = kv_ref[pl.ds(k0, TK), 0:LANES]
        bias = bias_ref[pl.ds(k0, TK), :]
        for h in range(N_GH):
            flash.scores(h, _dot_nt(kk, q_ref[:, h * LANES:(h + 1) * LANES]) + bias)

    def value(kt):
        k0 = pl.multiple_of(kt * TK, TK)
        for h in range(N_GH):
            flash.values(h, vt_ref[:, pl.ds(k0, TK)])

    flash.pipeline(nkt - 1, nkt - 1, score, value)
    y = jnp.concatenate([_head_norm_t(flash.result(h), g_ref[h * HEAD_DIM:(h + 1) * HEAD_DIM, :])
                         for h in range(N_GH)], axis=0)
    o_ref[...] = jnp.transpose(y).astype(BF16)


def _dsa_call(qa, wa, kva, vta, g_t, B, T):
    TQ = TQ_DSA
    assert TK % TQ == 0
    nq = T // TQ
    ridx = jnp.broadcast_to((T - 1 - jnp.arange(T, dtype=I32)).astype(I16)[:, None], (T, TQ))
    topk = min(TOPK_MAX, T // 4)
    return pl.pallas_call(
        functools.partial(_dsa_kernel, topk=topk),
        out_shape=jax.ShapeDtypeStruct((B * T, GROUP_W), BF16),
        grid=(B, nq),
        in_specs=[pl.BlockSpec((TQ, QA_W), lambda b, i: (b * nq + i, 0)),
                  pl.BlockSpec((TQ, LANES), lambda b, i: (b * nq + i, 0)),
                  pl.BlockSpec((T, KVA_W), lambda b, i: (b, 0)),
                  pl.BlockSpec((VT_ROWS, T), lambda b, i: (b, 0)),
                  pl.BlockSpec((GROUP_W, TQ), lambda b, i: (0, 0)),
                  _resident((T, TQ), lambda b, i: (0, 0))],
        out_specs=pl.BlockSpec((TQ, GROUP_W), lambda b, i: (b * nq + i, 0)),
        scratch_shapes=[pltpu.VMEM((T, TQ), F32), pltpu.VMEM((T, TQ), I32), pltpu.VMEM((T, TQ), I16),
                        pltpu.VMEM((T, TQ), I16), pltpu.VMEM((T, TQ), I16)] + _Flash.scratch(N_GH, TQ),
        compiler_params=_cparams("arbitrary", "arbitrary"),
        name="dsa",
    )(qa, wa, kva, vta, g_t, ridx)


def _diff_kernel(q_ref, k_ref, vt_ref, lam_ref, g_ref, o_ref, qm_ref, *flash_refs, lam_init):
    TQ = TQ_DIFF
    q0 = pl.program_id(1) * TQ
    nfull = (q0 + CHUNK) // TK
    qchunk = (q0 + _lane((1, TQ))) // CHUNK
    lv = lam_ref[...]
    lam = (jnp.exp(jnp.sum(lv[0:1] * lv[1:2], axis=1, keepdims=True))
           - jnp.exp(jnp.sum(lv[2:3] * lv[3:4], axis=1, keepdims=True)) + lam_init)
    n_maps = 2 * N_GH
    qt = jnp.transpose(q_ref[...].astype(F32))
    sub = lax.broadcasted_iota(I32, (LANES, 1), 0) // DIFF_DK
    for mi in range(n_maps):
        qg = qt[(mi // 4) * LANES:(mi // 4 + 1) * LANES, :]
        qm_ref[mi] = jnp.where(sub == mi % 4, qg, 0.0).astype(BF16)

    flash = _Flash(n_maps, *flash_refs)
    flash.init()

    def score(kt, masked):
        k0 = pl.multiple_of(kt * TK, TK)
        if masked:
            kchunk = (k0 + lax.broadcasted_iota(I32, (TK, 1), 0)) // CHUNK
            admissible = kchunk <= qchunk
        for mi in range(n_maps):
            g = mi // 4
            s = _dot(k_ref[pl.ds(k0, TK), g * LANES:(g + 1) * LANES], qm_ref[mi])
            flash.scores(mi, jnp.where(admissible, s, NEG_INF) if masked else s)

    def value(kt):
        k0 = pl.multiple_of(kt * TK, TK)
        for mi in range(n_maps):
            h = mi // 2
            flash.values(mi, vt_ref[h * VT_ROWS:(h + 1) * VT_ROWS, pl.ds(k0, TK)])

    flash.pipeline(nfull, nfull, score, value)
    outs = []
    for h in range(N_GH):
        o = flash.result(2 * h) - lam * flash.result(2 * h + 1)
        outs.append(_head_norm_t(o, g_ref[h * HEAD_DIM:(h + 1) * HEAD_DIM, :]) * (1.0 - lam_init))
    o_ref[...] = jnp.transpose(jnp.concatenate(outs, axis=0)).astype(BF16)


def _diff_call(dqkv, vtd, lam_vecs, g_t, lam_init, B, T):
    TQ = TQ_DIFF
    nq = T // TQ
    return pl.pallas_call(
        functools.partial(_diff_kernel, lam_init=lam_init),
        out_shape=jax.ShapeDtypeStruct((B * T, GROUP_W), BF16),
        grid=(B, nq),
        in_specs=[pl.BlockSpec((TQ, 256), lambda b, i: (b * nq + i, 0)),
                  pl.BlockSpec((T, 256), lambda b, i: (b, 1)),
                  pl.BlockSpec((N_GH * VT_ROWS, T), lambda b, i: (b, 0)),
                  pl.BlockSpec((4, LANES), lambda b, i: (0, 0)),
                  pl.BlockSpec((GROUP_W, TQ), lambda b, i: (0, 0))],
        out_specs=pl.BlockSpec((TQ, GROUP_W), lambda b, i: (b * nq + i, 0)),
        scratch_shapes=[pltpu.VMEM((2 * N_GH, LANES, TQ), BF16)] + _Flash.scratch(2 * N_GH, TQ),
        compiler_params=_cparams("arbitrary", "arbitrary"),
        name="diff",
    )(dqkv, dqkv, vtd, lam_vecs, g_t)


def _ret_tables():
    h = np.arange(N_GH, dtype=np.float32)
    log_g = jnp.log(1.0 - 2.0 ** (-5.0 - jnp.asarray(h)))
    t = np.arange(TB)
    same = (t[:, None] // CHUNK) == (t[None, :] // CHUNK)
    earlier = (t[None, :] // CHUNK) < (t[:, None] // CHUNK)
    dist = jnp.asarray(np.where(same, np.abs(t[:, None] - t[None, :]), t[:, None] - t[None, :]), F32)
    dmat = jnp.where(jnp.asarray(same | earlier)[None],
                     jnp.exp(dist[None] * log_g[:, None, None]), 0.0)
    tt = jnp.asarray(t, F32)
    xi = jnp.exp((tt + 1.0)[None, :] * log_g[:, None])
    zeta = jnp.exp((TB - 1.0 - tt)[None, :] * log_g[:, None])
    gblk = jnp.exp(TB * log_g)

    def lanes(tab):
        return jnp.repeat(tab.reshape(2, 2, TB), HEAD_DIM, axis=1).transpose(0, 2, 1)

    r = np.arange(LANES)
    blockdiag = (r[:, None] // HEAD_DIM) == (r[None, :] // HEAD_DIM)
    gb = jnp.where(jnp.asarray(blockdiag)[None],
                   jnp.repeat(gblk.reshape(2, 2), HEAD_DIM, axis=1)[:, :, None], 0.0)
    return dmat, lanes(xi), lanes(zeta), gb


def _ret_kernel(qkv_ref, gate_ref, dmat_ref, xi_ref, zeta_ref, gb_ref, g_ref, o_ref, r_ref):
    @pl.when(pl.program_id(1) == 0)
    def _():
        r_ref[...] = jnp.zeros_like(r_ref)

    lane = _lane((1, LANES))
    low = lane < HEAD_DIM
    r_idx = lax.broadcasted_iota(I32, (LANES, LANES), 0) // HEAD_DIM
    c_idx = lax.broadcasted_iota(I32, (LANES, LANES), 1) // HEAD_DIM
    for g in range(2):
        qg = qkv_ref[:, g * LANES:(g + 1) * LANES]
        kg = qkv_ref[:, 256 + g * LANES:256 + (g + 1) * LANES]
        vg = qkv_ref[:, 512 + g * LANES:512 + (g + 1) * LANES]
        state = r_ref[g]
        inter = _dot(qg, state.astype(BF16)) * xi_ref[g]
        parts = []
        for hh in range(2):
            qm = jnp.where(low, qg, jnp.zeros_like(qg)) if hh == 0 else jnp.where(low, jnp.zeros_like(qg), qg)
            s = _dot_nt(qm, kg) * dmat_ref[2 * g + hh]
            parts.append(_dot(s.astype(BF16), vg))
        o = jnp.where(low, parts[0], parts[1]) + inter
        kz = (kg.astype(F32) * zeta_ref[g]).astype(BF16)
        upd = jnp.where(r_idx == c_idx, _dot_tn(kz, vg), 0.0)
        r_ref[g] = state * gb_ref[g] + upd
        y = _pair_head_norm(o, g_ref[:, g * LANES:(g + 1) * LANES], True)
        o_ref[:, g * LANES:(g + 1) * LANES] = (_silu(gate_ref[:, g * LANES:(g + 1) * LANES]) * y).astype(BF16)


def _ret_specs(nb):
    const3 = lambda b, i: (0, 0, 0)
    in_specs = [pl.BlockSpec((TB, QKV_W), lambda b, i: (b * nb + i, 0)),
                pl.BlockSpec((TB, GROUP_W), lambda b, i: (b * nb + i, 0)),
                pl.BlockSpec((N_GH, TB, TB), const3),
                pl.BlockSpec((2, TB, LANES), const3),
                pl.BlockSpec((2, TB, LANES), const3),
                pl.BlockSpec((2, LANES, LANES), const3),
                pl.BlockSpec((1, GROUP_W), lambda b, i: (0, 0))]
    return in_specs, [pltpu.VMEM((2, LANES, LANES), F32)]


def _gla_kernel(c_ref, gate_ref, wa2_ref, ba_ref, g_ref, o_ref, s_ref, b_scr, q_scr, k_scr, v_scr, acc_scr):
    @pl.when(pl.program_id(1) == 0)
    def _():
        s_ref[...] = jnp.zeros_like(s_ref)

    n_chunks = TB // CHUNK
    hmask = (lax.broadcasted_iota(I32, (LANES, GROUP_W), 0) // GLA_DK
             == lax.broadcasted_iota(I32, (LANES, GROUP_W), 1) // HEAD_DIM)
    hsum = jnp.where(hmask, 1.0, 0.0).astype(BF16)
    r_idx = lax.broadcasted_iota(I32, (TB, TB), 0)
    c_idx = lax.broadcasted_iota(I32, (TB, TB), 1)
    tri = jnp.where((r_idx >= c_idx) & (r_idx // CHUNK == c_idx // CHUNK), 1.0, 0.0).astype(BF16)

    z = _dot_split(c_ref[:, 512:640], wa2_ref[...]) + ba_ref[...]
    log_a = (jnp.minimum(z, 0.0) - jnp.log(1.0 + jnp.exp(-jnp.abs(z)))) * (1.0 / GLA_TAU)
    b = sum(_dot(tri, piece) for piece in _split_bf16(log_a, 3))
    q = c_ref[:, 0:128] * (GLA_DK ** -0.5)
    b_scr[...] = b * LOG2E
    q_scr[...] = q
    k_scr[...] = c_ref[:, 128:256]
    v_scr[...] = c_ref[:, 256:512]

    b_last = [b[(c + 1) * CHUNK - 1:(c + 1) * CHUNK, :] for c in range(n_chunks)]
    b_end = jnp.concatenate([jnp.broadcast_to(bl, (CHUNK, LANES)) for bl in b_last], axis=0)
    kd = (c_ref[:, 128:256] * jnp.exp(b_end - b)).astype(BF16)
    qe = (q * jnp.exp(b)).astype(BF16)
    vb = c_ref[:, 256:512].astype(BF16)
    state = s_ref[...]
    inters = []
    for c in range(n_chunks):
        rows = slice(c * CHUNK, (c + 1) * CHUNK)
        inters.append(_dot(qe[rows], state.astype(BF16)))
        scale = jnp.transpose(jnp.broadcast_to(jnp.exp(b_last[c]), (LANES, LANES)))
        state = jnp.where(hmask, state * jnp.concatenate([scale, scale], axis=1) + _dot_tn(kd[rows], vb[rows]), 0.0)
    s_ref[...] = state

    acc_scr[...] = jnp.concatenate(inters, axis=0)

    keys_per_step = 64

    def key_step(s4, carry):
        for c in range(n_chunks):
            rows = slice(c * CHUNK, (c + 1) * CHUNK)
            part = None
            for u in range(keys_per_step):
                r = c * CHUNK + s4 * keys_per_step + u
                gap = pltpu.bitcast(b_scr[rows, :] - b_scr[pl.ds(r, 1), :], I32)
                decay = jnp.exp2(pltpu.bitcast(gap | jnp.int32(INT_MIN), F32))
                w = (q_scr[rows, :] * decay * k_scr[pl.ds(r, 1), :]).astype(BF16)
                term = _dot(w, hsum) * v_scr[pl.ds(r, 1), :]
                part = term if part is None else part + term
            acc_scr[rows, :] += part
        return carry

    lax.fori_loop(0, CHUNK // keys_per_step, key_step, 0)
    o = acc_scr[...]
    for gi in range(2):
        y = _pair_head_norm(o[:, gi * LANES:(gi + 1) * LANES], g_ref[:, gi * LANES:(gi + 1) * LANES], True)
        o_ref[:, gi * LANES:(gi + 1) * LANES] = (_silu(gate_ref[:, gi * LANES:(gi + 1) * LANES]) * y).astype(BF16)


def _gla_specs(nb):
    in_specs = [pl.BlockSpec((TB, C_W), lambda b, i: (b * nb + i, 0)),
                pl.BlockSpec((TB, GROUP_W), lambda b, i: (b * nb + i, 1)),
                pl.BlockSpec((LANES, LANES), lambda b, i: (0, 0)),
                pl.BlockSpec((1, LANES), lambda b, i: (0, 0)),
                pl.BlockSpec((1, GROUP_W), lambda b, i: (0, 0))]
    scratch = [pltpu.VMEM((LANES, GROUP_W), F32), pltpu.VMEM((TB, LANES), F32),
               pltpu.VMEM((TB, LANES), F32), pltpu.VMEM((TB, LANES), F32),
               pltpu.VMEM((TB, GROUP_W), F32), pltpu.VMEM((TB, GROUP_W), F32)]
    return in_specs, scratch


def _ret_gla_kernel(*refs):
    n_ret, n_gla = 7, 5
    o_b, o_c = refs[n_ret + n_gla:n_ret + n_gla + 2]
    scratch = refs[n_ret + n_gla + 2:]
    _ret_kernel(*refs[:n_ret], o_b, scratch[0])
    _gla_kernel(*refs[n_ret:n_ret + n_gla], o_c, *scratch[1:])


def _ret_gla_call(qkvb, gates, ret_tabs, g_b, cc, wa2_p, ba, g_c, B, T):
    nb = T // TB
    ret_in, ret_scratch = _ret_specs(nb)
    gla_in, gla_scratch = _gla_specs(nb)
    out = jax.ShapeDtypeStruct((B * T, GROUP_W), BF16)
    out_spec = pl.BlockSpec((TB, GROUP_W), lambda b, i: (b * nb + i, 0))
    return pl.pallas_call(
        _ret_gla_kernel,
        out_shape=[out, out],
        grid=(B, nb),
        in_specs=ret_in + gla_in,
        out_specs=[out_spec, out_spec],
        scratch_shapes=ret_scratch + gla_scratch,
        compiler_params=_cparams("arbitrary", "arbitrary"),
        name="ret_gla",
    )(qkvb, gates, *ret_tabs, g_b, cc, gates, wa2_p, ba, g_c)


def _rms(y, g):
    return y * lax.rsqrt(jnp.mean(y * y, axis=-1, keepdims=True) + EPS) * g


def _out_mlp_kernel(x_ref, ya_ref, yb_ref, yc_ref, yd_ref, mod_ref, wo_ref, w1_ref, w2_ref,
                    gpost_ref, gpre_ref, gpost2_ref, o_ref):
    y = _dot(ya_ref[...], wo_ref[0:256, :])
    y = y + _dot(yb_ref[...], wo_ref[256:512, :])
    y = y + _dot(yc_ref[...], wo_ref[512:768, :])
    y = y + _dot(yd_ref[...], wo_ref[768:1024, :])
    x = x_ref[...] + mod_ref[0, 2:3, :] * _rms(y, gpost_ref[...])
    h = _rms(x, gpre_ref[...]) * (1.0 + mod_ref[0, 4:5, :]) + mod_ref[0, 3:4, :]
    hb = h.astype(BF16)
    acc = jnp.zeros(x.shape, F32)
    fc = 1024
    for f in range(D_FF // fc):
        u = jnp.maximum(_dot(hb, w1_ref[:, f * fc:(f + 1) * fc]), 0.0)
        acc = acc + _dot((u * u).astype(BF16), w2_ref[f * fc:(f + 1) * fc, :])
    o_ref[...] = x + mod_ref[0, 5:6, :] * _rms(acc, gpost2_ref[...])


def _out_mlp_call(x2, ys, mod3, w_out, w1, w2, g_post, g_pre2, g_post2, T):
    N, D = x2.shape
    nt = T // TM
    row = lambda i: (i, 0)
    const = lambda i: (0, 0)
    return pl.pallas_call(
        _out_mlp_kernel,
        out_shape=jax.ShapeDtypeStruct((N, D), F32),
        grid=(N // TM,),
        in_specs=[pl.BlockSpec((TM, D), row)] + [pl.BlockSpec((TM, GROUP_W), row)] * 4
                 + [pl.BlockSpec((1, 6, D), lambda i: (i // nt, 0, 0)),
                    _resident((D, D), const), _resident((D, D_FF), const), _resident((D_FF, D), const),
                    _resident((1, D), const), _resident((1, D), const), _resident((1, D), const)],
        out_specs=pl.BlockSpec((TM, D), row),
        compiler_params=_cparams("arbitrary"),
        name="out_mlp",
    )(x2, *ys, mod3, w_out, w1, w2, g_post, g_pre2, g_post2)


def _relayout_w_in(w_in):
    col = lambda k: w_in[:, _IN_OFF[k]:_IN_OFF[k + 1]]
    zeros = lambda n: jnp.zeros((w_in.shape[0], n), w_in.dtype)
    parts = [col(_A_Q), col(_A_QI), col(_A_K), col(_A_KI), col(_A_V), zeros(64),
             col(_A_WI), zeros(120),
             col(_B_Q), col(_B_K), col(_B_V),
             col(_B_G), col(_C_G),
             col(_C_Q), col(_C_K), col(_C_V), col(_C_A), zeros(112),
             col(_D_Q), col(_D_K), col(_D_V)]
    return jnp.concatenate(parts, axis=1).astype(BF16)


def _rope_tables(pos, rot, period, theta):
    half = rot // 2
    inv = theta ** (-jnp.arange(half, dtype=F32) / half)
    ang = pos.astype(F32)[..., None] * inv
    cos, sin = jnp.cos(ang), jnp.sin(ang)
    pad = ang.shape[:-1] + (period - rot,)
    ct = jnp.concatenate([cos, cos, jnp.ones(pad, F32)], axis=-1)
    st = jnp.concatenate([-sin, sin, jnp.zeros(pad, F32)], axis=-1)
    return ct.reshape(-1, period), st.reshape(-1, period)


def _layer(x2, cond, tabs, ret_tabs, layer_idx, B, T, mod_w, mod_b, attn_pre_g, attn_post_g,
           mlp_pre_g, mlp_post_g, w_in, gla_wa2, gla_ba, lam_q1, lam_k1, lam_q2, lam_k2,
           head_norm_g, w_out, mlp_w1, mlp_w2):
    D = D_MODEL
    mod3 = _mod_call(cond, mod_w, mod_b).reshape(B, 6, D)
    qa, kva, wa, qkvb, gates, cc, dqkv, vta, vtd = _in_call(
        x2, mod3, attn_pre_g.reshape(1, D), _relayout_w_in(w_in), tabs, T)

    g_a, g_b, g_c, g_d = jnp.split(head_norm_g, 4)
    gain_t = lambda g, tq: jnp.broadcast_to(g[:, None], (GROUP_W, tq))
    y_a = _dsa_call(qa, wa, kva, vta, gain_t(g_a, TQ_DSA), B, T)
    wa2_p = jnp.zeros((LANES, LANES), F32).at[:GLA_RANK].set(gla_wa2)
    y_b, y_c = _ret_gla_call(qkvb, gates, ret_tabs, g_b.reshape(1, GROUP_W), cc, wa2_p,
                             gla_ba.reshape(1, LANES), g_c.reshape(1, GROUP_W), B, T)
    lam_init = 0.8 - 0.6 * math.exp(-0.3 * layer_idx)
    lam_vecs = jnp.zeros((4, LANES), F32).at[:, :DIFF_DK].set(jnp.stack([lam_q1, lam_k1, lam_q2, lam_k2]))
    y_d = _diff_call(dqkv, vtd, lam_vecs, gain_t(g_d, TQ_DIFF), lam_init, B, T)

    return _out_mlp_call(x2, (y_a, y_b, y_c, y_d), mod3, w_out.astype(BF16), mlp_w1.astype(BF16),
                         mlp_w2.astype(BF16), attn_post_g.reshape(1, D), mlp_pre_g.reshape(1, D),
                         mlp_post_g.reshape(1, D), T)


def kernel(x, c, positions, mod_w, mod_b, attn_pre_g, attn_post_g, mlp_pre_g, mlp_post_g, w_in,
           gla_wa2, gla_ba, lam_q1, lam_k1, lam_q2, lam_k2, head_norm_g, w_out, mlp_w1, mlp_w2):
    B, T, D = x.shape
    assert D == D_MODEL and T % TM == 0 and T % TB == 0
    tabs = (_rope_tables(positions, HEAD_DIM // ROPE_FRAC, HEAD_DIM, ROPE_THETA)
            + _rope_tables(positions, HEAD_DIM, HEAD_DIM, RET_THETA)
            + _rope_tables(positions, DIFF_DK // ROPE_FRAC, DIFF_DK, ROPE_THETA))
    ret_tabs = _ret_tables()
    x2 = x.reshape(B * T, D)
    for l in range(mod_w.shape[0]):
        x2 = _layer(x2, c, tabs, ret_tabs, l, B, T, mod_w[l], mod_b[l], attn_pre_g[l], attn_post_g[l],
                    mlp_pre_g[l], mlp_post_g[l], w_in[l], gla_wa2[l], gla_ba[l], lam_q1[l], lam_k1[l],
                    lam_q2[l], lam_k2[l], head_norm_g[l], w_out[l], mlp_w1[l], mlp_w2[l])
    return x2.reshape(B, T, D)
```

```python
import functools
import math

import numpy as np
import jax
import jax.numpy as jnp
from jax import lax
from jax.experimental import pallas as pl
from jax.experimental.pallas import tpu as pltpu

F32 = jnp.float32
BF16 = jnp.bfloat16
I32 = jnp.int32
I16 = jnp.int16

D_MODEL = 1024
CHUNK = 64
HEAD_DIM = 64
N_GH = 4
GROUP_W = N_GH * HEAD_DIM
IDX_HEADS = 8
IDX_DIM = 64
TOPK_MAX = 256
RET_THETA = 10000.0
GLA_DK = HEAD_DIM // 2
GLA_RANK = 16
GLA_TAU = 16.0
DIFF_DK = HEAD_DIM // 2
ROPE_THETA = 500000.0
ROPE_FRAC = 4
D_FF = 4 * D_MODEL
EPS = 1e-6
NEG_INF = -1e30

LANES = 128
VMEM_LIMIT = 56 * 1024 * 1024

_IN_SPLITS = (
    GROUP_W, HEAD_DIM, HEAD_DIM, IDX_HEADS * IDX_DIM, IDX_DIM, IDX_HEADS,
    GROUP_W, GROUP_W, GROUP_W, GROUP_W,
    N_GH * GLA_DK, N_GH * GLA_DK, GROUP_W, GLA_RANK, GROUP_W,
    2 * N_GH * DIFF_DK, 2 * N_GH * DIFF_DK, GROUP_W,
)
_IN_OFF = np.concatenate([[0], np.cumsum(_IN_SPLITS)]).tolist()
(_A_Q, _A_K, _A_V, _A_QI, _A_KI, _A_WI, _B_Q, _B_K, _B_V, _B_G,
 _C_Q, _C_K, _C_V, _C_A, _C_G, _D_Q, _D_K, _D_V) = range(18)

SEG_A = 0
SEG_W = 1024
SEG_B = 1152
SEG_G = 1920
SEG_C = 2432
SEG_D = 3072
W_COLS = 3840
QA_W = 12 * LANES
KVA_W = 2 * LANES
QKV_W = 3 * GROUP_W
GATES_W = 2 * GROUP_W
C_W = 5 * LANES
I16_MIN, I16_MAX = -32768, 32767

NT_DIMS = (((1,), (1,)), ((), ()))
TN_DIMS = (((0,), (0,)), ((), ()))

TM = 512
TQ_DSA = 256
TQ_DIFF = 512
TK = 512
COUNT_ROWS = 512
FLASH_SLAB = 32
VT_ROWS = HEAD_DIM + 16
LOG2E = math.log2(math.e)
TB = 256


def _float_key(v):
    bits = int(np.array(v, np.float32).view(np.int32))
    return -(bits & 0x7FFFFFFF) if bits < 0 else bits


NEG_KEY = _float_key(NEG_INF)
INT_MIN = -(2 ** 31)


def _cparams(*sem):
    return pltpu.CompilerParams(dimension_semantics=sem, vmem_limit_bytes=VMEM_LIMIT)


def _resident(shape, index_map):
    return pl.BlockSpec(shape, index_map, pipeline_mode=pl.Buffered(1))


def _dot(a, b):
    return jnp.dot(a, b, preferred_element_type=F32)


def _dot_nt(a, b):
    return lax.dot_general(a, b, NT_DIMS, preferred_element_type=F32)


def _dot_tn(a, b):
    return lax.dot_general(a, b, TN_DIMS, preferred_element_type=F32)


def _dot_f32(a, b):
    return jnp.dot(a, b, preferred_element_type=F32, precision=lax.Precision.HIGHEST)


def _split_bf16(x, n):
    parts = []
    for _ in range(n):
        p = x.astype(BF16)
        parts.append(p)
        x = x - p.astype(F32)
    return parts


def _dot_split(a, b):
    (a0, a1), (b0, b1) = _split_bf16(a, 2), _split_bf16(b, 2)
    return _dot(a0, b0) + (_dot(a0, b1) + _dot(a1, b0))


def _lane(shape):
    return lax.broadcasted_iota(I32, shape, len(shape) - 1)


def _silu(x):
    return x / (1.0 + jnp.exp(-x))


def _mod_kernel(c_ref, w_ref, b_ref, o_ref):
    cond = _silu(c_ref[...])
    o_ref[...] = _dot_f32(cond, w_ref[...]) + b_ref[...]


def _mod_call(c, mod_w, mod_b):
    B, D = c.shape
    n = mod_w.shape[1] // D
    return pl.pallas_call(
        _mod_kernel,
        out_shape=jax.ShapeDtypeStruct((B, n * D), F32),
        grid=(n,),
        in_specs=[pl.BlockSpec((B, D), lambda j: (0, 0)),
                  pl.BlockSpec((D, D), lambda j: (0, j)),
                  pl.BlockSpec((1, D), lambda j: (0, j))],
        out_specs=pl.BlockSpec((B, D), lambda j: (0, j)),
        compiler_params=_cparams("arbitrary"),
        name="mod",
    )(c, mod_w, mod_b.reshape(1, -1))


def _rope_group(xg, cos, sin, half, period):
    lo = (_lane((1, LANES)) % period) < half
    swapped = jnp.where(lo, pltpu.roll(xg, LANES - half, 1), pltpu.roll(xg, half, 1))
    return xg * cos + swapped * sin


def _in_kernel(x_ref, mod_ref, g_ref, w_ref, ca_ref, sa_ref, cb_ref, sb_ref, cd_ref, sd_ref,
               qa_ref, kva_ref, wa_ref, qkvb_ref, gate_ref, c_ref, d_ref, vta_ref, vtd_ref):
    x = x_ref[...]
    h = x * lax.rsqrt(jnp.mean(x * x, axis=-1, keepdims=True) + EPS) * g_ref[...]
    h = h * (1.0 + mod_ref[0, 1:2, :]) + mod_ref[0, 0:1, :]
    hb = h.astype(BF16)

    def seg(off, width):
        return _dot(hb, w_ref[:, off:off + width])

    lane = _lane((1, LANES))
    low = lane < HEAD_DIM

    def lanes(t_ref):
        t = t_ref[...]
        return jnp.concatenate([t] * (LANES // t.shape[1]), axis=1)

    def put_q(slot, variant):
        qa_ref[slot * LANES:(slot + 1) * LANES, :] = jnp.transpose(variant).astype(BF16)

    ca, sa = lanes(ca_ref), lanes(sa_ref)
    a = seg(SEG_A, 1024)
    half_a = HEAD_DIM // ROPE_FRAC // 2
    for gi in range(2):
        r = _rope_group(a[:, gi * LANES:(gi + 1) * LANES], ca, sa, half_a, HEAD_DIM) * (HEAD_DIM ** -0.5 * LOG2E)
        rs = pltpu.roll(r, HEAD_DIM, 1)
        put_q(2 * gi, jnp.where(low, r, 0.0))
        put_q(2 * gi + 1, jnp.where(low, rs, 0.0))
    for gi in range(4):
        r = _rope_group(a[:, (2 + gi) * LANES:(3 + gi) * LANES], ca, sa, half_a, HEAD_DIM)
        rs = pltpu.roll(r, HEAD_DIM, 1)
        put_q(4 + 2 * gi, jnp.where(low, 0.0, rs))
        put_q(5 + 2 * gi, jnp.where(low, 0.0, r))
    kva_ref[:, 0:LANES] = _rope_group(a[:, 6 * LANES:7 * LANES], ca, sa, half_a, HEAD_DIM).astype(BF16)
    kva_ref[:, LANES:2 * LANES] = a[:, 7 * LANES:8 * LANES].astype(BF16)
    ones_rows = jnp.ones((VT_ROWS - HEAD_DIM, TM), BF16)
    vta_ref[0:HEAD_DIM, :] = jnp.transpose(a[:, 7 * LANES:8 * LANES])[0:HEAD_DIM, :].astype(BF16)
    vta_ref[HEAD_DIM:VT_ROWS, :] = ones_rows

    wa_ref[...] = seg(SEG_W, LANES)

    cb, sb = lanes(cb_ref), lanes(sb_ref)
    b = seg(SEG_B, QKV_W)
    for gi in range(4):
        r = _rope_group(b[:, gi * LANES:(gi + 1) * LANES], cb, sb, HEAD_DIM // 2, HEAD_DIM)
        if gi < 2:
            r = r * (HEAD_DIM ** -0.5)
        qkvb_ref[:, gi * LANES:(gi + 1) * LANES] = r.astype(BF16)
    qkvb_ref[:, 512:768] = b[:, 512:768].astype(BF16)

    gate_ref[...] = seg(SEG_G, GATES_W)
    c_ref[...] = seg(SEG_C, C_W)

    cd, sd = lanes(cd_ref), lanes(sd_ref)
    d = seg(SEG_D, QKV_W)
    half_d = DIFF_DK // ROPE_FRAC // 2
    for gi in range(4):
        r = _rope_group(d[:, gi * LANES:(gi + 1) * LANES], cd, sd, half_d, DIFF_DK)
        if gi < 2:
            r = r * (DIFF_DK ** -0.5 * LOG2E)
        d_ref[:, gi * LANES:(gi + 1) * LANES] = r.astype(BF16)
    d_ref[:, 512:768] = d[:, 512:768].astype(BF16)
    vt = jnp.transpose(d[:, 512:768])
    for h in range(N_GH):
        vtd_ref[h * VT_ROWS:h * VT_ROWS + HEAD_DIM, :] = vt[h * HEAD_DIM:(h + 1) * HEAD_DIM, :].astype(BF16)
        vtd_ref[h * VT_ROWS + HEAD_DIM:(h + 1) * VT_ROWS, :] = ones_rows


def _in_call(x2, mod3, g, w_p, tabs, T):
    N, D = x2.shape
    nt = T // TM
    row = lambda i: (i, 0)
    tab_specs = [pl.BlockSpec((TM, t.shape[1]), row) for t in tabs]
    outs = [(KVA_W, BF16), (LANES, F32), (QKV_W, BF16), (GATES_W, F32), (C_W, F32), (QKV_W, BF16)]
    qa_shape = jax.ShapeDtypeStruct((QA_W, N), BF16)
    qa_spec = pl.BlockSpec((QA_W, TM), lambda i: (0, i))
    vt_rows = (VT_ROWS, N_GH * VT_ROWS)
    vt_shapes = [jax.ShapeDtypeStruct((N // T * r, T), BF16) for r in vt_rows]
    vt_specs = [pl.BlockSpec((r, TM), lambda i: (i // nt, i % nt)) for r in vt_rows]
    return pl.pallas_call(
        _in_kernel,
        out_shape=[qa_shape] + [jax.ShapeDtypeStruct((N, w), dt) for w, dt in outs] + vt_shapes,
        grid=(N // TM,),
        in_specs=[pl.BlockSpec((TM, D), row),
                  pl.BlockSpec((1, 6, D), lambda i: (i // nt, 0, 0)),
                  _resident((1, D), lambda i: (0, 0)),
                  _resident((D, W_COLS), lambda i: (0, 0))] + tab_specs,
        out_specs=[qa_spec] + [pl.BlockSpec((TM, w), row) for w, _ in outs] + vt_specs,
        compiler_params=_cparams("arbitrary"),
        name="in_proj",
    )(x2, mod3, g, w_p, *tabs)


def _pair_head_norm(o, gain, center):
    low = _lane((1, LANES)) < HEAD_DIM
    inv = 1.0 / HEAD_DIM

    def seg_mean(v):
        m0 = jnp.sum(jnp.where(low, v, 0.0), axis=1, keepdims=True)
        m1 = jnp.sum(jnp.where(low, 0.0, v), axis=1, keepdims=True)
        return jnp.where(low, m0, m1) * inv

    if center:
        o = o - seg_mean(o)
    return o * lax.rsqrt(seg_mean(o * o) + EPS) * gain


class _Flash:
    def __init__(self, n, s_ref, p_ref, acc_ref, st_ref):
        self.n, self.s, self.p, self.acc, self.st = n, s_ref, p_ref, acc_ref, st_ref
        self.tq = s_ref.shape[-1]

    @staticmethod
    def scratch(n, tq):
        return [pltpu.VMEM((n, TK, tq), F32), pltpu.VMEM((n, TK, tq), BF16),
                pltpu.VMEM((n, VT_ROWS, tq), F32), pltpu.VMEM((3, n, tq), F32)]

    def _row(self, k, i):
        return self.st.at[k, i:i + 1, :]

    def init(self):
        self.st[0] = jnp.full((self.n, self.tq), NEG_INF, F32)
        self.acc[...] = jnp.zeros_like(self.acc)

    def scores(self, i, s):
        self.s[i] = s
        self._row(1, i)[...] = jnp.max(_fold8(s, jnp.maximum), axis=0, keepdims=True)

    def probs(self, i):
        m_old = self._row(0, i)[...]
        m_new = jnp.maximum(m_old, self._row(1, i)[...])
        for r in range(0, TK, FLASH_SLAB):
            self.p[i, r:r + FLASH_SLAB, :] = jnp.exp2(self.s[i, r:r + FLASH_SLAB, :] - m_new).astype(BF16)
        self._row(0, i)[...] = m_new
        self._row(2, i)[...] = jnp.exp2(m_old - m_new)

    def values(self, i, vt):
        self.acc[i] = self._row(2, i)[...] * self.acc[i] + _dot(vt, self.p[i])

    def result(self, i):
        return self.acc[i, 0:HEAD_DIM, :] / self.acc[i, HEAD_DIM:HEAD_DIM + 1, :]

    def pipeline(self, first, n_rest, score, value):
        score(first, True)

        def body(t, carry):
            for i in range(self.n):
                self.probs(i)
            score(t, False)
            value(jnp.where(t == 0, first, t - 1))
            return carry

        lax.fori_loop(0, n_rest, body, 0)
        for i in range(self.n):
            self.probs(i)
        value(jnp.where(n_rest == 0, first, n_rest - 1))


def _fold8(x, op=jnp.add, rows=8):
    parts = [x[r:r + rows] for r in range(0, x.shape[0], rows)]
    while len(parts) > 1:
        parts = [op(a, b) for a, b in zip(parts[0::2], parts[1::2])]
    return parts[0]


def _head_norm_t(o, gain):
    ms = jnp.mean(o * o, axis=0, keepdims=True)
    return o * lax.rsqrt(ms + EPS) * gain


def _dsa_kernel(q_ref, w_ref, kv_ref, vt_ref, g_ref, ridx_ref, o_ref, bias_ref, key_ref, hi_ref, lo_ref, u_ref,
                *flash_refs, topk):
    TQ = TQ_DSA
    q0 = pl.program_id(1) * TQ
    nkt = (q0 + TQ + TK - 1) // TK
    qchunk = (q0 + _lane((1, TQ))) // CHUNK
    wts = jnp.transpose(w_ref[...]) * (IDX_HEADS ** -0.5 * IDX_DIM ** -0.5)

    def score_tile(kt, carry, masked=False):
        k0 = pl.multiple_of(kt * TK, TK)
        kk = kv_ref[pl.ds(k0, TK), 0:LANES]
        tile = bias_ref.at[pl.ds(k0, TK), :]

        def head_term(h):
            return jnp.maximum(_dot(kk, q_ref[(4 + h) * LANES:(5 + h) * LANES, :]), 0.0) * wts[h:h + 1, :]

        for h in range(0, IDX_HEADS, 2):
            term = head_term(h) + head_term(h + 1)
            if h == 0:
                tile[...] = term
            elif h < IDX_HEADS - 2:
                tile[...] += term
        acc = tile[...] + term
        if masked:
            kchunk = (k0 + lax.broadcasted_iota(I32, (TK, 1), 0)) // CHUNK
            acc = jnp.where(kchunk <= qchunk, acc, NEG_INF)
        bits = pltpu.bitcast(acc, I32)
        sign = bits >> 31
        key = ((bits & 0x7FFFFFFF) ^ sign) - sign
        key_ref[pl.ds(k0, TK), :] = key
        hi_ref[pl.ds(k0, TK), :] = (key >> 16).astype(I16)
        lo_ref[pl.ds(k0, TK), :] = (key ^ 0x8000).astype(I16)
        return carry

    lax.fori_loop(0, nkt - 1, score_tile, 0)
    score_tile(nkt - 1, 0, masked=True)

    cr = min(COUNT_ROWS, key_ref.shape[0])
    n_count = (nkt * TK + cr - 1) // cr

    def pad_tile(kt, carry):
        k0 = pl.multiple_of(kt * TK, TK)
        hi_ref[pl.ds(k0, TK), :] = jnp.full((TK, TQ), I16_MIN, I16)
        lo_ref[pl.ds(k0, TK), :] = jnp.full((TK, TQ), I16_MIN, I16)
        return carry

    lax.fori_loop(nkt, n_count * (cr // TK), pad_tile, 0)
    kf = float(topk)

    def count16(ref, cand):
        def body(kc, acc):
            k0 = pl.multiple_of(kc * cr, cr)
            hit = jnp.where(ref[pl.ds(k0, cr), :] >= cand, jnp.ones((cr, TQ), I16), jnp.zeros((cr, TQ), I16))
            return acc + _fold8(hit, rows=16)
        acc = lax.fori_loop(0, n_count, body, jnp.zeros((16, TQ), I16))
        return jnp.sum(acc.astype(F32), axis=0, keepdims=True)

    def search16(ref, cnt_all, signed=True, bits=15):
        if signed:
            c0 = count16(ref, jnp.zeros((1, TQ), I16))
            t = jnp.where(c0 >= kf, 0, I16_MIN).astype(I32)
            cnt = jnp.where(c0 >= kf, c0, cnt_all)
        else:
            t, cnt = jnp.zeros((1, TQ), I32), cnt_all

        def step(it, carry):
            t, cnt = carry
            cand = t + (jnp.int32(1) << (bits - 1 - it))
            c = count16(ref, cand.astype(I16))
            ok = c >= kf
            return jnp.where(ok, cand, t), jnp.where(ok, c, cnt)

        return lax.fori_loop(0, bits, step, (t, cnt))

    t_hi, cnt_hi = search16(hi_ref, (nkt * TK).astype(F32))
    t_hi16 = t_hi.astype(I16)

    top = jnp.full((cr, TQ), I16_MAX, I16)
    bottom = jnp.full((cr, TQ), I16_MIN, I16)

    def low_digit_tile(kc, carry):
        k0 = pl.multiple_of(kc * cr, cr)
        hi = hi_ref[pl.ds(k0, cr), :]
        in_bucket = jnp.where(hi == t_hi16, lo_ref[pl.ds(k0, cr), :], bottom)
        u_ref[pl.ds(k0, cr), :] = jnp.where(hi > t_hi16, top, in_bucket)
        return carry

    lax.fori_loop(0, n_count, low_digit_tile, 0)
    t_lo, cnt_lo = search16(u_ref, cnt_hi)
    t_lo16 = t_lo.astype(I16)
    thr = t_hi * 65536 + (t_lo + 32768)

    def index_digit_tile(kc, carry):
        k0 = pl.multiple_of(kc * cr, cr)
        hi, lo = hi_ref[pl.ds(k0, cr), :], lo_ref[pl.ds(k0, cr), :]
        same_hi = jnp.where(lo == t_lo16, ridx_ref[pl.ds(k0, cr), :], jnp.where(lo > t_lo16, top, bottom))
        u_ref[pl.ds(k0, cr), :] = jnp.where(hi == t_hi16, same_hi, jnp.where(hi > t_hi16, top, bottom))
        return carry

    n_keys = key_ref.shape[0]
    floor_key = jnp.int32(NEG_KEY + 1)
    short = thr < floor_key

    def index_search():
        lax.fori_loop(0, n_count, index_digit_tile, 0)
        return search16(u_ref, cnt_lo, signed=False, bits=(n_keys - 1).bit_length())[0]

    any_tie = jnp.max(jnp.where(short, 0.0, cnt_lo - kf)) > 0.0
    t_idx = lax.cond(any_tie, index_search, lambda: jnp.zeros((1, TQ), I32))
    lim_tie = jnp.where(short, floor_key, thr)
    lim_late = jnp.where(short, floor_key, thr + 1)
    last_tie = jnp.where(short, -1, (n_keys - 1) - t_idx)

    def select_tile(kt, carry):
        k0 = pl.multiple_of(kt * TK, TK)
        idx = k0 + lax.broadcasted_iota(I32, (TK, 1), 0)
        lim = jnp.where(idx <= last_tie, lim_tie, lim_late)
        bias_ref[pl.ds(k0, TK), :] = jnp.where(key_ref[pl.ds(k0, TK), :] >= lim, 0.0, NEG_INF)
        return carry

    lax.fori_loop(0, nkt, select_tile, 0)

    flash = _Flash(N_GH, *flash_refs)
    flash.init()

    def score(kt, is_first):
        k0 = pl.multiple_of(kt * TK, TK)
        kk = kv_ref[pl.ds(k0, TK), 0:LANES]
        bias = bias_ref[pl.ds(k0, TK), :]
        for h in range(N_GH):
            flash.scores(h, _dot(kk, q_ref[h * LANES:(h + 1) * LANES, :]) + bias)

    def value(kt):
        k0 = pl.multiple_of(kt * TK, TK)
        for h in range(N_GH):
            flash.values(h, vt_ref[:, pl.ds(k0, TK)])

    flash.pipeline(nkt - 1, nkt - 1, score, value)
    y = jnp.concatenate([_head_norm_t(flash.result(h), g_ref[h * HEAD_DIM:(h + 1) * HEAD_DIM, :])
                         for h in range(N_GH)], axis=0)
    o_ref[...] = jnp.transpose(y).astype(BF16)


def _dsa_call(qa, wa, kva, vta, g_t, B, T):
    TQ = TQ_DSA
    assert TK % TQ == 0
    nq = T // TQ
    ridx = jnp.broadcast_to((T - 1 - jnp.arange(T, dtype=I32)).astype(I16)[:, None], (T, TQ))
    topk = min(TOPK_MAX, T // 4)
    return pl.pallas_call(
        functools.partial(_dsa_kernel, topk=topk),
        out_shape=jax.ShapeDtypeStruct((B * T, GROUP_W), BF16),
        grid=(B, nq),
        in_specs=[pl.BlockSpec((QA_W, TQ), lambda b, i: (0, b * nq + i)),
                  pl.BlockSpec((TQ, LANES), lambda b, i: (b * nq + i, 0)),
                  pl.BlockSpec((T, KVA_W), lambda b, i: (b, 0)),
                  pl.BlockSpec((VT_ROWS, T), lambda b, i: (b, 0)),
                  pl.BlockSpec((GROUP_W, TQ), lambda b, i: (0, 0)),
                  _resident((T, TQ), lambda b, i: (0, 0))],
        out_specs=pl.BlockSpec((TQ, GROUP_W), lambda b, i: (b * nq + i, 0)),
        scratch_shapes=[pltpu.VMEM((T, TQ), F32), pltpu.VMEM((T, TQ), I32), pltpu.VMEM((T, TQ), I16),
                        pltpu.VMEM((T, TQ), I16), pltpu.VMEM((T, TQ), I16)] + _Flash.scratch(N_GH, TQ),
        compiler_params=_cparams("arbitrary", "arbitrary"),
        name="dsa",
    )(qa, wa, kva, vta, g_t, ridx)


def _diff_kernel(q_ref, k_ref, vt_ref, lam_ref, g_ref, o_ref, qm_ref, *flash_refs, lam_init):
    TQ = TQ_DIFF
    q0 = pl.program_id(1) * TQ
    nfull = (q0 + CHUNK) // TK
    qchunk = (q0 + _lane((1, TQ))) // CHUNK
    lv = lam_ref[...]
    lam = (jnp.exp(jnp.sum(lv[0:1] * lv[1:2], axis=1, keepdims=True))
           - jnp.exp(jnp.sum(lv[2:3] * lv[3:4], axis=1, keepdims=True)) + lam_init)
    n_maps = 2 * N_GH
    qt = jnp.transpose(q_ref[...].astype(F32))
    sub = lax.broadcasted_iota(I32, (LANES, 1), 0) // DIFF_DK
    for mi in range(n_maps):
        qg = qt[(mi // 4) * LANES:(mi // 4 + 1) * LANES, :]
        qm_ref[mi] = jnp.where(sub == mi % 4, qg, 0.0).astype(BF16)

    flash = _Flash(n_maps, *flash_refs)
    flash.init()

    def score(kt, masked):
        k0 = pl.multiple_of(kt * TK, TK)
        if masked:
            kchunk = (k0 + lax.broadcasted_iota(I32, (TK, 1), 0)) // CHUNK
            admissible = kchunk <= qchunk
        for mi in range(n_maps):
            g = mi // 4
            s = _dot(k_ref[pl.ds(k0, TK), g * LANES:(g + 1) * LANES], qm_ref[mi])
            flash.scores(mi, jnp.where(admissible, s, NEG_INF) if masked else s)

    def value(kt):
        k0 = pl.multiple_of(kt * TK, TK)
        for mi in range(n_maps):
            h = mi // 2
            flash.values(mi, vt_ref[h * VT_ROWS:(h + 1) * VT_ROWS, pl.ds(k0, TK)])

    flash.pipeline(nfull, nfull, score, value)
    outs = []
    for h in range(N_GH):
        o = flash.result(2 * h) - lam * flash.result(2 * h + 1)
        outs.append(_head_norm_t(o, g_ref[h * HEAD_DIM:(h + 1) * HEAD_DIM, :]) * (1.0 - lam_init))
    o_ref[...] = jnp.transpose(jnp.concatenate(outs, axis=0)).astype(BF16)


def _diff_call(dqkv, vtd, lam_vecs, g_t, lam_init, B, T):
    TQ = TQ_DIFF
    nq = T // TQ
    return pl.pallas_call(
        functools.partial(_diff_kernel, lam_init=lam_init),
        out_shape=jax.ShapeDtypeStruct((B * T, GROUP_W), BF16),
        grid=(B, nq),
        in_specs=[pl.BlockSpec((TQ, 256), lambda b, i: (b * nq + i, 0)),
                  pl.BlockSpec((T, 256), lambda b, i: (b, 1)),
                  pl.BlockSpec((N_GH * VT_ROWS, T), lambda b, i: (b, 0)),
                  pl.BlockSpec((4, LANES), lambda b, i: (0, 0)),
                  pl.BlockSpec((GROUP_W, TQ), lambda b, i: (0, 0))],
        out_specs=pl.BlockSpec((TQ, GROUP_W), lambda b, i: (b * nq + i, 0)),
        scratch_shapes=[pltpu.VMEM((2 * N_GH, LANES, TQ), BF16)] + _Flash.scratch(2 * N_GH, TQ),
        compiler_params=_cparams("arbitrary", "arbitrary"),
        name="diff",
    )(dqkv, dqkv, vtd, lam_vecs, g_t)


def _ret_tables():
    h = np.arange(N_GH, dtype=np.float32)
    log_g = jnp.log(1.0 - 2.0 ** (-5.0 - jnp.asarray(h)))
    t = np.arange(TB)
    same = (t[:, None] // CHUNK) == (t[None, :] // CHUNK)
    earlier = (t[None, :] // CHUNK) < (t[:, None] // CHUNK)
    dist = jnp.asarray(np.where(same, np.abs(t[:, None] - t[None, :]), t[:, None] - t[None, :]), F32)
    dmat = jnp.where(jnp.asarray(same | earlier)[None],
                     jnp.exp(dist[None] * log_g[:, None, None]), 0.0)
    tt = jnp.asarray(t, F32)
    xi = jnp.exp((tt + 1.0)[None, :] * log_g[:, None])
    zeta = jnp.exp((TB - 1.0 - tt)[None, :] * log_g[:, None])
    gblk = jnp.exp(TB * log_g)

    def lanes(tab):
        return jnp.repeat(tab.reshape(2, 2, TB), HEAD_DIM, axis=1).transpose(0, 2, 1)

    r = np.arange(LANES)
    blockdiag = (r[:, None] // HEAD_DIM) == (r[None, :] // HEAD_DIM)
    gb = jnp.where(jnp.asarray(blockdiag)[None],
                   jnp.repeat(gblk.reshape(2, 2), HEAD_DIM, axis=1)[:, :, None], 0.0)
    return dmat, lanes(xi), lanes(zeta), gb


def _ret_kernel(qkv_ref, gate_ref, dmat_ref, xi_ref, zeta_ref, gb_ref, g_ref, o_ref, r_ref):
    @pl.when(pl.program_id(1) == 0)
    def _():
        r_ref[...] = jnp.zeros_like(r_ref)

    lane = _lane((1, LANES))
    low = lane < HEAD_DIM
    r_idx = lax.broadcasted_iota(I32, (LANES, LANES), 0) // HEAD_DIM
    c_idx = lax.broadcasted_iota(I32, (LANES, LANES), 1) // HEAD_DIM
    for g in range(2):
        qg = qkv_ref[:, g * LANES:(g + 1) * LANES]
        kg = qkv_ref[:, 256 + g * LANES:256 + (g + 1) * LANES]
        vg = qkv_ref[:, 512 + g * LANES:512 + (g + 1) * LANES]
        state = r_ref[g]
        inter = _dot(qg, state.astype(BF16)) * xi_ref[g]
        parts = []
        for hh in range(2):
            qm = jnp.where(low, qg, jnp.zeros_like(qg)) if hh == 0 else jnp.where(low, jnp.zeros_like(qg), qg)
            s = _dot_nt(qm, kg) * dmat_ref[2 * g + hh]
            parts.append(_dot(s.astype(BF16), vg))
        o = jnp.where(low, parts[0], parts[1]) + inter
        kz = (kg.astype(F32) * zeta_ref[g]).astype(BF16)
        upd = jnp.where(r_idx == c_idx, _dot_tn(kz, vg), 0.0)
        r_ref[g] = state * gb_ref[g] + upd
        y = _pair_head_norm(o, g_ref[:, g * LANES:(g + 1) * LANES], True)
        o_ref[:, g * LANES:(g + 1) * LANES] = (_silu(gate_ref[:, g * LANES:(g + 1) * LANES]) * y).astype(BF16)


def _ret_specs(nb):
    const3 = lambda b, i: (0, 0, 0)
    in_specs = [pl.BlockSpec((TB, QKV_W), lambda b, i: (b * nb + i, 0)),
                pl.BlockSpec((TB, GROUP_W), lambda b, i: (b * nb + i, 0)),
                pl.BlockSpec((N_GH, TB, TB), const3),
                pl.BlockSpec((2, TB, LANES), const3),
                pl.BlockSpec((2, TB, LANES), const3),
                pl.BlockSpec((2, LANES, LANES), const3),
                pl.BlockSpec((1, GROUP_W), lambda b, i: (0, 0))]
    return in_specs, [pltpu.VMEM((2, LANES, LANES), F32)]


def _gla_kernel(c_ref, gate_ref, wa2_ref, ba_ref, g_ref, o_ref, s_ref, b_scr, q_scr, k_scr, v_scr, acc_scr):
    @pl.when(pl.program_id(1) == 0)
    def _():
        s_ref[...] = jnp.zeros_like(s_ref)

    n_chunks = TB // CHUNK
    hmask = (lax.broadcasted_iota(I32, (LANES, GROUP_W), 0) // GLA_DK
             == lax.broadcasted_iota(I32, (LANES, GROUP_W), 1) // HEAD_DIM)
    hsum = jnp.where(hmask, 1.0, 0.0).astype(BF16)
    r_idx = lax.broadcasted_iota(I32, (TB, TB), 0)
    c_idx = lax.broadcasted_iota(I32, (TB, TB), 1)
    tri = jnp.where((r_idx >= c_idx) & (r_idx // CHUNK == c_idx // CHUNK), 1.0, 0.0).astype(BF16)

    z = _dot_split(c_ref[:, 512:640], wa2_ref[...]) + ba_ref[...]
    log_a = (jnp.minimum(z, 0.0) - jnp.log(1.0 + jnp.exp(-jnp.abs(z)))) * (1.0 / GLA_TAU)
    b = sum(_dot(tri, piece) for piece in _split_bf16(log_a, 3))
    q = c_ref[:, 0:128] * (GLA_DK ** -0.5)
    b_scr[...] = b * LOG2E
    q_scr[...] = q
    k_scr[...] = c_ref[:, 128:256]
    v_scr[...] = c_ref[:, 256:512]

    b_last = [b[(c + 1) * CHUNK - 1:(c + 1) * CHUNK, :] for c in range(n_chunks)]
    b_end = jnp.concatenate([jnp.broadcast_to(bl, (CHUNK, LANES)) for bl in b_last], axis=0)
    kd = (c_ref[:, 128:256] * jnp.exp(b_end - b)).astype(BF16)
    qe = (q * jnp.exp(b)).astype(BF16)
    vb = c_ref[:, 256:512].astype(BF16)
    state = s_ref[...]
    inters = []
    for c in range(n_chunks):
        rows = slice(c * CHUNK, (c + 1) * CHUNK)
        inters.append(_dot(qe[rows], state.astype(BF16)))
        scale = jnp.transpose(jnp.broadcast_to(jnp.exp(b_last[c]), (LANES, LANES)))
        state = jnp.where(hmask, state * jnp.concatenate([scale, scale], axis=1) + _dot_tn(kd[rows], vb[rows]), 0.0)
    s_ref[...] = state

    acc_scr[...] = jnp.concatenate(inters, axis=0)

    keys_per_step = 64

    def key_step(s4, carry):
        for c in range(n_chunks):
            rows = slice(c * CHUNK, (c + 1) * CHUNK)
            part = None
            for u in range(keys_per_step):
                r = c * CHUNK + s4 * keys_per_step + u
                gap = pltpu.bitcast(b_scr[rows, :] - b_scr[pl.ds(r, 1), :], I32)
                decay = jnp.exp2(pltpu.bitcast(gap | jnp.int32(INT_MIN), F32))
                w = (q_scr[rows, :] * decay * k_scr[pl.ds(r, 1), :]).astype(BF16)
                term = _dot(w, hsum) * v_scr[pl.ds(r, 1), :]
                part = term if part is None else part + term
            acc_scr[rows, :] += part
        return carry

    lax.fori_loop(0, CHUNK // keys_per_step, key_step, 0)
    o = acc_scr[...]
    for gi in range(2):
        y = _pair_head_norm(o[:, gi * LANES:(gi + 1) * LANES], g_ref[:, gi * LANES:(gi + 1) * LANES], True)
        o_ref[:, gi * LANES:(gi + 1) * LANES] = (_silu(gate_ref[:, gi * LANES:(gi + 1) * LANES]) * y).astype(BF16)


def _gla_specs(nb):
    in_specs = [pl.BlockSpec((TB, C_W), lambda b, i: (b * nb + i, 0)),
                pl.BlockSpec((TB, GROUP_W), lambda b, i: (b * nb + i, 1)),
                pl.BlockSpec((LANES, LANES), lambda b, i: (0, 0)),
                pl.BlockSpec((1, LANES), lambda b, i: (0, 0)),
                pl.BlockSpec((1, GROUP_W), lambda b, i: (0, 0))]
    scratch = [pltpu.VMEM((LANES, GROUP_W), F32), pltpu.VMEM((TB, LANES), F32),
               pltpu.VMEM((TB, LANES), F32), pltpu.VMEM((TB, LANES), F32),
               pltpu.VMEM((TB, GROUP_W), F32), pltpu.VMEM((TB, GROUP_W), F32)]
    return in_specs, scratch


def _ret_gla_kernel(*refs):
    n_ret, n_gla = 7, 5
    o_b, o_c = refs[n_ret + n_gla:n_ret + n_gla + 2]
    scratch = refs[n_ret + n_gla + 2:]
    _ret_kernel(*refs[:n_ret], o_b, scratch[0])
    _gla_kernel(*refs[n_ret:n_ret + n_gla], o_c, *scratch[1:])


def _ret_gla_call(qkvb, gates, ret_tabs, g_b, cc, wa2_p, ba, g_c, B, T):
    nb = T // TB
    ret_in, ret_scratch = _ret_specs(nb)
    gla_in, gla_scratch = _gla_specs(nb)
    out = jax.ShapeDtypeStruct((B * T, GROUP_W), BF16)
    out_spec = pl.BlockSpec((TB, GROUP_W), lambda b, i: (b * nb + i, 0))
    return pl.pallas_call(
        _ret_gla_kernel,
        out_shape=[out, out],
        grid=(B, nb),
        in_specs=ret_in + gla_in,
        out_specs=[out_spec, out_spec],
        scratch_shapes=ret_scratch + gla_scratch,
        compiler_params=_cparams("arbitrary", "arbitrary"),
        name="ret_gla",
    )(qkvb, gates, *ret_tabs, g_b, cc, gates, wa2_p, ba, g_c)


def _rms(y, g):
    return y * lax.rsqrt(jnp.mean(y * y, axis=-1, keepdims=True) + EPS) * g


def _out_mlp_kernel(x_ref, ya_ref, yb_ref, yc_ref, yd_ref, mod_ref, wo_ref, w1_ref, w2_ref,
                    gpost_ref, gpre_ref, gpost2_ref, o_ref):
    y = _dot(ya_ref[...], wo_ref[0:256, :])
    y = y + _dot(yb_ref[...], wo_ref[256:512, :])
    y = y + _dot(yc_ref[...], wo_ref[512:768, :])
    y = y + _dot(yd_ref[...], wo_ref[768:1024, :])
    x = x_ref[...] + mod_ref[0, 2:3, :] * _rms(y, gpost_ref[...])
    h = _rms(x, gpre_ref[...]) * (1.0 + mod_ref[0, 4:5, :]) + mod_ref[0, 3:4, :]
    hb = h.astype(BF16)
    acc = jnp.zeros(x.shape, F32)
    fc = 1024
    for f in range(D_FF // fc):
        u = jnp.maximum(_dot(hb, w1_ref[:, f * fc:(f + 1) * fc]), 0.0)
        acc = acc + _dot((u * u).astype(BF16), w2_ref[f * fc:(f + 1) * fc, :])
    o_ref[...] = x + mod_ref[0, 5:6, :] * _rms(acc, gpost2_ref[...])


def _out_mlp_call(x2, ys, mod3, w_out, w1, w2, g_post, g_pre2, g_post2, T):
    N, D = x2.shape
    nt = T // TM
    row = lambda i: (i, 0)
    const = lambda i: (0, 0)
    return pl.pallas_call(
        _out_mlp_kernel,
        out_shape=jax.ShapeDtypeStruct((N, D), F32),
        grid=(N // TM,),
        in_specs=[pl.BlockSpec((TM, D), row)] + [pl.BlockSpec((TM, GROUP_W), row)] * 4
                 + [pl.BlockSpec((1, 6, D), lambda i: (i // nt, 0, 0)),
                    _resident((D, D), const), _resident((D, D_FF), const), _resident((D_FF, D), const),
                    _resident((1, D), const), _resident((1, D), const), _resident((1, D), const)],
        out_specs=pl.BlockSpec((TM, D), row),
        compiler_params=_cparams("arbitrary"),
        name="out_mlp",
    )(x2, *ys, mod3, w_out, w1, w2, g_post, g_pre2, g_post2)


def _relayout_w_in(w_in):
    col = lambda k: w_in[:, _IN_OFF[k]:_IN_OFF[k + 1]]
    zeros = lambda n: jnp.zeros((w_in.shape[0], n), w_in.dtype)
    parts = [col(_A_Q), col(_A_QI), col(_A_K), col(_A_KI), col(_A_V), zeros(64),
             col(_A_WI), zeros(120),
             col(_B_Q), col(_B_K), col(_B_V),
             col(_B_G), col(_C_G),
             col(_C_Q), col(_C_K), col(_C_V), col(_C_A), zeros(112),
             col(_D_Q), col(_D_K), col(_D_V)]
    return jnp.concatenate(parts, axis=1).astype(BF16)


def _rope_tables(pos, rot, period, theta):
    half = rot // 2
    inv = theta ** (-jnp.arange(half, dtype=F32) / half)
    ang = pos.astype(F32)[..., None] * inv
    cos, sin = jnp.cos(ang), jnp.sin(ang)
    pad = ang.shape[:-1] + (period - rot,)
    ct = jnp.concatenate([cos, cos, jnp.ones(pad, F32)], axis=-1)
    st = jnp.concatenate([-sin, sin, jnp.zeros(pad, F32)], axis=-1)
    return ct.reshape(-1, period), st.reshape(-1, period)


def _layer(x2, cond, tabs, ret_tabs, layer_idx, B, T, mod_w, mod_b, attn_pre_g, attn_post_g,
           mlp_pre_g, mlp_post_g, w_in, gla_wa2, gla_ba, lam_q1, lam_k1, lam_q2, lam_k2,
           head_norm_g, w_out, mlp_w1, mlp_w2):
    D = D_MODEL
    mod3 = _mod_call(cond, mod_w, mod_b).reshape(B, 6, D)
    qa, kva, wa, qkvb, gates, cc, dqkv, vta, vtd = _in_call(
        x2, mod3, attn_pre_g.reshape(1, D), _relayout_w_in(w_in), tabs, T)

    g_a, g_b, g_c, g_d = jnp.split(head_norm_g, 4)
    gain_t = lambda g, tq: jnp.broadcast_to(g[:, None], (GROUP_W, tq))
    y_a = _dsa_call(qa, wa, kva, vta, gain_t(g_a, TQ_DSA), B, T)
    wa2_p = jnp.zeros((LANES, LANES), F32).at[:GLA_RANK].set(gla_wa2)
    y_b, y_c = _ret_gla_call(qkvb, gates, ret_tabs, g_b.reshape(1, GROUP_W), cc, wa2_p,
                             gla_ba.reshape(1, LANES), g_c.reshape(1, GROUP_W), B, T)
    lam_init = 0.8 - 0.6 * math.exp(-0.3 * layer_idx)
    lam_vecs = jnp.zeros((4, LANES), F32).at[:, :DIFF_DK].set(jnp.stack([lam_q1, lam_k1, lam_q2, lam_k2]))
    y_d = _diff_call(dqkv, vtd, lam_vecs, gain_t(g_d, TQ_DIFF), lam_init, B, T)

    return _out_mlp_call(x2, (y_a, y_b, y_c, y_d), mod3, w_out.astype(BF16), mlp_w1.astype(BF16),
                         mlp_w2.astype(BF16), attn_post_g.reshape(1, D), mlp_pre_g.reshape(1, D),
                         mlp_post_g.reshape(1, D), T)


def kernel(x, c, positions, mod_w, mod_b, attn_pre_g, attn_post_g, mlp_pre_g, mlp_post_g, w_in,
           gla_wa2, gla_ba, lam_q1, lam_k1, lam_q2, lam_k2, head_norm_g, w_out, mlp_w1, mlp_w2):
    B, T, D = x.shape
    assert D == D_MODEL and T % TM == 0 and T % TB == 0
    tabs = (_rope_tables(positions, HEAD_DIM // ROPE_FRAC, HEAD_DIM, ROPE_THETA)
            + _rope_tables(positions, HEAD_DIM, HEAD_DIM, RET_THETA)
            + _rope_tables(positions, DIFF_DK // ROPE_FRAC, DIFF_DK, ROPE_THETA))
    ret_tabs = _ret_tables()
    x2 = x.reshape(B * T, D)
    for l in range(mod_w.shape[0]):
        x2 = _layer(x2, c, tabs, ret_tabs, l, B, T, mod_w[l], mod_b[l], attn_pre_g[l], attn_post_g[l],
                    mlp_pre_g[l], mlp_post_g[l], w_in[l], gla_wa2[l], gla_ba[l], lam_q1[l], lam_k1[l],
                    lam_q2[l], lam_k2[l], head_norm_g[l], w_out[l], mlp_w1[l], mlp_w2[l])
    return x2.reshape(B, T, D)
```

```python
import functools
import math

import numpy as np
import jax
import jax.numpy as jnp
from jax import lax
from jax.experimental import pallas as pl
from jax.experimental.pallas import tpu as pltpu

F32 = jnp.float32
BF16 = jnp.bfloat16
I32 = jnp.int32
I16 = jnp.int16

D_MODEL = 1024
CHUNK = 64
HEAD_DIM = 64
N_GH = 4
GROUP_W = N_GH * HEAD_DIM
IDX_HEADS = 8
IDX_DIM = 64
TOPK_MAX = 256
RET_THETA = 10000.0
GLA_DK = HEAD_DIM // 2
GLA_RANK = 16
GLA_TAU = 16.0
DIFF_DK = HEAD_DIM // 2
ROPE_THETA = 500000.0
ROPE_FRAC = 4
D_FF = 4 * D_MODEL
EPS = 1e-6
NEG_INF = -1e30

LANES = 128
VMEM_LIMIT = 56 * 1024 * 1024

_IN_SPLITS = (
    GROUP_W, HEAD_DIM, HEAD_DIM, IDX_HEADS * IDX_DIM, IDX_DIM, IDX_HEADS,
    GROUP_W, GROUP_W, GROUP_W, GROUP_W,
    N_GH * GLA_DK, N_GH * GLA_DK, GROUP_W, GLA_RANK, GROUP_W,
    2 * N_GH * DIFF_DK, 2 * N_GH * DIFF_DK, GROUP_W,
)
_IN_OFF = np.concatenate([[0], np.cumsum(_IN_SPLITS)]).tolist()
(_A_Q, _A_K, _A_V, _A_QI, _A_KI, _A_WI, _B_Q, _B_K, _B_V, _B_G,
 _C_Q, _C_K, _C_V, _C_A, _C_G, _D_Q, _D_K, _D_V) = range(18)

SEG_A = 0
SEG_W = 1024
SEG_B = 1152
SEG_G = 1920
SEG_C = 2432
SEG_D = 3072
W_COLS = 3840
QA_W = 12 * LANES
KVA_W = 2 * LANES
QKV_W = 3 * GROUP_W
GATES_W = 2 * GROUP_W
C_W = 5 * LANES
I16_MIN, I16_MAX = -32768, 32767

NT_DIMS = (((1,), (1,)), ((), ()))
TN_DIMS = (((0,), (0,)), ((), ()))

TM = 512
TQ_DSA = 256
TQ_DIFF = 512
TK = 512
COUNT_ROWS = 512
FLASH_SLAB = 32
VT_ROWS = HEAD_DIM + 16
LOG2E = math.log2(math.e)
TB = 256


def _float_key(v):
    bits = int(np.array(v, np.float32).view(np.int32))
    return -(bits & 0x7FFFFFFF) if bits < 0 else bits


NEG_KEY = _float_key(NEG_INF)
INT_MIN = -(2 ** 31)


def _cparams(*sem):
    return pltpu.CompilerParams(dimension_semantics=sem, vmem_limit_bytes=VMEM_LIMIT)


def _resident(shape, index_map):
    return pl.BlockSpec(shape, index_map, pipeline_mode=pl.Buffered(1))


def _dot(a, b):
    return jnp.dot(a, b, preferred_element_type=F32)


def _dot_nt(a, b):
    return lax.dot_general(a, b, NT_DIMS, preferred_element_type=F32)


def _dot_tn(a, b):
    return lax.dot_general(a, b, TN_DIMS, preferred_element_type=F32)


def _dot_f32(a, b):
    return jnp.dot(a, b, preferred_element_type=F32, precision=lax.Precision.HIGHEST)


def _split_bf16(x, n):
    parts = []
    for _ in range(n):
        p = x.astype(BF16)
        parts.append(p)
        x = x - p.astype(F32)
    return parts


def _dot_split(a, b):
    (a0, a1), (b0, b1) = _split_bf16(a, 2), _split_bf16(b, 2)
    return _dot(a0, b0) + (_dot(a0, b1) + _dot(a1, b0))


def _lane(shape):
    return lax.broadcasted_iota(I32, shape, len(shape) - 1)


def _silu(x):
    return x / (1.0 + jnp.exp(-x))


def _mod_kernel(c_ref, w_ref, b_ref, o_ref):
    cond = _silu(c_ref[...])
    o_ref[...] = _dot_f32(cond, w_ref[...]) + b_ref[...]


def _mod_call(c, mod_w, mod_b):
    B, D = c.shape
    n = mod_w.shape[1] // D
    return pl.pallas_call(
        _mod_kernel,
        out_shape=jax.ShapeDtypeStruct((B, n * D), F32),
        grid=(n,),
        in_specs=[pl.BlockSpec((B, D), lambda j: (0, 0)),
                  pl.BlockSpec((D, D), lambda j: (0, j)),
                  pl.BlockSpec((1, D), lambda j: (0, j))],
        out_specs=pl.BlockSpec((B, D), lambda j: (0, j)),
        compiler_params=_cparams("arbitrary"),
        name="mod",
    )(c, mod_w, mod_b.reshape(1, -1))


def _rope_group(xg, cos, sin, half, period):
    lo = (_lane((1, LANES)) % period) < half
    swapped = jnp.where(lo, pltpu.roll(xg, LANES - half, 1), pltpu.roll(xg, half, 1))
    return xg * cos + swapped * sin


def _in_kernel(x_ref, mod_ref, g_ref, w_ref, ca_ref, sa_ref, cb_ref, sb_ref, cd_ref, sd_ref,
               qa_ref, kva_ref, wa_ref, qkvb_ref, gate_ref, c_ref, d_ref, vta_ref, vtd_ref):
    x = x_ref[...]
    h = x * lax.rsqrt(jnp.mean(x * x, axis=-1, keepdims=True) + EPS) * g_ref[...]
    h = h * (1.0 + mod_ref[0, 1:2, :]) + mod_ref[0, 0:1, :]
    hb = h.astype(BF16)

    def seg(off, width):
        return _dot(hb, w_ref[:, off:off + width])

    lane = _lane((1, LANES))
    low = lane < HEAD_DIM

    def lanes(t_ref):
        t = t_ref[...]
        return jnp.concatenate([t] * (LANES // t.shape[1]), axis=1)

    def put_q(slot, variant):
        qa_ref[slot * LANES:(slot + 1) * LANES, :] = jnp.transpose(variant).astype(BF16)

    ca, sa = lanes(ca_ref), lanes(sa_ref)
    a = seg(SEG_A, 1024)
    half_a = HEAD_DIM // ROPE_FRAC // 2
    for gi in range(2):
        r = _rope_group(a[:, gi * LANES:(gi + 1) * LANES], ca, sa, half_a, HEAD_DIM) * (HEAD_DIM ** -0.5 * LOG2E)
        rs = pltpu.roll(r, HEAD_DIM, 1)
        put_q(2 * gi, jnp.where(low, r, 0.0))
        put_q(2 * gi + 1, jnp.where(low, rs, 0.0))
    for gi in range(4):
        r = _rope_group(a[:, (2 + gi) * LANES:(3 + gi) * LANES], ca, sa, half_a, HEAD_DIM)
        rs = pltpu.roll(r, HEAD_DIM, 1)
        put_q(4 + 2 * gi, jnp.where(low, 0.0, rs))
        put_q(5 + 2 * gi, jnp.where(low, 0.0, r))
    kva_ref[:, 0:LANES] = _rope_group(a[:, 6 * LANES:7 * LANES], ca, sa, half_a, HEAD_DIM).astype(BF16)
    kva_ref[:, LANES:2 * LANES] = a[:, 7 * LANES:8 * LANES].astype(BF16)
    ones_rows = jnp.ones((VT_ROWS - HEAD_DIM, TM), BF16)
    vta_ref[0:HEAD_DIM, :] = jnp.transpose(a[:, 7 * LANES:8 * LANES])[0:HEAD_DIM, :].astype(BF16)
    vta_ref[HEAD_DIM:VT_ROWS, :] = ones_rows

    wa_ref[...] = seg(SEG_W, LANES)

    cb, sb = lanes(cb_ref), lanes(sb_ref)
    b = seg(SEG_B, QKV_W)
    for gi in range(4):
        r = _rope_group(b[:, gi * LANES:(gi + 1) * LANES], cb, sb, HEAD_DIM // 2, HEAD_DIM)
        if gi < 2:
            r = r * (HEAD_DIM ** -0.5)
        qkvb_ref[:, gi * LANES:(gi + 1) * LANES] = r.astype(BF16)
    qkvb_ref[:, 512:768] = b[:, 512:768].astype(BF16)

    gate_ref[...] = seg(SEG_G, GATES_W)
    c_ref[...] = seg(SEG_C, C_W)

    cd, sd = lanes(cd_ref), lanes(sd_ref)
    d = seg(SEG_D, QKV_W)
    half_d = DIFF_DK // ROPE_FRAC // 2
    for gi in range(4):
        r = _rope_group(d[:, gi * LANES:(gi + 1) * LANES], cd, sd, half_d, DIFF_DK)
        if gi < 2:
            r = r * (DIFF_DK ** -0.5 * LOG2E)
        d_ref[:, gi * LANES:(gi + 1) * LANES] = r.astype(BF16)
    d_ref[:, 512:768] = d[:, 512:768].astype(BF16)
    vt = jnp.transpose(d[:, 512:768])
    for h in range(N_GH):
        vtd_ref[h * VT_ROWS:h * VT_ROWS + HEAD_DIM, :] = vt[h * HEAD_DIM:(h + 1) * HEAD_DIM, :].astype(BF16)
        vtd_ref[h * VT_ROWS + HEAD_DIM:(h + 1) * VT_ROWS, :] = ones_rows


def _in_call(x2, mod3, g, w_p, tabs, T):
    N, D = x2.shape
    nt = T // TM
    row = lambda i: (i, 0)
    tab_specs = [pl.BlockSpec((TM, t.shape[1]), row) for t in tabs]
    outs = [(KVA_W, BF16), (LANES, F32), (QKV_W, BF16), (GATES_W, F32), (C_W, F32), (QKV_W, BF16)]
    qa_shape = jax.ShapeDtypeStruct((QA_W, N), BF16)
    qa_spec = pl.BlockSpec((QA_W, TM), lambda i: (0, i))
    vt_rows = (VT_ROWS, N_GH * VT_ROWS)
    vt_shapes = [jax.ShapeDtypeStruct((N // T * r, T), BF16) for r in vt_rows]
    vt_specs = [pl.BlockSpec((r, TM), lambda i: (i // nt, i % nt)) for r in vt_rows]
    return pl.pallas_call(
        _in_kernel,
        out_shape=[qa_shape] + [jax.ShapeDtypeStruct((N, w), dt) for w, dt in outs] + vt_shapes,
        grid=(N // TM,),
        in_specs=[pl.BlockSpec((TM, D), row),
                  pl.BlockSpec((1, 6, D), lambda i: (i // nt, 0, 0)),
                  _resident((1, D), lambda i: (0, 0)),
                  _resident((D, W_COLS), lambda i: (0, 0))] + tab_specs,
        out_specs=[qa_spec] + [pl.BlockSpec((TM, w), row) for w, _ in outs] + vt_specs,
        compiler_params=_cparams("arbitrary"),
        name="in_proj",
    )(x2, mod3, g, w_p, *tabs)


def _pair_head_norm(o, gain, center):
    low = _lane((1, LANES)) < HEAD_DIM
    inv = 1.0 / HEAD_DIM

    def seg_mean(v):
        m0 = jnp.sum(jnp.where(low, v, 0.0), axis=1, keepdims=True)
        m1 = jnp.sum(jnp.where(low, 0.0, v), axis=1, keepdims=True)
        return jnp.where(low, m0, m1) * inv

    if center:
        o = o - seg_mean(o)
    return o * lax.rsqrt(seg_mean(o * o) + EPS) * gain


class _Flash:
    def __init__(self, n, s_ref, p_ref, acc_ref, st_ref):
        self.n, self.s, self.p, self.acc, self.st = n, s_ref, p_ref, acc_ref, st_ref
        self.tq = s_ref.shape[-1]

    @staticmethod
    def scratch(n, tq):
        return [pltpu.VMEM((n, TK, tq), F32), pltpu.VMEM((n, TK, tq), BF16),
                pltpu.VMEM((n, VT_ROWS, tq), F32), pltpu.VMEM((3, n, tq), F32)]

    def _row(self, k, i):
        return self.st.at[k, i:i + 1, :]

    def init(self):
        self.st[0] = jnp.full((self.n, self.tq), NEG_INF, F32)
        self.acc[...] = jnp.zeros_like(self.acc)

    def scores(self, i, s):
        self.s[i] = s
        self._row(1, i)[...] = jnp.max(_fold8(s, jnp.maximum), axis=0, keepdims=True)

    def probs(self, i):
        m_old = self._row(0, i)[...]
        m_new = jnp.maximum(m_old, self._row(1, i)[...])
        for r in range(0, TK, FLASH_SLAB):
            self.p[i, r:r + FLASH_SLAB, :] = jnp.exp2(self.s[i, r:r + FLASH_SLAB, :] - m_new).astype(BF16)
        self._row(0, i)[...] = m_new
        self._row(2, i)[...] = jnp.exp2(m_old - m_new)

    def values(self, i, vt):
        self.acc[i] = self._row(2, i)[...] * self.acc[i] + _dot(vt, self.p[i])

    def result(self, i):
        return self.acc[i, 0:HEAD_DIM, :] / self.acc[i, HEAD_DIM:HEAD_DIM + 1, :]

    def pipeline(self, first, n_rest, score, value):
        score(first, True)

        def body(t, carry):
            for i in range(self.n):
                self.probs(i)
            score(t, False)
            value(jnp.where(t == 0, first, t - 1))
            return carry

        lax.fori_loop(0, n_rest, body, 0)
        for i in range(self.n):
            self.probs(i)
        value(jnp.where(n_rest == 0, first, n_rest - 1))


def _fold8(x, op=jnp.add, rows=8):
    parts = [x[r:r + rows] for r in range(0, x.shape[0], rows)]
    while len(parts) > 1:
        parts = [op(a, b) for a, b in zip(parts[0::2], parts[1::2])]
    return parts[0]


def _head_norm_t(o, gain):
    ms = jnp.mean(o * o, axis=0, keepdims=True)
    return o * lax.rsqrt(ms + EPS) * gain


def _dsa_kernel(q_ref, w_ref, kv_ref, vt_ref, g_ref, ridx_ref, o_ref, bias_ref, key_ref, hi_ref, lo_ref, u_ref,
                *flash_refs, topk):
    TQ = TQ_DSA
    q0 = pl.program_id(1) * TQ
    nkt = (q0 + TQ + TK - 1) // TK
    qchunk = (q0 + _lane((1, TQ))) // CHUNK
    wts = jnp.transpose(w_ref[...]) * (IDX_HEADS ** -0.5 * IDX_DIM ** -0.5)

    def score_tile(kt, carry, masked=False):
        k0 = pl.multiple_of(kt * TK, TK)
        kk = kv_ref[pl.ds(k0, TK), 0:LANES]
        tile = bias_ref.at[pl.ds(k0, TK), :]

        def head_term(h):
            return jnp.maximum(_dot(kk, q_ref[(4 + h) * LANES:(5 + h) * LANES, :]), 0.0) * wts[h:h + 1, :]

        for h in range(0, IDX_HEADS, 2):
            term = head_term(h) + head_term(h + 1)
            if h == 0:
                tile[...] = term
            elif h < IDX_HEADS - 2:
                tile[...] += term
        acc = tile[...] + term
        if masked:
            kchunk = (k0 + lax.broadcasted_iota(I32, (TK, 1), 0)) // CHUNK
            acc = jnp.where(kchunk <= qchunk, acc, NEG_INF)
        bits = pltpu.bitcast(acc, I32)
        sign = bits >> 31
        key = ((bits & 0x7FFFFFFF) ^ sign) - sign
        key_ref[pl.ds(k0, TK), :] = key
        hi_ref[pl.ds(k0, TK), :] = (key >> 16).astype(I16)
        lo_ref[pl.ds(k0, TK), :] = (key ^ 0x8000).astype(I16)
        return carry

    lax.fori_loop(0, nkt - 1, score_tile, 0)
    score_tile(nkt - 1, 0, masked=True)

    cr = min(COUNT_ROWS, key_ref.shape[0])
    n_count = (nkt * TK + cr - 1) // cr

    def pad_tile(kt, carry):
        k0 = pl.multiple_of(kt * TK, TK)
        hi_ref[pl.ds(k0, TK), :] = jnp.full((TK, TQ), I16_MIN, I16)
        lo_ref[pl.ds(k0, TK), :] = jnp.full((TK, TQ), I16_MIN, I16)
        return carry

    lax.fori_loop(nkt, n_count * (cr // TK), pad_tile, 0)
    kf = float(topk)

    def count16(ref, cand):
        def body(kc, acc):
            k0 = pl.multiple_of(kc * cr, cr)
            hit = jnp.where(ref[pl.ds(k0, cr), :] >= cand, jnp.ones((cr, TQ), I16), jnp.zeros((cr, TQ), I16))
            return acc + _fold8(hit, rows=16)
        acc = lax.fori_loop(0, n_count, body, jnp.zeros((16, TQ), I16))
        return jnp.sum(acc.astype(F32), axis=0, keepdims=True)

    def search16(ref, cnt_all, signed=True, bits=15):
        if signed:
            c0 = count16(ref, jnp.zeros((1, TQ), I16))
            t = jnp.where(c0 >= kf, 0, I16_MIN).astype(I32)
            cnt = jnp.where(c0 >= kf, c0, cnt_all)
        else:
            t, cnt = jnp.zeros((1, TQ), I32), cnt_all

        def step(it, carry):
            t, cnt = carry
            cand = t + (jnp.int32(1) << (bits - 1 - it))
            c = count16(ref, cand.astype(I16))
            ok = c >= kf
            return jnp.where(ok, cand, t), jnp.where(ok, c, cnt)

        return lax.fori_loop(0, bits, step, (t, cnt))

    t_hi, cnt_hi = search16(hi_ref, (nkt * TK).astype(F32))
    t_hi16 = t_hi.astype(I16)

    top = jnp.full((cr, TQ), I16_MAX, I16)
    bottom = jnp.full((cr, TQ), I16_MIN, I16)

    def low_digit_tile(kc, carry):
        k0 = pl.multiple_of(kc * cr, cr)
        hi = hi_ref[pl.ds(k0, cr), :]
        in_bucket = jnp.where(hi == t_hi16, lo_ref[pl.ds(k0, cr), :], bottom)
        u_ref[pl.ds(k0, cr), :] = jnp.where(hi > t_hi16, top, in_bucket)
        return carry

    lax.fori_loop(0, n_count, low_digit_tile, 0)
    t_lo, cnt_lo = search16(u_ref, cnt_hi)
    t_lo16 = t_lo.astype(I16)
    thr = t_hi * 65536 + (t_lo + 32768)

    def index_digit_tile(kc, carry):
        k0 = pl.multiple_of(kc * cr, cr)
        hi, lo = hi_ref[pl.ds(k0, cr), :], lo_ref[pl.ds(k0, cr), :]
        same_hi = jnp.where(lo == t_lo16, ridx_ref[pl.ds(k0, cr), :], jnp.where(lo > t_lo16, top, bottom))
        u_ref[pl.ds(k0, cr), :] = jnp.where(hi == t_hi16, same_hi, jnp.where(hi > t_hi16, top, bottom))
        return carry

    n_keys = key_ref.shape[0]
    floor_key = jnp.int32(NEG_KEY + 1)
    short = thr < floor_key

    def index_search():
        lax.fori_loop(0, n_count, index_digit_tile, 0)
        return search16(u_ref, cnt_lo, signed=False, bits=(n_keys - 1).bit_length())[0]

    any_tie = jnp.max(jnp.where(short, 0.0, cnt_lo - kf)) > 0.0
    t_idx = lax.cond(any_tie, index_search, lambda: jnp.zeros((1, TQ), I32))
    lim_tie = jnp.where(short, floor_key, thr)
    lim_late = jnp.where(short, floor_key, thr + 1)
    last_tie = jnp.where(short, -1, (n_keys - 1) - t_idx)

    def select_tile(kt, carry):
        k0 = pl.multiple_of(kt * TK, TK)
        idx = k0 + lax.broadcasted_iota(I32, (TK, 1), 0)
        lim = jnp.where(idx <= last_tie, lim_tie, lim_late)
        bias_ref[pl.ds(k0, TK), :] = jnp.where(key_ref[pl.ds(k0, TK), :] >= lim, 0.0, NEG_INF)
        return carry

    lax.fori_loop(0, nkt, select_tile, 0)

    flash = _Flash(N_GH, *flash_refs)
    flash.init()

    def score(kt, is_first):
        k0 = pl.multiple_of(kt * TK, TK)
        kk = kv_ref[pl.ds(k0, TK), 0:LANES]
        bias = bias_ref[pl.ds(k0, TK), :]
        for h in range(N_GH):
            flash.scores(h, _dot(kk, q_ref[h * LANES:(h + 1) * LANES, :]) + bias)

    def value(kt):
        k0 = pl.multiple_of(kt * TK, TK)
        for h in range(N_GH):
            flash.values(h, vt_ref[:, pl.ds(k0, TK)])

    flash.pipeline(nkt - 1, nkt - 1, score, value)
    y = jnp.concatenate([_head_norm_t(flash.result(h), g_ref[h * HEAD_DIM:(h + 1) * HEAD_DIM, :])
                         for h in range(N_GH)], axis=0)
    o_ref[...] = jnp.transpose(y).astype(BF16)


def _dsa_call(qa, wa, kva, vta, g_t, B, T):
    TQ = TQ_DSA
    assert TK % TQ == 0
    nq = T // TQ
    ridx = jnp.broadcast_to((T - 1 - jnp.arange(T, dtype=I32)).astype(I16)[:, None], (T, TQ))
    topk = min(TOPK_MAX, T // 4)
    return pl.pallas_call(
        functools.partial(_dsa_kernel, topk=topk),
        out_shape=jax.ShapeDtypeStruct((B * T, GROUP_W), BF16),
        grid=(B, nq),
        in_specs=[pl.BlockSpec((QA_W, TQ), lambda b, i: (0, b * nq + i)),
                  pl.BlockSpec((TQ, LANES), lambda b, i: (b * nq + i, 0)),
                  pl.BlockSpec((T, KVA_W), lambda b, i: (b, 0)),
                  pl.BlockSpec((VT_ROWS, T), lambda b, i: (b, 0)),
                  pl.BlockSpec((GROUP_W, TQ), lambda b, i: (0, 0)),
                  _resident((T, TQ), lambda b, i: (0, 0))],
        out_specs=pl.BlockSpec((TQ, GROUP_W), lambda b, i: (b * nq + i, 0)),
        scratch_shapes=[pltpu.VMEM((T, TQ), F32), pltpu.VMEM((T, TQ), I32), pltpu.VMEM((T, TQ), I16),
                        pltpu.VMEM((T, TQ), I16), pltpu.VMEM((T, TQ), I16)] + _Flash.scratch(N_GH, TQ),
        compiler_params=_cparams("arbitrary", "arbitrary"),
        name="dsa",
    )(qa, wa, kva, vta, g_t, ridx)


def _diff_kernel(q_ref, k_ref, vt_ref, lam_ref, g_ref, o_ref, qm_ref, *flash_refs, lam_init):
    TQ = TQ_DIFF
    q0 = pl.program_id(1) * TQ
    nfull = (q0 + CHUNK) // TK
    qchunk = (q0 + _lane((1, TQ))) // CHUNK
    lv = lam_ref[...]
    lam = (jnp.exp(jnp.sum(lv[0:1] * lv[1:2], axis=1, keepdims=True))
           - jnp.exp(jnp.sum(lv[2:3] * lv[3:4], axis=1, keepdims=True)) + lam_init)
    n_maps = 2 * N_GH
    qt = jnp.transpose(q_ref[...].astype(F32))
    sub = lax.broadcasted_iota(I32, (LANES, 1), 0) // DIFF_DK
    for mi in range(n_maps):
        qg = qt[(mi // 4) * LANES:(mi // 4 + 1) * LANES, :]
        qm_ref[mi] = jnp.where(sub == mi % 4, qg, 0.0).astype(BF16)

    flash = _Flash(n_maps, *flash_refs)
    flash.init()

    def score(kt, masked):
        k0 = pl.multiple_of(kt * TK, TK)
        if masked:
            kchunk = (k0 + lax.broadcasted_iota(I32, (TK, 1), 0)) // CHUNK
            admissible = kchunk <= qchunk
        for mi in range(n_maps):
            g = mi // 4
            s = _dot(k_ref[pl.ds(k0, TK), g * LANES:(g + 1) * LANES], qm_ref[mi])
            flash.scores(mi, jnp.where(admissible, s, NEG_INF) if masked else s)

    def value(kt):
        k0 = pl.multiple_of(kt * TK, TK)
        for mi in range(n_maps):
            h = mi // 2
            flash.values(mi, vt_ref[h * VT_ROWS:(h + 1) * VT_ROWS, pl.ds(k0, TK)])

    flash.pipeline(nfull, nfull, score, value)
    outs = []
    for h in range(N_GH):
        o = flash.result(2 * h) - lam * flash.result(2 * h + 1)
        outs.append(_head_norm_t(o, g_ref[h * HEAD_DIM:(h + 1) * HEAD_DIM, :]) * (1.0 - lam_init))
    o_ref[...] = jnp.transpose(jnp.concatenate(outs, axis=0)).astype(BF16)


def _diff_call(dqkv, vtd, lam_vecs, g_t, lam_init, B, T):
    TQ = TQ_DIFF
    nq = T // TQ
    return pl.pallas_call(
        functools.partial(_diff_kernel, lam_init=lam_init),
        out_shape=jax.ShapeDtypeStruct((B * T, GROUP_W), BF16),
        grid=(B, nq),
        in_specs=[pl.BlockSpec((TQ, 256), lambda b, i: (b * nq + i, 0)),
                  pl.BlockSpec((T, 256), lambda b, i: (b, 1)),
                  pl.BlockSpec((N_GH * VT_ROWS, T), lambda b, i: (b, 0)),
                  pl.BlockSpec((4, LANES), lambda b, i: (0, 0)),
                  pl.BlockSpec((GROUP_W, TQ), lambda b, i: (0, 0))],
        out_specs=pl.BlockSpec((TQ, GROUP_W), lambda b, i: (b * nq + i, 0)),
        scratch_shapes=[pltpu.VMEM((2 * N_GH, LANES, TQ), BF16)] + _Flash.scratch(2 * N_GH, TQ),
        compiler_params=_cparams("arbitrary", "arbitrary"),
        name="diff",
    )(dqkv, dqkv, vtd, lam_vecs, g_t)


def _ret_tables():
    h = np.arange(N_GH, dtype=np.float32)
    log_g = jnp.log(1.0 - 2.0 ** (-5.0 - jnp.asarray(h)))
    t = np.arange(TB)
    same = (t[:, None] // CHUNK) == (t[None, :] // CHUNK)
    earlier = (t[None, :] // CHUNK) < (t[:, None] // CHUNK)
    dist = jnp.asarray(np.where(same, np.abs(t[:, None] - t[None, :]), t[:, None] - t[None, :]), F32)
    dmat = jnp.where(jnp.asarray(same | earlier)[None],
                     jnp.exp(dist[None] * log_g[:, None, None]), 0.0)
    tt = jnp.asarray(t, F32)
    xi = jnp.exp((tt + 1.0)[None, :] * log_g[:, None])
    zeta = jnp.exp((TB - 1.0 - tt)[None, :] * log_g[:, None])
    gblk = jnp.exp(TB * log_g)

    def lanes(tab):
        return jnp.repeat(tab.reshape(2, 2, TB), HEAD_DIM, axis=1).transpose(0, 2, 1)

    r = np.arange(LANES)
    blockdiag = (r[:, None] // HEAD_DIM) == (r[None, :] // HEAD_DIM)
    gb = jnp.where(jnp.asarray(blockdiag)[None],
                   jnp.repeat(gblk.reshape(2, 2), HEAD_DIM, axis=1)[:, :, None], 0.0)
    return dmat, lanes(xi), lanes(zeta), gb


def _ret_kernel(qkv_ref, gate_ref, dmat_ref, xi_ref, zeta_ref, gb_ref, g_ref, o_ref, r_ref):
    @pl.when(pl.program_id(1) == 0)
    def _():
        r_ref[...] = jnp.zeros_like(r_ref)

    lane = _lane((1, LANES))
    low = lane < HEAD_DIM
    r_idx = lax.broadcasted_iota(I32, (LANES, LANES), 0) // HEAD_DIM
    c_idx = lax.broadcasted_iota(I32, (LANES, LANES), 1) // HEAD_DIM
    for g in range(2):
        qg = qkv_ref[:, g * LANES:(g + 1) * LANES]
        kg = qkv_ref[:, 256 + g * LANES:256 + (g + 1) * LANES]
        vg = qkv_ref[:, 512 + g * LANES:512 + (g + 1) * LANES]
        state = r_ref[g]
        inter = _dot(qg, state.astype(BF16)) * xi_ref[g]
        parts = []
        for hh in range(2):
            qm = jnp.where(low, qg, jnp.zeros_like(qg)) if hh == 0 else jnp.where(low, jnp.zeros_like(qg), qg)
            s = _dot_nt(qm, kg) * dmat_ref[2 * g + hh]
            parts.append(_dot(s.astype(BF16), vg))
        o = jnp.where(low, parts[0], parts[1]) + inter
        kz = (kg.astype(F32) * zeta_ref[g]).astype(BF16)
        upd = jnp.where(r_idx == c_idx, _dot_tn(kz, vg), 0.0)
        r_ref[g] = state * gb_ref[g] + upd
        y = _pair_head_norm(o, g_ref[:, g * LANES:(g + 1) * LANES], True)
        o_ref[:, g * LANES:(g + 1) * LANES] = (_silu(gate_ref[:, g * LANES:(g + 1) * LANES]) * y).astype(BF16)


def _ret_specs(nb):
    const3 = lambda b, i: (0, 0, 0)
    in_specs = [pl.BlockSpec((TB, QKV_W), lambda b, i: (b * nb + i, 0)),
                pl.BlockSpec((TB, GROUP_W), lambda b, i: (b * nb + i, 0)),
                pl.BlockSpec((N_GH, TB, TB), const3),
                pl.BlockSpec((2, TB, LANES), const3),
                pl.BlockSpec((2, TB, LANES), const3),
                pl.BlockSpec((2, LANES, LANES), const3),
                pl.BlockSpec((1, GROUP_W), lambda b, i: (0, 0))]
    return in_specs, [pltpu.VMEM((2, LANES, LANES), F32)]


def _gla_kernel(c_ref, gate_ref, wa2_ref, ba_ref, g_ref, o_ref, s_ref, b_scr, q_scr, k_scr, v_scr, acc_scr):
    @pl.when(pl.program_id(1) == 0)
    def _():
        s_ref[...] = jnp.zeros_like(s_ref)

    n_chunks = TB // CHUNK
    hmask = (lax.broadcasted_iota(I32, (LANES, GROUP_W), 0) // GLA_DK
             == lax.broadcasted_iota(I32, (LANES, GROUP_W), 1) // HEAD_DIM)
    hsum = jnp.where(hmask, 1.0, 0.0).astype(BF16)
    r_idx = lax.broadcasted_iota(I32, (TB, TB), 0)
    c_idx = lax.broadcasted_iota(I32, (TB, TB), 1)
    tri = jnp.where((r_idx >= c_idx) & (r_idx // CHUNK == c_idx // CHUNK), 1.0, 0.0).astype(BF16)

    z = _dot_split(c_ref[:, 512:640], wa2_ref[...]) + ba_ref[...]
    log_a = (jnp.minimum(z, 0.0) - jnp.log(1.0 + jnp.exp(-jnp.abs(z)))) * (1.0 / GLA_TAU)
    b = sum(_dot(tri, piece) for piece in _split_bf16(log_a, 3))
    q = c_ref[:, 0:128] * (GLA_DK ** -0.5)
    b_scr[...] = b * LOG2E
    q_scr[...] = q
    k_scr[...] = c_ref[:, 128:256]
    v_scr[...] = c_ref[:, 256:512]

    b_last = [b[(c + 1) * CHUNK - 1:(c + 1) * CHUNK, :] for c in range(n_chunks)]
    b_end = jnp.concatenate([jnp.broadcast_to(bl, (CHUNK, LANES)) for bl in b_last], axis=0)
    kd = (c_ref[:, 128:256] * jnp.exp(b_end - b)).astype(BF16)
    qe = (q * jnp.exp(b)).astype(BF16)
    vb = c_ref[:, 256:512].astype(BF16)
    state = s_ref[...]
    inters = []
    for c in range(n_chunks):
        rows = slice(c * CHUNK, (c + 1) * CHUNK)
        inters.append(_dot(qe[rows], state.astype(BF16)))
        scale = jnp.transpose(jnp.broadcast_to(jnp.exp(b_last[c]), (LANES, LANES)))
        state = jnp.where(hmask, state * jnp.concatenate([scale, scale], axis=1) + _dot_tn(kd[rows], vb[rows]), 0.0)
    s_ref[...] = state

    acc_scr[...] = jnp.concatenate(inters, axis=0)

    keys_per_step = 64

    def key_step(s4, carry):
        for c in range(n_chunks):
            rows = slice(c * CHUNK, (c + 1) * CHUNK)
            part = None
            for u in range(keys_per_step):
                r = c * CHUNK + s4 * keys_per_step + u
                gap = pltpu.bitcast(b_scr[rows, :] - b_scr[pl.ds(r, 1), :], I32)
                decay = jnp.exp2(pltpu.bitcast(gap | jnp.int32(INT_MIN), F32))
                w = (q_scr[rows, :] * decay * k_scr[pl.ds(r, 1), :]).astype(BF16)
                term = _dot(w, hsum) * v_scr[pl.ds(r, 1), :]
                part = term if part is None else part + term
            acc_scr[rows, :] += part
        return carry

    lax.fori_loop(0, CHUNK // keys_per_step, key_step, 0)
    o = acc_scr[...]
    for gi in range(2):
        y = _pair_head_norm(o[:, gi * LANES:(gi + 1) * LANES], g_ref[:, gi * LANES:(gi + 1) * LANES], True)
        o_ref[:, gi * LANES:(gi + 1) * LANES] = (_silu(gate_ref[:, gi * LANES:(gi + 1) * LANES]) * y).astype(BF16)


def _gla_specs(nb):
    in_specs = [pl.BlockSpec((TB, C_W), lambda b, i: (b * nb + i, 0)),
                pl.BlockSpec((TB, GROUP_W), lambda b, i: (b * nb + i, 1)),
                pl.BlockSpec((LANES, LANES), lambda b, i: (0, 0)),
                pl.BlockSpec((1, LANES), lambda b, i: (0, 0)),
                pl.BlockSpec((1, GROUP_W), lambda b, i: (0, 0))]
    scratch = [pltpu.VMEM((LANES, GROUP_W), F32), pltpu.VMEM((TB, LANES), F32),
               pltpu.VMEM((TB, LANES), F32), pltpu.VMEM((TB, LANES), F32),
               pltpu.VMEM((TB, GROUP_W), F32), pltpu.VMEM((TB, GROUP_W), F32)]
    return in_specs, scratch


def _ret_gla_kernel(*refs):
    n_ret, n_gla = 7, 5
    o_b, o_c = refs[n_ret + n_gla:n_ret + n_gla + 2]
    scratch = refs[n_ret + n_gla + 2:]
    _ret_kernel(*refs[:n_ret], o_b, scratch[0])
    _gla_kernel(*refs[n_ret:n_ret + n_gla], o_c, *scratch[1:])


def _ret_gla_call(qkvb, gates, ret_tabs, g_b, cc, wa2_p, ba, g_c, B, T):
    nb = T // TB
    ret_in, ret_scratch = _ret_specs(nb)
    gla_in, gla_scratch = _gla_specs(nb)
    out = jax.ShapeDtypeStruct((B * T, GROUP_W), BF16)
    out_spec = pl.BlockSpec((TB, GROUP_W), lambda b, i: (b * nb + i, 0))
    return pl.pallas_call(
        _ret_gla_kernel,
        out_shape=[out, out],
        grid=(B, nb),
        in_specs=ret_in + gla_in,
        out_specs=[out_spec, out_spec],
        scratch_shapes=ret_scratch + gla_scratch,
        compiler_params=_cparams("arbitrary", "arbitrary"),
        name="ret_gla",
    )(qkvb, gates, *ret_tabs, g_b, cc, gates, wa2_p, ba, g_c)


def _rms(y, g):
    return y * lax.rsqrt(jnp.mean(y * y, axis=-1, keepdims=True) + EPS) * g


def _out_mlp_kernel(x_ref, ya_ref, yb_ref, yc_ref, yd_ref, mod_ref, wo_ref, w1_ref, w2_ref,
                    gpost_ref, gpre_ref, gpost2_ref, o_ref):
    y = _dot(ya_ref[...], wo_ref[0:256, :])
    y = y + _dot(yb_ref[...], wo_ref[256:512, :])
    y = y + _dot(yc_ref[...], wo_ref[512:768, :])
    y = y + _dot(yd_ref[...], wo_ref[768:1024, :])
    x = x_ref[...] + mod_ref[0, 2:3, :] * _rms(y, gpost_ref[...])
    h = _rms(x, gpre_ref[...]) * (1.0 + mod_ref[0, 4:5, :]) + mod_ref[0, 3:4, :]
    hb = h.astype(BF16)
    acc = jnp.zeros(x.shape, F32)
    fc = 1024
    for f in range(D_FF // fc):
        u = jnp.maximum(_dot(hb, w1_ref[:, f * fc:(f + 1) * fc]), 0.0)
        acc = acc + _dot((u * u).astype(BF16), w2_ref[f * fc:(f + 1) * fc, :])
    o_ref[...] = x + mod_ref[0, 5:6, :] * _rms(acc, gpost2_ref[...])


def _out_mlp_call(x2, ys, mod3, w_out, w1, w2, g_post, g_pre2, g_post2, T):
    N, D = x2.shape
    nt = T // TM
    row = lambda i: (i, 0)
    const = lambda i: (0, 0)
    return pl.pallas_call(
        _out_mlp_kernel,
        out_shape=jax.ShapeDtypeStruct((N, D), F32),
        grid=(N // TM,),
        in_specs=[pl.BlockSpec((TM, D), row)] + [pl.BlockSpec((TM, GROUP_W), row)] * 4
                 + [pl.BlockSpec((1, 6, D), lambda i: (i // nt, 0, 0)),
                    _resident((D, D), const), _resident((D, D_FF), const), _resident((D_FF, D), const),
                    _resident((1, D), const), _resident((1, D), const), _resident((1, D), const)],
        out_specs=pl.BlockSpec((TM, D), row),
        compiler_params=pltpu.CompilerParams(dimension_semantics=("arbitrary",), vmem_limit_bytes=VMEM_LIMIT,
                                             allow_input_fusion=[False] * 6 + [True] * 3 + [False] * 3),
        name="out_mlp",
    )(x2, *ys, mod3, w_out, w1, w2, g_post, g_pre2, g_post2)


def _relayout_w_in(w_in):
    col = lambda k: w_in[:, _IN_OFF[k]:_IN_OFF[k + 1]]
    zeros = lambda n: jnp.zeros((w_in.shape[0], n), w_in.dtype)
    parts = [col(_A_Q), col(_A_QI), col(_A_K), col(_A_KI), col(_A_V), zeros(64),
             col(_A_WI), zeros(120),
             col(_B_Q), col(_B_K), col(_B_V),
             col(_B_G), col(_C_G),
             col(_C_Q), col(_C_K), col(_C_V), col(_C_A), zeros(112),
             col(_D_Q), col(_D_K), col(_D_V)]
    return jnp.concatenate(parts, axis=1).astype(BF16)


def _rope_tables(pos, rot, period, theta):
    half = rot // 2
    inv = theta ** (-jnp.arange(half, dtype=F32) / half)
    ang = pos.astype(F32)[..., None] * inv
    cos, sin = jnp.cos(ang), jnp.sin(ang)
    pad = ang.shape[:-1] + (period - rot,)
    ct = jnp.concatenate([cos, cos, jnp.ones(pad, F32)], axis=-1)
    st = jnp.concatenate([-sin, sin, jnp.zeros(pad, F32)], axis=-1)
    return ct.reshape(-1, period), st.reshape(-1, period)


def _layer(x2, cond, tabs, ret_tabs, layer_idx, B, T, mod_w, mod_b, attn_pre_g, attn_post_g,
           mlp_pre_g, mlp_post_g, w_in, gla_wa2, gla_ba, lam_q1, lam_k1, lam_q2, lam_k2,
           head_norm_g, w_out, mlp_w1, mlp_w2):
    D = D_MODEL
    mod3 = _mod_call(cond, mod_w, mod_b).reshape(B, 6, D)
    qa, kva, wa, qkvb, gates, cc, dqkv, vta, vtd = _in_call(
        x2, mod3, attn_pre_g.reshape(1, D), _relayout_w_in(w_in), tabs, T)

    g_a, g_b, g_c, g_d = jnp.split(head_norm_g, 4)
    gain_t = lambda g, tq: jnp.broadcast_to(g[:, None], (GROUP_W, tq))
    y_a = _dsa_call(qa, wa, kva, vta, gain_t(g_a, TQ_DSA), B, T)
    wa2_p = jnp.zeros((LANES, LANES), F32).at[:GLA_RANK].set(gla_wa2)
    y_b, y_c = _ret_gla_call(qkvb, gates, ret_tabs, g_b.reshape(1, GROUP_W), cc, wa2_p,
                             gla_ba.reshape(1, LANES), g_c.reshape(1, GROUP_W), B, T)
    lam_init = 0.8 - 0.6 * math.exp(-0.3 * layer_idx)
    lam_vecs = jnp.zeros((4, LANES), F32).at[:, :DIFF_DK].set(jnp.stack([lam_q1, lam_k1, lam_q2, lam_k2]))
    y_d = _diff_call(dqkv, vtd, lam_vecs, gain_t(g_d, TQ_DIFF), lam_init, B, T)

    return _out_mlp_call(x2, (y_a, y_b, y_c, y_d), mod3, w_out.astype(BF16), mlp_w1.astype(BF16),
                         mlp_w2.astype(BF16), attn_post_g.reshape(1, D), mlp_pre_g.reshape(1, D),
                         mlp_post_g.reshape(1, D), T)


def kernel(x, c, positions, mod_w, mod_b, attn_pre_g, attn_post_g, mlp_pre_g, mlp_post_g, w_in,
           gla_wa2, gla_ba, lam_q1, lam_k1, lam_q2, lam_k2, head_norm_g, w_out, mlp_w1, mlp_w2):
    B, T, D = x.shape
    assert D == D_MODEL and T % TM == 0 and T % TB == 0
    tabs = (_rope_tables(positions, HEAD_DIM // ROPE_FRAC, HEAD_DIM, ROPE_THETA)
            + _rope_tables(positions, HEAD_DIM, HEAD_DIM, RET_THETA)
            + _rope_tables(positions, DIFF_DK // ROPE_FRAC, DIFF_DK, ROPE_THETA))
    ret_tabs = _ret_tables()
    x2 = x.reshape(B * T, D)
    for l in range(mod_w.shape[0]):
        x2 = _layer(x2, c, tabs, ret_tabs, l, B, T, mod_w[l], mod_b[l], attn_pre_g[l], attn_post_g[l],
                    mlp_pre_g[l], mlp_post_g[l], w_in[l], gla_wa2[l], gla_ba[l], lam_q1[l], lam_k1[l],
                    lam_q2[l], lam_k2[l], head_norm_g[l], w_out[l], mlp_w1[l], mlp_w2[l])
    return x2.reshape(B, T, D)
```
